```python
import jax
import jax.numpy as jnp
from jax import lax
import numpy as np


D_MODEL = 1024
BATCH = 8
SEQ = 2048
DEPTH = 4

GRID_W = 64
CTX_LEN = 256

A_HEADS = 4
A_DQK = 128
A_DV = 256
A_QK = A_HEADS * A_DQK
A_V = A_HEADS * A_DV
A_CHUNK = 64
A_CONV = 3
ROPE_BASE = 10000.0

B_HEADS = 16
B_DH = 64
B_W = B_HEADS * B_DH
NA_ROWS = 8
NA_COLS = 16
NA_QBR = 2
NA_QBC = 32

D_FF = 2816
N_EXPERTS = 8
TOP_K = 2
D_FF_EXPERT = 3584

EPS = 1e-6

IN_SPLITS = (A_QK, A_QK, A_V, A_V, 4 * A_HEADS, B_W, B_W, B_W, D_MODEL, D_MODEL)
D_IN = sum(IN_SPLITS)

kernel_name = 'hybrid_mlstm_natten_moe_dit'


def rms_norm(x, g):
    x32 = x.astype(jnp.float32)
    y = x32 * lax.rsqrt(jnp.mean(x32 * x32, axis=-1, keepdims=True) + EPS)
    return (y * g.astype(jnp.float32)).astype(x.dtype)


def split_cols(p):
    idx = np.cumsum(IN_SPLITS)[:-1].tolist()
    return jnp.split(p, idx, axis=-1)


def centred_dwconv(u, w):
    kw = w.shape[0]
    return lax.conv_general_dilated(u, w[:, None, :].astype(u.dtype), window_strides=(1,),
                                    padding=[(kw // 2, kw - 1 - kw // 2)],
                                    dimension_numbers=('NWC', 'WIO', 'NWC'),
                                    feature_group_count=u.shape[-1])


def axial_rope(x):
    t, d = x.shape[1], x.shape[-1]
    pos = jnp.arange(t)
    rows = (pos // GRID_W).astype(jnp.float32)
    cols = (pos % GRID_W).astype(jnp.float32)
    quarter = d // 4
    inv = 1.0 / (ROPE_BASE ** (jnp.arange(quarter, dtype=jnp.float32) / quarter))

    def rot(xp, p):
        ang = p[:, None] * inv[None, :]
        cos = jnp.cos(ang)[None, :, None, :]
        sin = jnp.sin(ang)[None, :, None, :]
        x1, x2 = xp[..., :quarter], xp[..., quarter:]
        return jnp.concatenate([x1 * cos - x2 * sin, x1 * sin + x2 * cos], axis=-1)

    xf = x.astype(jnp.float32)
    out = jnp.concatenate([rot(xf[..., :d // 2], rows), rot(xf[..., d // 2:], cols)], axis=-1)
    return out.astype(x.dtype)


def mlstm_chunkwise(q, k, v, log_i, log_f, state):
    b_, h, t, dk = q.shape
    dv = v.shape[-1]
    nc = t // A_CHUNK

    def chunks(a):
        return jnp.moveaxis(a.reshape(b_, h, nc, A_CHUNK, *a.shape[3:]), 2, 0)

    causal = jnp.tril(jnp.ones((A_CHUNK, A_CHUNK), dtype=bool))

    def step(carry, inp):
        c_mat, n_vec, m = carry
        qc, kc, vc, ic, fc = inp
        b = jnp.cumsum(fc, axis=-1)
        d_mat = jnp.where(causal, b[..., :, None] - b[..., None, :] + ic[..., None, :], -jnp.inf)
        inter = b + m[..., None]
        m_row = jnp.maximum(inter, jnp.max(d_mat, axis=-1))
        s = jnp.einsum('bhld,bhsd->bhls', qc, kc) * jnp.exp(d_mat - m_row[..., None])
        w_int = jnp.exp(inter - m_row)
        num = jnp.einsum('bhls,bhsv->bhlv', s, vc) + w_int[..., None] * jnp.einsum('bhvd,bhld->bhlv', c_mat, qc)
        den = jnp.sum(s, axis=-1) + w_int * jnp.einsum('bhd,bhld->bhl', n_vec, qc)
        h_out = num / jnp.maximum(jnp.abs(den), jnp.exp(-m_row))[..., None]
        b_last = b[..., -1]
        g = b_last[..., None] - b + ic
        m_new = jnp.maximum(b_last + m, jnp.max(g, axis=-1))
        w_k = jnp.exp(g - m_new[..., None])
        decay = jnp.exp(b_last + m - m_new)
        c_new = decay[..., None, None] * c_mat + jnp.einsum('bhlv,bhld->bhvd', w_k[..., None] * vc, kc)
        n_new = decay[..., None] * n_vec + jnp.einsum('bhl,bhld->bhd', w_k, kc)
        return (c_new, n_new, m_new), h_out

    state, hs = lax.scan(step, state, (chunks(q), chunks(k), chunks(v), chunks(log_i), chunks(log_f)))
    hs = jnp.moveaxis(hs, 0, 2).reshape(b_, h, t, dv)
    return hs, state


def head_layer_norm(h, g):
    mu = jnp.mean(h, axis=-1, keepdims=True)
    var = jnp.mean(jnp.square(h - mu), axis=-1, keepdims=True)
    y = (h - mu) * lax.rsqrt(var + EPS)
    y = y.transpose(0, 2, 1, 3).reshape(h.shape[0], h.shape[2], -1)
    return y * g.astype(jnp.float32)


def na_latent(q, k, v, k_ctx, v_ctx, rpb):
    b_, t, h, d = q.shape
    rows = t // GRID_W
    wr = min(NA_ROWS, rows)
    wc = NA_COLS
    qbr, qbc = NA_QBR, NA_QBC
    kbr = min(qbr + wr - 1, rows)
    kbc = min(qbc + wc - 1, GRID_W)
    nbr, nbc = rows // qbr, GRID_W // qbc
    nq, nk = qbr * qbc, kbr * kbc
    r = np.arange(rows)
    col = np.arange(GRID_W)
    rs = np.clip(r - wr // 2, 0, rows - wr)
    cs = np.clip(col - wc // 2, 0, GRID_W - wc)
    key_r = np.minimum(rs[::qbr], rows - kbr)[:, None] + np.arange(kbr)
    key_c = np.minimum(cs[::qbc], GRID_W - kbc)[:, None] + np.arange(kbc)
    full = (nbr, nbc, qbr, qbc, kbr, kbc)
    qr = r.reshape(nbr, qbr)[:, None, :, None, None, None]
    qcl = col.reshape(nbc, qbc)[None, :, None, :, None, None]
    kr = key_r[:, None, None, None, :, None]
    kcl = key_c[None, :, None, None, None, :]
    valid = (kr >= rs[qr]) & (kr < rs[qr] + wr) & (kcl >= cs[qcl]) & (kcl < cs[qcl] + wc)
    valid = np.broadcast_to(valid, full).reshape(nbr, nbc, nq, nk)
    dr = np.broadcast_to(np.clip(kr - qr + NA_ROWS - 1, 0, 2 * NA_ROWS - 2), full).reshape(nbr, nbc, nq, nk)
    dc = np.broadcast_to(np.clip(kcl - qcl + NA_COLS - 1, 0, 2 * NA_COLS - 2), full).reshape(nbr, nbc, nq, nk)
    key_idx = (key_r[:, None, :, None] * GRID_W + key_c[None, :, None, :]).reshape(-1)
    bias = jnp.moveaxis(rpb.astype(jnp.float32)[:, dr, dc], 0, 2)
    bias = jnp.where(valid[:, :, None], bias, -jnp.inf)
    qb = q.reshape(b_, nbr, qbr, nbc, qbc, h, d).transpose(0, 1, 3, 2, 4, 5, 6).reshape(b_, nbr, nbc, nq, h, d)
    kg = jnp.take(k, key_idx, axis=1).reshape(b_, nbr, nbc, nk, h, d)
    vg = jnp.take(v, key_idx, axis=1).reshape(b_, nbr, nbc, nk, h, d)
    s_loc = jnp.einsum('bijqhd,bijkhd->bijhqk', qb, kg).astype(jnp.float32) + bias
    s_ctx = jnp.einsum('bijqhd,bchd->bijhqc', qb, k_ctx).astype(jnp.float32)
    p = jax.nn.softmax(jnp.concatenate([s_loc, s_ctx], axis=-1), axis=-1).astype(v.dtype)
    o = (jnp.einsum('bijhqk,bijkhd->bijqhd', p[..., :nk], vg)
         + jnp.einsum('bijhqc,bchd->bijqhd', p[..., nk:], v_ctx))
    o = o.reshape(b_, nbr, nbc, qbr, qbc, h, d).transpose(0, 1, 3, 2, 4, 5, 6)
    return o.reshape(b_, t, h * d)


def ctx_attention(q, k, v):
    s = jnp.einsum('bqhd,bkhd->bhqk', q, k).astype(jnp.float32)
    p = jax.nn.softmax(s, axis=-1).astype(v.dtype)
    o = jnp.einsum('bhqk,bkhd->bqhd', p, v)
    return o.reshape(o.shape[0], o.shape[1], -1)


def mixer(hx, hc, w_in, a_conv, a_gate_b, a_hnorm_g, na_rpb, w_br_a, w_br_b, w_out, ctx_out):
    b_ = hx.shape[0]
    px = split_cols(hx @ w_in)
    pc = split_cols(hc @ w_in)

    def mlstm_inputs(p, rope):
        q_pre, k_pre, v, o_pre, gates = p[0], p[1], p[2], p[3], p[4]
        bb, tt = v.shape[0], v.shape[1]
        qk = jax.nn.silu(centred_dwconv(jnp.concatenate([q_pre, k_pre], axis=-1), a_conv))
        q = qk[..., :A_QK].reshape(bb, tt, A_HEADS, A_DQK)
        k = qk[..., A_QK:].reshape(bb, tt, A_HEADS, A_DQK)
        if rope:
            q, k = axial_rope(q), axial_rope(k)
        q = q.transpose(0, 2, 1, 3).astype(jnp.float32)
        k = k.transpose(0, 2, 1, 3).astype(jnp.float32) * (A_DQK ** -0.5)
        v = v.reshape(bb, tt, A_HEADS, A_DV).transpose(0, 2, 1, 3).astype(jnp.float32)
        g = (gates + a_gate_b).astype(jnp.float32).reshape(bb, tt, 4, A_HEADS).transpose(2, 0, 3, 1)
        fwd = (g[0], jax.nn.log_sigmoid(g[1]))
        bwd = (g[2], jax.nn.log_sigmoid(g[3]))
        return q, k, v, fwd, bwd, o_pre

    def rev(a):
        return jnp.flip(a, axis=2)

    qca, kca, vca, gcf, gcb, oc = mlstm_inputs(pc, False)
    qxa, kxa, vxa, gxf, gxb, ox = mlstm_inputs(px, True)
    zero = (jnp.zeros((b_, A_HEADS, A_DV, A_DQK), jnp.float32),
            jnp.zeros((b_, A_HEADS, A_DQK), jnp.float32),
            jnp.zeros((b_, A_HEADS), jnp.float32))
    hcf, st_cf = mlstm_chunkwise(qca, kca, vca, gcf[0], gcf[1], zero)
    hxf, _ = mlstm_chunkwise(qxa, kxa, vxa, gxf[0], gxf[1], st_cf)
    hcb, st_cb = mlstm_chunkwise(rev(qca), rev(kca), rev(vca), rev(gcb[0]), rev(gcb[1]), zero)
    hxb, _ = mlstm_chunkwise(rev(qxa), rev(kxa), rev(vxa), rev(gxb[0]), rev(gxb[1]), st_cb)
    ha_x = (head_layer_norm(hxf + rev(hxb), a_hnorm_g) * jax.nn.sigmoid(ox.astype(jnp.float32))).astype(hx.dtype)

    def na_heads(a):
        return a.reshape(a.shape[0], a.shape[1], B_HEADS, B_DH)

    qxb, kxb, vxb = na_heads(px[5]) * (B_DH ** -0.5), na_heads(px[6]), na_heads(px[7])
    qcb, kcb, vcb = na_heads(pc[5]) * (B_DH ** -0.5), na_heads(pc[6]), na_heads(pc[7])
    hb_x = na_latent(qxb, kxb, vxb, kcb, vcb, na_rpb)

    def merge(p, ha, hb):
        return (jax.nn.sigmoid(p[8]) * (ha @ w_br_a) + jax.nn.sigmoid(p[9]) * (hb @ w_br_b)) @ w_out

    yx = merge(px, ha_x, hb_x)
    if ctx_out:
        ha_c = (head_layer_norm(hcf + rev(hcb), a_hnorm_g) * jax.nn.sigmoid(oc.astype(jnp.float32))).astype(hc.dtype)
        hb_c = ctx_attention(qcb, kcb, vcb)
        yc = merge(pc, ha_c, hb_c)
    else:
        yc = None
    return yx, yc


def swiglu(h, w1, w3, w2):
    return (jax.nn.silu(h @ w1) * (h @ w3)) @ w2


def moe_swiglu(h, router, w1, w3, w2):
    logits = (h @ router).astype(jnp.float32)
    top_v, top_i = lax.top_k(logits, TOP_K)
    top_w = jax.nn.softmax(top_v, axis=-1)
    gates = jnp.sum(jax.nn.one_hot(top_i, N_EXPERTS, dtype=jnp.float32) * top_w[..., None], axis=-2)
    out = jnp.zeros_like(h)
    for e in range(N_EXPERTS):
        out = out + gates[..., e:e + 1].astype(h.dtype) * swiglu(h, w1[e], w3[e], w2[e])
    return out


def setup_inputs(seed: int = 0) -> dict:
    key = jax.random.key(seed)
    ks = jax.random.split(key, 24)
    f32 = jnp.float32

    def nrm(k, shape, s):
        return jax.random.normal(k, shape, f32) * s

    nl = DEPTH
    nd = (DEPTH + 1) // 2
    nm = DEPTH // 2
    gate_base = jnp.repeat(jnp.array([0.0, 3.0, 0.0, 3.0], f32), A_HEADS)[None, :]
    return {
        'x': nrm(ks[0], (BATCH, SEQ, D_MODEL), 1.0),
        'c': nrm(ks[1], (BATCH, D_MODEL), 1.0),
        'ctx': nrm(ks[2], (BATCH, CTX_LEN, D_MODEL), 1.0),
        'c_ctx': nrm(ks[3], (D_MODEL,), 1.0),
        'w_mod': nrm(ks[4], (nl, D_MODEL, 6 * D_MODEL), 0.5 * D_MODEL ** -0.5),
        'b_mod': nrm(ks[5], (nl, 6 * D_MODEL), 0.02),
        'g_norm1': 1.0 + nrm(ks[6], (nl, D_MODEL), 0.02),
        'g_norm2': 1.0 + nrm(ks[7], (nl, D_MODEL), 0.02),
        'w_in': nrm(ks[8], (nl, D_MODEL, D_IN), D_MODEL ** -0.5),
        'a_conv': nrm(ks[9], (nl, A_CONV, 2 * A_QK), A_CONV ** -0.5),
        'a_gate_b': gate_base + nrm(ks[10], (nl, 4 * A_HEADS), 0.1),
        'a_hnorm_g': 1.0 + nrm(ks[11], (nl, A_V), 0.02),
        'na_rpb': nrm(ks[12], (nl, B_HEADS, 2 * NA_ROWS - 1, 2 * NA_COLS - 1), 0.1),
        'w_br_a': nrm(ks[13], (nl, A_V, D_MODEL), A_V ** -0.5),
        'w_br_b': nrm(ks[14], (nl, B_W, D_MODEL), B_W ** -0.5),
        'w_out': nrm(ks[15], (nl, D_MODEL, D_MODEL), D_MODEL ** -0.5),
        'ffn_w1': nrm(ks[16], (nd, D_MODEL, D_FF), D_MODEL ** -0.5),
        'ffn_w3': nrm(ks[17], (nd, D_MODEL, D_FF), D_MODEL ** -0.5),
        'ffn_w2': nrm(ks[18], (nd, D_FF, D_MODEL), D_FF ** -0.5),
        'moe_router': nrm(ks[19], (nm, D_MODEL, N_EXPERTS), D_MODEL ** -0.5),
        'moe_w1': nrm(ks[20], (nm, N_EXPERTS, D_MODEL, D_FF_EXPERT), D_MODEL ** -0.5),
        'moe_w3': nrm(ks[21], (nm, N_EXPERTS, D_MODEL, D_FF_EXPERT), D_MODEL ** -0.5),
        'moe_w2': nrm(ks[22], (nm, N_EXPERTS, D_FF_EXPERT, D_MODEL), D_FF_EXPERT ** -0.5),
        'g_final': 1.0 + nrm(ks[23], (D_MODEL,), 0.02),
    }


def reference(x, c, ctx, c_ctx, w_mod, b_mod, g_norm1, g_norm2, w_in, a_conv, a_gate_b, a_hnorm_g,
              na_rpb, w_br_a, w_br_b, w_out, ffn_w1, ffn_w3, ffn_w2, moe_router, moe_w1, moe_w3,
              moe_w2, g_final):
    s_lat = jax.nn.silu(c)
    s_ctx = jax.nn.silu(c_ctx)
    xc = ctx
    for l in range(DEPTH):
        last = l == DEPTH - 1
        mod_x = (s_lat @ w_mod[l] + b_mod[l])[:, None, :]
        mod_c = s_ctx @ w_mod[l] + b_mod[l]
        shx1, scx1, gx1, shx2, scx2, gx2 = jnp.split(mod_x, 6, axis=-1)
        shc1, scc1, gc1, shc2, scc2, gc2 = jnp.split(mod_c, 6, axis=-1)

        hx = rms_norm(x, g_norm1[l]) * (1.0 + scx1) + shx1
        hc = rms_norm(xc, g_norm1[l]) * (1.0 + scc1) + shc1
        yx, yc = mixer(hx, hc, w_in[l], a_conv[l], a_gate_b[l], a_hnorm_g[l], na_rpb[l],
                       w_br_a[l], w_br_b[l], w_out[l], not last)
        x = x + gx1 * yx
        if not last:
            xc = xc + gc1 * yc

        if l % 2 == 0:
            j = l // 2

            def ffn(h, j=j):
                return swiglu(h, ffn_w1[j], ffn_w3[j], ffn_w2[j])
        else:
            j = l // 2

            def ffn(h, j=j):
                return moe_swiglu(h, moe_router[j], moe_w1[j], moe_w3[j], moe_w2[j])
        x = x + gx2 * ffn(rms_norm(x, g_norm2[l]) * (1.0 + scx2) + shx2)
        if not last:
            xc = xc + gc2 * ffn(rms_norm(xc, g_norm2[l]) * (1.0 + scc2) + shc2)
    return rms_norm(x, g_final)
```

```python
import functools

import numpy as np
import jax
import jax.numpy as jnp
from jax import lax
from jax.experimental import pallas as pl
from jax.experimental.pallas import tpu as pltpu

F32 = jnp.float32
BF16 = jnp.bfloat16

D = 1024
BATCH = 8
SEQ = 2048
CTX = 256
TT = CTX + SEQ
NTOK = BATCH * TT
DEPTH = 4
GRID_W = 64
GRID_H = SEQ // GRID_W

A_HEADS = 4
A_DQK = 128
A_DV = 256
A_QK = A_HEADS * A_DQK
A_V = A_HEADS * A_DV
ROPE_BASE = 10000.0
LCH = 256
NCH = TT // LCH

B_HEADS = 16
B_DH = 64
NA_ROWS = 8
NA_COLS = 16
NA_QROWS = 2
NA_KROWS = NA_QROWS + NA_ROWS - 1
NA_NQ = NA_QROWS * GRID_W
NA_NK = NA_KROWS * GRID_W
NA_STEPS = GRID_H // NA_QROWS

D_FF = 2816
N_EXPERTS = 8
D_FF_EXPERT = 3584
EPS = 1e-6

RB = 256
NRB = TT // RB
MM_TM = 1024
MOE_TM = 512
MOE_FC = 512
LANES = 128

PB_QK, PB_V, PB_O, PB_NQ, PB_NK, PB_NV, PB_GA, PB_GB = range(8)
P_COLS = 8 * D

VMEM_LIMIT = 56 * 1024 * 1024


def _cp(sem, vmem=VMEM_LIMIT):
    return pltpu.CompilerParams(dimension_semantics=sem, vmem_limit_bytes=vmem)


def _sigmoid(x):
    return 1.0 / (1.0 + jnp.exp(-x))


def _silu(x):
    return x * _sigmoid(x)


def _log_sigmoid(x):
    return jnp.minimum(x, 0.0) - jnp.log(1.0 + jnp.exp(-jnp.abs(x)))


def _norm_mod(x, g, sc, sh):
    ms = jnp.mean(x * x, axis=-1, keepdims=True)
    y = x * lax.rsqrt(ms + EPS)
    return (y * g) * (1.0 + sc) + sh


def _mod_row(b, i):
    return jnp.where(i == 0, BATCH, b)


def _mod_kernel(c_ref, w_ref, b_ref, o_ref):
    c = c_ref[...]
    s = _silu(c).astype(BF16)
    o_ref[0] = jnp.dot(s, w_ref[0].astype(BF16), preferred_element_type=F32) + b_ref[0]


def _modulation(cc, w_mod, b_mod):
    tn = 2048
    nl = w_mod.shape[0]
    return pl.pallas_call(
        _mod_kernel,
        grid=(nl, 6 * D // tn),
        in_specs=[pl.BlockSpec((16, D), lambda l, j: (0, 0)),
                  pl.BlockSpec((1, D, tn), lambda l, j: (l, 0, j)),
                  pl.BlockSpec((1, 1, tn), lambda l, j: (l, 0, j))],
        out_specs=pl.BlockSpec((1, 16, tn), lambda l, j: (l, 0, j)),
        out_shape=jax.ShapeDtypeStruct((nl, 16, 6 * D), F32),
        compiler_params=_cp(("parallel", "parallel")),
        name="modulation",
    )(cc, w_mod, b_mod.reshape(nl, 1, 6 * D))


def _norm_kernel(x_ref, g_ref, mod_ref, o_ref):
    mod = mod_ref[0]
    o_ref[0] = _norm_mod(x_ref[0], g_ref[...], mod[1:2], mod[0:1]).astype(o_ref.dtype)


def _first_norm(x, g, mod):
    return pl.pallas_call(
        _norm_kernel,
        grid=(BATCH, NRB),
        in_specs=[pl.BlockSpec((1, RB, D), lambda b, i: (b, i, 0)),
                  pl.BlockSpec((1, D), lambda b, i: (0, 0)),
                  pl.BlockSpec((1, 6, D), lambda b, i: (_mod_row(b, i), 0, 0))],
        out_specs=pl.BlockSpec((1, RB, D), lambda b, i: (b, i, 0)),
        out_shape=jax.ShapeDtypeStruct((BATCH, TT, D), BF16),
        compiler_params=_cp(("parallel", "parallel")),
        name="first_norm",
    )(x, g, mod)


def _mm_kernel(a_ref, w_ref, o_ref):
    o_ref[...] = jnp.dot(a_ref[...], w_ref[...], preferred_element_type=F32).astype(o_ref.dtype)


def _in_proj(h, w):
    tn = 1024
    return pl.pallas_call(
        _mm_kernel,
        grid=(P_COLS // tn, NTOK // MM_TM),
        in_specs=[pl.BlockSpec((MM_TM, D), lambda j, i: (i, 0)),
                  pl.BlockSpec((D, tn), lambda j, i: (0, j))],
        out_specs=pl.BlockSpec((MM_TM, tn), lambda j, i: (i, j)),
        out_shape=jax.ShapeDtypeStruct((NTOK, P_COLS), BF16),
        compiler_params=_cp(("parallel", "parallel")),
        name="in_proj",
    )(h, w)


def _gate_kernel(a_ref, w_ref, b_ref, o_ref):
    o_ref[...] = jnp.dot(a_ref[...], w_ref[...], preferred_element_type=F32) + b_ref[...]


def _gate_proj(h, wg, bg):
    return pl.pallas_call(
        _gate_kernel,
        grid=(NTOK // MM_TM,),
        in_specs=[pl.BlockSpec((MM_TM, D), lambda i: (i, 0)),
                  pl.BlockSpec((D, LANES), lambda i: (0, 0)),
                  pl.BlockSpec((1, LANES), lambda i: (0, 0))],
        out_specs=pl.BlockSpec((MM_TM, LANES), lambda i: (i, 0)),
        out_shape=jax.ShapeDtypeStruct((NTOK, LANES), F32),
        compiler_params=_cp(("parallel",)),
        name="gate_proj",
    )(h, wg, bg)


def _prep_kernel(u_ref, up_ref, un_ref, w_ref, c_ref, s_ref, o_ref):
    i = pl.program_id(1)
    u = u_ref[0].astype(F32)
    prev_row = jnp.where(i >= 2, up_ref[0, 15:16, :].astype(F32), 0.0)
    next_row = jnp.where((i >= 1) & (i <= NRB - 2), un_ref[0, 0:1, :].astype(F32), 0.0)
    rid = lax.broadcasted_iota(jnp.int32, (RB, 1), 0)
    u_m1 = jnp.where(rid == 0, prev_row, pltpu.roll(u, 1, 0))
    u_p1 = jnp.where(rid == RB - 1, next_row, pltpu.roll(u, RB - 1, 0))
    w = w_ref[...]
    y = w[0:1] * u_m1 + w[1:2] * u + w[2:3] * u_p1
    y = _silu(y)
    c = c_ref[...]
    s = s_ref[...]
    cfull = jnp.concatenate([c[:, :A_DQK]] * A_HEADS + [c[:, A_DQK:]] * A_HEADS, axis=1)
    sfull = jnp.concatenate([s[:, :A_DQK]] * A_HEADS + [s[:, A_DQK:]] * A_HEADS, axis=1)
    lane = lax.broadcasted_iota(jnp.int32, (1, 2 * A_QK), 1)
    partner = jnp.where((lane & 32) == 0,
                        pltpu.roll(y, 2 * A_QK - 32, 1), pltpu.roll(y, 32, 1))
    o_ref[0] = (y * cfull + partner * sfull).astype(o_ref.dtype)


def _mlstm_prep(p3, conv_w, rope_c, rope_s):
    nb16 = TT // 16
    return pl.pallas_call(
        _prep_kernel,
        grid=(BATCH, NRB),
        in_specs=[pl.BlockSpec((1, RB, D), lambda b, i: (b, i, PB_QK)),
                  pl.BlockSpec((1, 16, D), lambda b, i: (b, jnp.maximum(i * (RB // 16) - 1, 0), PB_QK)),
                  pl.BlockSpec((1, 16, D), lambda b, i: (b, jnp.minimum((i + 1) * (RB // 16), nb16 - 1), PB_QK)),
                  pl.BlockSpec((3, D), lambda b, i: (0, 0)),
                  pl.BlockSpec((RB, 2 * A_DQK), lambda b, i: (i, 0)),
                  pl.BlockSpec((RB, 2 * A_DQK), lambda b, i: (i, 0))],
        out_specs=pl.BlockSpec((1, RB, D), lambda b, i: (b, i, 0)),
        out_shape=jax.ShapeDtypeStruct((BATCH, TT, D), BF16),
        compiler_params=_cp(("parallel", "parallel")),
        name="mlstm_prep",
    )(p3, p3, p3, conv_w, rope_c, rope_s)


def _mlstm_kernel(qk_ref, v_ref, gr_ref, gc_ref, o_ref, ct_ref, n_ref, m_ref):
    dr = pl.program_id(1)
    st = pl.program_id(2)

    @pl.when(st == 0)
    def _():
        ct_ref[...] = jnp.zeros_like(ct_ref)
        n_ref[...] = jnp.zeros_like(n_ref)
        m_ref[...] = jnp.zeros_like(m_ref)

    r = lax.broadcasted_iota(jnp.int32, (LCH, LCH), 0)
    c = lax.broadcasted_iota(jnp.int32, (LCH, LCH), 1)
    sgn = 1 - 2 * dr
    mask_a = (c - r) * sgn <= 0
    mask_b = (r - c) * sgn <= 0
    gr = gr_ref[0, 0, 0]
    gc = gc_ref[0, 0, 0]
    for h in range(A_HEADS):
        i_row = gr[h:h + 1, :]
        f_row = _log_sigmoid(gr[A_HEADS + h:A_HEADS + h + 1, :])
        i_col = gc[:, h:h + 1]
        f_col = _log_sigmoid(gc[:, A_HEADS + h:A_HEADS + h + 1])
        b_col = jnp.sum(jnp.where(mask_a, f_row, 0.0), axis=1, keepdims=True)
        b_row = jnp.sum(jnp.where(mask_b, f_col, 0.0), axis=0, keepdims=True)
        b_last = jnp.sum(f_row, axis=1, keepdims=True)
        m = m_ref[h][:, 0:1]
        dmat = jnp.where(mask_a, b_col - b_row + i_row, -jnp.inf)
        inter = b_col + m
        m_row = jnp.maximum(inter, jnp.max(dmat, axis=1, keepdims=True))
        q = qk_ref[0, :, h * A_DQK:(h + 1) * A_DQK]
        k = qk_ref[0, :, A_QK + h * A_DQK:A_QK + (h + 1) * A_DQK]
        v = v_ref[0, :, h * A_DV:(h + 1) * A_DV]
        s = lax.dot_general(q, k, (((1,), (1,)), ((), ())), preferred_element_type=F32)
        s = s * jnp.exp(dmat - m_row)
        w_int = jnp.exp(inter - m_row)
        ct = ct_ref[h]
        nvec = n_ref[h]
        num = (jnp.dot(s.astype(BF16), v, preferred_element_type=F32)
               + w_int * jnp.dot(q, ct.astype(BF16), preferred_element_type=F32))
        den = (jnp.sum(s, axis=1, keepdims=True)
               + w_int * jnp.sum(q.astype(F32) * nvec, axis=1, keepdims=True))
        hout = num / jnp.maximum(jnp.abs(den), jnp.exp(-m_row))
        o_ref[0, 0, :, h * A_DV:(h + 1) * A_DV] = hout.astype(o_ref.dtype)
        g_col = b_last - b_col + i_col
        m_new = jnp.maximum(b_last + m, jnp.max(g_col, axis=0, keepdims=True))
        w_k = jnp.exp(g_col - m_new)
        decay = jnp.exp(b_last + m - m_new)
        wv = (w_k * v.astype(F32)).astype(BF16)
        ct_ref[h] = decay * ct + lax.dot_general(k, wv, (((0,), (0,)), ((), ())),
                                                 preferred_element_type=F32)
        n_ref[h] = decay * nvec + jnp.sum(w_k * k.astype(F32), axis=0, keepdims=True)
        m_ref[h] = jnp.broadcast_to(m_new, (1, LANES))


def _scan_chunk(d, s):
    return jnp.where(d == 0, s, jnp.where(s == 0, 0, NCH - s))


def _mlstm_scan(qk, p3, gr, gc):
    return pl.pallas_call(
        _mlstm_kernel,
        grid=(BATCH, 2, NCH),
        in_specs=[pl.BlockSpec((1, LCH, D), lambda b, d, s: (b, _scan_chunk(d, s), 0)),
                  pl.BlockSpec((1, LCH, A_V), lambda b, d, s: (b, _scan_chunk(d, s), PB_V)),
                  pl.BlockSpec((1, 1, 1, 8, LCH), lambda b, d, s: (b, d, _scan_chunk(d, s), 0, 0)),
                  pl.BlockSpec((1, 1, 1, LCH, 8), lambda b, d, s: (b, d, _scan_chunk(d, s), 0, 0))],
        out_specs=pl.BlockSpec((1, 1, LCH, A_V), lambda b, d, s: (d, b, _scan_chunk(d, s), 0)),
        out_shape=jax.ShapeDtypeStruct((2, BATCH, TT, A_V), BF16),
        scratch_shapes=[pltpu.VMEM((A_HEADS, A_DQK, A_DV), F32),
                        pltpu.VMEM((A_HEADS, 1, A_DQK), F32),
                        pltpu.VMEM((A_HEADS, 1, LANES), F32)],
        compiler_params=_cp(("parallel", "parallel", "arbitrary")),
        name="mlstm_scan",
    )(qk, p3, gr, gc)


def _na_kernel(q_ref, k_ref, v_ref, bias_ref, o_ref):
    i = pl.program_id(1)
    kr0 = jnp.minimum(jnp.clip(NA_QROWS * i - NA_ROWS // 2, 0, GRID_H - NA_ROWS), GRID_H - NA_KROWS)
    start = pl.multiple_of(CTX + kr0 * GRID_W, GRID_W)
    for h in range(B_HEADS):
        lo, hi = h * B_DH, (h + 1) * B_DH
        qh = q_ref[0, :, lo:hi]
        kl = k_ref[0, pl.ds(start, NA_NK), lo:hi]
        vl = v_ref[0, pl.ds(start, NA_NK), lo:hi]
        kc = k_ref[0, 0:CTX, lo:hi]
        vc = v_ref[0, 0:CTX, lo:hi]
        s_loc = lax.dot_general(qh, kl, (((1,), (1,)), ((), ())), preferred_element_type=F32)
        s_loc = s_loc * (B_DH ** -0.5) + bias_ref[0, h].astype(F32)
        s_ctx = lax.dot_general(qh, kc, (((1,), (1,)), ((), ())), preferred_element_type=F32)
        s_ctx = s_ctx * (B_DH ** -0.5)
        mx = jnp.maximum(jnp.max(s_loc, axis=1, keepdims=True), jnp.max(s_ctx, axis=1, keepdims=True))
        p_loc = jnp.exp(s_loc - mx)
        p_ctx = jnp.exp(s_ctx - mx)
        den = jnp.sum(p_loc, axis=1, keepdims=True) + jnp.sum(p_ctx, axis=1, keepdims=True)
        o = (jnp.dot(p_loc.astype(BF16), vl, preferred_element_type=F32)
             + jnp.dot(p_ctx.astype(BF16), vc, preferred_element_type=F32))
        o_ref[0, :, lo:hi] = (o / den).astype(o_ref.dtype)


def _na_pattern(i):
    return jnp.where(i < 2, i, jnp.where(i < NA_STEPS - 2, 2, i - (NA_STEPS - 5)))


def _na_attention(p3, bias):
    qoff = CTX // NA_NQ
    return pl.pallas_call(
        _na_kernel,
        grid=(BATCH, NA_STEPS),
        in_specs=[pl.BlockSpec((1, NA_NQ, D), lambda b, i: (b, i + qoff, PB_NQ)),
                  pl.BlockSpec((1, TT, D), lambda b, i: (b, 0, PB_NK)),
                  pl.BlockSpec((1, TT, D), lambda b, i: (b, 0, PB_NV)),
                  pl.BlockSpec((1, B_HEADS, NA_NQ, NA_NK), lambda b, i: (_na_pattern(i), 0, 0, 0))],
        out_specs=pl.BlockSpec((1, NA_NQ, D), lambda b, i: (b, i, 0)),
        out_shape=jax.ShapeDtypeStruct((BATCH, SEQ, D), BF16),
        compiler_params=_cp(("parallel", "arbitrary")),
        name="na_attention",
    )(p3, p3, p3, bias)


def _ctx_attn_kernel(q_ref, k_ref, v_ref, o_ref):
    for h in range(B_HEADS):
        lo, hi = h * B_DH, (h + 1) * B_DH
        qh = q_ref[0, :, lo:hi]
        s = lax.dot_general(qh, k_ref[0, :, lo:hi], (((1,), (1,)), ((), ())),
                            preferred_element_type=F32) * (B_DH ** -0.5)
        p = jnp.exp(s - jnp.max(s, axis=1, keepdims=True))
        den = jnp.sum(p, axis=1, keepdims=True)
        o = jnp.dot(p.astype(BF16), v_ref[0, :, lo:hi], preferred_element_type=F32)
        o_ref[0, :, lo:hi] = (o / den).astype(o_ref.dtype)


def _ctx_attention(p3):
    return pl.pallas_call(
        _ctx_attn_kernel,
        grid=(BATCH,),
        in_specs=[pl.BlockSpec((1, CTX, D), lambda b: (b, 0, PB_NQ)),
                  pl.BlockSpec((1, CTX, D), lambda b: (b, 0, PB_NK)),
                  pl.BlockSpec((1, CTX, D), lambda b: (b, 0, PB_NV))],
        out_specs=pl.BlockSpec((1, CTX, D), lambda b: (b, 0, 0)),
        out_shape=jax.ShapeDtypeStruct((BATCH, CTX, D), BF16),
        compiler_params=_cp(("parallel",)),
        name="ctx_attention",
    )(p3, p3, p3)


def _merge_kernel(hf_ref, hbw_ref, o_ref, ga_ref, gb_ref, hnl_ref, hnc_ref, x_ref, mod_ref, ghn_ref,
                  wa_ref, wb_ref, wo_ref, g2_ref, x1_ref, h2_ref):
    hn = jnp.where(pl.program_id(1) == 0, hnc_ref[0], hnl_ref[0])
    hs = hf_ref[0, 0].astype(F32) + hbw_ref[0, 0].astype(F32)
    parts = []
    for h in range(A_HEADS):
        seg = hs[:, h * A_DV:(h + 1) * A_DV]
        mu = jnp.mean(seg, axis=-1, keepdims=True)
        cen = seg - mu
        var = jnp.mean(cen * cen, axis=-1, keepdims=True)
        parts.append(cen * lax.rsqrt(var + EPS))
    ya = jnp.concatenate(parts, axis=1) * ghn_ref[...] * _sigmoid(o_ref[0].astype(F32))
    a = jnp.dot(ya.astype(BF16), wa_ref[...], preferred_element_type=F32)
    bm = jnp.dot(hn, wb_ref[...], preferred_element_type=F32)
    mrg = _sigmoid(ga_ref[0].astype(F32)) * a + _sigmoid(gb_ref[0].astype(F32)) * bm
    y = jnp.dot(mrg.astype(BF16), wo_ref[...], preferred_element_type=F32)
    mod = mod_ref[0]
    x1 = x_ref[0] + mod[2:3] * y
    x1_ref[0] = x1
    h2_ref[0] = _norm_mod(x1, g2_ref[...], mod[4:5], mod[3:4]).astype(h2_ref.dtype)


def _merge(hd, p3, hb_lat, hb_ctx, x, mod, ghn, wa, wb, wo, g2, h2_dtype):
    row = lambda b, i: (b, i, 0)
    full = lambda b, i: (0, 0)
    return pl.pallas_call(
        _merge_kernel,
        grid=(BATCH, NRB),
        in_specs=[pl.BlockSpec((1, 1, RB, A_V), lambda b, i: (0, b, i, 0)),
                  pl.BlockSpec((1, 1, RB, A_V), lambda b, i: (1, b, i, 0)),
                  pl.BlockSpec((1, RB, D), lambda b, i: (b, i, PB_O)),
                  pl.BlockSpec((1, RB, D), lambda b, i: (b, i, PB_GA)),
                  pl.BlockSpec((1, RB, D), lambda b, i: (b, i, PB_GB)),
                  pl.BlockSpec((1, RB, D), lambda b, i: (b, jnp.maximum(i - 1, 0), 0)),
                  pl.BlockSpec((1, CTX, D), lambda b, i: (b, 0, 0)),
                  pl.BlockSpec((1, RB, D), row),
                  pl.BlockSpec((1, 6, D), lambda b, i: (_mod_row(b, i), 0, 0)),
                  pl.BlockSpec((1, A_V), full),
                  pl.BlockSpec((A_V, D), full),
                  pl.BlockSpec((D, D), full),
                  pl.BlockSpec((D, D), full),
                  pl.BlockSpec((1, D), full)],
        out_specs=[pl.BlockSpec((1, RB, D), row), pl.BlockSpec((1, RB, D), row)],
        out_shape=[jax.ShapeDtypeStruct((BATCH, TT, D), F32),
                   jax.ShapeDtypeStruct((BATCH, TT, D), h2_dtype)],
        compiler_params=_cp(("parallel", "parallel")),
        name="merge",
    )(hd, hd, p3, p3, p3, hb_lat, hb_ctx, x, mod, ghn, wa, wb, wo, g2)


FFN_CH = D_FF // 2


def _ffn_kernel(h_ref, x_ref, mod_ref, w1_ref, w3_ref, w2_ref, gn_ref, modn_ref, x2_ref, hn_ref):
    h = h_ref[0]
    y = None
    for cidx in range(D_FF // FFN_CH):
        lo, hi = cidx * FFN_CH, (cidx + 1) * FFN_CH
        a = jnp.dot(h, w1_ref[:, lo:hi], preferred_element_type=F32)
        g = jnp.dot(h, w3_ref[:, lo:hi], preferred_element_type=F32)
        part = jnp.dot((_silu(a) * g).astype(BF16), w2_ref[lo:hi, :], preferred_element_type=F32)
        y = part if y is None else y + part
    mod = mod_ref[0]
    x2 = x_ref[0] + mod[5:6] * y
    x2_ref[0] = x2
    modn = modn_ref[0]
    hn_ref[0] = _norm_mod(x2, gn_ref[...], modn[1:2], modn[0:1]).astype(hn_ref.dtype)


def _dense_ffn(h2, x1, mod, w1, w3, w2, gn, modn):
    row = lambda b, i: (b, i, 0)
    full = lambda b, i: (0, 0)
    modspec = pl.BlockSpec((1, 6, D), lambda b, i: (_mod_row(b, i), 0, 0))
    return pl.pallas_call(
        _ffn_kernel,
        grid=(BATCH, NRB),
        in_specs=[pl.BlockSpec((1, RB, D), row), pl.BlockSpec((1, RB, D), row), modspec,
                  pl.BlockSpec((D, D_FF), full), pl.BlockSpec((D, D_FF), full),
                  pl.BlockSpec((D_FF, D), full), pl.BlockSpec((1, D), full), modspec],
        out_specs=[pl.BlockSpec((1, RB, D), row), pl.BlockSpec((1, RB, D), row)],
        out_shape=[jax.ShapeDtypeStruct((BATCH, TT, D), F32),
                   jax.ShapeDtypeStruct((BATCH, TT, D), BF16)],
        compiler_params=_cp(("parallel", "parallel")),
        name="dense_ffn",
    )(h2, x1, mod, w1, w3, w2, gn, modn)


def _router_kernel(h_ref, wr_ref, route_ref, cnt_ref, run_ref):
    i = pl.program_id(0)

    @pl.when(i == 0)
    def _():
        run_ref[...] = jnp.zeros_like(run_ref)

    logits = jnp.dot(h_ref[...].astype(BF16), wr_ref[...], preferred_element_type=F32)
    lane = lax.broadcasted_iota(jnp.int32, (RB, LANES), 1).astype(F32)
    lg = jnp.where(lane < N_EXPERTS, logits, -jnp.inf)
    v1 = jnp.max(lg, axis=1, keepdims=True)
    i1 = jnp.min(jnp.where(lg == v1, lane, float(LANES)), axis=1, keepdims=True)
    lg2 = jnp.where(lane == i1, -jnp.inf, lg)
    v2 = jnp.max(lg2, axis=1, keepdims=True)
    i2 = jnp.min(jnp.where(lg2 == v2, lane, float(LANES)), axis=1, keepdims=True)
    e = jnp.exp(v2 - v1)
    w1 = 1.0 / (1.0 + e)
    w2 = e / (1.0 + e)
    oh1 = (lane == i1).astype(F32)
    oh2 = (lane == i2).astype(F32)
    r = lax.broadcasted_iota(jnp.int32, (RB, RB), 0)
    c = lax.broadcasted_iota(jnp.int32, (RB, RB), 1)
    tri = (r > c).astype(BF16)
    cs1 = jnp.dot(tri, oh1.astype(BF16), preferred_element_type=F32)
    cs2 = jnp.dot(tri, oh2.astype(BF16), preferred_element_type=F32)
    tot1 = jnp.sum(oh1, axis=0, keepdims=True)
    tot2 = jnp.sum(oh2, axis=0, keepdims=True)
    run = run_ref[...]
    rank1 = jnp.sum(oh1 * (run + cs1), axis=1, keepdims=True)
    rank2 = jnp.sum(oh2 * (run + tot1 + cs2), axis=1, keepdims=True)
    new_run = run + tot1 + tot2
    run_ref[...] = new_run
    cnt_ref[...] = new_run
    out = jnp.where(lane == 0, i1,
          jnp.where(lane == 1, i2,
          jnp.where(lane == 2, w1,
          jnp.where(lane == 3, w2,
          jnp.where(lane == 4, rank1,
          jnp.where(lane == 5, rank2, 0.0))))))
    route_ref[...] = out


def _router(h2_flat, wr, nblk, blk_of):
    return pl.pallas_call(
        _router_kernel,
        grid=(nblk,),
        in_specs=[pl.BlockSpec((RB, D), lambda i: (blk_of(i), 0)),
                  pl.BlockSpec((D, LANES), lambda i: (0, 0))],
        out_specs=[pl.BlockSpec((RB, LANES), lambda i: (i, 0)),
                   pl.BlockSpec((1, LANES), lambda i: (0, 0))],
        out_shape=[jax.ShapeDtypeStruct((nblk * RB, LANES), F32),
                   jax.ShapeDtypeStruct((1, LANES), F32)],
        scratch_shapes=[pltpu.VMEM((1, LANES), F32)],
        compiler_params=_cp(("arbitrary",)),
        name="moe_router",
    )(h2_flat, wr)


def _dispatch_kernel(pos_ref, h_ref, init_ref, o_ref, sem, *, blk_of):
    del init_ref
    i = pl.program_id(0)
    base = blk_of(i) * RB

    def issue(t, carry):
        src = h_ref.at[pl.ds(base + t, 1)]
        pltpu.make_async_copy(src, o_ref.at[pl.ds(pos_ref[0, 0, t], 1)], sem).start()
        pltpu.make_async_copy(src, o_ref.at[pl.ds(pos_ref[0, 0, RB + t], 1)], sem).start()
        return carry

    lax.fori_loop(0, RB, issue, 0)
    pltpu.make_async_copy(h_ref.at[pl.ds(0, 2 * RB)], o_ref.at[pl.ds(0, 2 * RB)], sem).wait()


def _dispatch(pos, h2_flat, nblk, blk_of, n_sorted):
    init = jnp.zeros((n_sorted, D), F32)
    return pl.pallas_call(
        functools.partial(_dispatch_kernel, blk_of=blk_of),
        grid=(nblk,),
        in_specs=[pl.BlockSpec((1, 1, 2 * RB), lambda i: (i, 0, 0), memory_space=pltpu.SMEM),
                  pl.BlockSpec(memory_space=pl.ANY),
                  pl.BlockSpec(memory_space=pl.ANY)],
        out_specs=pl.BlockSpec(memory_space=pl.ANY),
        out_shape=jax.ShapeDtypeStruct((n_sorted, D), F32),
        scratch_shapes=[pltpu.SemaphoreType.DMA(())],
        input_output_aliases={2: 0},
        compiler_params=_cp(("arbitrary",)),
        name="moe_dispatch",
    )(pos, h2_flat, init)


def _gmm_kernel(be_ref, nb_ref, a_ref, w1_ref, w3_ref, w2_ref, o_ref):
    del be_ref
    i = pl.program_id(0)
    f = pl.program_id(1)

    @pl.when(i < nb_ref[0])
    def _():
        a = a_ref[...].astype(BF16)
        h1 = jnp.dot(a, w1_ref[0], preferred_element_type=F32)
        h3 = jnp.dot(a, w3_ref[0], preferred_element_type=F32)
        part = jnp.dot((_silu(h1) * h3).astype(BF16), w2_ref[0], preferred_element_type=F32)

        @pl.when(f == 0)
        def _():
            o_ref[...] = part

        @pl.when(f > 0)
        def _():
            o_ref[...] += part

    @pl.when((i >= nb_ref[0]) & (f == 0))
    def _():
        o_ref[...] = jnp.zeros_like(o_ref)


def _expert_ffn(blk_e, nb, hs, w1, w3, w2):
    nbmax = hs.shape[0] // MOE_TM
    nf = D_FF_EXPERT // MOE_FC

    def ieff(i, nb_ref):
        return jnp.minimum(i, nb_ref[0] - 1)

    def feff(i, f, nb_ref):
        return jnp.where(i < nb_ref[0], f, nf - 1)

    grid_spec = pltpu.PrefetchScalarGridSpec(
        num_scalar_prefetch=2,
        grid=(nbmax, nf),
        in_specs=[pl.BlockSpec((MOE_TM, D), lambda i, f, be, nbr: (ieff(i, nbr), 0)),
                  pl.BlockSpec((1, D, MOE_FC), lambda i, f, be, nbr: (be[ieff(i, nbr)], 0, feff(i, f, nbr))),
                  pl.BlockSpec((1, D, MOE_FC), lambda i, f, be, nbr: (be[ieff(i, nbr)], 0, feff(i, f, nbr))),
                  pl.BlockSpec((1, MOE_FC, D), lambda i, f, be, nbr: (be[ieff(i, nbr)], feff(i, f, nbr), 0))],
        out_specs=pl.BlockSpec((MOE_TM, D), lambda i, f, be, nbr: (i, 0)),
    )
    return pl.pallas_call(
        _gmm_kernel,
        grid_spec=grid_spec,
        out_shape=jax.ShapeDtypeStruct(hs.shape, F32),
        compiler_params=_cp(("arbitrary", "arbitrary")),
        name="moe_expert_ffn",
    )(blk_e, nb, hs, w1, w3, w2)


def _combine_kernel(pos_ref, y_ref, route_ref, x_ref, mod_ref, gn_ref, modn_ref,
                    x2_ref, hn_ref, ybuf, sem):
    def issue(t, carry):
        pltpu.make_async_copy(y_ref.at[pl.ds(pos_ref[0, 0, t], 1)], ybuf.at[0, pl.ds(t, 1)], sem).start()
        pltpu.make_async_copy(y_ref.at[pl.ds(pos_ref[0, 0, RB + t], 1)], ybuf.at[1, pl.ds(t, 1)], sem).start()
        return carry

    lax.fori_loop(0, RB, issue, 0)
    pltpu.make_async_copy(y_ref.at[pl.ds(0, RB)], ybuf.at[0], sem).wait()
    pltpu.make_async_copy(y_ref.at[pl.ds(0, RB)], ybuf.at[1], sem).wait()
    route = route_ref[...]
    y = route[:, 2:3] * ybuf[0] + route[:, 3:4] * ybuf[1]
    mod = mod_ref[0]
    x2 = x_ref[0] + mod[5:6] * y
    x2_ref[0] = x2
    modn = modn_ref[0]
    hn_ref[0] = _norm_mod(x2, gn_ref[...], modn[1:2], modn[0:1]).astype(hn_ref.dtype)


def _combine(pos, ys, route, x1, mod, gn, modn, latent_only, hn_dtype):
    nrb = NRB - 1 if latent_only else NRB
    off = 1 if latent_only else 0
    rows = nrb * RB
    full = lambda b, i: (0, 0)
    modspec = pl.BlockSpec((1, 6, D), lambda b, i: (_mod_row(b, i + off), 0, 0))
    return pl.pallas_call(
        _combine_kernel,
        grid=(BATCH, nrb),
        in_specs=[pl.BlockSpec((1, 1, 2 * RB), lambda b, i: (b * nrb + i, 0, 0), memory_space=pltpu.SMEM),
                  pl.BlockSpec(memory_space=pl.ANY),
                  pl.BlockSpec((RB, LANES), lambda b, i: (b * nrb + i, 0)),
                  pl.BlockSpec((1, RB, D), lambda b, i: (b, i + off, 0)),
                  modspec,
                  pl.BlockSpec((1, D), full),
                  modspec],
        out_specs=[pl.BlockSpec((1, RB, D), lambda b, i: (b, i, 0)),
                   pl.BlockSpec((1, RB, D), lambda b, i: (b, i, 0))],
        out_shape=[jax.ShapeDtypeStruct((BATCH, rows, D), F32),
                   jax.ShapeDtypeStruct((BATCH, rows, D), hn_dtype)],
        scratch_shapes=[pltpu.VMEM((2, RB, D), F32), pltpu.SemaphoreType.DMA(())],
        compiler_params=_cp(("arbitrary", "arbitrary")),
        name="moe_combine",
    )(pos, ys, route, x1, mod, gn, modn)


def _moe_ffn(h2, x1, mod, wr, w1, w3, w2, gn, modn, latent_only, hn_dtype):
    h2_flat = h2.reshape(NTOK, D)
    if latent_only:
        nrb = NRB - 1
        blk_of = lambda i: (i // nrb) * NRB + (i % nrb) + 1
    else:
        nrb = NRB
        blk_of = lambda i: i
    nblk = BATCH * nrb
    n_pairs = 2 * nblk * RB
    nbmax = n_pairs // MOE_TM + N_EXPERTS
    n_sorted = nbmax * MOE_TM

    route, counts = _router(h2_flat, wr, nblk, blk_of)
    cnt = counts[0, :N_EXPERTS].astype(jnp.int32)
    gsz = ((cnt + MOE_TM - 1) // MOE_TM) * MOE_TM
    gend = jnp.cumsum(gsz)
    goff = gend - gsz
    e12 = route[:, 0:2].astype(jnp.int32)
    rank = route[:, 4:6].astype(jnp.int32)
    pos = goff[e12] + rank
    pos = pos.reshape(nblk, RB, 2).transpose(0, 2, 1).reshape(nblk, 1, 2 * RB)
    nb = (gend[-1] // MOE_TM).reshape(1)
    blk_start = jnp.arange(nbmax, dtype=jnp.int32) * MOE_TM
    blk_e = jnp.minimum(jnp.sum(blk_start[:, None] >= gend[None, :], axis=1), N_EXPERTS - 1).astype(jnp.int32)

    hs = _dispatch(pos, h2_flat, nblk, blk_of, n_sorted)
    ys = _expert_ffn(blk_e, nb, hs, w1, w3, w2)
    return _combine(pos, ys, route, x1, mod, gn, modn, latent_only, hn_dtype)


def _rope_tables():
    quarter = A_DQK // 4
    inv = 1.0 / (ROPE_BASE ** (jnp.arange(quarter, dtype=F32) / quarter))
    pos = jnp.arange(SEQ)
    rows = (pos // GRID_W).astype(F32)
    cols = (pos % GRID_W).astype(F32)
    ang_r = rows[:, None] * inv[None, :]
    ang_c = cols[:, None] * inv[None, :]
    cos = jnp.concatenate([jnp.cos(ang_r)] * 2 + [jnp.cos(ang_c)] * 2, axis=1)
    sin = jnp.concatenate([-jnp.sin(ang_r), jnp.sin(ang_r), -jnp.sin(ang_c), jnp.sin(ang_c)], axis=1)
    cos = jnp.concatenate([jnp.ones((CTX, A_DQK), F32), cos], axis=0)
    sin = jnp.concatenate([jnp.zeros((CTX, A_DQK), F32), sin], axis=0)
    kscale = A_DQK ** -0.5
    return (jnp.concatenate([cos, cos * kscale], axis=1),
            jnp.concatenate([sin, sin * kscale], axis=1))


def _na_bias_tables(rpb):
    rs = np.clip(np.arange(GRID_H) - NA_ROWS // 2, 0, GRID_H - NA_ROWS)
    cs = np.clip(np.arange(GRID_W) - NA_COLS // 2, 0, GRID_W - NA_COLS)
    steps = [0, 1, 2, NA_STEPS - 2, NA_STEPS - 1]
    kc = np.arange(GRID_W)
    qc = np.arange(GRID_W)
    dc = np.clip(kc[None, :] - qc[:, None] + NA_COLS - 1, 0, 2 * NA_COLS - 2)
    valid_c = (kc[None, :] >= cs[qc][:, None]) & (kc[None, :] < cs[qc][:, None] + NA_COLS)
    out = []
    for i in steps:
        kr0 = min(rs[NA_QROWS * i], GRID_H - NA_KROWS)
        qr = NA_QROWS * i + np.arange(NA_QROWS)
        kr = kr0 + np.arange(NA_KROWS)
        dr = np.clip(kr[None, :] - qr[:, None] + NA_ROWS - 1, 0, 2 * NA_ROWS - 2)
        valid_r = (kr[None, :] >= rs[qr][:, None]) & (kr[None, :] < rs[qr][:, None] + NA_ROWS)
        t = rpb[:, dr.reshape(-1), :][:, :, dc.reshape(-1)]
        t = t.reshape(B_HEADS, NA_QROWS, NA_KROWS, GRID_W, GRID_W).transpose(0, 1, 3, 2, 4)
        valid = (valid_r[:, None, :, None] & valid_c[None, :, None, :])
        t = jnp.where(jnp.asarray(valid)[None], t, -jnp.inf)
        out.append(t.reshape(B_HEADS, NA_NQ, NA_NK))
    return jnp.stack(out).astype(BF16)


def kernel(x, c, ctx, c_ctx, w_mod, b_mod, g_norm1, g_norm2, w_in, a_conv, a_gate_b, a_hnorm_g, na_rpb,
           w_br_a, w_br_b, w_out, ffn_w1, ffn_w3, ffn_w2, moe_router, moe_w1, moe_w3, moe_w2, g_final):
    cc = jnp.concatenate([c, c_ctx[None, :], jnp.zeros((16 - BATCH - 1, D), F32)], axis=0)
    mod_all = _modulation(cc, w_mod, b_mod).reshape(DEPTH, 16, 6, D)[:, :BATCH + 1]
    mod_zero = jnp.zeros((BATCH + 1, 6, D), F32)
    rope_c, rope_s = _rope_tables()

    xs = jnp.concatenate([ctx, x], axis=1)
    h1 = _first_norm(xs, g_norm1[0][None, :], mod_all[0])
    out = None
    for l in range(DEPTH):
        last = l == DEPTH - 1
        mod = mod_all[l]
        wl = w_in[l]
        wp = jnp.concatenate([wl[:, :3 * D], wl[:, 3 * D + 4 * A_HEADS:]], axis=1).astype(BF16)
        wg = jnp.pad(wl[:, 3 * D:3 * D + 4 * A_HEADS], ((0, 0), (0, LANES - 4 * A_HEADS))).astype(BF16)
        bg = jnp.pad(a_gate_b[l], (0, LANES - 4 * A_HEADS))[None, :]

        h1_flat = h1.reshape(NTOK, D)
        p3 = _in_proj(h1_flat, wp).reshape(BATCH, TT, P_COLS)
        gates = _gate_proj(h1_flat, wg, bg)[:, :4 * A_HEADS]
        gc = gates.reshape(BATCH, NCH, LCH, 2, 2 * A_HEADS).transpose(0, 3, 1, 2, 4)
        gr = gc.transpose(0, 1, 2, 4, 3)

        qk = _mlstm_prep(p3, a_conv[l], rope_c, rope_s)
        hd = _mlstm_scan(qk, p3, gr, gc)
        hb_lat = _na_attention(p3, _na_bias_tables(na_rpb[l]))
        hb_ctx = _ctx_attention(p3)

        moe = l % 2 == 1
        x1, h2 = _merge(hd, p3, hb_lat, hb_ctx, xs, mod, a_hnorm_g[l][None, :],
                        w_br_a[l].astype(BF16), w_br_b[l].astype(BF16), w_out[l].astype(BF16),
                        g_norm2[l][None, :], F32 if moe else BF16)
        if last:
            gn, modn = g_final[None, :], mod_zero
        else:
            gn, modn = g_norm1[l + 1][None, :], mod_all[l + 1]
        j = l // 2
        if not moe:
            xs, h1 = _dense_ffn(h2, x1, mod, ffn_w1[j].astype(BF16), ffn_w3[j].astype(BF16),
                                ffn_w2[j].astype(BF16), gn, modn)
        else:
            wr = jnp.pad(moe_router[j], ((0, 0), (0, LANES - N_EXPERTS))).astype(BF16)
            xs, h1 = _moe_ffn(h2, x1, mod, wr, moe_w1[j].astype(BF16), moe_w3[j].astype(BF16),
                              moe_w2[j].astype(BF16), gn, modn, last, F32 if last else BF16)
            if last:
                out = h1
    return out
```

```python
import functools

import numpy as np
import jax
import jax.numpy as jnp
from jax import lax
from jax.experimental import pallas as pl
from jax.experimental.pallas import tpu as pltpu

F32 = jnp.float32
BF16 = jnp.bfloat16

D = 1024
BATCH = 8
SEQ = 2048
CTX = 256
TT = CTX + SEQ
NTOK = BATCH * TT
DEPTH = 4
GRID_W = 64
GRID_H = SEQ // GRID_W

A_HEADS = 4
A_DQK = 128
A_DV = 256
A_QK = A_HEADS * A_DQK
A_V = A_HEADS * A_DV
ROPE_BASE = 10000.0
LCH = 256
NCH = TT // LCH

B_HEADS = 16
B_DH = 64
NA_ROWS = 8
NA_COLS = 16
NA_QROWS = 1
NA_KROWS = NA_QROWS + NA_ROWS - 1
NA_NQ = NA_QROWS * GRID_W
NA_NK = NA_KROWS * GRID_W
NA_STEPS = GRID_H // NA_QROWS

D_FF = 2816
N_EXPERTS = 8
D_FF_EXPERT = 3584
EPS = 1e-6

RB = 256
NRB = TT // RB
MM_TM = 1024
MOE_TM = 512
MOE_FC = 512
LANES = 128

PB_QK, PB_V, PB_O, PB_NQ, PB_NK, PB_NV, PB_GA, PB_GB = range(8)
P_COLS = 8 * D

VMEM_LIMIT = 56 * 1024 * 1024


def _cp(sem, vmem=VMEM_LIMIT):
    return pltpu.CompilerParams(dimension_semantics=sem, vmem_limit_bytes=vmem)


def _sigmoid(x):
    return 1.0 / (1.0 + jnp.exp(-x))


def _silu(x):
    return x * _sigmoid(x)


def _log_sigmoid(x):
    return jnp.minimum(x, 0.0) - jnp.log(1.0 + jnp.exp(-jnp.abs(x)))


def _norm_mod(x, g, sc, sh):
    ms = jnp.mean(x * x, axis=-1, keepdims=True)
    y = x * lax.rsqrt(ms + EPS)
    return (y * g) * (1.0 + sc) + sh


def _mod_row(b, i):
    return jnp.where(i == 0, BATCH, b)


def _mod_kernel(c_ref, w_ref, b_ref, o_ref):
    c = c_ref[...]
    s = _silu(c).astype(BF16)
    o_ref[0] = jnp.dot(s, w_ref[0].astype(BF16), preferred_element_type=F32) + b_ref[0]


def _modulation(cc, w_mod, b_mod):
    tn = 2048
    nl = w_mod.shape[0]
    return pl.pallas_call(
        _mod_kernel,
        grid=(nl, 6 * D // tn),
        in_specs=[pl.BlockSpec((16, D), lambda l, j: (0, 0)),
                  pl.BlockSpec((1, D, tn), lambda l, j: (l, 0, j)),
                  pl.BlockSpec((1, 1, tn), lambda l, j: (l, 0, j))],
        out_specs=pl.BlockSpec((1, 16, tn), lambda l, j: (l, 0, j)),
        out_shape=jax.ShapeDtypeStruct((nl, 16, 6 * D), F32),
        compiler_params=_cp(("parallel", "parallel")),
        name="modulation",
    )(cc, w_mod, b_mod.reshape(nl, 1, 6 * D))


def _norm_kernel(x_ref, g_ref, mod_ref, o_ref):
    mod = mod_ref[0]
    o_ref[0] = _norm_mod(x_ref[0], g_ref[...], mod[1:2], mod[0:1]).astype(o_ref.dtype)


def _first_norm(x, g, mod):
    return pl.pallas_call(
        _norm_kernel,
        grid=(BATCH, NRB),
        in_specs=[pl.BlockSpec((1, RB, D), lambda b, i: (b, i, 0)),
                  pl.BlockSpec((1, D), lambda b, i: (0, 0)),
                  pl.BlockSpec((1, 6, D), lambda b, i: (_mod_row(b, i), 0, 0))],
        out_specs=pl.BlockSpec((1, RB, D), lambda b, i: (b, i, 0)),
        out_shape=jax.ShapeDtypeStruct((BATCH, TT, D), BF16),
        compiler_params=_cp(("parallel", "parallel")),
        name="first_norm",
    )(x, g, mod)


def _mm_kernel(a_ref, w_ref, o_ref):
    o_ref[...] = jnp.dot(a_ref[...], w_ref[...], preferred_element_type=F32).astype(o_ref.dtype)


def _in_proj(h, w):
    tn = 1024
    return pl.pallas_call(
        _mm_kernel,
        grid=(P_COLS // tn, NTOK // MM_TM),
        in_specs=[pl.BlockSpec((MM_TM, D), lambda j, i: (i, 0)),
                  pl.BlockSpec((D, tn), lambda j, i: (0, j))],
        out_specs=pl.BlockSpec((MM_TM, tn), lambda j, i: (i, j)),
        out_shape=jax.ShapeDtypeStruct((NTOK, P_COLS), BF16),
        compiler_params=_cp(("parallel", "parallel")),
        name="in_proj",
    )(h, w)


NGATE = 4 * A_HEADS


def _gate_kernel(a_ref, w_ref, b_ref, g_ref, lf_ref):
    g = jnp.dot(a_ref[...], w_ref[...], preferred_element_type=F32) + b_ref[...]
    g_ref[...] = g
    lf = _log_sigmoid(g)
    p0 = lf.astype(BF16).astype(F32)
    r1 = lf - p0
    p1 = r1.astype(BF16).astype(F32)
    p2 = (r1 - p1).astype(BF16).astype(F32)
    lane = lax.broadcasted_iota(jnp.int32, (1, LANES), 1)
    parts = jnp.where(lane < NGATE, p0,
                      jnp.where(lane < 2 * NGATE, pltpu.roll(p1, NGATE, 1),
                                jnp.where(lane < 3 * NGATE, pltpu.roll(p2, 2 * NGATE, 1), 0.0)))
    lf_ref[...] = parts.astype(BF16)


def _gate_proj(h, wg, bg):
    return pl.pallas_call(
        _gate_kernel,
        grid=(NTOK // MM_TM,),
        in_specs=[pl.BlockSpec((MM_TM, D), lambda i: (i, 0)),
                  pl.BlockSpec((D, LANES), lambda i: (0, 0)),
                  pl.BlockSpec((1, LANES), lambda i: (0, 0))],
        out_specs=[pl.BlockSpec((MM_TM, LANES), lambda i: (i, 0)),
                   pl.BlockSpec((MM_TM, LANES), lambda i: (i, 0))],
        out_shape=[jax.ShapeDtypeStruct((NTOK, LANES), F32),
                   jax.ShapeDtypeStruct((NTOK, LANES), BF16)],
        compiler_params=_cp(("parallel",)),
        name="gate_proj",
    )(h, wg, bg)


def _prep_kernel(u_ref, up_ref, un_ref, w_ref, c_ref, s_ref, o_ref):
    i = pl.program_id(1)
    u = u_ref[0].astype(F32)
    prev_row = jnp.where(i >= 2, up_ref[0, 15:16, :].astype(F32), 0.0)
    next_row = jnp.where((i >= 1) & (i <= NRB - 2), un_ref[0, 0:1, :].astype(F32), 0.0)
    rid = lax.broadcasted_iota(jnp.int32, (RB, 1), 0)
    u_m1 = jnp.where(rid == 0, prev_row, pltpu.roll(u, 1, 0))
    u_p1 = jnp.where(rid == RB - 1, next_row, pltpu.roll(u, RB - 1, 0))
    w = w_ref[...]
    y = w[0:1] * u_m1 + w[1:2] * u + w[2:3] * u_p1
    y = _silu(y)
    c = c_ref[...]
    s = s_ref[...]
    cfull = jnp.concatenate([c[:, :A_DQK]] * A_HEADS + [c[:, A_DQK:]] * A_HEADS, axis=1)
    sfull = jnp.concatenate([s[:, :A_DQK]] * A_HEADS + [s[:, A_DQK:]] * A_HEADS, axis=1)
    lane = lax.broadcasted_iota(jnp.int32, (1, 2 * A_QK), 1)
    partner = jnp.where((lane & 32) == 0,
                        pltpu.roll(y, 2 * A_QK - 32, 1), pltpu.roll(y, 32, 1))
    o_ref[0] = (y * cfull + partner * sfull).astype(o_ref.dtype)


def _mlstm_prep(p3, conv_w, rope_c, rope_s):
    nb16 = TT // 16
    return pl.pallas_call(
        _prep_kernel,
        grid=(BATCH, NRB),
        in_specs=[pl.BlockSpec((1, RB, D), lambda b, i: (b, i, PB_QK)),
                  pl.BlockSpec((1, 16, D), lambda b, i: (b, jnp.maximum(i * (RB // 16) - 1, 0), PB_QK)),
                  pl.BlockSpec((1, 16, D), lambda b, i: (b, jnp.minimum((i + 1) * (RB // 16), nb16 - 1), PB_QK)),
                  pl.BlockSpec((3, D), lambda b, i: (0, 0)),
                  pl.BlockSpec((RB, 2 * A_DQK), lambda b, i: (i, 0)),
                  pl.BlockSpec((RB, 2 * A_DQK), lambda b, i: (i, 0))],
        out_specs=pl.BlockSpec((1, RB, D), lambda b, i: (b, i, 0)),
        out_shape=jax.ShapeDtypeStruct((BATCH, TT, D), BF16),
        compiler_params=_cp(("parallel", "parallel")),
        name="mlstm_prep",
    )(p3, p3, p3, conv_w, rope_c, rope_s)


def _mlstm_kernel(qkf_ref, vf_ref, gf_ref, lff_ref, qkb_ref, vb_ref, gb_ref, lfb_ref,
                  of_ref, ob_ref, ct_ref, n_ref, m_ref):
    @pl.when(pl.program_id(1) == 0)
    def _():
        ct_ref[...] = jnp.zeros_like(ct_ref)
        n_ref[...] = jnp.zeros_like(n_ref)
        m_ref[...] = jnp.zeros_like(m_ref)

    r = lax.broadcasted_iota(jnp.int32, (LCH, LCH), 0)
    c = lax.broadcasted_iota(jnp.int32, (LCH, LCH), 1)
    masks = [c <= r, c >= r]
    ins = [(qkf_ref, vf_ref, gf_ref, lff_ref, of_ref), (qkb_ref, vb_ref, gb_ref, lfb_ref, ob_ref)]
    b_all, g_all, b_t, g_t, b_end = [], [], [], [], []
    for d in range(2):
        tri = jnp.where(masks[d], 1.0, 0.0).astype(BF16)
        bc = jnp.dot(tri, ins[d][3][0], preferred_element_type=F32)
        ba = bc + pltpu.roll(bc, LANES - NGATE, 1) + pltpu.roll(bc, LANES - 2 * NGATE, 1)
        ga = ins[d][2][0]
        if d == 1:
            ba = pltpu.roll(ba, LANES - 2 * A_HEADS, 1)
            ga = pltpu.roll(ga, LANES - 2 * A_HEADS, 1)
        b_all.append(ba)
        g_all.append(ga)
        b_t.append(ba.T)
        g_t.append(ga.T)
        b_end.append(ba[LCH - 1:LCH, :] if d == 0 else ba[0:1, :])
    ones = jnp.ones((LCH, LANES), BF16)
    tn_dims = (((0,), (0,)), ((), ()))
    combos = [(d, h) for d in range(2) for h in range(A_HEADS)]
    idx = range(len(combos))
    qs = [ins[d][0][0, :, h * A_DQK:(h + 1) * A_DQK] for d, h in combos]
    ks = [ins[d][0][0, :, A_QK + h * A_DQK:A_QK + (h + 1) * A_DQK] for d, h in combos]
    qk = [lax.dot_general(qs[i], ks[i], NT_DIMS, preferred_element_type=F32) for i in idx]
    cq = [jnp.dot(qs[i], ct_ref[i].astype(BF16), preferred_element_type=F32) for i in idx]
    nq = [jnp.dot(qs[i], n_ref[i].astype(BF16), preferred_element_type=F32)[:, 0:1] for i in idx]
    b_col = [b_all[d][:, A_HEADS + h:A_HEADS + h + 1] for d, h in combos]
    b_last = [b_end[d][:, A_HEADS + h:A_HEADS + h + 1] for d, h in combos]
    m_old = [m_ref[i][:, 0:1] for i in idx]
    dmat = [jnp.where(masks[d], b_col[i] - (b_t[d][A_HEADS + h:A_HEADS + h + 1, :] - g_t[d][h:h + 1, :]),
                      -jnp.inf) for i, (d, h) in enumerate(combos)]
    m_row = [jnp.maximum(b_col[i] + m_old[i], jnp.max(dmat[i], axis=1, keepdims=True)) for i in idx]
    s = [qk[i] * jnp.exp(dmat[i] - m_row[i]) for i in idx]
    w_int = [jnp.exp(b_col[i] + m_old[i] - m_row[i]) for i in idx]
    g_col = [b_last[i] - b_col[i] + g_all[d][:, h:h + 1] for i, (d, h) in enumerate(combos)]
    m_new = [jnp.maximum(b_last[i] + m_old[i], jnp.max(g_col[i], axis=0, keepdims=True)) for i in idx]
    kw = [(jnp.exp(g_col[i] - m_new[i]) * ks[i].astype(F32)).astype(BF16) for i in idx]
    decay = [jnp.exp(b_last[i] + m_old[i] - m_new[i]) for i in idx]
    for i, (d, h) in enumerate(combos):
        v = ins[d][1][0, :, h * A_DV:(h + 1) * A_DV]
        num = jnp.dot(s[i].astype(BF16), v, preferred_element_type=F32) + w_int[i] * cq[i]
        den = jnp.sum(s[i], axis=1, keepdims=True) + w_int[i] * nq[i]
        hout = num / jnp.maximum(jnp.abs(den), jnp.exp(-m_row[i]))
        ins[d][4][0, :, h * A_DV:(h + 1) * A_DV] = hout.astype(of_ref.dtype)
    for i, (d, h) in enumerate(combos):
        v = ins[d][1][0, :, h * A_DV:(h + 1) * A_DV]
        ct_ref[i] = decay[i] * ct_ref[i] + lax.dot_general(kw[i], v, tn_dims, preferred_element_type=F32)
        n_ref[i] = decay[i] * n_ref[i] + lax.dot_general(kw[i], ones, tn_dims, preferred_element_type=F32)
        m_ref[i] = jnp.broadcast_to(m_new[i], (1, LANES))


def _rev_chunk(s):
    return jnp.where(s == 0, 0, NCH - s)


def _mlstm_scan(qk, p3, gates, lfp):
    fwd = lambda b, s: (b, s, 0)
    bwd = lambda b, s: (b, _rev_chunk(s), 0)
    specs = lambda im, imv: [pl.BlockSpec((1, LCH, D), im), pl.BlockSpec((1, LCH, A_V), imv),
                             pl.BlockSpec((1, LCH, LANES), im), pl.BlockSpec((1, LCH, LANES), im)]
    nstate = 2 * A_HEADS
    return pl.pallas_call(
        _mlstm_kernel,
        grid=(BATCH, NCH),
        in_specs=(specs(fwd, lambda b, s: (b, s, PB_V))
                  + specs(bwd, lambda b, s: (b, _rev_chunk(s), PB_V))),
        out_specs=[pl.BlockSpec((1, LCH, A_V), fwd), pl.BlockSpec((1, LCH, A_V), bwd)],
        out_shape=[jax.ShapeDtypeStruct((BATCH, TT, A_V), BF16)] * 2,
        scratch_shapes=[pltpu.VMEM((nstate, A_DQK, A_DV), F32),
                        pltpu.VMEM((nstate, A_DQK, LANES), F32),
                        pltpu.VMEM((nstate, 1, LANES), F32)],
        compiler_params=_cp(("parallel", "arbitrary")),
        name="mlstm_scan",
    )(qk, p3, gates, lfp, qk, p3, gates, lfp)


NPAIR = B_HEADS // 2
NT_DIMS = (((1,), (1,)), ((), ()))


def _stack_pair(q, low):
    zero = jnp.zeros_like(q)
    return jnp.concatenate([jnp.where(low, q, zero), jnp.where(low, zero, q)], axis=0)


def _softmax_rows(s_ref, p_ref):
    s = s_ref[...]
    p = jnp.exp(s - jnp.max(s, axis=2, keepdims=True))
    p_ref[...] = p.astype(p_ref.dtype)
    return 1.0 / jnp.sum(p, axis=2, keepdims=True)


def _na_kernel(q_ref, k_ref, v_ref, bias_ref, o_ref, s_ref, p_ref):
    row = pl.program_id(1)
    rs = jnp.clip(row - NA_ROWS // 2, 0, GRID_H - NA_ROWS)
    start = pl.multiple_of(CTX + rs * GRID_W, GRID_W)
    low = lax.broadcasted_iota(jnp.int32, (1, 2 * B_DH), 1) < B_DH
    for hp in range(NPAIR):
        lo, hi = hp * 2 * B_DH, (hp + 1) * 2 * B_DH
        q2 = _stack_pair(q_ref[0, :, lo:hi], low)
        bias = jnp.concatenate([bias_ref[0, 0, 2 * hp], bias_ref[0, 0, 2 * hp + 1]], axis=0)
        s_ref[hp, :, 0:NA_NK] = lax.dot_general(q2, k_ref[0, pl.ds(start, NA_NK), lo:hi], NT_DIMS,
                                                preferred_element_type=F32) + bias.astype(F32)
        s_ref[hp, :, NA_NK:] = lax.dot_general(q2, k_ref[0, 0:CTX, lo:hi], NT_DIMS,
                                               preferred_element_type=F32)
    rinv = _softmax_rows(s_ref, p_ref)
    for hp in range(NPAIR):
        lo, hi = hp * 2 * B_DH, (hp + 1) * 2 * B_DH
        o2 = (jnp.dot(p_ref[hp, :, 0:NA_NK], v_ref[0, pl.ds(start, NA_NK), lo:hi],
                      preferred_element_type=F32)
              + jnp.dot(p_ref[hp, :, NA_NK:], v_ref[0, 0:CTX, lo:hi], preferred_element_type=F32))
        o2 = o2 * rinv[hp]
        o_ref[0, :, lo:hi] = jnp.where(low, o2[0:NA_NQ], o2[NA_NQ:]).astype(o_ref.dtype)


def _na_pattern(row):
    edge = NA_ROWS // 2
    return jnp.where(row < edge, row, jnp.where(row <= GRID_H - edge, edge, row - (GRID_H - 2 * edge)))


NA_NPAT = NA_ROWS


def _na_attention(p3, bias, layer):
    qoff = CTX // NA_NQ
    return pl.pallas_call(
        _na_kernel,
        grid=(BATCH, GRID_H),
        in_specs=[pl.BlockSpec((1, NA_NQ, D), lambda b, i: (b, i + qoff, PB_NQ)),
                  pl.BlockSpec((1, TT, D), lambda b, i: (b, 0, PB_NK)),
                  pl.BlockSpec((1, TT, D), lambda b, i: (b, 0, PB_NV)),
                  pl.BlockSpec((1, 1, B_HEADS, NA_NQ, NA_NK),
                               lambda b, i: (layer, _na_pattern(i), 0, 0, 0))],
        out_specs=pl.BlockSpec((1, NA_NQ, D), lambda b, i: (b, i, 0)),
        out_shape=jax.ShapeDtypeStruct((BATCH, SEQ, D), BF16),
        scratch_shapes=[pltpu.VMEM((NPAIR, 2 * NA_NQ, NA_NK + CTX), F32),
                        pltpu.VMEM((NPAIR, 2 * NA_NQ, NA_NK + CTX), BF16)],
        compiler_params=_cp(("parallel", "arbitrary")),
        name="na_attention",
    )(p3, p3, p3, bias)


def _ctx_attn_kernel(q_ref, k_ref, v_ref, o_ref, s_ref, p_ref):
    low = lax.broadcasted_iota(jnp.int32, (1, 2 * B_DH), 1) < B_DH
    for hp in range(NPAIR):
        lo, hi = hp * 2 * B_DH, (hp + 1) * 2 * B_DH
        s_ref[hp] = lax.dot_general(_stack_pair(q_ref[0, :, lo:hi], low), k_ref[0, :, lo:hi], NT_DIMS,
                                    preferred_element_type=F32)
    rinv = _softmax_rows(s_ref, p_ref)
    for hp in range(NPAIR):
        lo, hi = hp * 2 * B_DH, (hp + 1) * 2 * B_DH
        o2 = jnp.dot(p_ref[hp], v_ref[0, :, lo:hi], preferred_element_type=F32) * rinv[hp]
        o_ref[0, :, lo:hi] = jnp.where(low, o2[0:CTX], o2[CTX:]).astype(o_ref.dtype)


def _ctx_attention(p3):
    return pl.pallas_call(
        _ctx_attn_kernel,
        grid=(BATCH,),
        in_specs=[pl.BlockSpec((1, CTX, D), lambda b: (b, 0, PB_NQ)),
                  pl.BlockSpec((1, CTX, D), lambda b: (b, 0, PB_NK)),
                  pl.BlockSpec((1, CTX, D), lambda b: (b, 0, PB_NV))],
        out_specs=pl.BlockSpec((1, CTX, D), lambda b: (b, 0, 0)),
        out_shape=jax.ShapeDtypeStruct((BATCH, CTX, D), BF16),
        scratch_shapes=[pltpu.VMEM((NPAIR, 2 * CTX, CTX), F32),
                        pltpu.VMEM((NPAIR, 2 * CTX, CTX), BF16)],
        compiler_params=_cp(("parallel",)),
        name="ctx_attention",
    )(p3, p3, p3)


def _merge_kernel(hf_ref, hbw_ref, o_ref, ga_ref, gb_ref, hnl_ref, hnc_ref, x_ref, mod_ref, ghn_ref,
                  wa_ref, wb_ref, wo_ref, g2_ref, x1_ref, h2_ref):
    hn = jnp.where(pl.program_id(1) == 0, hnc_ref[0], hnl_ref[0])
    hs = hf_ref[0].astype(F32) + hbw_ref[0].astype(F32)
    parts = []
    for h in range(A_HEADS):
        seg = hs[:, h * A_DV:(h + 1) * A_DV]
        mu = jnp.mean(seg, axis=-1, keepdims=True)
        cen = seg - mu
        var = jnp.mean(cen * cen, axis=-1, keepdims=True)
        parts.append(cen * lax.rsqrt(var + EPS))
    ya = jnp.concatenate(parts, axis=1) * ghn_ref[...] * _sigmoid(o_ref[0].astype(F32))
    a = jnp.dot(ya.astype(BF16), wa_ref[...], preferred_element_type=F32)
    bm = jnp.dot(hn, wb_ref[...], preferred_element_type=F32)
    mrg = _sigmoid(ga_ref[0].astype(F32)) * a + _sigmoid(gb_ref[0].astype(F32)) * bm
    y = jnp.dot(mrg.astype(BF16), wo_ref[...], preferred_element_type=F32)
    mod = mod_ref[0]
    x1 = x_ref[0] + mod[2:3] * y
    x1_ref[0] = x1
    h2_ref[0] = _norm_mod(x1, g2_ref[...], mod[4:5], mod[3:4]).astype(h2_ref.dtype)


def _merge(hf, hbw, p3, hb_lat, hb_ctx, x, mod, ghn, wa, wb, wo, g2, h2_dtype):
    row = lambda b, i: (b, i, 0)
    full = lambda b, i: (0, 0)
    return pl.pallas_call(
        _merge_kernel,
        grid=(BATCH, NRB),
        in_specs=[pl.BlockSpec((1, RB, A_V), row),
                  pl.BlockSpec((1, RB, A_V), row),
                  pl.BlockSpec((1, RB, D), lambda b, i: (b, i, PB_O)),
                  pl.BlockSpec((1, RB, D), lambda b, i: (b, i, PB_GA)),
                  pl.BlockSpec((1, RB, D), lambda b, i: (b, i, PB_GB)),
                  pl.BlockSpec((1, RB, D), lambda b, i: (b, jnp.maximum(i - 1, 0), 0)),
                  pl.BlockSpec((1, CTX, D), lambda b, i: (b, 0, 0)),
                  pl.BlockSpec((1, RB, D), row),
                  pl.BlockSpec((1, 6, D), lambda b, i: (_mod_row(b, i), 0, 0)),
                  pl.BlockSpec((1, A_V), full),
                  pl.BlockSpec((A_V, D), full),
                  pl.BlockSpec((D, D), full),
                  pl.BlockSpec((D, D), full),
                  pl.BlockSpec((1, D), full)],
        out_specs=[pl.BlockSpec((1, RB, D), row), pl.BlockSpec((1, RB, D), row)],
        out_shape=[jax.ShapeDtypeStruct((BATCH, TT, D), F32),
                   jax.ShapeDtypeStruct((BATCH, TT, D), h2_dtype)],
        compiler_params=_cp(("parallel", "parallel")),
        name="merge",
    )(hf, hbw, p3, p3, p3, hb_lat, hb_ctx, x, mod, ghn, wa, wb, wo, g2)


FFN_CH = D_FF // 2


def _ffn_kernel(h_ref, x_ref, mod_ref, w1_ref, w3_ref, w2_ref, gn_ref, modn_ref, x2_ref, hn_ref):
    h = h_ref[0]
    y = None
    for cidx in range(D_FF // FFN_CH):
        lo, hi = cidx * FFN_CH, (cidx + 1) * FFN_CH
        a = jnp.dot(h, w1_ref[:, lo:hi], preferred_element_type=F32)
        g = jnp.dot(h, w3_ref[:, lo:hi], preferred_element_type=F32)
        part = jnp.dot((_silu(a) * g).astype(BF16), w2_ref[lo:hi, :], preferred_element_type=F32)
        y = part if y is None else y + part
    mod = mod_ref[0]
    x2 = x_ref[0] + mod[5:6] * y
    x2_ref[0] = x2
    modn = modn_ref[0]
    hn_ref[0] = _norm_mod(x2, gn_ref[...], modn[1:2], modn[0:1]).astype(hn_ref.dtype)


def _dense_ffn(h2, x1, mod, w1, w3, w2, gn, modn):
    row = lambda b, i: (b, i, 0)
    full = lambda b, i: (0, 0)
    modspec = pl.BlockSpec((1, 6, D), lambda b, i: (_mod_row(b, i), 0, 0))
    return pl.pallas_call(
        _ffn_kernel,
        grid=(BATCH, NRB),
        in_specs=[pl.BlockSpec((1, RB, D), row), pl.BlockSpec((1, RB, D), row), modspec,
                  pl.BlockSpec((D, D_FF), full), pl.BlockSpec((D, D_FF), full),
                  pl.BlockSpec((D_FF, D), full), pl.BlockSpec((1, D), full), modspec],
        out_specs=[pl.BlockSpec((1, RB, D), row), pl.BlockSpec((1, RB, D), row)],
        out_shape=[jax.ShapeDtypeStruct((BATCH, TT, D), F32),
                   jax.ShapeDtypeStruct((BATCH, TT, D), BF16)],
        compiler_params=_cp(("parallel", "parallel")),
        name="dense_ffn",
    )(h2, x1, mod, w1, w3, w2, gn, modn)


def _router_kernel(h_ref, wr_ref, route_ref, cnt_ref, run_ref):
    i = pl.program_id(0)

    @pl.when(i == 0)
    def _():
        run_ref[...] = jnp.zeros_like(run_ref)

    logits = jnp.dot(h_ref[...].astype(BF16), wr_ref[...], preferred_element_type=F32)
    lane = lax.broadcasted_iota(jnp.int32, (RB, LANES), 1).astype(F32)
    lg = jnp.where(lane < N_EXPERTS, logits, -jnp.inf)
    v1 = jnp.max(lg, axis=1, keepdims=True)
    i1 = jnp.min(jnp.where(lg == v1, lane, float(LANES)), axis=1, keepdims=True)
    lg2 = jnp.where(lane == i1, -jnp.inf, lg)
    v2 = jnp.max(lg2, axis=1, keepdims=True)
    i2 = jnp.min(jnp.where(lg2 == v2, lane, float(LANES)), axis=1, keepdims=True)
    e = jnp.exp(v2 - v1)
    w1 = 1.0 / (1.0 + e)
    w2 = e / (1.0 + e)
    oh1 = (lane == i1).astype(F32)
    oh2 = (lane == i2).astype(F32)
    r = lax.broadcasted_iota(jnp.int32, (RB, RB), 0)
    c = lax.broadcasted_iota(jnp.int32, (RB, RB), 1)
    tri = (r > c).astype(BF16)
    cs1 = jnp.dot(tri, oh1.astype(BF16), preferred_element_type=F32)
    cs2 = jnp.dot(tri, oh2.astype(BF16), preferred_element_type=F32)
    tot1 = jnp.sum(oh1, axis=0, keepdims=True)
    tot2 = jnp.sum(oh2, axis=0, keepdims=True)
    run = run_ref[...]
    rank1 = jnp.sum(oh1 * (run + cs1), axis=1, keepdims=True)
    rank2 = jnp.sum(oh2 * (run + tot1 + cs2), axis=1, keepdims=True)
    new_run = run + tot1 + tot2
    run_ref[...] = new_run
    cnt_ref[...] = new_run
    out = jnp.where(lane == 0, i1,
          jnp.where(lane == 1, i2,
          jnp.where(lane == 2, w1,
          jnp.where(lane == 3, w2,
          jnp.where(lane == 4, rank1,
          jnp.where(lane == 5, rank2, 0.0))))))
    route_ref[...] = out


def _router(h2_flat, wr, nblk, blk_of):
    return pl.pallas_call(
        _router_kernel,
        grid=(nblk,),
        in_specs=[pl.BlockSpec((RB, D), lambda i: (blk_of(i), 0)),
                  pl.BlockSpec((D, LANES), lambda i: (0, 0))],
        out_specs=[pl.BlockSpec((RB, LANES), lambda i: (i, 0)),
                   pl.BlockSpec((1, LANES), lambda i: (0, 0))],
        out_shape=[jax.ShapeDtypeStruct((nblk * RB, LANES), F32),
                   jax.ShapeDtypeStruct((1, LANES), F32)],
        scratch_shapes=[pltpu.VMEM((1, LANES), F32)],
        compiler_params=_cp(("arbitrary",)),
        name="moe_router",
    )(h2_flat, wr)


def _dispatch_kernel(pos_ref, h_ref, init_ref, o_ref, sem):
    del init_ref

    def issue(t, carry):
        src = h_ref.at[pl.ds(t, 1)]
        pltpu.make_async_copy(src, o_ref.at[pl.ds(pos_ref[0, 0, t], 1)], sem).start()
        pltpu.make_async_copy(src, o_ref.at[pl.ds(pos_ref[0, 0, RB + t], 1)], sem).start()
        return carry

    lax.fori_loop(0, RB, issue, 0)
    for _ in range(2):
        pltpu.make_async_copy(h_ref, o_ref.at[pl.ds(0, RB)], sem).wait()


def _dispatch(pos, h2_flat, nblk, blk_of, n_sorted):
    init = jnp.zeros((n_sorted, D), F32)
    return pl.pallas_call(
        _dispatch_kernel,
        grid=(nblk,),
        in_specs=[pl.BlockSpec((1, 1, 2 * RB), lambda i: (i, 0, 0), memory_space=pltpu.SMEM),
                  pl.BlockSpec((RB, D), lambda i: (blk_of(i), 0)),
                  pl.BlockSpec(memory_space=pl.ANY)],
        out_specs=pl.BlockSpec(memory_space=pl.ANY),
        out_shape=jax.ShapeDtypeStruct((n_sorted, D), F32),
        scratch_shapes=[pltpu.SemaphoreType.DMA(())],
        input_output_aliases={2: 0},
        compiler_params=_cp(("arbitrary",)),
        name="moe_dispatch",
    )(pos, h2_flat, init)


def _gmm_kernel(be_ref, nb_ref, a_ref, w1_ref, w3_ref, w2_ref, o_ref):
    del be_ref
    i = pl.program_id(0)
    f = pl.program_id(1)

    @pl.when(i < nb_ref[0])
    def _():
        a = a_ref[...].astype(BF16)
        h1 = jnp.dot(a, w1_ref[0], preferred_element_type=F32)
        h3 = jnp.dot(a, w3_ref[0], preferred_element_type=F32)
        part = jnp.dot((_silu(h1) * h3).astype(BF16), w2_ref[0], preferred_element_type=F32)

        @pl.when(f == 0)
        def _():
            o_ref[...] = part

        @pl.when(f > 0)
        def _():
            o_ref[...] += part

    @pl.when((i >= nb_ref[0]) & (f == 0))
    def _():
        o_ref[...] = jnp.zeros_like(o_ref)


def _expert_ffn(blk_e, nb, hs, w1, w3, w2):
    nbmax = hs.shape[0] // MOE_TM
    nf = D_FF_EXPERT // MOE_FC

    def ieff(i, nb_ref):
        return jnp.minimum(i, nb_ref[0] - 1)

    def feff(i, f, nb_ref):
        return jnp.where(i < nb_ref[0], f, nf - 1)

    grid_spec = pltpu.PrefetchScalarGridSpec(
        num_scalar_prefetch=2,
        grid=(nbmax, nf),
        in_specs=[pl.BlockSpec((MOE_TM, D), lambda i, f, be, nbr: (ieff(i, nbr), 0)),
                  pl.BlockSpec((1, D, MOE_FC), lambda i, f, be, nbr: (be[ieff(i, nbr)], 0, feff(i, f, nbr))),
                  pl.BlockSpec((1, D, MOE_FC), lambda i, f, be, nbr: (be[ieff(i, nbr)], 0, feff(i, f, nbr))),
                  pl.BlockSpec((1, MOE_FC, D), lambda i, f, be, nbr: (be[ieff(i, nbr)], feff(i, f, nbr), 0))],
        out_specs=pl.BlockSpec((MOE_TM, D), lambda i, f, be, nbr: (i, 0)),
    )
    return pl.pallas_call(
        _gmm_kernel,
        grid_spec=grid_spec,
        out_shape=jax.ShapeDtypeStruct(hs.shape, F32),
        compiler_params=_cp(("arbitrary", "arbitrary")),
        name="moe_expert_ffn",
    )(blk_e, nb, hs, w1, w3, w2)


def _combine_kernel(pos_ref, y_ref, route_ref, x_ref, mod_ref, gn_ref, modn_ref,
                    x2_ref, hn_ref, ybuf, sem):
    def issue(t, carry):
        pltpu.make_async_copy(y_ref.at[pl.ds(pos_ref[0, 0, t], 1)], ybuf.at[0, pl.ds(t, 1)], sem).start()
        pltpu.make_async_copy(y_ref.at[pl.ds(pos_ref[0, 0, RB + t], 1)], ybuf.at[1, pl.ds(t, 1)], sem).start()
        return carry

    lax.fori_loop(0, RB, issue, 0)
    pltpu.make_async_copy(y_ref.at[pl.ds(0, RB)], ybuf.at[0], sem).wait()
    pltpu.make_async_copy(y_ref.at[pl.ds(0, RB)], ybuf.at[1], sem).wait()
    route = route_ref[...]
    y = route[:, 2:3] * ybuf[0] + route[:, 3:4] * ybuf[1]
    mod = mod_ref[0]
    x2 = x_ref[0] + mod[5:6] * y
    x2_ref[0] = x2
    modn = modn_ref[0]
    hn_ref[0] = _norm_mod(x2, gn_ref[...], modn[1:2], modn[0:1]).astype(hn_ref.dtype)


def _combine(pos, ys, route, x1, mod, gn, modn, latent_only, hn_dtype):
    nrb = NRB - 1 if latent_only else NRB
    off = 1 if latent_only else 0
    rows = nrb * RB
    full = lambda b, i: (0, 0)
    modspec = pl.BlockSpec((1, 6, D), lambda b, i: (_mod_row(b, i + off), 0, 0))
    return pl.pallas_call(
        _combine_kernel,
        grid=(BATCH, nrb),
        in_specs=[pl.BlockSpec((1, 1, 2 * RB), lambda b, i: (b * nrb + i, 0, 0), memory_space=pltpu.SMEM),
                  pl.BlockSpec(memory_space=pl.ANY),
                  pl.BlockSpec((RB, LANES), lambda b, i: (b * nrb + i, 0)),
                  pl.BlockSpec((1, RB, D), lambda b, i: (b, i + off, 0)),
                  modspec,
                  pl.BlockSpec((1, D), full),
                  modspec],
        out_specs=[pl.BlockSpec((1, RB, D), lambda b, i: (b, i, 0)),
                   pl.BlockSpec((1, RB, D), lambda b, i: (b, i, 0))],
        out_shape=[jax.ShapeDtypeStruct((BATCH, rows, D), F32),
                   jax.ShapeDtypeStruct((BATCH, rows, D), hn_dtype)],
        scratch_shapes=[pltpu.VMEM((2, RB, D), F32), pltpu.SemaphoreType.DMA(())],
        compiler_params=_cp(("arbitrary", "arbitrary")),
        name="moe_combine",
    )(pos, ys, route, x1, mod, gn, modn)


def _moe_ffn(h2, x1, mod, wr, w1, w3, w2, gn, modn, latent_only, hn_dtype):
    h2_flat = h2.reshape(NTOK, D)
    if latent_only:
        nrb = NRB - 1
        blk_of = lambda i: (i // nrb) * NRB + (i % nrb) + 1
    else:
        nrb = NRB
        blk_of = lambda i: i
    nblk = BATCH * nrb
    n_pairs = 2 * nblk * RB
    nbmax = n_pairs // MOE_TM + N_EXPERTS
    n_sorted = nbmax * MOE_TM

    route, counts = _router(h2_flat, wr, nblk, blk_of)
    cnt = counts[0, :N_EXPERTS].astype(jnp.int32)
    gsz = ((cnt + MOE_TM - 1) // MOE_TM) * MOE_TM
    gend = jnp.cumsum(gsz)
    goff = gend - gsz
    e12 = route[:, 0:2].astype(jnp.int32)
    rank = route[:, 4:6].astype(jnp.int32)
    pos = goff[e12] + rank
    pos = pos.reshape(nblk, RB, 2).transpose(0, 2, 1).reshape(nblk, 1, 2 * RB)
    nb = (gend[-1] // MOE_TM).reshape(1)
    blk_start = jnp.arange(nbmax, dtype=jnp.int32) * MOE_TM
    blk_e = jnp.minimum(jnp.sum(blk_start[:, None] >= gend[None, :], axis=1), N_EXPERTS - 1).astype(jnp.int32)

    hs = _dispatch(pos, h2_flat, nblk, blk_of, n_sorted)
    ys = _expert_ffn(blk_e, nb, hs, w1, w3, w2)
    return _combine(pos, ys, route, x1, mod, gn, modn, latent_only, hn_dtype)


def _rope_tables():
    quarter = A_DQK // 4
    inv = 1.0 / (ROPE_BASE ** (jnp.arange(quarter, dtype=F32) / quarter))
    pos = jnp.arange(SEQ)
    rows = (pos // GRID_W).astype(F32)
    cols = (pos % GRID_W).astype(F32)
    ang_r = rows[:, None] * inv[None, :]
    ang_c = cols[:, None] * inv[None, :]
    cos = jnp.concatenate([jnp.cos(ang_r)] * 2 + [jnp.cos(ang_c)] * 2, axis=1)
    sin = jnp.concatenate([-jnp.sin(ang_r), jnp.sin(ang_r), -jnp.sin(ang_c), jnp.sin(ang_c)], axis=1)
    cos = jnp.concatenate([jnp.ones((CTX, A_DQK), F32), cos], axis=0)
    sin = jnp.concatenate([jnp.zeros((CTX, A_DQK), F32), sin], axis=0)
    kscale = A_DQK ** -0.5
    return (jnp.concatenate([cos, cos * kscale], axis=1),
            jnp.concatenate([sin, sin * kscale], axis=1))


def _na_bias_tables(rpb):
    rs = np.clip(np.arange(GRID_H) - NA_ROWS // 2, 0, GRID_H - NA_ROWS)
    cs = np.clip(np.arange(GRID_W) - NA_COLS // 2, 0, GRID_W - NA_COLS)
    rows = [0, 1, 2, 3, NA_ROWS // 2, GRID_H - 3, GRID_H - 2, GRID_H - 1]
    col = np.arange(GRID_W)
    dc = np.clip(col[None, :] - col[:, None] + NA_COLS - 1, 0, 2 * NA_COLS - 2)
    valid_c = (col[None, :] >= cs[:, None]) & (col[None, :] < cs[:, None] + NA_COLS)
    sel_c = np.eye(2 * NA_COLS - 1, dtype=np.float32)[dc]
    dr = np.stack([np.clip(rs[r] + np.arange(NA_ROWS) - r + NA_ROWS - 1, 0, 2 * NA_ROWS - 2)
                   for r in rows])
    sel_r = np.eye(2 * NA_ROWS - 1, dtype=np.float32)[dr]
    t = jnp.einsum('pya,lhab,uvb->lphuyv', jnp.asarray(sel_r), rpb, jnp.asarray(sel_c),
                   precision=lax.Precision.HIGHEST)
    neg = np.where(valid_c, 0.0, -np.inf).astype(np.float32)[:, None, :]
    t = t + jnp.asarray(neg)
    return t.reshape(rpb.shape[0], NA_NPAT, B_HEADS, NA_NQ, NA_NK).astype(BF16)


def kernel(x, c, ctx, c_ctx, w_mod, b_mod, g_norm1, g_norm2, w_in, a_conv, a_gate_b, a_hnorm_g, na_rpb,
           w_br_a, w_br_b, w_out, ffn_w1, ffn_w3, ffn_w2, moe_router, moe_w1, moe_w3, moe_w2, g_final):
    cc = jnp.concatenate([c, c_ctx[None, :], jnp.zeros((16 - BATCH - 1, D), F32)], axis=0)
    mod_all = _modulation(cc, w_mod, b_mod).reshape(DEPTH, 16, 6, D)[:, :BATCH + 1]
    mod_zero = jnp.zeros((BATCH + 1, 6, D), F32)
    rope_c, rope_s = _rope_tables()
    na_bias = _na_bias_tables(na_rpb)

    xs = jnp.concatenate([ctx, x], axis=1)
    h1 = _first_norm(xs, g_norm1[0][None, :], mod_all[0])
    out = None
    for l in range(DEPTH):
        last = l == DEPTH - 1
        mod = mod_all[l]
        wl = w_in[l]
        g0 = 3 * D
        g1 = g0 + NGATE
        wp = jnp.concatenate([wl[:, :g0], wl[:, g1:g1 + D] * (B_DH ** -0.5), wl[:, g1 + D:]],
                             axis=1).astype(BF16)
        wg = jnp.pad(wl[:, g0:g1], ((0, 0), (0, LANES - NGATE))).astype(BF16)
        bg = jnp.pad(a_gate_b[l], (0, LANES - NGATE))[None, :]

        h1_flat = h1.reshape(NTOK, D)
        p3 = _in_proj(h1_flat, wp).reshape(BATCH, TT, P_COLS)
        gates, lfp = _gate_proj(h1_flat, wg, bg)

        qk = _mlstm_prep(p3, a_conv[l], rope_c, rope_s)
        hf, hbw = _mlstm_scan(qk, p3, gates.reshape(BATCH, TT, LANES), lfp.reshape(BATCH, TT, LANES))
        hb_lat = _na_attention(p3, na_bias, l)
        hb_ctx = _ctx_attention(p3)

        moe = l % 2 == 1
        x1, h2 = _merge(hf, hbw, p3, hb_lat, hb_ctx, xs, mod, a_hnorm_g[l][None, :],
                        w_br_a[l].astype(BF16), w_br_b[l].astype(BF16), w_out[l].astype(BF16),
                        g_norm2[l][None, :], F32 if moe else BF16)
        if last:
            gn, modn = g_final[None, :], mod_zero
        else:
            gn, modn = g_norm1[l + 1][None, :], mod_all[l + 1]
        j = l // 2
        if not moe:
            xs, h1 = _dense_ffn(h2, x1, mod, ffn_w1[j].astype(BF16), ffn_w3[j].astype(BF16),
                                ffn_w2[j].astype(BF16), gn, modn)
        else:
            wr = jnp.pad(moe_router[j], ((0, 0), (0, LANES - N_EXPERTS))).astype(BF16)
            xs, h1 = _moe_ffn(h2, x1, mod, wr, moe_w1[j].astype(BF16), moe_w3[j].astype(BF16),
                              moe_w2[j].astype(BF16), gn, modn, last, F32 if last else BF16)
            if last:
                out = h1
    return out
```

```python
import functools

import numpy as np
import jax
import jax.numpy as jnp
from jax import lax
from jax.experimental import pallas as pl
from jax.experimental.pallas import tpu as pltpu

F32 = jnp.float32
BF16 = jnp.bfloat16

D = 1024
BATCH = 8
SEQ = 2048
CTX = 256
TT = CTX + SEQ
NTOK = BATCH * TT
DEPTH = 4
GRID_W = 64
GRID_H = SEQ // GRID_W

A_HEADS = 4
A_DQK = 128
A_DV = 256
A_QK = A_HEADS * A_DQK
A_V = A_HEADS * A_DV
ROPE_BASE = 10000.0
LCH = 256
NCH = TT // LCH

B_HEADS = 16
B_DH = 64
NA_ROWS = 8
NA_COLS = 16
NA_QROWS = 1
NA_KROWS = NA_QROWS + NA_ROWS - 1
NA_NQ = NA_QROWS * GRID_W
NA_NK = NA_KROWS * GRID_W
NA_STEPS = GRID_H // NA_QROWS

D_FF = 2816
N_EXPERTS = 8
D_FF_EXPERT = 3584
EPS = 1e-6

RB = 256
NRB = TT // RB
MM_TM = 1024
MOE_TM = 1024
MOE_FC = 512
MOE_SUB = 256
LANES = 128

PB_QK, PB_V, PB_O, PB_NQ, PB_NK, PB_NV, PB_GA, PB_GB = range(8)
P_COLS = 8 * D

VMEM_LIMIT = 56 * 1024 * 1024


def _cp(sem, vmem=VMEM_LIMIT):
    return pltpu.CompilerParams(dimension_semantics=sem, vmem_limit_bytes=vmem)


def _sigmoid(x):
    return 1.0 / (1.0 + jnp.exp(-x))


def _silu(x):
    return x * _sigmoid(x)


def _log_sigmoid(x):
    return jnp.minimum(x, 0.0) - jnp.log(1.0 + jnp.exp(-jnp.abs(x)))


def _norm_mod(x, g, sc, sh):
    ms = jnp.mean(x * x, axis=-1, keepdims=True)
    y = x * lax.rsqrt(ms + EPS)
    return (y * g) * (1.0 + sc) + sh


def _mod_row(b, i):
    return jnp.where(i == 0, BATCH, b)


def _mod_kernel(c_ref, w_ref, b_ref, o_ref):
    c = c_ref[...]
    s = _silu(c).astype(BF16)
    o_ref[0] = jnp.dot(s, w_ref[0].astype(BF16), preferred_element_type=F32) + b_ref[0]


def _modulation(cc, w_mod, b_mod):
    tn = 2048
    nl = w_mod.shape[0]
    return pl.pallas_call(
        _mod_kernel,
        grid=(nl, 6 * D // tn),
        in_specs=[pl.BlockSpec((16, D), lambda l, j: (0, 0)),
                  pl.BlockSpec((1, D, tn), lambda l, j: (l, 0, j)),
                  pl.BlockSpec((1, 1, tn), lambda l, j: (l, 0, j))],
        out_specs=pl.BlockSpec((1, 16, tn), lambda l, j: (l, 0, j)),
        out_shape=jax.ShapeDtypeStruct((nl, 16, 6 * D), F32),
        compiler_params=_cp(("parallel", "parallel")),
        name="modulation",
    )(cc, w_mod, b_mod.reshape(nl, 1, 6 * D))


def _norm_kernel(x_ref, g_ref, mod_ref, o_ref):
    mod = mod_ref[0]
    o_ref[0] = _norm_mod(x_ref[0], g_ref[...], mod[1:2], mod[0:1]).astype(o_ref.dtype)


def _first_norm(x, g, mod):
    return pl.pallas_call(
        _norm_kernel,
        grid=(BATCH, NRB),
        in_specs=[pl.BlockSpec((1, RB, D), lambda b, i: (b, i, 0)),
                  pl.BlockSpec((1, D), lambda b, i: (0, 0)),
                  pl.BlockSpec((1, 6, D), lambda b, i: (_mod_row(b, i), 0, 0))],
        out_specs=pl.BlockSpec((1, RB, D), lambda b, i: (b, i, 0)),
        out_shape=jax.ShapeDtypeStruct((BATCH, TT, D), BF16),
        compiler_params=_cp(("parallel", "parallel")),
        name="first_norm",
    )(x, g, mod)


def _mm_kernel(a_ref, w_ref, o_ref):
    o_ref[...] = jnp.dot(a_ref[...], w_ref[...], preferred_element_type=F32).astype(o_ref.dtype)


def _in_proj(h, w):
    tn = 1024
    return pl.pallas_call(
        _mm_kernel,
        grid=(P_COLS // tn, NTOK // MM_TM),
        in_specs=[pl.BlockSpec((MM_TM, D), lambda j, i: (i, 0)),
                  pl.BlockSpec((D, tn), lambda j, i: (0, j))],
        out_specs=pl.BlockSpec((MM_TM, tn), lambda j, i: (i, j)),
        out_shape=jax.ShapeDtypeStruct((NTOK, P_COLS), BF16),
        compiler_params=_cp(("parallel", "parallel")),
        name="in_proj",
    )(h, w)


NGATE = 4 * A_HEADS


def _gate_kernel(a_ref, w_ref, b_ref, g_ref, lf_ref):
    g = jnp.dot(a_ref[...], w_ref[...], preferred_element_type=F32) + b_ref[...]
    g_ref[...] = g
    lf = _log_sigmoid(g)
    p0 = lf.astype(BF16).astype(F32)
    r1 = lf - p0
    p1 = r1.astype(BF16).astype(F32)
    p2 = (r1 - p1).astype(BF16).astype(F32)
    lane = lax.broadcasted_iota(jnp.int32, (1, LANES), 1)
    parts = jnp.where(lane < NGATE, p0,
                      jnp.where(lane < 2 * NGATE, pltpu.roll(p1, NGATE, 1),
                                jnp.where(lane < 3 * NGATE, pltpu.roll(p2, 2 * NGATE, 1), 0.0)))
    lf_ref[...] = parts.astype(BF16)


def _gate_proj(h, wg, bg):
    return pl.pallas_call(
        _gate_kernel,
        grid=(NTOK // MM_TM,),
        in_specs=[pl.BlockSpec((MM_TM, D), lambda i: (i, 0)),
                  pl.BlockSpec((D, LANES), lambda i: (0, 0)),
                  pl.BlockSpec((1, LANES), lambda i: (0, 0))],
        out_specs=[pl.BlockSpec((MM_TM, LANES), lambda i: (i, 0)),
                   pl.BlockSpec((MM_TM, LANES), lambda i: (i, 0))],
        out_shape=[jax.ShapeDtypeStruct((NTOK, LANES), F32),
                   jax.ShapeDtypeStruct((NTOK, LANES), BF16)],
        compiler_params=_cp(("parallel",)),
        name="gate_proj",
    )(h, wg, bg)


def _prep_kernel(u_ref, up_ref, un_ref, w_ref, c_ref, s_ref, o_ref):
    i = pl.program_id(1)
    u = u_ref[0].astype(F32)
    prev_row = jnp.where(i >= 2, up_ref[0, 15:16, :].astype(F32), 0.0)
    next_row = jnp.where((i >= 1) & (i <= NRB - 2), un_ref[0, 0:1, :].astype(F32), 0.0)
    rid = lax.broadcasted_iota(jnp.int32, (RB, 1), 0)
    u_m1 = jnp.where(rid == 0, prev_row, pltpu.roll(u, 1, 0))
    u_p1 = jnp.where(rid == RB - 1, next_row, pltpu.roll(u, RB - 1, 0))
    w = w_ref[...]
    y = w[0:1] * u_m1 + w[1:2] * u + w[2:3] * u_p1
    y = _silu(y)
    c = c_ref[...]
    s = s_ref[...]
    cfull = jnp.concatenate([c[:, :A_DQK]] * A_HEADS + [c[:, A_DQK:]] * A_HEADS, axis=1)
    sfull = jnp.concatenate([s[:, :A_DQK]] * A_HEADS + [s[:, A_DQK:]] * A_HEADS, axis=1)
    lane = lax.broadcasted_iota(jnp.int32, (1, 2 * A_QK), 1)
    partner = jnp.where((lane & 32) == 0,
                        pltpu.roll(y, 2 * A_QK - 32, 1), pltpu.roll(y, 32, 1))
    o_ref[0] = (y * cfull + partner * sfull).astype(o_ref.dtype)


def _mlstm_prep(p3, conv_w, rope_c, rope_s):
    nb16 = TT // 16
    return pl.pallas_call(
        _prep_kernel,
        grid=(BATCH, NRB),
        in_specs=[pl.BlockSpec((1, RB, D), lambda b, i: (b, i, PB_QK)),
                  pl.BlockSpec((1, 16, D), lambda b, i: (b, jnp.maximum(i * (RB // 16) - 1, 0), PB_QK)),
                  pl.BlockSpec((1, 16, D), lambda b, i: (b, jnp.minimum((i + 1) * (RB // 16), nb16 - 1), PB_QK)),
                  pl.BlockSpec((3, D), lambda b, i: (0, 0)),
                  pl.BlockSpec((RB, 2 * A_DQK), lambda b, i: (i, 0)),
                  pl.BlockSpec((RB, 2 * A_DQK), lambda b, i: (i, 0))],
        out_specs=pl.BlockSpec((1, RB, D), lambda b, i: (b, i, 0)),
        out_shape=jax.ShapeDtypeStruct((BATCH, TT, D), BF16),
        compiler_params=_cp(("parallel", "parallel")),
        name="mlstm_prep",
    )(p3, p3, p3, conv_w, rope_c, rope_s)


def _mlstm_kernel(qkf_ref, vf_ref, gf_ref, lff_ref, qkb_ref, vb_ref, gb_ref, lfb_ref,
                  of_ref, ob_ref, ct_ref, n_ref, m_ref):
    @pl.when(pl.program_id(1) == 0)
    def _():
        ct_ref[...] = jnp.zeros_like(ct_ref)
        n_ref[...] = jnp.zeros_like(n_ref)
        m_ref[...] = jnp.zeros_like(m_ref)

    r = lax.broadcasted_iota(jnp.int32, (LCH, LCH), 0)
    c = lax.broadcasted_iota(jnp.int32, (LCH, LCH), 1)
    masks = [c <= r, c >= r]
    ins = [(qkf_ref, vf_ref, gf_ref, lff_ref, of_ref), (qkb_ref, vb_ref, gb_ref, lfb_ref, ob_ref)]
    b_all, g_all, b_t, g_t, b_end = [], [], [], [], []
    for d in range(2):
        tri = jnp.where(masks[d], 1.0, 0.0).astype(BF16)
        bc = jnp.dot(tri, ins[d][3][0], preferred_element_type=F32)
        ba = bc + pltpu.roll(bc, LANES - NGATE, 1) + pltpu.roll(bc, LANES - 2 * NGATE, 1)
        ga = ins[d][2][0]
        if d == 1:
            ba = pltpu.roll(ba, LANES - 2 * A_HEADS, 1)
            ga = pltpu.roll(ga, LANES - 2 * A_HEADS, 1)
        b_all.append(ba)
        g_all.append(ga)
        b_t.append(ba.T)
        g_t.append(ga.T)
        b_end.append(ba[LCH - 1:LCH, :] if d == 0 else ba[0:1, :])
    ones = jnp.ones((LCH, LANES), BF16)
    tn_dims = (((0,), (0,)), ((), ()))
    combos = [(d, h) for d in range(2) for h in range(A_HEADS)]
    idx = range(len(combos))
    qs = [ins[d][0][0, :, h * A_DQK:(h + 1) * A_DQK] for d, h in combos]
    ks = [ins[d][0][0, :, A_QK + h * A_DQK:A_QK + (h + 1) * A_DQK] for d, h in combos]
    qk = [lax.dot_general(qs[i], ks[i], NT_DIMS, preferred_element_type=F32) for i in idx]
    cq = [jnp.dot(qs[i], ct_ref[i].astype(BF16), preferred_element_type=F32) for i in idx]
    nq = [jnp.dot(qs[i], n_ref[i].astype(BF16), preferred_element_type=F32)[:, 0:1] for i in idx]
    b_col = [b_all[d][:, A_HEADS + h:A_HEADS + h + 1] for d, h in combos]
    b_last = [b_end[d][:, A_HEADS + h:A_HEADS + h + 1] for d, h in combos]
    m_old = [m_ref[i][:, 0:1] for i in idx]
    dmat = [jnp.where(masks[d], b_col[i] - (b_t[d][A_HEADS + h:A_HEADS + h + 1, :] - g_t[d][h:h + 1, :]),
                      -jnp.inf) for i, (d, h) in enumerate(combos)]
    m_row = [jnp.maximum(b_col[i] + m_old[i], jnp.max(dmat[i], axis=1, keepdims=True)) for i in idx]
    s = [qk[i] * jnp.exp(dmat[i] - m_row[i]) for i in idx]
    w_int = [jnp.exp(b_col[i] + m_old[i] - m_row[i]) for i in idx]
    g_col = [b_last[i] - b_col[i] + g_all[d][:, h:h + 1] for i, (d, h) in enumerate(combos)]
    m_new = [jnp.maximum(b_last[i] + m_old[i], jnp.max(g_col[i], axis=0, keepdims=True)) for i in idx]
    kw = [(jnp.exp(g_col[i] - m_new[i]) * ks[i].astype(F32)).astype(BF16) for i in idx]
    decay = [jnp.exp(b_last[i] + m_old[i] - m_new[i]) for i in idx]
    for i, (d, h) in enumerate(combos):
        v = ins[d][1][0, :, h * A_DV:(h + 1) * A_DV]
        num = jnp.dot(s[i].astype(BF16), v, preferred_element_type=F32) + w_int[i] * cq[i]
        den = jnp.sum(s[i], axis=1, keepdims=True) + w_int[i] * nq[i]
        hout = num / jnp.maximum(jnp.abs(den), jnp.exp(-m_row[i]))
        ins[d][4][0, :, h * A_DV:(h + 1) * A_DV] = hout.astype(of_ref.dtype)
    for i, (d, h) in enumerate(combos):
        v = ins[d][1][0, :, h * A_DV:(h + 1) * A_DV]
        ct_ref[i] = decay[i] * ct_ref[i] + lax.dot_general(kw[i], v, tn_dims, preferred_element_type=F32)
        n_ref[i] = decay[i] * n_ref[i] + lax.dot_general(kw[i], ones, tn_dims, preferred_element_type=F32)
        m_ref[i] = jnp.broadcast_to(m_new[i], (1, LANES))


def _rev_chunk(s):
    return jnp.where(s == 0, 0, NCH - s)


def _mlstm_scan(qk, p3, gates, lfp):
    fwd = lambda b, s: (b, s, 0)
    bwd = lambda b, s: (b, _rev_chunk(s), 0)
    specs = lambda im, imv: [pl.BlockSpec((1, LCH, D), im), pl.BlockSpec((1, LCH, A_V), imv),
                             pl.BlockSpec((1, LCH, LANES), im), pl.BlockSpec((1, LCH, LANES), im)]
    nstate = 2 * A_HEADS
    return pl.pallas_call(
        _mlstm_kernel,
        grid=(BATCH, NCH),
        in_specs=(specs(fwd, lambda b, s: (b, s, PB_V))
                  + specs(bwd, lambda b, s: (b, _rev_chunk(s), PB_V))),
        out_specs=[pl.BlockSpec((1, LCH, A_V), fwd), pl.BlockSpec((1, LCH, A_V), bwd)],
        out_shape=[jax.ShapeDtypeStruct((BATCH, TT, A_V), BF16)] * 2,
        scratch_shapes=[pltpu.VMEM((nstate, A_DQK, A_DV), F32),
                        pltpu.VMEM((nstate, A_DQK, LANES), F32),
                        pltpu.VMEM((nstate, 1, LANES), F32)],
        compiler_params=_cp(("parallel", "arbitrary")),
        name="mlstm_scan",
    )(qk, p3, gates, lfp, qk, p3, gates, lfp)


NPAIR = B_HEADS // 2
NT_DIMS = (((1,), (1,)), ((), ()))


def _stack_pair(q, low):
    zero = jnp.zeros_like(q)
    return jnp.concatenate([jnp.where(low, q, zero), jnp.where(low, zero, q)], axis=0)


def _softmax_rows(s_ref, p_ref):
    s = s_ref[...]
    p = jnp.exp(s - jnp.max(s, axis=2, keepdims=True))
    p_ref[...] = p.astype(p_ref.dtype)
    return 1.0 / jnp.sum(p, axis=2, keepdims=True)


def _na_kernel(q_ref, k_ref, v_ref, bias_ref, o_ref, s_ref, p_ref):
    row = pl.program_id(1)
    rs = jnp.clip(row - NA_ROWS // 2, 0, GRID_H - NA_ROWS)
    start = pl.multiple_of(CTX + rs * GRID_W, GRID_W)
    low = lax.broadcasted_iota(jnp.int32, (1, 2 * B_DH), 1) < B_DH
    for hp in range(NPAIR):
        lo, hi = hp * 2 * B_DH, (hp + 1) * 2 * B_DH
        q2 = _stack_pair(q_ref[0, :, lo:hi], low)
        s_ref[hp, :, 0:NA_NK] = lax.dot_general(q2, k_ref[0, pl.ds(start, NA_NK), lo:hi], NT_DIMS,
                                                preferred_element_type=F32) + bias_ref[0, 0, hp]
        s_ref[hp, :, NA_NK:] = lax.dot_general(q2, k_ref[0, 0:CTX, lo:hi], NT_DIMS,
                                               preferred_element_type=F32)
    rinv = _softmax_rows(s_ref, p_ref)
    for hp in range(NPAIR):
        lo, hi = hp * 2 * B_DH, (hp + 1) * 2 * B_DH
        o2 = (jnp.dot(p_ref[hp, :, 0:NA_NK], v_ref[0, pl.ds(start, NA_NK), lo:hi],
                      preferred_element_type=F32)
              + jnp.dot(p_ref[hp, :, NA_NK:], v_ref[0, 0:CTX, lo:hi], preferred_element_type=F32))
        o2 = o2 * rinv[hp]
        o_ref[0, :, lo:hi] = jnp.where(low, o2[0:NA_NQ], o2[NA_NQ:]).astype(o_ref.dtype)


def _na_pattern(row):
    edge = NA_ROWS // 2
    return jnp.where(row < edge, row, jnp.where(row <= GRID_H - edge, edge, row - (GRID_H - 2 * edge)))


NA_NPAT = NA_ROWS


def _na_attention(p3, bias, layer):
    qoff = CTX // NA_NQ
    return pl.pallas_call(
        _na_kernel,
        grid=(BATCH, GRID_H),
        in_specs=[pl.BlockSpec((1, NA_NQ, D), lambda b, i: (b, i + qoff, PB_NQ)),
                  pl.BlockSpec((1, TT, D), lambda b, i: (b, 0, PB_NK)),
                  pl.BlockSpec((1, TT, D), lambda b, i: (b, 0, PB_NV)),
                  pl.BlockSpec((1, 1, NPAIR, 2 * NA_NQ, NA_NK),
                               lambda b, i: (layer, _na_pattern(i), 0, 0, 0))],
        out_specs=pl.BlockSpec((1, NA_NQ, D), lambda b, i: (b, i, 0)),
        out_shape=jax.ShapeDtypeStruct((BATCH, SEQ, D), BF16),
        scratch_shapes=[pltpu.VMEM((NPAIR, 2 * NA_NQ, NA_NK + CTX), F32),
                        pltpu.VMEM((NPAIR, 2 * NA_NQ, NA_NK + CTX), BF16)],
        compiler_params=_cp(("parallel", "arbitrary")),
        name="na_attention",
    )(p3, p3, p3, bias)


def _ctx_attn_kernel(q_ref, k_ref, v_ref, o_ref, s_ref, p_ref):
    low = lax.broadcasted_iota(jnp.int32, (1, 2 * B_DH), 1) < B_DH
    for hp in range(NPAIR):
        lo, hi = hp * 2 * B_DH, (hp + 1) * 2 * B_DH
        s_ref[hp] = lax.dot_general(_stack_pair(q_ref[0, :, lo:hi], low), k_ref[0, :, lo:hi], NT_DIMS,
                                    preferred_element_type=F32)
    rinv = _softmax_rows(s_ref, p_ref)
    for hp in range(NPAIR):
        lo, hi = hp * 2 * B_DH, (hp + 1) * 2 * B_DH
        o2 = jnp.dot(p_ref[hp], v_ref[0, :, lo:hi], preferred_element_type=F32) * rinv[hp]
        o_ref[0, :, lo:hi] = jnp.where(low, o2[0:CTX], o2[CTX:]).astype(o_ref.dtype)


def _ctx_attention(p3):
    return pl.pallas_call(
        _ctx_attn_kernel,
        grid=(BATCH,),
        in_specs=[pl.BlockSpec((1, CTX, D), lambda b: (b, 0, PB_NQ)),
                  pl.BlockSpec((1, CTX, D), lambda b: (b, 0, PB_NK)),
                  pl.BlockSpec((1, CTX, D), lambda b: (b, 0, PB_NV))],
        out_specs=pl.BlockSpec((1, CTX, D), lambda b: (b, 0, 0)),
        out_shape=jax.ShapeDtypeStruct((BATCH, CTX, D), BF16),
        scratch_shapes=[pltpu.VMEM((NPAIR, 2 * CTX, CTX), F32),
                        pltpu.VMEM((NPAIR, 2 * CTX, CTX), BF16)],
        compiler_params=_cp(("parallel",)),
        name="ctx_attention",
    )(p3, p3, p3)


def _merge_kernel(hf_ref, hbw_ref, o_ref, ga_ref, gb_ref, hnl_ref, hnc_ref, x_ref, mod_ref, ghn_ref,
                  wa_ref, wb_ref, wo_ref, g2_ref, x1_ref, h2_ref):
    hn = jnp.where(pl.program_id(1) == 0, hnc_ref[0], hnl_ref[0])
    hs = hf_ref[0].astype(F32) + hbw_ref[0].astype(F32)
    parts = []
    for h in range(A_HEADS):
        seg = hs[:, h * A_DV:(h + 1) * A_DV]
        mu = jnp.mean(seg, axis=-1, keepdims=True)
        cen = seg - mu
        var = jnp.mean(cen * cen, axis=-1, keepdims=True)
        parts.append(cen * lax.rsqrt(var + EPS))
    ya = jnp.concatenate(parts, axis=1) * ghn_ref[...] * _sigmoid(o_ref[0].astype(F32))
    a = jnp.dot(ya.astype(BF16), wa_ref[...], preferred_element_type=F32)
    bm = jnp.dot(hn, wb_ref[...], preferred_element_type=F32)
    mrg = _sigmoid(ga_ref[0].astype(F32)) * a + _sigmoid(gb_ref[0].astype(F32)) * bm
    y = jnp.dot(mrg.astype(BF16), wo_ref[...], preferred_element_type=F32)
    mod = mod_ref[0]
    x1 = x_ref[0] + mod[2:3] * y
    x1_ref[0] = x1
    h2_ref[0] = _norm_mod(x1, g2_ref[...], mod[4:5], mod[3:4]).astype(h2_ref.dtype)


def _merge(hf, hbw, p3, hb_lat, hb_ctx, x, mod, ghn, wa, wb, wo, g2, h2_dtype):
    row = lambda b, i: (b, i, 0)
    full = lambda b, i: (0, 0)
    return pl.pallas_call(
        _merge_kernel,
        grid=(BATCH, NRB),
        in_specs=[pl.BlockSpec((1, RB, A_V), row),
                  pl.BlockSpec((1, RB, A_V), row),
                  pl.BlockSpec((1, RB, D), lambda b, i: (b, i, PB_O)),
                  pl.BlockSpec((1, RB, D), lambda b, i: (b, i, PB_GA)),
                  pl.BlockSpec((1, RB, D), lambda b, i: (b, i, PB_GB)),
                  pl.BlockSpec((1, RB, D), lambda b, i: (b, jnp.maximum(i - 1, 0), 0)),
                  pl.BlockSpec((1, CTX, D), lambda b, i: (b, 0, 0)),
                  pl.BlockSpec((1, RB, D), row),
                  pl.BlockSpec((1, 6, D), lambda b, i: (_mod_row(b, i), 0, 0)),
                  pl.BlockSpec((1, A_V), full),
                  pl.BlockSpec((A_V, D), full),
                  pl.BlockSpec((D, D), full),
                  pl.BlockSpec((D, D), full),
                  pl.BlockSpec((1, D), full)],
        out_specs=[pl.BlockSpec((1, RB, D), row), pl.BlockSpec((1, RB, D), row)],
        out_shape=[jax.ShapeDtypeStruct((BATCH, TT, D), F32),
                   jax.ShapeDtypeStruct((BATCH, TT, D), h2_dtype)],
        compiler_params=_cp(("parallel", "parallel")),
        name="merge",
    )(hf, hbw, p3, p3, p3, hb_lat, hb_ctx, x, mod, ghn, wa, wb, wo, g2)


FFN_CH = D_FF // 2


def _ffn_kernel(h_ref, x_ref, mod_ref, w1_ref, w3_ref, w2_ref, gn_ref, modn_ref, x2_ref, hn_ref):
    h = h_ref[0]
    y = None
    for cidx in range(D_FF // FFN_CH):
        lo, hi = cidx * FFN_CH, (cidx + 1) * FFN_CH
        a = jnp.dot(h, w1_ref[:, lo:hi], preferred_element_type=F32)
        g = jnp.dot(h, w3_ref[:, lo:hi], preferred_element_type=F32)
        part = jnp.dot((_silu(a) * g).astype(BF16), w2_ref[lo:hi, :], preferred_element_type=F32)
        y = part if y is None else y + part
    mod = mod_ref[0]
    x2 = x_ref[0] + mod[5:6] * y
    x2_ref[0] = x2
    modn = modn_ref[0]
    hn_ref[0] = _norm_mod(x2, gn_ref[...], modn[1:2], modn[0:1]).astype(hn_ref.dtype)


def _dense_ffn(h2, x1, mod, w1, w3, w2, gn, modn):
    row = lambda b, i: (b, i, 0)
    full = lambda b, i: (0, 0)
    modspec = pl.BlockSpec((1, 6, D), lambda b, i: (_mod_row(b, i), 0, 0))
    return pl.pallas_call(
        _ffn_kernel,
        grid=(BATCH, NRB),
        in_specs=[pl.BlockSpec((1, RB, D), row), pl.BlockSpec((1, RB, D), row), modspec,
                  pl.BlockSpec((D, D_FF), full), pl.BlockSpec((D, D_FF), full),
                  pl.BlockSpec((D_FF, D), full), pl.BlockSpec((1, D), full), modspec],
        out_specs=[pl.BlockSpec((1, RB, D), row), pl.BlockSpec((1, RB, D), row)],
        out_shape=[jax.ShapeDtypeStruct((BATCH, TT, D), F32),
                   jax.ShapeDtypeStruct((BATCH, TT, D), BF16)],
        compiler_params=_cp(("parallel", "parallel")),
        name="dense_ffn",
    )(h2, x1, mod, w1, w3, w2, gn, modn)


def _router_kernel(h_ref, wr_ref, route_ref, cnt_ref, run_ref):
    i = pl.program_id(0)

    @pl.when(i == 0)
    def _():
        run_ref[...] = jnp.zeros_like(run_ref)

    logits = jnp.dot(h_ref[...].astype(BF16), wr_ref[...], preferred_element_type=F32)
    lane = lax.broadcasted_iota(jnp.int32, (RB, LANES), 1).astype(F32)
    lg = jnp.where(lane < N_EXPERTS, logits, -jnp.inf)
    v1 = jnp.max(lg, axis=1, keepdims=True)
    i1 = jnp.min(jnp.where(lg == v1, lane, float(LANES)), axis=1, keepdims=True)
    lg2 = jnp.where(lane == i1, -jnp.inf, lg)
    v2 = jnp.max(lg2, axis=1, keepdims=True)
    i2 = jnp.min(jnp.where(lg2 == v2, lane, float(LANES)), axis=1, keepdims=True)
    e = jnp.exp(v2 - v1)
    w1 = 1.0 / (1.0 + e)
    w2 = e / (1.0 + e)
    oh1 = (lane == i1).astype(F32)
    oh2 = (lane == i2).astype(F32)
    r = lax.broadcasted_iota(jnp.int32, (RB, RB), 0)
    c = lax.broadcasted_iota(jnp.int32, (RB, RB), 1)
    tri = (r > c).astype(BF16)
    cs1 = jnp.dot(tri, oh1.astype(BF16), preferred_element_type=F32)
    cs2 = jnp.dot(tri, oh2.astype(BF16), preferred_element_type=F32)
    tot1 = jnp.sum(oh1, axis=0, keepdims=True)
    tot2 = jnp.sum(oh2, axis=0, keepdims=True)
    run = run_ref[...]
    rank1 = jnp.sum(oh1 * (run + cs1), axis=1, keepdims=True)
    rank2 = jnp.sum(oh2 * (run + tot1 + cs2), axis=1, keepdims=True)
    new_run = run + tot1 + tot2
    run_ref[...] = new_run
    cnt_ref[...] = new_run
    out = jnp.where(lane == 0, i1,
          jnp.where(lane == 1, i2,
          jnp.where(lane == 2, w1,
          jnp.where(lane == 3, w2,
          jnp.where(lane == 4, rank1,
          jnp.where(lane == 5, rank2, 0.0))))))
    route_ref[...] = out


def _router(h2_flat, wr, nblk, blk_of):
    return pl.pallas_call(
        _router_kernel,
        grid=(nblk,),
        in_specs=[pl.BlockSpec((RB, D), lambda i: (blk_of(i), 0)),
                  pl.BlockSpec((D, LANES), lambda i: (0, 0))],
        out_specs=[pl.BlockSpec((RB, LANES), lambda i: (i, 0)),
                   pl.BlockSpec((1, LANES), lambda i: (0, 0))],
        out_shape=[jax.ShapeDtypeStruct((nblk * RB, LANES), F32),
                   jax.ShapeDtypeStruct((1, LANES), F32)],
        scratch_shapes=[pltpu.VMEM((1, LANES), F32)],
        compiler_params=_cp(("arbitrary",)),
        name="moe_router",
    )(h2_flat, wr)


DMA_UNROLL = 8
N_ZERO = 2 * N_EXPERTS


def _dispatch_kernel(pos_ref, zstart_ref, h_ref, o_ref, zbuf, sem, zsem):
    @pl.when(pl.program_id(0) == 0)
    def _():
        zbuf[...] = jnp.zeros_like(zbuf)
        for z in range(N_ZERO):
            @pl.when(zstart_ref[z] >= 0)
            def _():
                zs = pl.multiple_of(zstart_ref[z], MOE_TM)
                pltpu.make_async_copy(zbuf, o_ref.at[pl.ds(zs, MOE_TM)], zsem).start()
        for z in range(N_ZERO):
            @pl.when(zstart_ref[z] >= 0)
            def _():
                pltpu.make_async_copy(zbuf, o_ref.at[pl.ds(0, MOE_TM)], zsem).wait()

    def issue(t, carry):
        src = h_ref.at[pl.ds(t, 1)]
        pltpu.make_async_copy(src, o_ref.at[pl.ds(pos_ref[0, 0, t], 1)], sem).start()
        pltpu.make_async_copy(src, o_ref.at[pl.ds(pos_ref[0, 0, RB + t], 1)], sem).start()
        return carry

    lax.fori_loop(0, RB, issue, 0, unroll=DMA_UNROLL)
    for _ in range(2):
        pltpu.make_async_copy(h_ref, o_ref.at[pl.ds(0, RB)], sem).wait()


def _dispatch(pos, zstart, h2_flat, nblk, blk_of, n_sorted):
    return pl.pallas_call(
        _dispatch_kernel,
        grid=(nblk,),
        in_specs=[pl.BlockSpec((1, 1, 2 * RB), lambda i: (i, 0, 0), memory_space=pltpu.SMEM),
                  pl.BlockSpec(memory_space=pltpu.SMEM),
                  pl.BlockSpec((RB, D), lambda i: (blk_of(i), 0))],
        out_specs=pl.BlockSpec(memory_space=pl.ANY),
        out_shape=jax.ShapeDtypeStruct((n_sorted, D), F32),
        scratch_shapes=[pltpu.VMEM((MOE_TM, D), F32), pltpu.SemaphoreType.DMA(()),
                        pltpu.SemaphoreType.DMA(())],
        compiler_params=_cp(("arbitrary",)),
        name="moe_dispatch",
    )(pos, zstart, h2_flat)


def _gmm_kernel(be_ref, nb_ref, a_ref, w1_ref, w3_ref, w2_ref, o_ref, abf_ref):
    del be_ref
    i = pl.program_id(0)
    f = pl.program_id(1)

    @pl.when((i < nb_ref[0]) & (f == 0))
    def _():
        abf_ref[...] = a_ref[...].astype(BF16)

    @pl.when(i < nb_ref[0])
    def _():
        a = abf_ref[...]
        gs = []
        for c in range(MOE_FC // MOE_SUB):
            lo, hi = c * MOE_SUB, (c + 1) * MOE_SUB
            h1 = jnp.dot(a, w1_ref[0, :, lo:hi].astype(BF16), preferred_element_type=F32)
            h3 = jnp.dot(a, w3_ref[0, :, lo:hi].astype(BF16), preferred_element_type=F32)
            gs.append((_silu(h1) * h3).astype(BF16))
        part = jnp.dot(jnp.concatenate(gs, axis=1), w2_ref[0].astype(BF16), preferred_element_type=F32)

        @pl.when(f == 0)
        def _():
            o_ref[...] = part

        @pl.when(f > 0)
        def _():
            o_ref[...] += part

    @pl.when((i >= nb_ref[0]) & (f == 0))
    def _():
        o_ref[...] = jnp.zeros_like(o_ref)


def _expert_ffn(blk_e, nb, hs, w1, w3, w2, j):
    nbmax = hs.shape[0] // MOE_TM
    nf = D_FF_EXPERT // MOE_FC

    def ieff(i, nb_ref):
        return jnp.minimum(i, nb_ref[0] - 1)

    def feff(i, f, nb_ref):
        return jnp.where(i < nb_ref[0], f, nf - 1)

    grid_spec = pltpu.PrefetchScalarGridSpec(
        num_scalar_prefetch=2,
        grid=(nbmax, nf),
        in_specs=[pl.BlockSpec((MOE_TM, D), lambda i, f, be, nbr: (ieff(i, nbr), 0)),
                  pl.BlockSpec((None, 1, D, MOE_FC),
                               lambda i, f, be, nbr: (j, be[ieff(i, nbr)], 0, feff(i, f, nbr))),
                  pl.BlockSpec((None, 1, D, MOE_FC),
                               lambda i, f, be, nbr: (j, be[ieff(i, nbr)], 0, feff(i, f, nbr))),
                  pl.BlockSpec((None, 1, MOE_FC, D),
                               lambda i, f, be, nbr: (j, be[ieff(i, nbr)], feff(i, f, nbr), 0))],
        out_specs=pl.BlockSpec((MOE_TM, D), lambda i, f, be, nbr: (i, 0)),
        scratch_shapes=[pltpu.VMEM((MOE_TM, D), BF16)],
    )
    return pl.pallas_call(
        _gmm_kernel,
        grid_spec=grid_spec,
        out_shape=jax.ShapeDtypeStruct(hs.shape, F32),
        compiler_params=_cp(("arbitrary", "arbitrary")),
        name="moe_expert_ffn",
    )(blk_e, nb, hs, w1, w3, w2)


def _combine_kernel(pos_ref, y_ref, route_ref, x_ref, mod_ref, gn_ref, modn_ref,
                    x2_ref, hn_ref, ybuf, sem):
    def issue(t, carry):
        pltpu.make_async_copy(y_ref.at[pl.ds(pos_ref[0, 0, t], 1)], ybuf.at[0, pl.ds(t, 1)], sem).start()
        pltpu.make_async_copy(y_ref.at[pl.ds(pos_ref[0, 0, RB + t], 1)], ybuf.at[1, pl.ds(t, 1)], sem).start()
        return carry

    lax.fori_loop(0, RB, issue, 0, unroll=DMA_UNROLL)
    pltpu.make_async_copy(y_ref.at[pl.ds(0, RB)], ybuf.at[0], sem).wait()
    pltpu.make_async_copy(y_ref.at[pl.ds(0, RB)], ybuf.at[1], sem).wait()
    route = route_ref[...]
    y = route[:, 2:3] * ybuf[0] + route[:, 3:4] * ybuf[1]
    mod = mod_ref[0]
    x2 = x_ref[0] + mod[5:6] * y
    x2_ref[0] = x2
    modn = modn_ref[0]
    hn_ref[0] = _norm_mod(x2, gn_ref[...], modn[1:2], modn[0:1]).astype(hn_ref.dtype)


def _combine(pos, ys, route, x1, mod, gn, modn, latent_only, hn_dtype):
    nrb = NRB - 1 if latent_only else NRB
    off = 1 if latent_only else 0
    rows = nrb * RB
    full = lambda b, i: (0, 0)
    modspec = pl.BlockSpec((1, 6, D), lambda b, i: (_mod_row(b, i + off), 0, 0))
    return pl.pallas_call(
        _combine_kernel,
        grid=(BATCH, nrb),
        in_specs=[pl.BlockSpec((1, 1, 2 * RB), lambda b, i: (b * nrb + i, 0, 0), memory_space=pltpu.SMEM),
                  pl.BlockSpec(memory_space=pl.ANY),
                  pl.BlockSpec((RB, LANES), lambda b, i: (b * nrb + i, 0)),
                  pl.BlockSpec((1, RB, D), lambda b, i: (b, i + off, 0)),
                  modspec,
                  pl.BlockSpec((1, D), full),
                  modspec],
        out_specs=[pl.BlockSpec((1, RB, D), lambda b, i: (b, i, 0)),
                   pl.BlockSpec((1, RB, D), lambda b, i: (b, i, 0))],
        out_shape=[jax.ShapeDtypeStruct((BATCH, rows, D), F32),
                   jax.ShapeDtypeStruct((BATCH, rows, D), hn_dtype)],
        scratch_shapes=[pltpu.VMEM((2, RB, D), F32), pltpu.SemaphoreType.DMA(())],
        compiler_params=_cp(("arbitrary", "arbitrary")),
        name="moe_combine",
    )(pos, ys, route, x1, mod, gn, modn)


def _moe_ffn(h2, x1, mod, wr, w1, w3, w2, j, gn, modn, latent_only, hn_dtype):
    h2_flat = h2.reshape(NTOK, D)
    if latent_only:
        nrb = NRB - 1
        blk_of = lambda i: (i // nrb) * NRB + (i % nrb) + 1
    else:
        nrb = NRB
        blk_of = lambda i: i
    nblk = BATCH * nrb
    n_pairs = 2 * nblk * RB
    nbmax = n_pairs // MOE_TM + N_EXPERTS
    n_sorted = nbmax * MOE_TM

    route, counts = _router(h2_flat, wr, nblk, blk_of)
    cnt = counts[0, :N_EXPERTS].astype(jnp.int32)
    gsz = ((cnt + MOE_TM - 1) // MOE_TM) * MOE_TM
    gend = jnp.cumsum(gsz)
    goff = gend - gsz
    e12 = route[:, 0:2].astype(jnp.int32)
    rank = route[:, 4:6].astype(jnp.int32)
    pos = goff[e12] + rank
    pos = pos.reshape(nblk, RB, 2).transpose(0, 2, 1).reshape(nblk, 1, 2 * RB)
    nb = (gend[-1] // MOE_TM).reshape(1)
    blk_start = jnp.arange(nbmax, dtype=jnp.int32) * MOE_TM
    blk_e = jnp.minimum(jnp.sum(blk_start[:, None] >= gend[None, :], axis=1), N_EXPERTS - 1).astype(jnp.int32)

    trail = (nbmax - N_EXPERTS + jnp.arange(N_EXPERTS, dtype=jnp.int32)) * MOE_TM
    zstart = jnp.concatenate([jnp.where(cnt > 0, gend - MOE_TM, -1),
                              jnp.where(trail >= gend[-1], trail, -1)]).astype(jnp.int32)

    hs = _dispatch(pos, zstart, h2_flat, nblk, blk_of, n_sorted)
    ys = _expert_ffn(blk_e, nb, hs, w1, w3, w2, j)
    return _combine(pos, ys, route, x1, mod, gn, modn, latent_only, hn_dtype)


def _rope_tables():
    quarter = A_DQK // 4
    inv = 1.0 / (ROPE_BASE ** (jnp.arange(quarter, dtype=F32) / quarter))
    pos = jnp.arange(SEQ)
    rows = (pos // GRID_W).astype(F32)
    cols = (pos % GRID_W).astype(F32)
    ang_r = rows[:, None] * inv[None, :]
    ang_c = cols[:, None] * inv[None, :]
    cos = jnp.concatenate([jnp.cos(ang_r)] * 2 + [jnp.cos(ang_c)] * 2, axis=1)
    sin = jnp.concatenate([-jnp.sin(ang_r), jnp.sin(ang_r), -jnp.sin(ang_c), jnp.sin(ang_c)], axis=1)
    cos = jnp.concatenate([jnp.ones((CTX, A_DQK), F32), cos], axis=0)
    sin = jnp.concatenate([jnp.zeros((CTX, A_DQK), F32), sin], axis=0)
    kscale = A_DQK ** -0.5
    return (jnp.concatenate([cos, cos * kscale], axis=1),
            jnp.concatenate([sin, sin * kscale], axis=1))


def _na_bias_tables(rpb):
    rs = np.clip(np.arange(GRID_H) - NA_ROWS // 2, 0, GRID_H - NA_ROWS)
    cs = np.clip(np.arange(GRID_W) - NA_COLS // 2, 0, GRID_W - NA_COLS)
    rows = [0, 1, 2, 3, NA_ROWS // 2, GRID_H - 3, GRID_H - 2, GRID_H - 1]
    col = np.arange(GRID_W)
    dc = np.clip(col[None, :] - col[:, None] + NA_COLS - 1, 0, 2 * NA_COLS - 2)
    valid_c = (col[None, :] >= cs[:, None]) & (col[None, :] < cs[:, None] + NA_COLS)
    sel_c = np.eye(2 * NA_COLS - 1, dtype=np.float32)[dc]
    dr = np.stack([np.clip(rs[r] + np.arange(NA_ROWS) - r + NA_ROWS - 1, 0, 2 * NA_ROWS - 2)
                   for r in rows])
    sel_r = np.eye(2 * NA_ROWS - 1, dtype=np.float32)[dr]
    t = jnp.einsum('pya,lhab,uvb->lphuyv', jnp.asarray(sel_r), rpb, jnp.asarray(sel_c),
                   precision=lax.Precision.HIGHEST)
    neg = np.where(valid_c, 0.0, -np.inf).astype(np.float32)[:, None, :]
    t = t + jnp.asarray(neg)
    return t.reshape(rpb.shape[0], NA_NPAT, NPAIR, 2 * NA_NQ, NA_NK)


def kernel(x, c, ctx, c_ctx, w_mod, b_mod, g_norm1, g_norm2, w_in, a_conv, a_gate_b, a_hnorm_g, na_rpb,
           w_br_a, w_br_b, w_out, ffn_w1, ffn_w3, ffn_w2, moe_router, moe_w1, moe_w3, moe_w2, g_final):
    cc = jnp.concatenate([c, c_ctx[None, :], jnp.zeros((16 - BATCH - 1, D), F32)], axis=0)
    mod_all = _modulation(cc, w_mod, b_mod).reshape(DEPTH, 16, 6, D)[:, :BATCH + 1]
    mod_zero = jnp.zeros((BATCH + 1, 6, D), F32)
    rope_c, rope_s = _rope_tables()
    na_bias = _na_bias_tables(na_rpb)

    xs = jnp.concatenate([ctx, x], axis=1)
    h1 = _first_norm(xs, g_norm1[0][None, :], mod_all[0])
    out = None
    for l in range(DEPTH):
        last = l == DEPTH - 1
        mod = mod_all[l]
        wl = w_in[l]
        g0 = 3 * D
        g1 = g0 + NGATE
        wp = jnp.concatenate([wl[:, :g0], wl[:, g1:g1 + D] * (B_DH ** -0.5), wl[:, g1 + D:]],
                             axis=1).astype(BF16)
        wg = jnp.pad(wl[:, g0:g1], ((0, 0), (0, LANES - NGATE))).astype(BF16)
        bg = jnp.pad(a_gate_b[l], (0, LANES - NGATE))[None, :]

        h1_flat = h1.reshape(NTOK, D)
        p3 = _in_proj(h1_flat, wp).reshape(BATCH, TT, P_COLS)
        gates, lfp = _gate_proj(h1_flat, wg, bg)

        qk = _mlstm_prep(p3, a_conv[l], rope_c, rope_s)
        hf, hbw = _mlstm_scan(qk, p3, gates.reshape(BATCH, TT, LANES), lfp.reshape(BATCH, TT, LANES))
        hb_lat = _na_attention(p3, na_bias, l)
        hb_ctx = _ctx_attention(p3)

        moe = l % 2 == 1
        x1, h2 = _merge(hf, hbw, p3, hb_lat, hb_ctx, xs, mod, a_hnorm_g[l][None, :],
                        w_br_a[l].astype(BF16), w_br_b[l].astype(BF16), w_out[l].astype(BF16),
                        g_norm2[l][None, :], F32 if moe else BF16)
        if last:
            gn, modn = g_final[None, :], mod_zero
        else:
            gn, modn = g_norm1[l + 1][None, :], mod_all[l + 1]
        j = l // 2
        if not moe:
            xs, h1 = _dense_ffn(h2, x1, mod, ffn_w1[j].astype(BF16), ffn_w3[j].astype(BF16),
                                ffn_w2[j].astype(BF16), gn, modn)
        else:
            wr = jnp.pad(moe_router[j], ((0, 0), (0, LANES - N_EXPERTS))).astype(BF16)
            xs, h1 = _moe_ffn(h2, x1, mod, wr, moe_w1, moe_w3, moe_w2, j, gn, modn, last,
                              F32 if last else BF16)
            if last:
                out = h1
    return out
```

```python
import functools

import numpy as np
import jax
import jax.numpy as jnp
from jax import lax
from jax.experimental import pallas as pl
from jax.experimental.pallas import tpu as pltpu

F32 = jnp.float32
BF16 = jnp.bfloat16

D = 1024
BATCH = 8
SEQ = 2048
CTX = 256
TT = CTX + SEQ
NTOK = BATCH * TT
DEPTH = 4
GRID_W = 64
GRID_H = SEQ // GRID_W

A_HEADS = 4
A_DQK = 128
A_DV = 256
A_QK = A_HEADS * A_DQK
A_V = A_HEADS * A_DV
ROPE_BASE = 10000.0
LCH = 256
NCH = TT // LCH

B_HEADS = 16
B_DH = 64
NA_ROWS = 8
NA_COLS = 16
NA_QROWS = 1
NA_KROWS = NA_QROWS + NA_ROWS - 1
NA_NQ = NA_QROWS * GRID_W
NA_NK = NA_KROWS * GRID_W
NA_STEPS = GRID_H // NA_QROWS

D_FF = 2816
N_EXPERTS = 8
D_FF_EXPERT = 3584
EPS = 1e-6
LOG2E = 1.4426950408889634

RB = 256
NRB = TT // RB
MM_TM = 1024
MOE_TM = 1024
MOE_FC = 512
MOE_SUB = 256
LANES = 128

PB_QK, PB_V, PB_O, PB_NQ, PB_NK, PB_NV, PB_GA, PB_GB = range(8)
P_COLS = 8 * D

VMEM_LIMIT = 56 * 1024 * 1024


def _cp(sem, vmem=VMEM_LIMIT):
    return pltpu.CompilerParams(dimension_semantics=sem, vmem_limit_bytes=vmem)


def _sigmoid(x):
    return 1.0 / (1.0 + jnp.exp(-x))


def _silu(x):
    return x * _sigmoid(x)


def _log_sigmoid(x):
    return jnp.minimum(x, 0.0) - jnp.log(1.0 + jnp.exp(-jnp.abs(x)))


def _norm_mod(x, g, sc, sh):
    ms = jnp.mean(x * x, axis=-1, keepdims=True)
    y = x * lax.rsqrt(ms + EPS)
    return (y * g) * (1.0 + sc) + sh


def _mod_row(b, i):
    return jnp.where(i == 0, BATCH, b)


def _mod_kernel(c_ref, w_ref, b_ref, o_ref):
    c = c_ref[...]
    s = _silu(c).astype(BF16)
    o_ref[0] = jnp.dot(s, w_ref[0].astype(BF16), preferred_element_type=F32) + b_ref[0]


def _modulation(cc, w_mod, b_mod):
    tn = 2048
    nl = w_mod.shape[0]
    return pl.pallas_call(
        _mod_kernel,
        grid=(nl, 6 * D // tn),
        in_specs=[pl.BlockSpec((16, D), lambda l, j: (0, 0)),
                  pl.BlockSpec((1, D, tn), lambda l, j: (l, 0, j)),
                  pl.BlockSpec((1, 1, tn), lambda l, j: (l, 0, j))],
        out_specs=pl.BlockSpec((1, 16, tn), lambda l, j: (l, 0, j)),
        out_shape=jax.ShapeDtypeStruct((nl, 16, 6 * D), F32),
        compiler_params=_cp(("parallel", "parallel")),
        name="modulation",
    )(cc, w_mod, b_mod.reshape(nl, 1, 6 * D))


def _norm_kernel(x_ref, g_ref, mod_ref, o_ref):
    mod = mod_ref[0]
    o_ref[0] = _norm_mod(x_ref[0], g_ref[...], mod[1:2], mod[0:1]).astype(o_ref.dtype)


def _first_norm(x, g, mod):
    return pl.pallas_call(
        _norm_kernel,
        grid=(BATCH, NRB),
        in_specs=[pl.BlockSpec((1, RB, D), lambda b, i: (b, i, 0)),
                  pl.BlockSpec((1, D), lambda b, i: (0, 0)),
                  pl.BlockSpec((1, 6, D), lambda b, i: (_mod_row(b, i), 0, 0))],
        out_specs=pl.BlockSpec((1, RB, D), lambda b, i: (b, i, 0)),
        out_shape=jax.ShapeDtypeStruct((BATCH, TT, D), BF16),
        compiler_params=_cp(("parallel", "parallel")),
        name="first_norm",
    )(x, g, mod)


def _mm_kernel(a_ref, w_ref, o_ref):
    o_ref[...] = jnp.dot(a_ref[...], w_ref[...], preferred_element_type=F32).astype(o_ref.dtype)


def _in_proj(h, w):
    tn = 1024
    return pl.pallas_call(
        _mm_kernel,
        grid=(P_COLS // tn, NTOK // MM_TM),
        in_specs=[pl.BlockSpec((MM_TM, D), lambda j, i: (i, 0)),
                  pl.BlockSpec((D, tn), lambda j, i: (0, j))],
        out_specs=pl.BlockSpec((MM_TM, tn), lambda j, i: (i, j)),
        out_shape=jax.ShapeDtypeStruct((NTOK, P_COLS), BF16),
        compiler_params=_cp(("parallel", "parallel")),
        name="in_proj",
    )(h, w)


NGATE = 4 * A_HEADS


def _gate_kernel(a_ref, w_ref, b_ref, g_ref, lf_ref):
    g = jnp.dot(a_ref[...], w_ref[...], preferred_element_type=F32) + b_ref[...]
    g_ref[...] = g
    lf = _log_sigmoid(g)
    p0 = lf.astype(BF16).astype(F32)
    r1 = lf - p0
    p1 = r1.astype(BF16).astype(F32)
    p2 = (r1 - p1).astype(BF16).astype(F32)
    lane = lax.broadcasted_iota(jnp.int32, (1, LANES), 1)
    parts = jnp.where(lane < NGATE, p0,
                      jnp.where(lane < 2 * NGATE, pltpu.roll(p1, NGATE, 1),
                                jnp.where(lane < 3 * NGATE, pltpu.roll(p2, 2 * NGATE, 1), 0.0)))
    lf_ref[...] = parts.astype(BF16)


def _gate_proj(h, wg, bg):
    return pl.pallas_call(
        _gate_kernel,
        grid=(NTOK // MM_TM,),
        in_specs=[pl.BlockSpec((MM_TM, D), lambda i: (i, 0)),
                  pl.BlockSpec((D, LANES), lambda i: (0, 0)),
                  pl.BlockSpec((1, LANES), lambda i: (0, 0))],
        out_specs=[pl.BlockSpec((MM_TM, LANES), lambda i: (i, 0)),
                   pl.BlockSpec((MM_TM, LANES), lambda i: (i, 0))],
        out_shape=[jax.ShapeDtypeStruct((NTOK, LANES), F32),
                   jax.ShapeDtypeStruct((NTOK, LANES), BF16)],
        compiler_params=_cp(("parallel",)),
        name="gate_proj",
    )(h, wg, bg)


def _prep_kernel(u_ref, up_ref, un_ref, w_ref, c_ref, s_ref, o_ref):
    i = pl.program_id(1)
    u = u_ref[0].astype(F32)
    prev_row = jnp.where(i >= 2, up_ref[0, 15:16, :].astype(F32), 0.0)
    next_row = jnp.where((i >= 1) & (i <= NRB - 2), un_ref[0, 0:1, :].astype(F32), 0.0)
    rid = lax.broadcasted_iota(jnp.int32, (RB, 1), 0)
    u_m1 = jnp.where(rid == 0, prev_row, pltpu.roll(u, 1, 0))
    u_p1 = jnp.where(rid == RB - 1, next_row, pltpu.roll(u, RB - 1, 0))
    w = w_ref[...]
    y = w[0:1] * u_m1 + w[1:2] * u + w[2:3] * u_p1
    y = _silu(y)
    c = c_ref[...]
    s = s_ref[...]
    cfull = jnp.concatenate([c[:, :A_DQK]] * A_HEADS + [c[:, A_DQK:]] * A_HEADS, axis=1)
    sfull = jnp.concatenate([s[:, :A_DQK]] * A_HEADS + [s[:, A_DQK:]] * A_HEADS, axis=1)
    lane = lax.broadcasted_iota(jnp.int32, (1, 2 * A_QK), 1)
    partner = jnp.where((lane & 32) == 0,
                        pltpu.roll(y, 2 * A_QK - 32, 1), pltpu.roll(y, 32, 1))
    o_ref[0] = (y * cfull + partner * sfull).astype(o_ref.dtype)


def _mlstm_prep(p3, conv_w, rope_c, rope_s):
    nb16 = TT // 16
    return pl.pallas_call(
        _prep_kernel,
        grid=(BATCH, NRB),
        in_specs=[pl.BlockSpec((1, RB, D), lambda b, i: (b, i, PB_QK)),
                  pl.BlockSpec((1, 16, D), lambda b, i: (b, jnp.maximum(i * (RB // 16) - 1, 0), PB_QK)),
                  pl.BlockSpec((1, 16, D), lambda b, i: (b, jnp.minimum((i + 1) * (RB // 16), nb16 - 1), PB_QK)),
                  pl.BlockSpec((3, D), lambda b, i: (0, 0)),
                  pl.BlockSpec((RB, 2 * A_DQK), lambda b, i: (i, 0)),
                  pl.BlockSpec((RB, 2 * A_DQK), lambda b, i: (i, 0))],
        out_specs=pl.BlockSpec((1, RB, D), lambda b, i: (b, i, 0)),
        out_shape=jax.ShapeDtypeStruct((BATCH, TT, D), BF16),
        compiler_params=_cp(("parallel", "parallel")),
        name="mlstm_prep",
    )(p3, p3, p3, conv_w, rope_c, rope_s)


def _mlstm_kernel(qkf_ref, vf_ref, gf_ref, lff_ref, qkb_ref, vb_ref, gb_ref, lfb_ref,
                  of_ref, ob_ref, ct_ref, n_ref, m_ref):
    @pl.when(pl.program_id(1) == 0)
    def _():
        ct_ref[...] = jnp.zeros_like(ct_ref)
        n_ref[...] = jnp.zeros_like(n_ref)
        m_ref[...] = jnp.zeros_like(m_ref)

    r = lax.broadcasted_iota(jnp.int32, (LCH, LCH), 0)
    c = lax.broadcasted_iota(jnp.int32, (LCH, LCH), 1)
    masks = [c <= r, c >= r]
    ins = [(qkf_ref, vf_ref, gf_ref, lff_ref, of_ref), (qkb_ref, vb_ref, gb_ref, lfb_ref, ob_ref)]
    b_all, g_all, b_t, g_t, b_end = [], [], [], [], []
    for d in range(2):
        tri = jnp.where(masks[d], 1.0, 0.0).astype(BF16)
        bc = jnp.dot(tri, ins[d][3][0], preferred_element_type=F32)
        ba = bc + pltpu.roll(bc, LANES - NGATE, 1) + pltpu.roll(bc, LANES - 2 * NGATE, 1)
        ga = ins[d][2][0]
        if d == 1:
            ba = pltpu.roll(ba, LANES - 2 * A_HEADS, 1)
            ga = pltpu.roll(ga, LANES - 2 * A_HEADS, 1)
        b_all.append(ba)
        g_all.append(ga)
        b_t.append(ba.T)
        g_t.append(ga.T)
        b_end.append(ba[LCH - 1:LCH, :] if d == 0 else ba[0:1, :])
    ones = jnp.ones((LCH, LANES), BF16)
    tn_dims = (((0,), (0,)), ((), ()))
    combos = [(d, h) for d in range(2) for h in range(A_HEADS)]
    idx = range(len(combos))
    qs = [ins[d][0][0, :, h * A_DQK:(h + 1) * A_DQK] for d, h in combos]
    ks = [ins[d][0][0, :, A_QK + h * A_DQK:A_QK + (h + 1) * A_DQK] for d, h in combos]
    qk = [lax.dot_general(qs[i], ks[i], NT_DIMS, preferred_element_type=F32) for i in idx]
    cq = [jnp.dot(qs[i], ct_ref[i].astype(BF16), preferred_element_type=F32) for i in idx]
    nq = [jnp.dot(qs[i], n_ref[i].astype(BF16), preferred_element_type=F32)[:, 0:1] for i in idx]
    b_col = [b_all[d][:, A_HEADS + h:A_HEADS + h + 1] for d, h in combos]
    b_last = [b_end[d][:, A_HEADS + h:A_HEADS + h + 1] for d, h in combos]
    m_old = [m_ref[i][:, 0:1] for i in idx]
    dmat = [jnp.where(masks[d], b_col[i] - (b_t[d][A_HEADS + h:A_HEADS + h + 1, :] - g_t[d][h:h + 1, :]),
                      -jnp.inf) for i, (d, h) in enumerate(combos)]
    m_row = [jnp.maximum(b_col[i] + m_old[i], jnp.max(dmat[i], axis=1, keepdims=True)) for i in idx]
    s = [qk[i] * jnp.exp(dmat[i] - m_row[i]) for i in idx]
    w_int = [jnp.exp(b_col[i] + m_old[i] - m_row[i]) for i in idx]
    g_col = [b_last[i] - b_col[i] + g_all[d][:, h:h + 1] for i, (d, h) in enumerate(combos)]
    m_new = [jnp.maximum(b_last[i] + m_old[i], jnp.max(g_col[i], axis=0, keepdims=True)) for i in idx]
    kw = [(jnp.exp(g_col[i] - m_new[i]) * ks[i].astype(F32)).astype(BF16) for i in idx]
    decay = [jnp.exp(b_last[i] + m_old[i] - m_new[i]) for i in idx]
    for i, (d, h) in enumerate(combos):
        v = ins[d][1][0, :, h * A_DV:(h + 1) * A_DV]
        num = jnp.dot(s[i].astype(BF16), v, preferred_element_type=F32) + w_int[i] * cq[i]
        den = jnp.sum(s[i], axis=1, keepdims=True) + w_int[i] * nq[i]
        hout = num / jnp.maximum(jnp.abs(den), jnp.exp(-m_row[i]))
        ins[d][4][0, :, h * A_DV:(h + 1) * A_DV] = hout.astype(of_ref.dtype)
    for i, (d, h) in enumerate(combos):
        v = ins[d][1][0, :, h * A_DV:(h + 1) * A_DV]
        ct_ref[i] = decay[i] * ct_ref[i] + lax.dot_general(kw[i], v, tn_dims, preferred_element_type=F32)
        n_ref[i] = decay[i] * n_ref[i] + lax.dot_general(kw[i], ones, tn_dims, preferred_element_type=F32)
        m_ref[i] = jnp.broadcast_to(m_new[i], (1, LANES))


def _rev_chunk(s):
    ncc = CTX // LCH
    return jnp.where(s < ncc, ncc - 1 - s, NCH + ncc - 1 - s)


def _mlstm_scan(qk, p3, gates, lfp):
    fwd = lambda b, s: (b, s, 0)
    bwd = lambda b, s: (b, _rev_chunk(s), 0)
    specs = lambda im, imv: [pl.BlockSpec((1, LCH, D), im), pl.BlockSpec((1, LCH, A_V), imv),
                             pl.BlockSpec((1, LCH, LANES), im), pl.BlockSpec((1, LCH, LANES), im)]
    nstate = 2 * A_HEADS
    return pl.pallas_call(
        _mlstm_kernel,
        grid=(BATCH, NCH),
        in_specs=(specs(fwd, lambda b, s: (b, s, PB_V))
                  + specs(bwd, lambda b, s: (b, _rev_chunk(s), PB_V))),
        out_specs=[pl.BlockSpec((1, LCH, A_V), fwd), pl.BlockSpec((1, LCH, A_V), bwd)],
        out_shape=[jax.ShapeDtypeStruct((BATCH, TT, A_V), BF16)] * 2,
        scratch_shapes=[pltpu.VMEM((nstate, A_DQK, A_DV), F32),
                        pltpu.VMEM((nstate, A_DQK, LANES), F32),
                        pltpu.VMEM((nstate, 1, LANES), F32)],
        compiler_params=_cp(("parallel", "arbitrary")),
        name="mlstm_scan",
    )(qk, p3, gates, lfp, qk, p3, gates, lfp)


NPAIR = B_HEADS // 2
NT_DIMS = (((1,), (1,)), ((), ()))


def _stack_pair(q, low):
    zero = jnp.zeros_like(q)
    return jnp.concatenate([jnp.where(low, q, zero), jnp.where(low, zero, q)], axis=0)


def _softmax_rows(s_ref, p_ref):
    s = s_ref[...]
    p = jnp.exp2(s - jnp.max(s, axis=2, keepdims=True))
    p_ref[...] = p.astype(p_ref.dtype)
    return 1.0 / jnp.sum(p, axis=2, keepdims=True)


def _na_kernel(q_ref, k_ref, v_ref, bias_ref, o_ref, s_ref, p_ref):
    row = pl.program_id(1)
    rs = jnp.clip(row - NA_ROWS // 2, 0, GRID_H - NA_ROWS)
    start = pl.multiple_of(CTX + rs * GRID_W, GRID_W)
    low = lax.broadcasted_iota(jnp.int32, (1, 2 * B_DH), 1) < B_DH
    for hp in range(NPAIR):
        lo, hi = hp * 2 * B_DH, (hp + 1) * 2 * B_DH
        q2 = _stack_pair(q_ref[0, :, lo:hi], low)
        s_ref[hp, :, 0:NA_NK] = lax.dot_general(q2, k_ref[0, pl.ds(start, NA_NK), lo:hi], NT_DIMS,
                                                preferred_element_type=F32) + bias_ref[0, 0, hp]
        s_ref[hp, :, NA_NK:] = lax.dot_general(q2, k_ref[0, 0:CTX, lo:hi], NT_DIMS,
                                               preferred_element_type=F32)
    rinv = _softmax_rows(s_ref, p_ref)
    for hp in range(NPAIR):
        lo, hi = hp * 2 * B_DH, (hp + 1) * 2 * B_DH
        o2 = (jnp.dot(p_ref[hp, :, 0:NA_NK], v_ref[0, pl.ds(start, NA_NK), lo:hi],
                      preferred_element_type=F32)
              + jnp.dot(p_ref[hp, :, NA_NK:], v_ref[0, 0:CTX, lo:hi], preferred_element_type=F32))
        o2 = o2 * rinv[hp]
        o_ref[0, :, lo:hi] = jnp.where(low, o2[0:NA_NQ], o2[NA_NQ:]).astype(o_ref.dtype)


def _na_pattern(row):
    edge = NA_ROWS // 2
    return jnp.where(row < edge, row, jnp.where(row <= GRID_H - edge, edge, row - (GRID_H - 2 * edge)))


NA_NPAT = NA_ROWS


def _na_attention(p3, bias, layer):
    qoff = CTX // NA_NQ
    return pl.pallas_call(
        _na_kernel,
        grid=(BATCH, GRID_H),
        in_specs=[pl.BlockSpec((1, NA_NQ, D), lambda b, i: (b, i + qoff, PB_NQ)),
                  pl.BlockSpec((1, TT, D), lambda b, i: (b, 0, PB_NK)),
                  pl.BlockSpec((1, TT, D), lambda b, i: (b, 0, PB_NV)),
                  pl.BlockSpec((1, 1, NPAIR, 2 * NA_NQ, NA_NK),
                               lambda b, i: (layer, _na_pattern(i), 0, 0, 0))],
        out_specs=pl.BlockSpec((1, NA_NQ, D), lambda b, i: (b, i, 0)),
        out_shape=jax.ShapeDtypeStruct((BATCH, SEQ, D), BF16),
        scratch_shapes=[pltpu.VMEM((NPAIR, 2 * NA_NQ, NA_NK + CTX), F32),
                        pltpu.VMEM((NPAIR, 2 * NA_NQ, NA_NK + CTX), BF16)],
        compiler_params=_cp(("parallel", "arbitrary")),
        name="na_attention",
    )(p3, p3, p3, bias)


def _ctx_attn_kernel(q_ref, k_ref, v_ref, o_ref, s_ref, p_ref):
    low = lax.broadcasted_iota(jnp.int32, (1, 2 * B_DH), 1) < B_DH
    for hp in range(NPAIR):
        lo, hi = hp * 2 * B_DH, (hp + 1) * 2 * B_DH
        s_ref[hp] = lax.dot_general(_stack_pair(q_ref[0, :, lo:hi], low), k_ref[0, :, lo:hi], NT_DIMS,
                                    preferred_element_type=F32)
    rinv = _softmax_rows(s_ref, p_ref)
    for hp in range(NPAIR):
        lo, hi = hp * 2 * B_DH, (hp + 1) * 2 * B_DH
        o2 = jnp.dot(p_ref[hp], v_ref[0, :, lo:hi], preferred_element_type=F32) * rinv[hp]
        o_ref[0, :, lo:hi] = jnp.where(low, o2[0:CTX], o2[CTX:]).astype(o_ref.dtype)


def _ctx_attention(p3):
    return pl.pallas_call(
        _ctx_attn_kernel,
        grid=(BATCH,),
        in_specs=[pl.BlockSpec((1, CTX, D), lambda b: (b, 0, PB_NQ)),
                  pl.BlockSpec((1, CTX, D), lambda b: (b, 0, PB_NK)),
                  pl.BlockSpec((1, CTX, D), lambda b: (b, 0, PB_NV))],
        out_specs=pl.BlockSpec((1, CTX, D), lambda b: (b, 0, 0)),
        out_shape=jax.ShapeDtypeStruct((BATCH, CTX, D), BF16),
        scratch_shapes=[pltpu.VMEM((NPAIR, 2 * CTX, CTX), F32),
                        pltpu.VMEM((NPAIR, 2 * CTX, CTX), BF16)],
        compiler_params=_cp(("parallel",)),
        name="ctx_attention",
    )(p3, p3, p3)


def _merge_kernel(hf_ref, hbw_ref, o_ref, ga_ref, gb_ref, hnl_ref, hnc_ref, x_ref, mod_ref, ghn_ref,
                  wa_ref, wb_ref, wo_ref, g2_ref, x1_ref, h2_ref):
    hn = jnp.where(pl.program_id(1) == 0, hnc_ref[0], hnl_ref[0])
    hs = hf_ref[0].astype(F32) + hbw_ref[0].astype(F32)
    parts = []
    for h in range(A_HEADS):
        seg = hs[:, h * A_DV:(h + 1) * A_DV]
        mu = jnp.mean(seg, axis=-1, keepdims=True)
        cen = seg - mu
        var = jnp.mean(cen * cen, axis=-1, keepdims=True)
        parts.append(cen * lax.rsqrt(var + EPS))
    ya = jnp.concatenate(parts, axis=1) * ghn_ref[...] * _sigmoid(o_ref[0].astype(F32))
    a = jnp.dot(ya.astype(BF16), wa_ref[...], preferred_element_type=F32)
    bm = jnp.dot(hn, wb_ref[...], preferred_element_type=F32)
    mrg = _sigmoid(ga_ref[0].astype(F32)) * a + _sigmoid(gb_ref[0].astype(F32)) * bm
    y = jnp.dot(mrg.astype(BF16), wo_ref[...], preferred_element_type=F32)
    mod = mod_ref[0]
    x1 = x_ref[0] + mod[2:3] * y
    x1_ref[0] = x1
    h2_ref[0] = _norm_mod(x1, g2_ref[...], mod[4:5], mod[3:4]).astype(h2_ref.dtype)


def _merge(hf, hbw, p3, hb_lat, hb_ctx, x, mod, ghn, wa, wb, wo, g2, h2_dtype):
    row = lambda b, i: (b, i, 0)
    full = lambda b, i: (0, 0)
    return pl.pallas_call(
        _merge_kernel,
        grid=(BATCH, NRB),
        in_specs=[pl.BlockSpec((1, RB, A_V), row),
                  pl.BlockSpec((1, RB, A_V), row),
                  pl.BlockSpec((1, RB, D), lambda b, i: (b, i, PB_O)),
                  pl.BlockSpec((1, RB, D), lambda b, i: (b, i, PB_GA)),
                  pl.BlockSpec((1, RB, D), lambda b, i: (b, i, PB_GB)),
                  pl.BlockSpec((1, RB, D), lambda b, i: (b, jnp.maximum(i - 1, 0), 0)),
                  pl.BlockSpec((1, CTX, D), lambda b, i: (b, 0, 0)),
                  pl.BlockSpec((1, RB, D), row),
                  pl.BlockSpec((1, 6, D), lambda b, i: (_mod_row(b, i), 0, 0)),
                  pl.BlockSpec((1, A_V), full),
                  pl.BlockSpec((A_V, D), full),
                  pl.BlockSpec((D, D), full),
                  pl.BlockSpec((D, D), full),
                  pl.BlockSpec((1, D), full)],
        out_specs=[pl.BlockSpec((1, RB, D), row), pl.BlockSpec((1, RB, D), row)],
        out_shape=[jax.ShapeDtypeStruct((BATCH, TT, D), F32),
                   jax.ShapeDtypeStruct((BATCH, TT, D), h2_dtype)],
        compiler_params=_cp(("parallel", "parallel")),
        name="merge",
    )(hf, hbw, p3, p3, p3, hb_lat, hb_ctx, x, mod, ghn, wa, wb, wo, g2)


FFN_CH = D_FF // 2


def _ffn_kernel(h_ref, x_ref, mod_ref, w1_ref, w3_ref, w2_ref, gn_ref, modn_ref, x2_ref, hn_ref):
    h = h_ref[0]
    y = None
    for cidx in range(D_FF // FFN_CH):
        lo, hi = cidx * FFN_CH, (cidx + 1) * FFN_CH
        a = jnp.dot(h, w1_ref[:, lo:hi], preferred_element_type=F32)
        g = jnp.dot(h, w3_ref[:, lo:hi], preferred_element_type=F32)
        part = jnp.dot((_silu(a) * g).astype(BF16), w2_ref[lo:hi, :], preferred_element_type=F32)
        y = part if y is None else y + part
    mod = mod_ref[0]
    x2 = x_ref[0] + mod[5:6] * y
    x2_ref[0] = x2
    modn = modn_ref[0]
    hn_ref[0] = _norm_mod(x2, gn_ref[...], modn[1:2], modn[0:1]).astype(hn_ref.dtype)


def _dense_ffn(h2, x1, mod, w1, w3, w2, gn, modn):
    row = lambda b, i: (b, i, 0)
    full = lambda b, i: (0, 0)
    modspec = pl.BlockSpec((1, 6, D), lambda b, i: (_mod_row(b, i), 0, 0))
    return pl.pallas_call(
        _ffn_kernel,
        grid=(BATCH, NRB),
        in_specs=[pl.BlockSpec((1, RB, D), row), pl.BlockSpec((1, RB, D), row), modspec,
                  pl.BlockSpec((D, D_FF), full), pl.BlockSpec((D, D_FF), full),
                  pl.BlockSpec((D_FF, D), full), pl.BlockSpec((1, D), full), modspec],
        out_specs=[pl.BlockSpec((1, RB, D), row), pl.BlockSpec((1, RB, D), row)],
        out_shape=[jax.ShapeDtypeStruct((BATCH, TT, D), F32),
                   jax.ShapeDtypeStruct((BATCH, TT, D), BF16)],
        compiler_params=_cp(("parallel", "parallel")),
        name="dense_ffn",
    )(h2, x1, mod, w1, w3, w2, gn, modn)


def _router_kernel(h_ref, wr_ref, route_ref, cnt_ref, run_ref):
    i = pl.program_id(0)

    @pl.when(i == 0)
    def _():
        run_ref[...] = jnp.zeros_like(run_ref)

    logits = jnp.dot(h_ref[...].astype(BF16), wr_ref[...], preferred_element_type=F32)
    lane = lax.broadcasted_iota(jnp.int32, (RB, LANES), 1).astype(F32)
    lg = jnp.where(lane < N_EXPERTS, logits, -jnp.inf)
    v1 = jnp.max(lg, axis=1, keepdims=True)
    i1 = jnp.min(jnp.where(lg == v1, lane, float(LANES)), axis=1, keepdims=True)
    lg2 = jnp.where(lane == i1, -jnp.inf, lg)
    v2 = jnp.max(lg2, axis=1, keepdims=True)
    i2 = jnp.min(jnp.where(lg2 == v2, lane, float(LANES)), axis=1, keepdims=True)
    e = jnp.exp(v2 - v1)
    w1 = 1.0 / (1.0 + e)
    w2 = e / (1.0 + e)
    oh1 = (lane == i1).astype(F32)
    oh2 = (lane == i2).astype(F32)
    r = lax.broadcasted_iota(jnp.int32, (RB, RB), 0)
    c = lax.broadcasted_iota(jnp.int32, (RB, RB), 1)
    tri = (r > c).astype(BF16)
    cs1 = jnp.dot(tri, oh1.astype(BF16), preferred_element_type=F32)
    cs2 = jnp.dot(tri, oh2.astype(BF16), preferred_element_type=F32)
    tot1 = jnp.sum(oh1, axis=0, keepdims=True)
    tot2 = jnp.sum(oh2, axis=0, keepdims=True)
    run = run_ref[...]
    rank1 = jnp.sum(oh1 * (run + cs1), axis=1, keepdims=True)
    rank2 = jnp.sum(oh2 * (run + tot1 + cs2), axis=1, keepdims=True)
    new_run = run + tot1 + tot2
    run_ref[...] = new_run
    cnt_ref[...] = new_run
    out = jnp.where(lane == 0, i1,
          jnp.where(lane == 1, i2,
          jnp.where(lane == 2, w1,
          jnp.where(lane == 3, w2,
          jnp.where(lane == 4, rank1,
          jnp.where(lane == 5, rank2, 0.0))))))
    route_ref[...] = out


def _router(h2_flat, wr, nblk, blk_of):
    return pl.pallas_call(
        _router_kernel,
        grid=(nblk,),
        in_specs=[pl.BlockSpec((RB, D), lambda i: (blk_of(i), 0)),
                  pl.BlockSpec((D, LANES), lambda i: (0, 0))],
        out_specs=[pl.BlockSpec((RB, LANES), lambda i: (i, 0)),
                   pl.BlockSpec((1, LANES), lambda i: (0, 0))],
        out_shape=[jax.ShapeDtypeStruct((nblk * RB, LANES), F32),
                   jax.ShapeDtypeStruct((1, LANES), F32)],
        scratch_shapes=[pltpu.VMEM((1, LANES), F32)],
        compiler_params=_cp(("arbitrary",)),
        name="moe_router",
    )(h2_flat, wr)


DMA_UNROLL = 8
N_ZERO = 2 * N_EXPERTS


def _dispatch_kernel(pos_ref, zstart_ref, h_ref, o_ref, zbuf, sem, zsem):
    @pl.when(pl.program_id(0) == 0)
    def _():
        zbuf[...] = jnp.zeros_like(zbuf)
        for z in range(N_ZERO):
            @pl.when(zstart_ref[z] >= 0)
            def _():
                zs = pl.multiple_of(zstart_ref[z], MOE_TM)
                pltpu.make_async_copy(zbuf, o_ref.at[pl.ds(zs, MOE_TM)], zsem).start()
        for z in range(N_ZERO):
            @pl.when(zstart_ref[z] >= 0)
            def _():
                pltpu.make_async_copy(zbuf, o_ref.at[pl.ds(0, MOE_TM)], zsem).wait()

    def issue(t, carry):
        src = h_ref.at[pl.ds(t, 1)]
        pltpu.make_async_copy(src, o_ref.at[pl.ds(pos_ref[0, 0, t], 1)], sem).start()
        pltpu.make_async_copy(src, o_ref.at[pl.ds(pos_ref[0, 0, RB + t], 1)], sem).start()
        return carry

    lax.fori_loop(0, RB, issue, 0, unroll=DMA_UNROLL)
    for _ in range(2):
        pltpu.make_async_copy(h_ref, o_ref.at[pl.ds(0, RB)], sem).wait()


def _dispatch(pos, zstart, h2_flat, nblk, blk_of, n_sorted):
    return pl.pallas_call(
        _dispatch_kernel,
        grid=(nblk,),
        in_specs=[pl.BlockSpec((1, 1, 2 * RB), lambda i: (i, 0, 0), memory_space=pltpu.SMEM),
                  pl.BlockSpec(memory_space=pltpu.SMEM),
                  pl.BlockSpec((RB, D), lambda i: (blk_of(i), 0))],
        out_specs=pl.BlockSpec(memory_space=pl.ANY),
        out_shape=jax.ShapeDtypeStruct((n_sorted, D), F32),
        scratch_shapes=[pltpu.VMEM((MOE_TM, D), F32), pltpu.SemaphoreType.DMA(()),
                        pltpu.SemaphoreType.DMA(())],
        compiler_params=_cp(("arbitrary",)),
        name="moe_dispatch",
    )(pos, zstart, h2_flat)


def _gmm_kernel(be_ref, nb_ref, a_ref, w1_ref, w3_ref, w2_ref, o_ref, abf_ref):
    del be_ref
    i = pl.program_id(0)
    f = pl.program_id(1)

    @pl.when((i < nb_ref[0]) & (f == 0))
    def _():
        abf_ref[...] = a_ref[...].astype(BF16)

    @pl.when(i < nb_ref[0])
    def _():
        a = abf_ref[...]
        gs = []
        for c in range(MOE_FC // MOE_SUB):
            lo, hi = c * MOE_SUB, (c + 1) * MOE_SUB
            h1 = jnp.dot(a, w1_ref[0, :, lo:hi].astype(BF16), preferred_element_type=F32)
            h3 = jnp.dot(a, w3_ref[0, :, lo:hi].astype(BF16), preferred_element_type=F32)
            gs.append((_silu(h1) * h3).astype(BF16))
        part = jnp.dot(jnp.concatenate(gs, axis=1), w2_ref[0].astype(BF16), preferred_element_type=F32)

        @pl.when(f == 0)
        def _():
            o_ref[...] = part

        @pl.when(f > 0)
        def _():
            o_ref[...] += part

    @pl.when((i >= nb_ref[0]) & (f == 0))
    def _():
        o_ref[...] = jnp.zeros_like(o_ref)


def _expert_ffn(blk_e, nb, hs, w1, w3, w2, j):
    nbmax = hs.shape[0] // MOE_TM
    nf = D_FF_EXPERT // MOE_FC

    def ieff(i, nb_ref):
        return jnp.minimum(i, nb_ref[0] - 1)

    def feff(i, f, nb_ref):
        return jnp.where(i < nb_ref[0], f, nf - 1)

    grid_spec = pltpu.PrefetchScalarGridSpec(
        num_scalar_prefetch=2,
        grid=(nbmax, nf),
        in_specs=[pl.BlockSpec((MOE_TM, D), lambda i, f, be, nbr: (ieff(i, nbr), 0)),
                  pl.BlockSpec((None, 1, D, MOE_FC),
                               lambda i, f, be, nbr: (j, be[ieff(i, nbr)], 0, feff(i, f, nbr))),
                  pl.BlockSpec((None, 1, D, MOE_FC),
                               lambda i, f, be, nbr: (j, be[ieff(i, nbr)], 0, feff(i, f, nbr))),
                  pl.BlockSpec((None, 1, MOE_FC, D),
                               lambda i, f, be, nbr: (j, be[ieff(i, nbr)], feff(i, f, nbr), 0))],
        out_specs=pl.BlockSpec((MOE_TM, D), lambda i, f, be, nbr: (i, 0)),
        scratch_shapes=[pltpu.VMEM((MOE_TM, D), BF16)],
    )
    return pl.pallas_call(
        _gmm_kernel,
        grid_spec=grid_spec,
        out_shape=jax.ShapeDtypeStruct(hs.shape, F32),
        compiler_params=_cp(("arbitrary", "arbitrary")),
        name="moe_expert_ffn",
    )(blk_e, nb, hs, w1, w3, w2)


def _combine_kernel(pos_ref, y_ref, route_ref, x_ref, mod_ref, gn_ref, modn_ref,
                    x2_ref, hn_ref, ybuf, sem):
    def issue(t, carry):
        pltpu.make_async_copy(y_ref.at[pl.ds(pos_ref[0, 0, t], 1)], ybuf.at[0, pl.ds(t, 1)], sem).start()
        pltpu.make_async_copy(y_ref.at[pl.ds(pos_ref[0, 0, RB + t], 1)], ybuf.at[1, pl.ds(t, 1)], sem).start()
        return carry

    lax.fori_loop(0, RB, issue, 0, unroll=DMA_UNROLL)
    pltpu.make_async_copy(y_ref.at[pl.ds(0, RB)], ybuf.at[0], sem).wait()
    pltpu.make_async_copy(y_ref.at[pl.ds(0, RB)], ybuf.at[1], sem).wait()
    route = route_ref[...]
    y = route[:, 2:3] * ybuf[0] + route[:, 3:4] * ybuf[1]
    mod = mod_ref[0]
    x2 = x_ref[0] + mod[5:6] * y
    x2_ref[0] = x2
    modn = modn_ref[0]
    hn_ref[0] = _norm_mod(x2, gn_ref[...], modn[1:2], modn[0:1]).astype(hn_ref.dtype)


def _combine(pos, ys, route, x1, mod, gn, modn, latent_only, hn_dtype):
    nrb = NRB - 1 if latent_only else NRB
    off = 1 if latent_only else 0
    rows = nrb * RB
    full = lambda b, i: (0, 0)
    modspec = pl.BlockSpec((1, 6, D), lambda b, i: (_mod_row(b, i + off), 0, 0))
    return pl.pallas_call(
        _combine_kernel,
        grid=(BATCH, nrb),
        in_specs=[pl.BlockSpec((1, 1, 2 * RB), lambda b, i: (b * nrb + i, 0, 0), memory_space=pltpu.SMEM),
                  pl.BlockSpec(memory_space=pl.ANY),
                  pl.BlockSpec((RB, LANES), lambda b, i: (b * nrb + i, 0)),
                  pl.BlockSpec((1, RB, D), lambda b, i: (b, i + off, 0)),
                  modspec,
                  pl.BlockSpec((1, D), full),
                  modspec],
        out_specs=[pl.BlockSpec((1, RB, D), lambda b, i: (b, i, 0)),
                   pl.BlockSpec((1, RB, D), lambda b, i: (b, i, 0))],
        out_shape=[jax.ShapeDtypeStruct((BATCH, rows, D), F32),
                   jax.ShapeDtypeStruct((BATCH, rows, D), hn_dtype)],
        scratch_shapes=[pltpu.VMEM((2, RB, D), F32), pltpu.SemaphoreType.DMA(())],
        compiler_params=_cp(("arbitrary", "arbitrary")),
        name="moe_combine",
    )(pos, ys, route, x1, mod, gn, modn)


def _moe_ffn(h2, x1, mod, wr, w1, w3, w2, j, gn, modn, latent_only, hn_dtype):
    h2_flat = h2.reshape(NTOK, D)
    if latent_only:
        nrb = NRB - 1
        blk_of = lambda i: (i // nrb) * NRB + (i % nrb) + 1
    else:
        nrb = NRB
        blk_of = lambda i: i
    nblk = BATCH * nrb
    n_pairs = 2 * nblk * RB
    nbmax = n_pairs // MOE_TM + N_EXPERTS
    n_sorted = nbmax * MOE_TM

    route, counts = _router(h2_flat, wr, nblk, blk_of)
    cnt = counts[0, :N_EXPERTS].astype(jnp.int32)
    gsz = ((cnt + MOE_TM - 1) // MOE_TM) * MOE_TM
    gend = jnp.cumsum(gsz)
    goff = gend - gsz
    e12 = route[:, 0:2].astype(jnp.int32)
    rank = route[:, 4:6].astype(jnp.int32)
    pos = goff[e12] + rank
    pos = pos.reshape(nblk, RB, 2).transpose(0, 2, 1).reshape(nblk, 1, 2 * RB)
    nb = (gend[-1] // MOE_TM).reshape(1)
    blk_start = jnp.arange(nbmax, dtype=jnp.int32) * MOE_TM
    blk_e = jnp.minimum(jnp.sum(blk_start[:, None] >= gend[None, :], axis=1), N_EXPERTS - 1).astype(jnp.int32)

    trail = (nbmax - N_EXPERTS + jnp.arange(N_EXPERTS, dtype=jnp.int32)) * MOE_TM
    zstart = jnp.concatenate([jnp.where(cnt > 0, gend - MOE_TM, -1),
                              jnp.where(trail >= gend[-1], trail, -1)]).astype(jnp.int32)

    hs = _dispatch(pos, zstart, h2_flat, nblk, blk_of, n_sorted)
    ys = _expert_ffn(blk_e, nb, hs, w1, w3, w2, j)
    return _combine(pos, ys, route, x1, mod, gn, modn, latent_only, hn_dtype)


def _rope_tables():
    quarter = A_DQK // 4
    inv = 1.0 / (ROPE_BASE ** (jnp.arange(quarter, dtype=F32) / quarter))
    pos = jnp.arange(SEQ)
    rows = (pos // GRID_W).astype(F32)
    cols = (pos % GRID_W).astype(F32)
    ang_r = rows[:, None] * inv[None, :]
    ang_c = cols[:, None] * inv[None, :]
    cos = jnp.concatenate([jnp.cos(ang_r)] * 2 + [jnp.cos(ang_c)] * 2, axis=1)
    sin = jnp.concatenate([-jnp.sin(ang_r), jnp.sin(ang_r), -jnp.sin(ang_c), jnp.sin(ang_c)], axis=1)
    cos = jnp.concatenate([jnp.ones((CTX, A_DQK), F32), cos], axis=0)
    sin = jnp.concatenate([jnp.zeros((CTX, A_DQK), F32), sin], axis=0)
    kscale = A_DQK ** -0.5
    return (jnp.concatenate([cos, cos * kscale], axis=1),
            jnp.concatenate([sin, sin * kscale], axis=1))


def _na_bias_tables(rpb):
    rs = np.clip(np.arange(GRID_H) - NA_ROWS // 2, 0, GRID_H - NA_ROWS)
    cs = np.clip(np.arange(GRID_W) - NA_COLS // 2, 0, GRID_W - NA_COLS)
    rows = [0, 1, 2, 3, NA_ROWS // 2, GRID_H - 3, GRID_H - 2, GRID_H - 1]
    col = np.arange(GRID_W)
    dc = np.clip(col[None, :] - col[:, None] + NA_COLS - 1, 0, 2 * NA_COLS - 2)
    valid_c = (col[None, :] >= cs[:, None]) & (col[None, :] < cs[:, None] + NA_COLS)
    sel_c = np.eye(2 * NA_COLS - 1, dtype=np.float32)[dc]
    t = jnp.einsum('lhab,uvb->lhuav', rpb * LOG2E, jnp.asarray(sel_c),
                   precision=lax.Precision.HIGHEST)
    neg = np.where(valid_c, 0.0, -np.inf).astype(np.float32)[:, None, :]
    nl = rpb.shape[0]
    t = (t + jnp.asarray(neg)).reshape(nl, NPAIR, 2 * NA_NQ, (2 * NA_ROWS - 1) * GRID_W)
    first = [rs[r] - r + NA_ROWS - 1 for r in rows]
    return jnp.stack([t[..., d0 * GRID_W:d0 * GRID_W + NA_NK] for d0 in first], axis=1)


def kernel(x, c, ctx, c_ctx, w_mod, b_mod, g_norm1, g_norm2, w_in, a_conv, a_gate_b, a_hnorm_g, na_rpb,
           w_br_a, w_br_b, w_out, ffn_w1, ffn_w3, ffn_w2, moe_router, moe_w1, moe_w3, moe_w2, g_final):
    cc = jnp.concatenate([c, c_ctx[None, :], jnp.zeros((16 - BATCH - 1, D), F32)], axis=0)
    mod_all = _modulation(cc, w_mod, b_mod).reshape(DEPTH, 16, 6, D)[:, :BATCH + 1]
    mod_zero = jnp.zeros((BATCH + 1, 6, D), F32)
    rope_c, rope_s = _rope_tables()
    na_bias = _na_bias_tables(na_rpb)

    xs = jnp.concatenate([ctx, x], axis=1)
    h1 = _first_norm(xs, g_norm1[0][None, :], mod_all[0])
    out = None
    for l in range(DEPTH):
        last = l == DEPTH - 1
        mod = mod_all[l]
        wl = w_in[l]
        g0 = 3 * D
        g1 = g0 + NGATE
        wp = jnp.concatenate([wl[:, :g0], wl[:, g1:g1 + D] * (B_DH ** -0.5 * LOG2E), wl[:, g1 + D:]],
                             axis=1).astype(BF16)
        wg = jnp.pad(wl[:, g0:g1], ((0, 0), (0, LANES - NGATE))).astype(BF16)
        bg = jnp.pad(a_gate_b[l], (0, LANES - NGATE))[None, :]

        h1_flat = h1.reshape(NTOK, D)
        p3 = _in_proj(h1_flat, wp).reshape(BATCH, TT, P_COLS)
        gates, lfp = _gate_proj(h1_flat, wg, bg)

        qk = _mlstm_prep(p3, a_conv[l], rope_c, rope_s)
        hf, hbw = _mlstm_scan(qk, p3, gates.reshape(BATCH, TT, LANES), lfp.reshape(BATCH, TT, LANES))
        hb_lat = _na_attention(p3, na_bias, l)
        hb_ctx = _ctx_attention(p3)

        moe = l % 2 == 1
        x1, h2 = _merge(hf, hbw, p3, hb_lat, hb_ctx, xs, mod, a_hnorm_g[l][None, :],
                        w_br_a[l].astype(BF16), w_br_b[l].astype(BF16), w_out[l].astype(BF16),
                        g_norm2[l][None, :], F32 if moe else BF16)
        if last:
            gn, modn = g_final[None, :], mod_zero
        else:
            gn, modn = g_norm1[l + 1][None, :], mod_all[l + 1]
        j = l // 2
        if not moe:
            xs, h1 = _dense_ffn(h2, x1, mod, ffn_w1[j].astype(BF16), ffn_w3[j].astype(BF16),
                                ffn_w2[j].astype(BF16), gn, modn)
        else:
            wr = jnp.pad(moe_router[j], ((0, 0), (0, LANES - N_EXPERTS))).astype(BF16)
            xs, h1 = _moe_ffn(h2, x1, mod, wr, moe_w1, moe_w3, moe_w2, j, gn, modn, last,
                              F32 if last else BF16)
            if last:
                out = h1
    return out
```

```python
import functools

import numpy as np
import jax
import jax.numpy as jnp
from jax import lax
from jax.experimental import pallas as pl
from jax.experimental.pallas import tpu as pltpu

F32 = jnp.float32
BF16 = jnp.bfloat16

D = 1024
BATCH = 8
SEQ = 2048
CTX = 256
TT = CTX + SEQ
NTOK = BATCH * TT
DEPTH = 4
GRID_W = 64
GRID_H = SEQ // GRID_W

A_HEADS = 4
A_DQK = 128
A_DV = 256
A_QK = A_HEADS * A_DQK
A_V = A_HEADS * A_DV
ROPE_BASE = 10000.0
LCH = 256
NCH = TT // LCH

B_HEADS = 16
B_DH = 64
NA_ROWS = 8
NA_COLS = 16
NA_QROWS = 1
NA_KROWS = NA_QROWS + NA_ROWS - 1
NA_NQ = NA_QROWS * GRID_W
NA_NK = NA_KROWS * GRID_W
NA_STEPS = GRID_H // NA_QROWS

D_FF = 2816
N_EXPERTS = 8
D_FF_EXPERT = 3584
EPS = 1e-6
LOG2E = 1.4426950408889634

RB = 256
NRB = TT // RB
MM_TM = 2048
MOE_TM = 1024
MOE_FC = 512
MOE_SUB = 256
LANES = 128

PB_QK, PB_V, PB_O, PB_NQ, PB_NK, PB_NV, PB_GA, PB_GB = range(8)
P_COLS = 8 * D

VMEM_LIMIT = 56 * 1024 * 1024


def _cp(sem, vmem=VMEM_LIMIT):
    return pltpu.CompilerParams(dimension_semantics=sem, vmem_limit_bytes=vmem)


def _sigmoid(x):
    return 1.0 / (1.0 + jnp.exp(-x))


def _silu(x):
    return x * _sigmoid(x)


def _log_sigmoid(x):
    return jnp.minimum(x, 0.0) - jnp.log(1.0 + jnp.exp(-jnp.abs(x)))


def _norm_mod(x, g, sc, sh):
    ms = jnp.mean(x * x, axis=-1, keepdims=True)
    y = x * lax.rsqrt(ms + EPS)
    return (y * g) * (1.0 + sc) + sh


def _mod_row(b, i):
    return jnp.where(i == 0, BATCH, b)


def _mod_kernel(c_ref, w_ref, b_ref, o_ref):
    c = c_ref[...]
    s = _silu(c).astype(BF16)
    o_ref[0] = jnp.dot(s, w_ref[0].astype(BF16), preferred_element_type=F32) + b_ref[0]


def _modulation(cc, w_mod, b_mod):
    tn = 2048
    nl = w_mod.shape[0]
    return pl.pallas_call(
        _mod_kernel,
        grid=(nl, 6 * D // tn),
        in_specs=[pl.BlockSpec((16, D), lambda l, j: (0, 0)),
                  pl.BlockSpec((1, D, tn), lambda l, j: (l, 0, j)),
                  pl.BlockSpec((1, 1, tn), lambda l, j: (l, 0, j))],
        out_specs=pl.BlockSpec((1, 16, tn), lambda l, j: (l, 0, j)),
        out_shape=jax.ShapeDtypeStruct((nl, 16, 6 * D), F32),
        compiler_params=_cp(("parallel", "parallel")),
        name="modulation",
    )(cc, w_mod, b_mod.reshape(nl, 1, 6 * D))


def _norm_kernel(x_ref, g_ref, mod_ref, o_ref):
    mod = mod_ref[0]
    o_ref[0] = _norm_mod(x_ref[0], g_ref[...], mod[1:2], mod[0:1]).astype(o_ref.dtype)


def _first_norm(x, g, mod):
    return pl.pallas_call(
        _norm_kernel,
        grid=(BATCH, NRB),
        in_specs=[pl.BlockSpec((1, RB, D), lambda b, i: (b, i, 0)),
                  pl.BlockSpec((1, D), lambda b, i: (0, 0)),
                  pl.BlockSpec((1, 6, D), lambda b, i: (_mod_row(b, i), 0, 0))],
        out_specs=pl.BlockSpec((1, RB, D), lambda b, i: (b, i, 0)),
        out_shape=jax.ShapeDtypeStruct((BATCH, TT, D), BF16),
        compiler_params=_cp(("parallel", "parallel")),
        name="first_norm",
    )(x, g, mod)


def _mm_kernel(a_ref, w_ref, o_ref):
    o_ref[...] = jnp.dot(a_ref[...], w_ref[...], preferred_element_type=F32).astype(o_ref.dtype)


def _in_proj(h, w):
    tn = 1024
    return pl.pallas_call(
        _mm_kernel,
        grid=(P_COLS // tn, NTOK // MM_TM),
        in_specs=[pl.BlockSpec((MM_TM, D), lambda j, i: (i, 0)),
                  pl.BlockSpec((D, tn), lambda j, i: (0, j))],
        out_specs=pl.BlockSpec((MM_TM, tn), lambda j, i: (i, j)),
        out_shape=jax.ShapeDtypeStruct((NTOK, P_COLS), BF16),
        compiler_params=_cp(("parallel", "parallel")),
        name="in_proj",
    )(h, w)


NGATE = 4 * A_HEADS


def _gate_kernel(a_ref, w_ref, b_ref, g_ref, lf_ref):
    g = jnp.dot(a_ref[...], w_ref[...], preferred_element_type=F32) + b_ref[...]
    g_ref[...] = g
    lf = _log_sigmoid(g)
    p0 = lf.astype(BF16).astype(F32)
    r1 = lf - p0
    p1 = r1.astype(BF16).astype(F32)
    p2 = (r1 - p1).astype(BF16).astype(F32)
    lane = lax.broadcasted_iota(jnp.int32, (1, LANES), 1)
    parts = jnp.where(lane < NGATE, p0,
                      jnp.where(lane < 2 * NGATE, pltpu.roll(p1, NGATE, 1),
                                jnp.where(lane < 3 * NGATE, pltpu.roll(p2, 2 * NGATE, 1), 0.0)))
    lf_ref[...] = parts.astype(BF16)


def _gate_proj(h, wg, bg):
    return pl.pallas_call(
        _gate_kernel,
        grid=(NTOK // MM_TM,),
        in_specs=[pl.BlockSpec((MM_TM, D), lambda i: (i, 0)),
                  pl.BlockSpec((D, LANES), lambda i: (0, 0)),
                  pl.BlockSpec((1, LANES), lambda i: (0, 0))],
        out_specs=[pl.BlockSpec((MM_TM, LANES), lambda i: (i, 0)),
                   pl.BlockSpec((MM_TM, LANES), lambda i: (i, 0))],
        out_shape=[jax.ShapeDtypeStruct((NTOK, LANES), F32),
                   jax.ShapeDtypeStruct((NTOK, LANES), BF16)],
        compiler_params=_cp(("parallel",)),
        name="gate_proj",
    )(h, wg, bg)


def _prep_kernel(u_ref, up_ref, un_ref, w_ref, c_ref, s_ref, o_ref):
    i = pl.program_id(1)
    u = u_ref[0].astype(F32)
    prev_row = jnp.where(i >= 2, up_ref[0, 15:16, :].astype(F32), 0.0)
    next_row = jnp.where((i >= 1) & (i <= NRB - 2), un_ref[0, 0:1, :].astype(F32), 0.0)
    rid = lax.broadcasted_iota(jnp.int32, (RB, 1), 0)
    u_m1 = jnp.where(rid == 0, prev_row, pltpu.roll(u, 1, 0))
    u_p1 = jnp.where(rid == RB - 1, next_row, pltpu.roll(u, RB - 1, 0))
    w = w_ref[...]
    y = w[0:1] * u_m1 + w[1:2] * u + w[2:3] * u_p1
    y = _silu(y)
    c = c_ref[...]
    s = s_ref[...]
    cfull = jnp.concatenate([c[:, :A_DQK]] * A_HEADS + [c[:, A_DQK:]] * A_HEADS, axis=1)
    sfull = jnp.concatenate([s[:, :A_DQK]] * A_HEADS + [s[:, A_DQK:]] * A_HEADS, axis=1)
    lane = lax.broadcasted_iota(jnp.int32, (1, 2 * A_QK), 1)
    partner = jnp.where((lane & 32) == 0,
                        pltpu.roll(y, 2 * A_QK - 32, 1), pltpu.roll(y, 32, 1))
    o_ref[0] = (y * cfull + partner * sfull).astype(o_ref.dtype)


def _mlstm_prep(p3, conv_w, rope_c, rope_s):
    nb16 = TT // 16
    return pl.pallas_call(
        _prep_kernel,
        grid=(BATCH, NRB),
        in_specs=[pl.BlockSpec((1, RB, D), lambda b, i: (b, i, PB_QK)),
                  pl.BlockSpec((1, 16, D), lambda b, i: (b, jnp.maximum(i * (RB // 16) - 1, 0), PB_QK)),
                  pl.BlockSpec((1, 16, D), lambda b, i: (b, jnp.minimum((i + 1) * (RB // 16), nb16 - 1), PB_QK)),
                  pl.BlockSpec((3, D), lambda b, i: (0, 0)),
                  pl.BlockSpec((RB, 2 * A_DQK), lambda b, i: (i, 0)),
                  pl.BlockSpec((RB, 2 * A_DQK), lambda b, i: (i, 0))],
        out_specs=pl.BlockSpec((1, RB, D), lambda b, i: (b, i, 0)),
        out_shape=jax.ShapeDtypeStruct((BATCH, TT, D), BF16),
        compiler_params=_cp(("parallel", "parallel")),
        name="mlstm_prep",
    )(p3, p3, p3, conv_w, rope_c, rope_s)


def _mlstm_kernel(qkf_ref, vf_ref, gf_ref, lff_ref, qkb_ref, vb_ref, gb_ref, lfb_ref,
                  of_ref, ob_ref, ct_ref, n_ref, m_ref):
    @pl.when(pl.program_id(1) == 0)
    def _():
        ct_ref[...] = jnp.zeros_like(ct_ref)
        n_ref[...] = jnp.zeros_like(n_ref)
        m_ref[...] = jnp.zeros_like(m_ref)

    r = lax.broadcasted_iota(jnp.int32, (LCH, LCH), 0)
    c = lax.broadcasted_iota(jnp.int32, (LCH, LCH), 1)
    masks = [c <= r, c >= r]
    ins = [(qkf_ref, vf_ref, gf_ref, lff_ref, of_ref), (qkb_ref, vb_ref, gb_ref, lfb_ref, ob_ref)]
    b_all, g_all, b_t, g_t, b_end = [], [], [], [], []
    for d in range(2):
        tri = jnp.where(masks[d], 1.0, 0.0).astype(BF16)
        bc = jnp.dot(tri, ins[d][3][0], preferred_element_type=F32)
        ba = bc + pltpu.roll(bc, LANES - NGATE, 1) + pltpu.roll(bc, LANES - 2 * NGATE, 1)
        ga = ins[d][2][0]
        if d == 1:
            ba = pltpu.roll(ba, LANES - 2 * A_HEADS, 1)
            ga = pltpu.roll(ga, LANES - 2 * A_HEADS, 1)
        b_all.append(ba)
        g_all.append(ga)
        b_t.append(ba.T)
        g_t.append(ga.T)
        b_end.append(ba[LCH - 1:LCH, :] if d == 0 else ba[0:1, :])
    ones = jnp.ones((LCH, LANES), BF16)
    tn_dims = (((0,), (0,)), ((), ()))
    combos = [(d, h) for d in range(2) for h in range(A_HEADS)]
    idx = range(len(combos))
    qs = [ins[d][0][0, :, h * A_DQK:(h + 1) * A_DQK] for d, h in combos]
    ks = [ins[d][0][0, :, A_QK + h * A_DQK:A_QK + (h + 1) * A_DQK] for d, h in combos]
    qk = [lax.dot_general(qs[i], ks[i], NT_DIMS, preferred_element_type=F32) for i in idx]
    b_col = [b_all[d][:, A_HEADS + h:A_HEADS + h + 1] for d, h in combos]
    b_last = [b_end[d][:, A_HEADS + h:A_HEADS + h + 1] for d, h in combos]
    m_old = [m_ref[i][:, 0:1] for i in idx]
    r_row = [b_t[d][A_HEADS + h:A_HEADS + h + 1, :] - g_t[d][h:h + 1, :] for d, h in combos]
    m_row = [jnp.maximum(b_col[i] + m_old[i],
                         jnp.max(jnp.where(masks[d], b_col[i] - r_row[i], -jnp.inf), axis=1, keepdims=True))
             for i, (d, h) in enumerate(combos)]
    s = [(qk[i] * jnp.exp(jnp.where(masks[d], (b_col[i] - m_row[i]) - r_row[i], -jnp.inf))).astype(BF16)
         for i, (d, h) in enumerate(combos)]
    wq = [(jnp.exp(b_col[i] + m_old[i] - m_row[i]) * qs[i].astype(F32)).astype(BF16) for i in idx]
    g_col = [b_last[i] - b_col[i] + g_all[d][:, h:h + 1] for i, (d, h) in enumerate(combos)]
    m_new = [jnp.maximum(b_last[i] + m_old[i], jnp.max(g_col[i], axis=0, keepdims=True)) for i in idx]
    kw = [(jnp.exp(g_col[i] - m_new[i]) * ks[i].astype(F32)).astype(BF16) for i in idx]
    decay = [jnp.exp(b_last[i] + m_old[i] - m_new[i]) for i in idx]
    for i, (d, h) in enumerate(combos):
        v = ins[d][1][0, :, h * A_DV:(h + 1) * A_DV]
        lhs = jnp.concatenate([s[i], wq[i]], axis=1)
        den = jnp.dot(lhs, jnp.concatenate([ones, n_ref[i].astype(BF16)], axis=0),
                      preferred_element_type=F32)[:, 0:1]
        rinv = 1.0 / jnp.maximum(jnp.abs(den), jnp.exp(-m_row[i]))
        num = jnp.dot(lhs, jnp.concatenate([v, ct_ref[i].astype(BF16)], axis=0), preferred_element_type=F32)
        ins[d][4][0, :, h * A_DV:(h + 1) * A_DV] = (num * rinv).astype(of_ref.dtype)
    for i, (d, h) in enumerate(combos):
        v = ins[d][1][0, :, h * A_DV:(h + 1) * A_DV]
        ct_ref[i] = decay[i] * ct_ref[i] + lax.dot_general(kw[i], v, tn_dims, preferred_element_type=F32)
        n_ref[i] = decay[i] * n_ref[i] + lax.dot_general(kw[i], ones, tn_dims, preferred_element_type=F32)
        m_ref[i] = jnp.broadcast_to(m_new[i], (1, LANES))


def _rev_chunk(s):
    ncc = CTX // LCH
    return jnp.where(s < ncc, ncc - 1 - s, NCH + ncc - 1 - s)


def _mlstm_scan(qk, p3, gates, lfp):
    fwd = lambda b, s: (b, s, 0)
    bwd = lambda b, s: (b, _rev_chunk(s), 0)
    specs = lambda im, imv: [pl.BlockSpec((1, LCH, D), im), pl.BlockSpec((1, LCH, A_V), imv),
                             pl.BlockSpec((1, LCH, LANES), im), pl.BlockSpec((1, LCH, LANES), im)]
    nstate = 2 * A_HEADS
    return pl.pallas_call(
        _mlstm_kernel,
        grid=(BATCH, NCH),
        in_specs=(specs(fwd, lambda b, s: (b, s, PB_V))
                  + specs(bwd, lambda b, s: (b, _rev_chunk(s), PB_V))),
        out_specs=[pl.BlockSpec((1, LCH, A_V), fwd), pl.BlockSpec((1, LCH, A_V), bwd)],
        out_shape=[jax.ShapeDtypeStruct((BATCH, TT, A_V), BF16)] * 2,
        scratch_shapes=[pltpu.VMEM((nstate, A_DQK, A_DV), F32),
                        pltpu.VMEM((nstate, A_DQK, LANES), F32),
                        pltpu.VMEM((nstate, 1, LANES), F32)],
        compiler_params=_cp(("parallel", "arbitrary")),
        name="mlstm_scan",
    )(qk, p3, gates, lfp, qk, p3, gates, lfp)


NPAIR = B_HEADS // 2
NT_DIMS = (((1,), (1,)), ((), ()))


def _stack_pair(q, low):
    zero = jnp.zeros_like(q)
    return jnp.concatenate([jnp.where(low, q, zero), jnp.where(low, zero, q)], axis=0)


def _softmax_rows(s_ref, p_ref):
    s = s_ref[...]
    p = jnp.exp2(s - jnp.max(s, axis=2, keepdims=True))
    p_ref[...] = p.astype(p_ref.dtype)
    return 1.0 / jnp.sum(p, axis=2, keepdims=True)


def _na_kernel(q_ref, k_ref, v_ref, bias_ref, o_ref, s_ref, p_ref):
    row = pl.program_id(1)
    rs = jnp.clip(row - NA_ROWS // 2, 0, GRID_H - NA_ROWS)
    start = pl.multiple_of(CTX + rs * GRID_W, GRID_W)
    low = lax.broadcasted_iota(jnp.int32, (1, 2 * B_DH), 1) < B_DH
    for hp in range(NPAIR):
        lo, hi = hp * 2 * B_DH, (hp + 1) * 2 * B_DH
        q2 = _stack_pair(q_ref[0, :, lo:hi], low)
        s_ref[hp, :, 0:NA_NK] = lax.dot_general(q2, k_ref[0, pl.ds(start, NA_NK), lo:hi], NT_DIMS,
                                                preferred_element_type=F32) + bias_ref[0, 0, hp]
        s_ref[hp, :, NA_NK:] = lax.dot_general(q2, k_ref[0, 0:CTX, lo:hi], NT_DIMS,
                                               preferred_element_type=F32)
    rinv = _softmax_rows(s_ref, p_ref)
    for hp in range(NPAIR):
        lo, hi = hp * 2 * B_DH, (hp + 1) * 2 * B_DH
        o2 = (jnp.dot(p_ref[hp, :, 0:NA_NK], v_ref[0, pl.ds(start, NA_NK), lo:hi],
                      preferred_element_type=F32)
              + jnp.dot(p_ref[hp, :, NA_NK:], v_ref[0, 0:CTX, lo:hi], preferred_element_type=F32))
        o2 = o2 * rinv[hp]
        o_ref[0, :, lo:hi] = jnp.where(low, o2[0:NA_NQ], o2[NA_NQ:]).astype(o_ref.dtype)


def _na_pattern(row):
    edge = NA_ROWS // 2
    return jnp.where(row < edge, row, jnp.where(row <= GRID_H - edge, edge, row - (GRID_H - 2 * edge)))


NA_NPAT = NA_ROWS


def _na_attention(p3, bias, layer):
    qoff = CTX // NA_NQ
    return pl.pallas_call(
        _na_kernel,
        grid=(BATCH, GRID_H),
        in_specs=[pl.BlockSpec((1, NA_NQ, D), lambda b, i: (b, i + qoff, PB_NQ)),
                  pl.BlockSpec((1, TT, D), lambda b, i: (b, 0, PB_NK)),
                  pl.BlockSpec((1, TT, D), lambda b, i: (b, 0, PB_NV)),
                  pl.BlockSpec((1, 1, NPAIR, 2 * NA_NQ, NA_NK),
                               lambda b, i: (layer, _na_pattern(i), 0, 0, 0))],
        out_specs=pl.BlockSpec((1, NA_NQ, D), lambda b, i: (b, i, 0)),
        out_shape=jax.ShapeDtypeStruct((BATCH, SEQ, D), BF16),
        scratch_shapes=[pltpu.VMEM((NPAIR, 2 * NA_NQ, NA_NK + CTX), F32),
                        pltpu.VMEM((NPAIR, 2 * NA_NQ, NA_NK + CTX), BF16)],
        compiler_params=_cp(("parallel", "arbitrary")),
        name="na_attention",
    )(p3, p3, p3, bias)


def _ctx_attn_kernel(q_ref, k_ref, v_ref, o_ref, s_ref, p_ref):
    low = lax.broadcasted_iota(jnp.int32, (1, 2 * B_DH), 1) < B_DH
    for hp in range(NPAIR):
        lo, hi = hp * 2 * B_DH, (hp + 1) * 2 * B_DH
        s_ref[hp] = lax.dot_general(_stack_pair(q_ref[0, :, lo:hi], low), k_ref[0, :, lo:hi], NT_DIMS,
                                    preferred_element_type=F32)
    rinv = _softmax_rows(s_ref, p_ref)
    for hp in range(NPAIR):
        lo, hi = hp * 2 * B_DH, (hp + 1) * 2 * B_DH
        o2 = jnp.dot(p_ref[hp], v_ref[0, :, lo:hi], preferred_element_type=F32) * rinv[hp]
        o_ref[0, :, lo:hi] = jnp.where(low, o2[0:CTX], o2[CTX:]).astype(o_ref.dtype)


def _ctx_attention(p3):
    return pl.pallas_call(
        _ctx_attn_kernel,
        grid=(BATCH,),
        in_specs=[pl.BlockSpec((1, CTX, D), lambda b: (b, 0, PB_NQ)),
                  pl.BlockSpec((1, CTX, D), lambda b: (b, 0, PB_NK)),
                  pl.BlockSpec((1, CTX, D), lambda b: (b, 0, PB_NV))],
        out_specs=pl.BlockSpec((1, CTX, D), lambda b: (b, 0, 0)),
        out_shape=jax.ShapeDtypeStruct((BATCH, CTX, D), BF16),
        scratch_shapes=[pltpu.VMEM((NPAIR, 2 * CTX, CTX), F32),
                        pltpu.VMEM((NPAIR, 2 * CTX, CTX), BF16)],
        compiler_params=_cp(("parallel",)),
        name="ctx_attention",
    )(p3, p3, p3)


def _merge_kernel(hf_ref, hbw_ref, o_ref, ga_ref, gb_ref, hnl_ref, hnc_ref, x_ref, mod_ref, ghn_ref,
                  wa_ref, wb_ref, wo_ref, g2_ref, x1_ref, h2_ref):
    hn = jnp.where(pl.program_id(1) == 0, hnc_ref[0], hnl_ref[0])
    hs = hf_ref[0].astype(F32) + hbw_ref[0].astype(F32)
    parts = []
    for h in range(A_HEADS):
        seg = hs[:, h * A_DV:(h + 1) * A_DV]
        mu = jnp.mean(seg, axis=-1, keepdims=True)
        cen = seg - mu
        var = jnp.mean(cen * cen, axis=-1, keepdims=True)
        parts.append(cen * lax.rsqrt(var + EPS))
    ya = jnp.concatenate(parts, axis=1) * ghn_ref[...] * _sigmoid(o_ref[0].astype(F32))
    a = jnp.dot(ya.astype(BF16), wa_ref[...], preferred_element_type=F32)
    bm = jnp.dot(hn, wb_ref[...], preferred_element_type=F32)
    mrg = _sigmoid(ga_ref[0].astype(F32)) * a + _sigmoid(gb_ref[0].astype(F32)) * bm
    y = jnp.dot(mrg.astype(BF16), wo_ref[...], preferred_element_type=F32)
    mod = mod_ref[0]
    x1 = x_ref[0] + mod[2:3] * y
    x1_ref[0] = x1
    h2_ref[0] = _norm_mod(x1, g2_ref[...], mod[4:5], mod[3:4]).astype(h2_ref.dtype)


def _merge(hf, hbw, p3, hb_lat, hb_ctx, x, mod, ghn, wa, wb, wo, g2, h2_dtype):
    row = lambda b, i: (b, i, 0)
    full = lambda b, i: (0, 0)
    return pl.pallas_call(
        _merge_kernel,
        grid=(BATCH, NRB),
        in_specs=[pl.BlockSpec((1, RB, A_V), row),
                  pl.BlockSpec((1, RB, A_V), row),
                  pl.BlockSpec((1, RB, D), lambda b, i: (b, i, PB_O)),
                  pl.BlockSpec((1, RB, D), lambda b, i: (b, i, PB_GA)),
                  pl.BlockSpec((1, RB, D), lambda b, i: (b, i, PB_GB)),
                  pl.BlockSpec((1, RB, D), lambda b, i: (b, jnp.maximum(i - 1, 0), 0)),
                  pl.BlockSpec((1, CTX, D), lambda b, i: (b, 0, 0)),
                  pl.BlockSpec((1, RB, D), row),
                  pl.BlockSpec((1, 6, D), lambda b, i: (_mod_row(b, i), 0, 0)),
                  pl.BlockSpec((1, A_V), full),
                  pl.BlockSpec((A_V, D), full),
                  pl.BlockSpec((D, D), full),
                  pl.BlockSpec((D, D), full),
                  pl.BlockSpec((1, D), full)],
        out_specs=[pl.BlockSpec((1, RB, D), row), pl.BlockSpec((1, RB, D), row)],
        out_shape=[jax.ShapeDtypeStruct((BATCH, TT, D), F32),
                   jax.ShapeDtypeStruct((BATCH, TT, D), h2_dtype)],
        compiler_params=_cp(("parallel", "parallel")),
        name="merge",
    )(hf, hbw, p3, p3, p3, hb_lat, hb_ctx, x, mod, ghn, wa, wb, wo, g2)


FFN_CH = D_FF // 2


def _ffn_kernel(h_ref, x_ref, mod_ref, w1_ref, w3_ref, w2_ref, gn_ref, modn_ref, x2_ref, hn_ref):
    h = h_ref[0]
    y = None
    for cidx in range(D_FF // FFN_CH):
        lo, hi = cidx * FFN_CH, (cidx + 1) * FFN_CH
        a = jnp.dot(h, w1_ref[:, lo:hi], preferred_element_type=F32)
        g = jnp.dot(h, w3_ref[:, lo:hi], preferred_element_type=F32)
        part = jnp.dot((_silu(a) * g).astype(BF16), w2_ref[lo:hi, :], preferred_element_type=F32)
        y = part if y is None else y + part
    mod = mod_ref[0]
    x2 = x_ref[0] + mod[5:6] * y
    x2_ref[0] = x2
    modn = modn_ref[0]
    hn_ref[0] = _norm_mod(x2, gn_ref[...], modn[1:2], modn[0:1]).astype(hn_ref.dtype)


def _dense_ffn(h2, x1, mod, w1, w3, w2, gn, modn):
    row = lambda b, i: (b, i, 0)
    full = lambda b, i: (0, 0)
    modspec = pl.BlockSpec((1, 6, D), lambda b, i: (_mod_row(b, i), 0, 0))
    return pl.pallas_call(
        _ffn_kernel,
        grid=(BATCH, NRB),
        in_specs=[pl.BlockSpec((1, RB, D), row), pl.BlockSpec((1, RB, D), row), modspec,
                  pl.BlockSpec((D, D_FF), full), pl.BlockSpec((D, D_FF), full),
                  pl.BlockSpec((D_FF, D), full), pl.BlockSpec((1, D), full), modspec],
        out_specs=[pl.BlockSpec((1, RB, D), row), pl.BlockSpec((1, RB, D), row)],
        out_shape=[jax.ShapeDtypeStruct((BATCH, TT, D), F32),
                   jax.ShapeDtypeStruct((BATCH, TT, D), BF16)],
        compiler_params=_cp(("parallel", "parallel")),
        name="dense_ffn",
    )(h2, x1, mod, w1, w3, w2, gn, modn)


def _router_kernel(h_ref, wr_ref, route_ref, cnt_ref, run_ref):
    i = pl.program_id(0)

    @pl.when(i == 0)
    def _():
        run_ref[...] = jnp.zeros_like(run_ref)

    logits = jnp.dot(h_ref[...].astype(BF16), wr_ref[...], preferred_element_type=F32)
    lane = lax.broadcasted_iota(jnp.int32, (RB, LANES), 1).astype(F32)
    lg = jnp.where(lane < N_EXPERTS, logits, -jnp.inf)
    v1 = jnp.max(lg, axis=1, keepdims=True)
    i1 = jnp.min(jnp.where(lg == v1, lane, float(LANES)), axis=1, keepdims=True)
    lg2 = jnp.where(lane == i1, -jnp.inf, lg)
    v2 = jnp.max(lg2, axis=1, keepdims=True)
    i2 = jnp.min(jnp.where(lg2 == v2, lane, float(LANES)), axis=1, keepdims=True)
    e = jnp.exp(v2 - v1)
    w1 = 1.0 / (1.0 + e)
    w2 = e / (1.0 + e)
    oh1 = (lane == i1).astype(F32)
    oh2 = (lane == i2).astype(F32)
    r = lax.broadcasted_iota(jnp.int32, (RB, RB), 0)
    c = lax.broadcasted_iota(jnp.int32, (RB, RB), 1)
    tri = (r > c).astype(BF16)
    cs1 = jnp.dot(tri, oh1.astype(BF16), preferred_element_type=F32)
    cs2 = jnp.dot(tri, oh2.astype(BF16), preferred_element_type=F32)
    tot1 = jnp.sum(oh1, axis=0, keepdims=True)
    tot2 = jnp.sum(oh2, axis=0, keepdims=True)
    run = run_ref[...]
    rank1 = jnp.sum(oh1 * (run + cs1), axis=1, keepdims=True)
    rank2 = jnp.sum(oh2 * (run + tot1 + cs2), axis=1, keepdims=True)
    new_run = run + tot1 + tot2
    run_ref[...] = new_run
    cnt_ref[...] = new_run
    out = jnp.where(lane == 0, i1,
          jnp.where(lane == 1, i2,
          jnp.where(lane == 2, w1,
          jnp.where(lane == 3, w2,
          jnp.where(lane == 4, rank1,
          jnp.where(lane == 5, rank2, 0.0))))))
    route_ref[...] = out


def _router(h2_flat, wr, nblk, blk_of):
    return pl.pallas_call(
        _router_kernel,
        grid=(nblk,),
        in_specs=[pl.BlockSpec((RB, D), lambda i: (blk_of(i), 0)),
                  pl.BlockSpec((D, LANES), lambda i: (0, 0))],
        out_specs=[pl.BlockSpec((RB, LANES), lambda i: (i, 0)),
                   pl.BlockSpec((1, LANES), lambda i: (0, 0))],
        out_shape=[jax.ShapeDtypeStruct((nblk * RB, LANES), F32),
                   jax.ShapeDtypeStruct((1, LANES), F32)],
        scratch_shapes=[pltpu.VMEM((1, LANES), F32)],
        compiler_params=_cp(("arbitrary",)),
        name="moe_router",
    )(h2_flat, wr)


DMA_UNROLL = 8
N_ZERO = 2 * N_EXPERTS


def _dispatch_kernel(pos_ref, zstart_ref, h_ref, o_ref, zbuf, sem, zsem):
    @pl.when(pl.program_id(0) == 0)
    def _():
        zbuf[...] = jnp.zeros_like(zbuf)
        for z in range(N_ZERO):
            @pl.when(zstart_ref[z] >= 0)
            def _():
                zs = pl.multiple_of(zstart_ref[z], MOE_TM)
                pltpu.make_async_copy(zbuf, o_ref.at[pl.ds(zs, MOE_TM)], zsem).start()
        for z in range(N_ZERO):
            @pl.when(zstart_ref[z] >= 0)
            def _():
                pltpu.make_async_copy(zbuf, o_ref.at[pl.ds(0, MOE_TM)], zsem).wait()

    def issue(t, carry):
        src = h_ref.at[pl.ds(t, 1)]
        pltpu.make_async_copy(src, o_ref.at[pl.ds(pos_ref[0, 0, t], 1)], sem).start()
        pltpu.make_async_copy(src, o_ref.at[pl.ds(pos_ref[0, 0, RB + t], 1)], sem).start()
        return carry

    lax.fori_loop(0, RB, issue, 0, unroll=DMA_UNROLL)
    for _ in range(2):
        pltpu.make_async_copy(h_ref, o_ref.at[pl.ds(0, RB)], sem).wait()


def _dispatch(pos, zstart, h2_flat, nblk, blk_of, n_sorted):
    return pl.pallas_call(
        _dispatch_kernel,
        grid=(nblk,),
        in_specs=[pl.BlockSpec((1, 1, 2 * RB), lambda i: (i, 0, 0), memory_space=pltpu.SMEM),
                  pl.BlockSpec(memory_space=pltpu.SMEM),
                  pl.BlockSpec((RB, D), lambda i: (blk_of(i), 0))],
        out_specs=pl.BlockSpec(memory_space=pl.ANY),
        out_shape=jax.ShapeDtypeStruct((n_sorted, D), F32),
        scratch_shapes=[pltpu.VMEM((MOE_TM, D), F32), pltpu.SemaphoreType.DMA(()),
                        pltpu.SemaphoreType.DMA(())],
        compiler_params=_cp(("arbitrary",)),
        name="moe_dispatch",
    )(pos, zstart, h2_flat)


def _gmm_kernel(be_ref, nb_ref, a_ref, w1_ref, w3_ref, w2_ref, o_ref, abf_ref):
    del be_ref
    i = pl.program_id(0)
    f = pl.program_id(1)

    @pl.when((i < nb_ref[0]) & (f == 0))
    def _():
        abf_ref[...] = a_ref[...].astype(BF16)

    @pl.when(i < nb_ref[0])
    def _():
        a = abf_ref[...]
        gs = []
        for c in range(MOE_FC // MOE_SUB):
            lo, hi = c * MOE_SUB, (c + 1) * MOE_SUB
            h1 = jnp.dot(a, w1_ref[0, :, lo:hi].astype(BF16), preferred_element_type=F32)
            h3 = jnp.dot(a, w3_ref[0, :, lo:hi].astype(BF16), preferred_element_type=F32)
            gs.append((_silu(h1) * h3).astype(BF16))
        part = jnp.dot(jnp.concatenate(gs, axis=1), w2_ref[0].astype(BF16), preferred_element_type=F32)

        @pl.when(f == 0)
        def _():
            o_ref[...] = part

        @pl.when(f > 0)
        def _():
            o_ref[...] += part

    @pl.when((i >= nb_ref[0]) & (f == 0))
    def _():
        o_ref[...] = jnp.zeros_like(o_ref)


def _expert_ffn(blk_e, nb, hs, w1, w3, w2, j):
    nbmax = hs.shape[0] // MOE_TM
    nf = D_FF_EXPERT // MOE_FC

    def ieff(i, nb_ref):
        return jnp.minimum(i, nb_ref[0] - 1)

    def feff(i, f, nb_ref):
        return jnp.where(i < nb_ref[0], f, nf - 1)

    grid_spec = pltpu.PrefetchScalarGridSpec(
        num_scalar_prefetch=2,
        grid=(nbmax, nf),
        in_specs=[pl.BlockSpec((MOE_TM, D), lambda i, f, be, nbr: (ieff(i, nbr), 0)),
                  pl.BlockSpec((None, 1, D, MOE_FC),
                               lambda i, f, be, nbr: (j, be[ieff(i, nbr)], 0, feff(i, f, nbr))),
                  pl.BlockSpec((None, 1, D, MOE_FC),
                               lambda i, f, be, nbr: (j, be[ieff(i, nbr)], 0, feff(i, f, nbr))),
                  pl.BlockSpec((None, 1, MOE_FC, D),
                               lambda i, f, be, nbr: (j, be[ieff(i, nbr)], feff(i, f, nbr), 0))],
        out_specs=pl.BlockSpec((MOE_TM, D), lambda i, f, be, nbr: (i, 0)),
        scratch_shapes=[pltpu.VMEM((MOE_TM, D), BF16)],
    )
    return pl.pallas_call(
        _gmm_kernel,
        grid_spec=grid_spec,
        out_shape=jax.ShapeDtypeStruct(hs.shape, F32),
        compiler_params=_cp(("arbitrary", "arbitrary")),
        name="moe_expert_ffn",
    )(blk_e, nb, hs, w1, w3, w2)


def _combine_kernel(pos_ref, y_ref, route_ref, x_ref, mod_ref, gn_ref, modn_ref,
                    x2_ref, hn_ref, ybuf, sem):
    def issue(t, carry):
        pltpu.make_async_copy(y_ref.at[pl.ds(pos_ref[0, 0, t], 1)], ybuf.at[0, pl.ds(t, 1)], sem).start()
        pltpu.make_async_copy(y_ref.at[pl.ds(pos_ref[0, 0, RB + t], 1)], ybuf.at[1, pl.ds(t, 1)], sem).start()
        return carry

    lax.fori_loop(0, RB, issue, 0, unroll=DMA_UNROLL)
    pltpu.make_async_copy(y_ref.at[pl.ds(0, RB)], ybuf.at[0], sem).wait()
    pltpu.make_async_copy(y_ref.at[pl.ds(0, RB)], ybuf.at[1], sem).wait()
    route = route_ref[...]
    y = route[:, 2:3] * ybuf[0] + route[:, 3:4] * ybuf[1]
    mod = mod_ref[0]
    x2 = x_ref[0] + mod[5:6] * y
    x2_ref[0] = x2
    modn = modn_ref[0]
    hn_ref[0] = _norm_mod(x2, gn_ref[...], modn[1:2], modn[0:1]).astype(hn_ref.dtype)


def _combine(pos, ys, route, x1, mod, gn, modn, latent_only, hn_dtype):
    nrb = NRB - 1 if latent_only else NRB
    off = 1 if latent_only else 0
    rows = nrb * RB
    full = lambda b, i: (0, 0)
    modspec = pl.BlockSpec((1, 6, D), lambda b, i: (_mod_row(b, i + off), 0, 0))
    return pl.pallas_call(
        _combine_kernel,
        grid=(BATCH, nrb),
        in_specs=[pl.BlockSpec((1, 1, 2 * RB), lambda b, i: (b * nrb + i, 0, 0), memory_space=pltpu.SMEM),
                  pl.BlockSpec(memory_space=pl.ANY),
                  pl.BlockSpec((RB, LANES), lambda b, i: (b * nrb + i, 0)),
                  pl.BlockSpec((1, RB, D), lambda b, i: (b, i + off, 0)),
                  modspec,
                  pl.BlockSpec((1, D), full),
                  modspec],
        out_specs=[pl.BlockSpec((1, RB, D), lambda b, i: (b, i, 0)),
                   pl.BlockSpec((1, RB, D), lambda b, i: (b, i, 0))],
        out_shape=[jax.ShapeDtypeStruct((BATCH, rows, D), F32),
                   jax.ShapeDtypeStruct((BATCH, rows, D), hn_dtype)],
        scratch_shapes=[pltpu.VMEM((2, RB, D), F32), pltpu.SemaphoreType.DMA(())],
        compiler_params=_cp(("arbitrary", "arbitrary")),
        name="moe_combine",
    )(pos, ys, route, x1, mod, gn, modn)


def _moe_ffn(h2, x1, mod, wr, w1, w3, w2, j, gn, modn, latent_only, hn_dtype):
    h2_flat = h2.reshape(NTOK, D)
    if latent_only:
        nrb = NRB - 1
        blk_of = lambda i: (i // nrb) * NRB + (i % nrb) + 1
    else:
        nrb = NRB
        blk_of = lambda i: i
    nblk = BATCH * nrb
    n_pairs = 2 * nblk * RB
    nbmax = n_pairs // MOE_TM + N_EXPERTS
    n_sorted = nbmax * MOE_TM

    route, counts = _router(h2_flat, wr, nblk, blk_of)
    cnt = counts[0, :N_EXPERTS].astype(jnp.int32)
    gsz = ((cnt + MOE_TM - 1) // MOE_TM) * MOE_TM
    gend = jnp.cumsum(gsz)
    goff = gend - gsz
    e12 = route[:, 0:2].astype(jnp.int32)
    rank = route[:, 4:6].astype(jnp.int32)
    pos = goff[e12] + rank
    pos = pos.reshape(nblk, RB, 2).transpose(0, 2, 1).reshape(nblk, 1, 2 * RB)
    nb = (gend[-1] // MOE_TM).reshape(1)
    blk_start = jnp.arange(nbmax, dtype=jnp.int32) * MOE_TM
    blk_e = jnp.minimum(jnp.sum(blk_start[:, None] >= gend[None, :], axis=1), N_EXPERTS - 1).astype(jnp.int32)

    trail = (nbmax - N_EXPERTS + jnp.arange(N_EXPERTS, dtype=jnp.int32)) * MOE_TM
    zstart = jnp.concatenate([jnp.where(cnt > 0, gend - MOE_TM, -1),
                              jnp.where(trail >= gend[-1], trail, -1)]).astype(jnp.int32)

    hs = _dispatch(pos, zstart, h2_flat, nblk, blk_of, n_sorted)
    ys = _expert_ffn(blk_e, nb, hs, w1, w3, w2, j)
    return _combine(pos, ys, route, x1, mod, gn, modn, latent_only, hn_dtype)


def _rope_tables():
    quarter = A_DQK // 4
    inv = 1.0 / (ROPE_BASE ** (jnp.arange(quarter, dtype=F32) / quarter))
    pos = jnp.arange(SEQ)
    rows = (pos // GRID_W).astype(F32)
    cols = (pos % GRID_W).astype(F32)
    ang_r = rows[:, None] * inv[None, :]
    ang_c = cols[:, None] * inv[None, :]
    cos = jnp.concatenate([jnp.cos(ang_r)] * 2 + [jnp.cos(ang_c)] * 2, axis=1)
    sin = jnp.concatenate([-jnp.sin(ang_r), jnp.sin(ang_r), -jnp.sin(ang_c), jnp.sin(ang_c)], axis=1)
    cos = jnp.concatenate([jnp.ones((CTX, A_DQK), F32), cos], axis=0)
    sin = jnp.concatenate([jnp.zeros((CTX, A_DQK), F32), sin], axis=0)
    kscale = A_DQK ** -0.5
    return (jnp.concatenate([cos, cos * kscale], axis=1),
            jnp.concatenate([sin, sin * kscale], axis=1))


def _na_bias_tables(rpb):
    rs = np.clip(np.arange(GRID_H) - NA_ROWS // 2, 0, GRID_H - NA_ROWS)
    cs = np.clip(np.arange(GRID_W) - NA_COLS // 2, 0, GRID_W - NA_COLS)
    rows = [0, 1, 2, 3, NA_ROWS // 2, GRID_H - 3, GRID_H - 2, GRID_H - 1]
    col = np.arange(GRID_W)
    dc = np.clip(col[None, :] - col[:, None] + NA_COLS - 1, 0, 2 * NA_COLS - 2)
    valid_c = (col[None, :] >= cs[:, None]) & (col[None, :] < cs[:, None] + NA_COLS)
    sel_c = np.eye(2 * NA_COLS - 1, dtype=np.float32)[dc]
    t = jnp.einsum('lhab,uvb->lhuav', rpb * LOG2E, jnp.asarray(sel_c),
                   precision=lax.Precision.HIGHEST)
    neg = np.where(valid_c, 0.0, -np.inf).astype(np.float32)[:, None, :]
    nl = rpb.shape[0]
    t = (t + jnp.asarray(neg)).reshape(nl, NPAIR, 2 * NA_NQ, (2 * NA_ROWS - 1) * GRID_W)
    first = [rs[r] - r + NA_ROWS - 1 for r in rows]
    return jnp.stack([t[..., d0 * GRID_W:d0 * GRID_W + NA_NK] for d0 in first], axis=1)


def kernel(x, c, ctx, c_ctx, w_mod, b_mod, g_norm1, g_norm2, w_in, a_conv, a_gate_b, a_hnorm_g, na_rpb,
           w_br_a, w_br_b, w_out, ffn_w1, ffn_w3, ffn_w2, moe_router, moe_w1, moe_w3, moe_w2, g_final):
    cc = jnp.concatenate([c, c_ctx[None, :], jnp.zeros((16 - BATCH - 1, D), F32)], axis=0)
    mod_all = _modulation(cc, w_mod, b_mod).reshape(DEPTH, 16, 6, D)[:, :BATCH + 1]
    mod_zero = jnp.zeros((BATCH + 1, 6, D), F32)
    rope_c, rope_s = _rope_tables()
    na_bias = _na_bias_tables(na_rpb)

    xs = jnp.concatenate([ctx, x], axis=1)
    h1 = _first_norm(xs, g_norm1[0][None, :], mod_all[0])
    out = None
    for l in range(DEPTH):
        last = l == DEPTH - 1
        mod = mod_all[l]
        wl = w_in[l]
        g0 = 3 * D
        g1 = g0 + NGATE
        wp = jnp.concatenate([wl[:, :g0], wl[:, g1:g1 + D] * (B_DH ** -0.5 * LOG2E), wl[:, g1 + D:]],
                             axis=1).astype(BF16)
        wg = jnp.pad(wl[:, g0:g1], ((0, 0), (0, LANES - NGATE))).astype(BF16)
        bg = jnp.pad(a_gate_b[l], (0, LANES - NGATE))[None, :]

        h1_flat = h1.reshape(NTOK, D)
        p3 = _in_proj(h1_flat, wp).reshape(BATCH, TT, P_COLS)
        gates, lfp = _gate_proj(h1_flat, wg, bg)

        qk = _mlstm_prep(p3, a_conv[l], rope_c, rope_s)
        hf, hbw = _mlstm_scan(qk, p3, gates.reshape(BATCH, TT, LANES), lfp.reshape(BATCH, TT, LANES))
        hb_lat = _na_attention(p3, na_bias, l)
        hb_ctx = _ctx_attention(p3)

        moe = l % 2 == 1
        x1, h2 = _merge(hf, hbw, p3, hb_lat, hb_ctx, xs, mod, a_hnorm_g[l][None, :],
                        w_br_a[l].astype(BF16), w_br_b[l].astype(BF16), w_out[l].astype(BF16),
                        g_norm2[l][None, :], F32 if moe else BF16)
        if last:
            gn, modn = g_final[None, :], mod_zero
        else:
            gn, modn = g_norm1[l + 1][None, :], mod_all[l + 1]
        j = l // 2
        if not moe:
            xs, h1 = _dense_ffn(h2, x1, mod, ffn_w1[j].astype(BF16), ffn_w3[j].astype(BF16),
                                ffn_w2[j].astype(BF16), gn, modn)
        else:
            wr = jnp.pad(moe_router[j], ((0, 0), (0, LANES - N_EXPERTS))).astype(BF16)
            xs, h1 = _moe_ffn(h2, x1, mod, wr, moe_w1, moe_w3, moe_w2, j, gn, modn, last,
                              F32 if last else BF16)
            if last:
                out = h1
    return out
```

```python
import functools

import numpy as np
import jax
import jax.numpy as jnp
from jax import lax
from jax.experimental import pallas as pl
from jax.experimental.pallas import tpu as pltpu

F32 = jnp.float32
BF16 = jnp.bfloat16

D = 1024
BATCH = 8
SEQ = 2048
CTX = 256
TT = CTX + SEQ
NTOK = BATCH * TT
DEPTH = 4
GRID_W = 64
GRID_H = SEQ // GRID_W

A_HEADS = 4
A_DQK = 128
A_DV = 256
A_QK = A_HEADS * A_DQK
A_V = A_HEADS * A_DV
ROPE_BASE = 10000.0
LCH = 256
NCH = TT // LCH

B_HEADS = 16
B_DH = 64
NA_ROWS = 8
NA_COLS = 16
NA_QROWS = 1
NA_KROWS = NA_QROWS + NA_ROWS - 1
NA_NQ = NA_QROWS * GRID_W
NA_NK = NA_KROWS * GRID_W
NA_STEPS = GRID_H // NA_QROWS

D_FF = 2816
N_EXPERTS = 8
D_FF_EXPERT = 3584
EPS = 1e-6
LOG2E = 1.4426950408889634

RB = 256
NRB = TT // RB
MM_TM = 2048
MOE_TM = 1024
MOE_FC = 512
MOE_SUB = 256
LANES = 128

PB_QK, PB_V, PB_O, PB_NQ, PB_NK, PB_NV, PB_GA, PB_GB = range(8)
P_COLS = 8 * D

VMEM_LIMIT = 56 * 1024 * 1024


def _cp(sem, vmem=VMEM_LIMIT):
    return pltpu.CompilerParams(dimension_semantics=sem, vmem_limit_bytes=vmem)


def _sigmoid(x):
    return 1.0 / (1.0 + jnp.exp(-x))


def _silu(x):
    return x * _sigmoid(x)


def _log_sigmoid(x):
    return jnp.minimum(x, 0.0) - jnp.log(1.0 + jnp.exp(-jnp.abs(x)))


def _norm_mod(x, g, sc, sh):
    ms = jnp.mean(x * x, axis=-1, keepdims=True)
    y = x * lax.rsqrt(ms + EPS)
    return (y * g) * (1.0 + sc) + sh


def _mod_row(b, i):
    return jnp.where(i == 0, BATCH, b)


def _mod_kernel(c_ref, w_ref, b_ref, o_ref):
    c = c_ref[...]
    s = _silu(c).astype(BF16)
    o_ref[0] = jnp.dot(s, w_ref[0].astype(BF16), preferred_element_type=F32) + b_ref[0]


def _modulation(cc, w_mod, b_mod):
    tn = 2048
    nl = w_mod.shape[0]
    return pl.pallas_call(
        _mod_kernel,
        grid=(nl, 6 * D // tn),
        in_specs=[pl.BlockSpec((16, D), lambda l, j: (0, 0)),
                  pl.BlockSpec((1, D, tn), lambda l, j: (l, 0, j)),
                  pl.BlockSpec((1, 1, tn), lambda l, j: (l, 0, j))],
        out_specs=pl.BlockSpec((1, 16, tn), lambda l, j: (l, 0, j)),
        out_shape=jax.ShapeDtypeStruct((nl, 16, 6 * D), F32),
        compiler_params=_cp(("parallel", "parallel")),
        name="modulation",
    )(cc, w_mod, b_mod.reshape(nl, 1, 6 * D))


def _norm_kernel(x_ref, g_ref, mod_ref, o_ref):
    mod = mod_ref[0]
    o_ref[0] = _norm_mod(x_ref[0], g_ref[...], mod[1:2], mod[0:1]).astype(o_ref.dtype)


def _first_norm(x, g, mod):
    return pl.pallas_call(
        _norm_kernel,
        grid=(BATCH, NRB),
        in_specs=[pl.BlockSpec((1, RB, D), lambda b, i: (b, i, 0)),
                  pl.BlockSpec((1, D), lambda b, i: (0, 0)),
                  pl.BlockSpec((1, 6, D), lambda b, i: (_mod_row(b, i), 0, 0))],
        out_specs=pl.BlockSpec((1, RB, D), lambda b, i: (b, i, 0)),
        out_shape=jax.ShapeDtypeStruct((BATCH, TT, D), BF16),
        compiler_params=_cp(("parallel", "parallel")),
        name="first_norm",
    )(x, g, mod)


def _mm_kernel(a_ref, w_ref, o_ref):
    o_ref[...] = jnp.dot(a_ref[...], w_ref[...], preferred_element_type=F32).astype(o_ref.dtype)


def _in_proj(h, w):
    tn = 1024
    return pl.pallas_call(
        _mm_kernel,
        grid=(P_COLS // tn, NTOK // MM_TM),
        in_specs=[pl.BlockSpec((MM_TM, D), lambda j, i: (i, 0)),
                  pl.BlockSpec((D, tn), lambda j, i: (0, j))],
        out_specs=pl.BlockSpec((MM_TM, tn), lambda j, i: (i, j)),
        out_shape=jax.ShapeDtypeStruct((NTOK, P_COLS), BF16),
        compiler_params=_cp(("parallel", "parallel")),
        name="in_proj",
    )(h, w)


NGATE = 4 * A_HEADS


def _gate_kernel(a_ref, w_ref, b_ref, g_ref, lf_ref):
    g = jnp.dot(a_ref[...], w_ref[...], preferred_element_type=F32) + b_ref[...]
    g_ref[...] = g
    lf = _log_sigmoid(g)
    p0 = lf.astype(BF16).astype(F32)
    r1 = lf - p0
    p1 = r1.astype(BF16).astype(F32)
    p2 = (r1 - p1).astype(BF16).astype(F32)
    lane = lax.broadcasted_iota(jnp.int32, (1, LANES), 1)
    parts = jnp.where(lane < NGATE, p0,
                      jnp.where(lane < 2 * NGATE, pltpu.roll(p1, NGATE, 1),
                                jnp.where(lane < 3 * NGATE, pltpu.roll(p2, 2 * NGATE, 1), 0.0)))
    lf_ref[...] = parts.astype(BF16)


def _gate_proj(h, wg, bg):
    return pl.pallas_call(
        _gate_kernel,
        grid=(NTOK // MM_TM,),
        in_specs=[pl.BlockSpec((MM_TM, D), lambda i: (i, 0)),
                  pl.BlockSpec((D, LANES), lambda i: (0, 0)),
                  pl.BlockSpec((1, LANES), lambda i: (0, 0))],
        out_specs=[pl.BlockSpec((MM_TM, LANES), lambda i: (i, 0)),
                   pl.BlockSpec((MM_TM, LANES), lambda i: (i, 0))],
        out_shape=[jax.ShapeDtypeStruct((NTOK, LANES), F32),
                   jax.ShapeDtypeStruct((NTOK, LANES), BF16)],
        compiler_params=_cp(("parallel",)),
        name="gate_proj",
    )(h, wg, bg)


def _prep_kernel(u_ref, up_ref, un_ref, w_ref, c_ref, s_ref, o_ref):
    i = pl.program_id(1)
    u = u_ref[0].astype(F32)
    prev_row = jnp.where(i >= 2, up_ref[0, 15:16, :].astype(F32), 0.0)
    next_row = jnp.where((i >= 1) & (i <= NRB - 2), un_ref[0, 0:1, :].astype(F32), 0.0)
    rid = lax.broadcasted_iota(jnp.int32, (RB, 1), 0)
    u_m1 = jnp.where(rid == 0, prev_row, pltpu.roll(u, 1, 0))
    u_p1 = jnp.where(rid == RB - 1, next_row, pltpu.roll(u, RB - 1, 0))
    w = w_ref[...]
    y = w[0:1] * u_m1 + w[1:2] * u + w[2:3] * u_p1
    y = _silu(y)
    c = c_ref[...]
    s = s_ref[...]
    cfull = jnp.concatenate([c[:, :A_DQK]] * A_HEADS + [c[:, A_DQK:]] * A_HEADS, axis=1)
    sfull = jnp.concatenate([s[:, :A_DQK]] * A_HEADS + [s[:, A_DQK:]] * A_HEADS, axis=1)
    lane = lax.broadcasted_iota(jnp.int32, (1, 2 * A_QK), 1)
    partner = jnp.where((lane & 32) == 0,
                        pltpu.roll(y, 2 * A_QK - 32, 1), pltpu.roll(y, 32, 1))
    o_ref[0] = (y * cfull + partner * sfull).astype(o_ref.dtype)


def _mlstm_prep(p3, conv_w, rope_c, rope_s):
    nb16 = TT // 16
    return pl.pallas_call(
        _prep_kernel,
        grid=(BATCH, NRB),
        in_specs=[pl.BlockSpec((1, RB, D), lambda b, i: (b, i, PB_QK)),
                  pl.BlockSpec((1, 16, D), lambda b, i: (b, jnp.maximum(i * (RB // 16) - 1, 0), PB_QK)),
                  pl.BlockSpec((1, 16, D), lambda b, i: (b, jnp.minimum((i + 1) * (RB // 16), nb16 - 1), PB_QK)),
                  pl.BlockSpec((3, D), lambda b, i: (0, 0)),
                  pl.BlockSpec((RB, 2 * A_DQK), lambda b, i: (i, 0)),
                  pl.BlockSpec((RB, 2 * A_DQK), lambda b, i: (i, 0))],
        out_specs=pl.BlockSpec((1, RB, D), lambda b, i: (b, i, 0)),
        out_shape=jax.ShapeDtypeStruct((BATCH, TT, D), BF16),
        compiler_params=_cp(("parallel", "parallel")),
        name="mlstm_prep",
    )(p3, p3, p3, conv_w, rope_c, rope_s)


def _mlstm_kernel(qkf_ref, vf_ref, gf_ref, lff_ref, qkb_ref, vb_ref, gb_ref, lfb_ref,
                  of_ref, ob_ref, ct_ref, n_ref, m_ref):
    @pl.when(pl.program_id(1) == 0)
    def _():
        ct_ref[...] = jnp.zeros_like(ct_ref)
        n_ref[...] = jnp.zeros_like(n_ref)
        m_ref[...] = jnp.zeros_like(m_ref)

    r = lax.broadcasted_iota(jnp.int32, (LCH, LCH), 0)
    c = lax.broadcasted_iota(jnp.int32, (LCH, LCH), 1)
    masks = [c <= r, c >= r]
    ins = [(qkf_ref, vf_ref, gf_ref, lff_ref, of_ref), (qkb_ref, vb_ref, gb_ref, lfb_ref, ob_ref)]
    b_all, g_all, b_t, g_t, b_end = [], [], [], [], []
    for d in range(2):
        tri = jnp.where(masks[d], 1.0, 0.0).astype(BF16)
        bc = jnp.dot(tri, ins[d][3][0], preferred_element_type=F32)
        ba = bc + pltpu.roll(bc, LANES - NGATE, 1) + pltpu.roll(bc, LANES - 2 * NGATE, 1)
        ga = ins[d][2][0]
        if d == 1:
            ba = pltpu.roll(ba, LANES - 2 * A_HEADS, 1)
            ga = pltpu.roll(ga, LANES - 2 * A_HEADS, 1)
        b_all.append(ba)
        g_all.append(ga)
        b_t.append(ba.T)
        g_t.append(ga.T)
        b_end.append(ba[LCH - 1:LCH, :] if d == 0 else ba[0:1, :])
    ones = jnp.ones((LCH, LANES), BF16)
    tn_dims = (((0,), (0,)), ((), ()))
    combos = [(d, h) for d in range(2) for h in range(A_HEADS)]
    idx = range(len(combos))
    qs = [ins[d][0][0, :, h * A_DQK:(h + 1) * A_DQK] for d, h in combos]
    ks = [ins[d][0][0, :, A_QK + h * A_DQK:A_QK + (h + 1) * A_DQK] for d, h in combos]
    qk = [lax.dot_general(qs[i], ks[i], NT_DIMS, preferred_element_type=F32) for i in idx]
    b_col = [b_all[d][:, A_HEADS + h:A_HEADS + h + 1] for d, h in combos]
    b_last = [b_end[d][:, A_HEADS + h:A_HEADS + h + 1] for d, h in combos]
    m_old = [m_ref[i][:, 0:1] for i in idx]
    r_row = [b_t[d][A_HEADS + h:A_HEADS + h + 1, :] - g_t[d][h:h + 1, :] for d, h in combos]
    m_row = [jnp.maximum(b_col[i] + m_old[i],
                         jnp.max(jnp.where(masks[d], b_col[i] - r_row[i], -jnp.inf), axis=1, keepdims=True))
             for i, (d, h) in enumerate(combos)]
    s = [(qk[i] * jnp.exp(jnp.where(masks[d], (b_col[i] - m_row[i]) - r_row[i], -jnp.inf))).astype(BF16)
         for i, (d, h) in enumerate(combos)]
    wq = [(jnp.exp(b_col[i] + m_old[i] - m_row[i]) * qs[i].astype(F32)).astype(BF16) for i in idx]
    g_col = [b_last[i] - b_col[i] + g_all[d][:, h:h + 1] for i, (d, h) in enumerate(combos)]
    m_new = [jnp.maximum(b_last[i] + m_old[i], jnp.max(g_col[i], axis=0, keepdims=True)) for i in idx]
    kw = [(jnp.exp(g_col[i] - m_new[i]) * ks[i].astype(F32)).astype(BF16) for i in idx]
    decay = [jnp.exp(b_last[i] + m_old[i] - m_new[i]) for i in idx]
    for i, (d, h) in enumerate(combos):
        v = ins[d][1][0, :, h * A_DV:(h + 1) * A_DV]
        lhs = jnp.concatenate([s[i], wq[i]], axis=1)
        den = jnp.dot(lhs, jnp.concatenate([ones, n_ref[i].astype(BF16)], axis=0),
                      preferred_element_type=F32)[:, 0:1]
        rinv = 1.0 / jnp.maximum(jnp.abs(den), jnp.exp(-m_row[i]))
        num = jnp.dot(lhs, jnp.concatenate([v, ct_ref[i].astype(BF16)], axis=0), preferred_element_type=F32)
        ins[d][4][0, :, h * A_DV:(h + 1) * A_DV] = (num * rinv).astype(of_ref.dtype)
    for i, (d, h) in enumerate(combos):
        v = ins[d][1][0, :, h * A_DV:(h + 1) * A_DV]
        ct_ref[i] = decay[i] * ct_ref[i] + lax.dot_general(kw[i], v, tn_dims, preferred_element_type=F32)
        n_ref[i] = decay[i] * n_ref[i] + lax.dot_general(kw[i], ones, tn_dims, preferred_element_type=F32)
        m_ref[i] = jnp.broadcast_to(m_new[i], (1, LANES))


def _rev_chunk(s):
    ncc = CTX // LCH
    return jnp.where(s < ncc, ncc - 1 - s, NCH + ncc - 1 - s)


def _mlstm_scan(qk, p3, gates, lfp):
    fwd = lambda b, s: (b, s, 0)
    bwd = lambda b, s: (b, _rev_chunk(s), 0)
    specs = lambda im, imv: [pl.BlockSpec((1, LCH, D), im), pl.BlockSpec((1, LCH, A_V), imv),
                             pl.BlockSpec((1, LCH, LANES), im), pl.BlockSpec((1, LCH, LANES), im)]
    nstate = 2 * A_HEADS
    return pl.pallas_call(
        _mlstm_kernel,
        grid=(BATCH, NCH),
        in_specs=(specs(fwd, lambda b, s: (b, s, PB_V))
                  + specs(bwd, lambda b, s: (b, _rev_chunk(s), PB_V))),
        out_specs=[pl.BlockSpec((1, LCH, A_V), fwd), pl.BlockSpec((1, LCH, A_V), bwd)],
        out_shape=[jax.ShapeDtypeStruct((BATCH, TT, A_V), BF16)] * 2,
        scratch_shapes=[pltpu.VMEM((nstate, A_DQK, A_DV), F32),
                        pltpu.VMEM((nstate, A_DQK, LANES), F32),
                        pltpu.VMEM((nstate, 1, LANES), F32)],
        compiler_params=_cp(("parallel", "arbitrary")),
        name="mlstm_scan",
    )(qk, p3, gates, lfp, qk, p3, gates, lfp)


NPAIR = B_HEADS // 2
NT_DIMS = (((1,), (1,)), ((), ()))


def _stack_pair(q, low):
    zero = jnp.zeros_like(q)
    return jnp.concatenate([jnp.where(low, q, zero), jnp.where(low, zero, q)], axis=0)


def _softmax_rows(s_ref, p_ref):
    s = s_ref[...]
    p = jnp.exp2(s - jnp.max(s, axis=2, keepdims=True))
    p_ref[...] = p.astype(p_ref.dtype)
    return 1.0 / jnp.sum(p, axis=2, keepdims=True)


def _na_kernel(q_ref, k_ref, v_ref, bias_ref, o_ref, s_ref, p_ref):
    row = pl.program_id(1)
    rs = jnp.clip(row - NA_ROWS // 2, 0, GRID_H - NA_ROWS)
    start = pl.multiple_of(CTX + rs * GRID_W, GRID_W)
    low = lax.broadcasted_iota(jnp.int32, (1, 2 * B_DH), 1) < B_DH
    for hp in range(NPAIR):
        lo, hi = hp * 2 * B_DH, (hp + 1) * 2 * B_DH
        q2 = _stack_pair(q_ref[0, :, lo:hi], low)
        s_ref[hp, :, 0:NA_NK] = lax.dot_general(q2, k_ref[0, pl.ds(start, NA_NK), lo:hi], NT_DIMS,
                                                preferred_element_type=F32) + bias_ref[0, 0, hp]
        s_ref[hp, :, NA_NK:] = lax.dot_general(q2, k_ref[0, 0:CTX, lo:hi], NT_DIMS,
                                               preferred_element_type=F32)
    rinv = _softmax_rows(s_ref, p_ref)
    for hp in range(NPAIR):
        lo, hi = hp * 2 * B_DH, (hp + 1) * 2 * B_DH
        o2 = (jnp.dot(p_ref[hp, :, 0:NA_NK], v_ref[0, pl.ds(start, NA_NK), lo:hi],
                      preferred_element_type=F32)
              + jnp.dot(p_ref[hp, :, NA_NK:], v_ref[0, 0:CTX, lo:hi], preferred_element_type=F32))
        o2 = o2 * rinv[hp]
        o_ref[0, :, lo:hi] = jnp.where(low, o2[0:NA_NQ], o2[NA_NQ:]).astype(o_ref.dtype)


def _na_pattern(row):
    edge = NA_ROWS // 2
    return jnp.where(row < edge, row, jnp.where(row <= GRID_H - edge, edge, row - (GRID_H - 2 * edge)))


NA_NPAT = NA_ROWS


def _na_attention(p3, bias, layer):
    qoff = CTX // NA_NQ
    return pl.pallas_call(
        _na_kernel,
        grid=(BATCH, GRID_H),
        in_specs=[pl.BlockSpec((1, NA_NQ, D), lambda b, i: (b, i + qoff, PB_NQ)),
                  pl.BlockSpec((1, TT, D), lambda b, i: (b, 0, PB_NK)),
                  pl.BlockSpec((1, TT, D), lambda b, i: (b, 0, PB_NV)),
                  pl.BlockSpec((1, 1, NPAIR, 2 * NA_NQ, NA_NK),
                               lambda b, i: (layer, _na_pattern(i), 0, 0, 0))],
        out_specs=pl.BlockSpec((1, NA_NQ, D), lambda b, i: (b, i, 0)),
        out_shape=jax.ShapeDtypeStruct((BATCH, SEQ, D), BF16),
        scratch_shapes=[pltpu.VMEM((NPAIR, 2 * NA_NQ, NA_NK + CTX), F32),
                        pltpu.VMEM((NPAIR, 2 * NA_NQ, NA_NK + CTX), BF16)],
        compiler_params=_cp(("parallel", "arbitrary")),
        name="na_attention",
    )(p3, p3, p3, bias)


def _ctx_attn_kernel(q_ref, k_ref, v_ref, o_ref, s_ref, p_ref):
    low = lax.broadcasted_iota(jnp.int32, (1, 2 * B_DH), 1) < B_DH
    for hp in range(NPAIR):
        lo, hi = hp * 2 * B_DH, (hp + 1) * 2 * B_DH
        s_ref[hp] = lax.dot_general(_stack_pair(q_ref[0, :, lo:hi], low), k_ref[0, :, lo:hi], NT_DIMS,
                                    preferred_element_type=F32)
    rinv = _softmax_rows(s_ref, p_ref)
    for hp in range(NPAIR):
        lo, hi = hp * 2 * B_DH, (hp + 1) * 2 * B_DH
        o2 = jnp.dot(p_ref[hp], v_ref[0, :, lo:hi], preferred_element_type=F32) * rinv[hp]
        o_ref[0, :, lo:hi] = jnp.where(low, o2[0:CTX], o2[CTX:]).astype(o_ref.dtype)


def _ctx_attention(p3):
    return pl.pallas_call(
        _ctx_attn_kernel,
        grid=(BATCH,),
        in_specs=[pl.BlockSpec((1, CTX, D), lambda b: (b, 0, PB_NQ)),
                  pl.BlockSpec((1, CTX, D), lambda b: (b, 0, PB_NK)),
                  pl.BlockSpec((1, CTX, D), lambda b: (b, 0, PB_NV))],
        out_specs=pl.BlockSpec((1, CTX, D), lambda b: (b, 0, 0)),
        out_shape=jax.ShapeDtypeStruct((BATCH, CTX, D), BF16),
        scratch_shapes=[pltpu.VMEM((NPAIR, 2 * CTX, CTX), F32),
                        pltpu.VMEM((NPAIR, 2 * CTX, CTX), BF16)],
        compiler_params=_cp(("parallel",)),
        name="ctx_attention",
    )(p3, p3, p3)


def _merge_kernel(hf_ref, hbw_ref, o_ref, ga_ref, gb_ref, hnl_ref, hnc_ref, x_ref, mod_ref, ghn_ref,
                  wa_ref, wb_ref, wo_ref, g2_ref, x1_ref, h2_ref):
    hn = jnp.where(pl.program_id(1) == 0, hnc_ref[0], hnl_ref[0])
    hs = hf_ref[0].astype(F32) + hbw_ref[0].astype(F32)
    parts = []
    for h in range(A_HEADS):
        seg = hs[:, h * A_DV:(h + 1) * A_DV]
        mu = jnp.mean(seg, axis=-1, keepdims=True)
        cen = seg - mu
        var = jnp.mean(cen * cen, axis=-1, keepdims=True)
        parts.append(cen * lax.rsqrt(var + EPS))
    ya = jnp.concatenate(parts, axis=1) * ghn_ref[...] * _sigmoid(o_ref[0].astype(F32))
    a = jnp.dot(ya.astype(BF16), wa_ref[...], preferred_element_type=F32)
    bm = jnp.dot(hn, wb_ref[...], preferred_element_type=F32)
    mrg = _sigmoid(ga_ref[0].astype(F32)) * a + _sigmoid(gb_ref[0].astype(F32)) * bm
    y = jnp.dot(mrg.astype(BF16), wo_ref[...], preferred_element_type=F32)
    mod = mod_ref[0]
    x1 = x_ref[0] + mod[2:3] * y
    x1_ref[0] = x1
    h2_ref[0] = _norm_mod(x1, g2_ref[...], mod[4:5], mod[3:4]).astype(h2_ref.dtype)


def _merge(hf, hbw, p3, hb_lat, hb_ctx, x, mod, ghn, wa, wb, wo, g2, h2_dtype):
    row = lambda b, i: (b, i, 0)
    full = lambda b, i: (0, 0)
    return pl.pallas_call(
        _merge_kernel,
        grid=(BATCH, NRB),
        in_specs=[pl.BlockSpec((1, RB, A_V), row),
                  pl.BlockSpec((1, RB, A_V), row),
                  pl.BlockSpec((1, RB, D), lambda b, i: (b, i, PB_O)),
                  pl.BlockSpec((1, RB, D), lambda b, i: (b, i, PB_GA)),
                  pl.BlockSpec((1, RB, D), lambda b, i: (b, i, PB_GB)),
                  pl.BlockSpec((1, RB, D), lambda b, i: (b, jnp.maximum(i - 1, 0), 0)),
                  pl.BlockSpec((1, CTX, D), lambda b, i: (b, 0, 0)),
                  pl.BlockSpec((1, RB, D), row),
                  pl.BlockSpec((1, 6, D), lambda b, i: (_mod_row(b, i), 0, 0)),
                  pl.BlockSpec((1, A_V), full),
                  pl.BlockSpec((A_V, D), full),
                  pl.BlockSpec((D, D), full),
                  pl.BlockSpec((D, D), full),
                  pl.BlockSpec((1, D), full)],
        out_specs=[pl.BlockSpec((1, RB, D), row), pl.BlockSpec((1, RB, D), row)],
        out_shape=[jax.ShapeDtypeStruct((BATCH, TT, D), F32),
                   jax.ShapeDtypeStruct((BATCH, TT, D), h2_dtype)],
        compiler_params=_cp(("parallel", "parallel")),
        name="merge",
    )(hf, hbw, p3, p3, p3, hb_lat, hb_ctx, x, mod, ghn, wa, wb, wo, g2)


FFN_CH = D_FF // 2


def _ffn_kernel(h_ref, x_ref, mod_ref, w1_ref, w3_ref, w2_ref, gn_ref, modn_ref, x2_ref, hn_ref):
    h = h_ref[0]
    y = None
    for cidx in range(D_FF // FFN_CH):
        lo, hi = cidx * FFN_CH, (cidx + 1) * FFN_CH
        a = jnp.dot(h, w1_ref[:, lo:hi], preferred_element_type=F32)
        g = jnp.dot(h, w3_ref[:, lo:hi], preferred_element_type=F32)
        part = jnp.dot((_silu(a) * g).astype(BF16), w2_ref[lo:hi, :], preferred_element_type=F32)
        y = part if y is None else y + part
    mod = mod_ref[0]
    x2 = x_ref[0] + mod[5:6] * y
    x2_ref[0] = x2
    modn = modn_ref[0]
    hn_ref[0] = _norm_mod(x2, gn_ref[...], modn[1:2], modn[0:1]).astype(hn_ref.dtype)


def _dense_ffn(h2, x1, mod, w1, w3, w2, gn, modn):
    row = lambda b, i: (b, i, 0)
    full = lambda b, i: (0, 0)
    modspec = pl.BlockSpec((1, 6, D), lambda b, i: (_mod_row(b, i), 0, 0))
    return pl.pallas_call(
        _ffn_kernel,
        grid=(BATCH, NRB),
        in_specs=[pl.BlockSpec((1, RB, D), row), pl.BlockSpec((1, RB, D), row), modspec,
                  pl.BlockSpec((D, D_FF), full), pl.BlockSpec((D, D_FF), full),
                  pl.BlockSpec((D_FF, D), full), pl.BlockSpec((1, D), full), modspec],
        out_specs=[pl.BlockSpec((1, RB, D), row), pl.BlockSpec((1, RB, D), row)],
        out_shape=[jax.ShapeDtypeStruct((BATCH, TT, D), F32),
                   jax.ShapeDtypeStruct((BATCH, TT, D), BF16)],
        compiler_params=_cp(("parallel", "parallel")),
        name="dense_ffn",
    )(h2, x1, mod, w1, w3, w2, gn, modn)


def _router_kernel(h_ref, wr_ref, route_ref, cnt_ref, run_ref):
    i = pl.program_id(0)

    @pl.when(i == 0)
    def _():
        run_ref[...] = jnp.zeros_like(run_ref)

    logits = jnp.dot(h_ref[...].astype(BF16), wr_ref[...], preferred_element_type=F32)
    lane = lax.broadcasted_iota(jnp.int32, (RB, LANES), 1).astype(F32)
    lg = jnp.where(lane < N_EXPERTS, logits, -jnp.inf)
    v1 = jnp.max(lg, axis=1, keepdims=True)
    i1 = jnp.min(jnp.where(lg == v1, lane, float(LANES)), axis=1, keepdims=True)
    lg2 = jnp.where(lane == i1, -jnp.inf, lg)
    v2 = jnp.max(lg2, axis=1, keepdims=True)
    i2 = jnp.min(jnp.where(lg2 == v2, lane, float(LANES)), axis=1, keepdims=True)
    e = jnp.exp(v2 - v1)
    w1 = 1.0 / (1.0 + e)
    w2 = e / (1.0 + e)
    oh1 = (lane == i1).astype(F32)
    oh2 = (lane == i2).astype(F32)
    r = lax.broadcasted_iota(jnp.int32, (RB, RB), 0)
    c = lax.broadcasted_iota(jnp.int32, (RB, RB), 1)
    tri = (r > c).astype(BF16)
    cs1 = jnp.dot(tri, oh1.astype(BF16), preferred_element_type=F32)
    cs2 = jnp.dot(tri, oh2.astype(BF16), preferred_element_type=F32)
    tot1 = jnp.sum(oh1, axis=0, keepdims=True)
    tot2 = jnp.sum(oh2, axis=0, keepdims=True)
    run = run_ref[...]
    rank1 = jnp.sum(oh1 * (run + cs1), axis=1, keepdims=True)
    rank2 = jnp.sum(oh2 * (run + tot1 + cs2), axis=1, keepdims=True)
    new_run = run + tot1 + tot2
    run_ref[...] = new_run
    cnt_ref[...] = new_run
    out = jnp.where(lane == 0, i1,
          jnp.where(lane == 1, i2,
          jnp.where(lane == 2, w1,
          jnp.where(lane == 3, w2,
          jnp.where(lane == 4, rank1,
          jnp.where(lane == 5, rank2, 0.0))))))
    route_ref[...] = out


def _router(h2_flat, wr, nblk, blk_of):
    return pl.pallas_call(
        _router_kernel,
        grid=(nblk,),
        in_specs=[pl.BlockSpec((RB, D), lambda i: (blk_of(i), 0)),
                  pl.BlockSpec((D, LANES), lambda i: (0, 0))],
        out_specs=[pl.BlockSpec((RB, LANES), lambda i: (i, 0)),
                   pl.BlockSpec((1, LANES), lambda i: (0, 0))],
        out_shape=[jax.ShapeDtypeStruct((nblk * RB, LANES), F32),
                   jax.ShapeDtypeStruct((1, LANES), F32)],
        scratch_shapes=[pltpu.VMEM((1, LANES), F32)],
        compiler_params=_cp(("arbitrary",)),
        name="moe_router",
    )(h2_flat, wr)


DMA_UNROLL = 8
N_ZERO = 2 * N_EXPERTS


def _dispatch_kernel(pos_ref, zstart_ref, h_ref, o_ref, zbuf, stage, sems, zsem):
    @pl.when(pl.program_id(0) == 0)
    def _():
        zbuf[...] = jnp.zeros_like(zbuf)
        for z in range(N_ZERO):
            @pl.when(zstart_ref[z] >= 0)
            def _():
                zs = pl.multiple_of(zstart_ref[z], MOE_TM)
                pltpu.make_async_copy(zbuf, o_ref.at[pl.ds(zs, MOE_TM)], zsem).start()
        for z in range(N_ZERO):
            @pl.when(zstart_ref[z] >= 0)
            def _():
                pltpu.make_async_copy(zbuf, o_ref.at[pl.ds(0, MOE_TM)], zsem).wait()

    step = pl.program_id(0)
    slot = step % 2
    stage[slot] = h_ref[...]

    def issue(t, carry):
        src = stage.at[slot, pl.ds(t, 1)]
        pltpu.make_async_copy(src, o_ref.at[pl.ds(pos_ref[0, 0, t], 1)], sems.at[slot]).start()
        pltpu.make_async_copy(src, o_ref.at[pl.ds(pos_ref[0, 0, RB + t], 1)], sems.at[slot]).start()
        return carry

    lax.fori_loop(0, RB, issue, 0, unroll=DMA_UNROLL)

    def drain(sl):
        for _ in range(2):
            pltpu.make_async_copy(stage.at[sl], o_ref.at[pl.ds(0, RB)], sems.at[sl]).wait()

    @pl.when(step > 0)
    def _():
        drain(1 - slot)

    @pl.when(step == pl.num_programs(0) - 1)
    def _():
        drain(slot)


def _dispatch(pos, zstart, h2_flat, nblk, blk_of, n_sorted):
    return pl.pallas_call(
        _dispatch_kernel,
        grid=(nblk,),
        in_specs=[pl.BlockSpec((1, 1, 2 * RB), lambda i: (i, 0, 0), memory_space=pltpu.SMEM),
                  pl.BlockSpec(memory_space=pltpu.SMEM),
                  pl.BlockSpec((RB, D), lambda i: (blk_of(i), 0))],
        out_specs=pl.BlockSpec(memory_space=pl.ANY),
        out_shape=jax.ShapeDtypeStruct((n_sorted, D), F32),
        scratch_shapes=[pltpu.VMEM((MOE_TM, D), F32), pltpu.VMEM((2, RB, D), F32),
                        pltpu.SemaphoreType.DMA((2,)), pltpu.SemaphoreType.DMA(())],
        compiler_params=_cp(("arbitrary",)),
        name="moe_dispatch",
    )(pos, zstart, h2_flat)


def _gmm_kernel(be_ref, nb_ref, a_ref, w1_ref, w3_ref, w2_ref, o_ref, abf_ref):
    del be_ref
    i = pl.program_id(0)
    f = pl.program_id(1)

    @pl.when((i < nb_ref[0]) & (f == 0))
    def _():
        abf_ref[...] = a_ref[...].astype(BF16)

    @pl.when(i < nb_ref[0])
    def _():
        a = abf_ref[...]
        gs = []
        for c in range(MOE_FC // MOE_SUB):
            lo, hi = c * MOE_SUB, (c + 1) * MOE_SUB
            h1 = jnp.dot(a, w1_ref[0, :, lo:hi].astype(BF16), preferred_element_type=F32)
            h3 = jnp.dot(a, w3_ref[0, :, lo:hi].astype(BF16), preferred_element_type=F32)
            gs.append((_silu(h1) * h3).astype(BF16))
        part = jnp.dot(jnp.concatenate(gs, axis=1), w2_ref[0].astype(BF16), preferred_element_type=F32)

        @pl.when(f == 0)
        def _():
            o_ref[...] = part

        @pl.when(f > 0)
        def _():
            o_ref[...] += part

    @pl.when((i >= nb_ref[0]) & (f == 0))
    def _():
        o_ref[...] = jnp.zeros_like(o_ref)


def _expert_ffn(blk_e, nb, hs, w1, w3, w2, j):
    nbmax = hs.shape[0] // MOE_TM
    nf = D_FF_EXPERT // MOE_FC

    def ieff(i, nb_ref):
        return jnp.minimum(i, nb_ref[0] - 1)

    def feff(i, f, nb_ref):
        return jnp.where(i < nb_ref[0], f, nf - 1)

    grid_spec = pltpu.PrefetchScalarGridSpec(
        num_scalar_prefetch=2,
        grid=(nbmax, nf),
        in_specs=[pl.BlockSpec((MOE_TM, D), lambda i, f, be, nbr: (ieff(i, nbr), 0)),
                  pl.BlockSpec((None, 1, D, MOE_FC),
                               lambda i, f, be, nbr: (j, be[ieff(i, nbr)], 0, feff(i, f, nbr))),
                  pl.BlockSpec((None, 1, D, MOE_FC),
                               lambda i, f, be, nbr: (j, be[ieff(i, nbr)], 0, feff(i, f, nbr))),
                  pl.BlockSpec((None, 1, MOE_FC, D),
                               lambda i, f, be, nbr: (j, be[ieff(i, nbr)], feff(i, f, nbr), 0))],
        out_specs=pl.BlockSpec((MOE_TM, D), lambda i, f, be, nbr: (i, 0)),
        scratch_shapes=[pltpu.VMEM((MOE_TM, D), BF16)],
    )
    return pl.pallas_call(
        _gmm_kernel,
        grid_spec=grid_spec,
        out_shape=jax.ShapeDtypeStruct(hs.shape, F32),
        compiler_params=_cp(("arbitrary", "arbitrary")),
        name="moe_expert_ffn",
    )(blk_e, nb, hs, w1, w3, w2)


def _combine_kernel(pos_ref, posn_ref, y_ref, route_ref, x_ref, mod_ref, gn_ref, modn_ref,
                    x2_ref, hn_ref, ybuf, sems):
    t = pl.program_id(0) * pl.num_programs(1) + pl.program_id(1)
    nsteps = pl.num_programs(0) * pl.num_programs(1)
    slot = t % 2

    def start_block(p_ref, sl):
        def issue(r, carry):
            pltpu.make_async_copy(y_ref.at[pl.ds(p_ref[0, 0, r], 1)],
                                  ybuf.at[sl, 0, pl.ds(r, 1)], sems.at[sl]).start()
            pltpu.make_async_copy(y_ref.at[pl.ds(p_ref[0, 0, RB + r], 1)],
                                  ybuf.at[sl, 1, pl.ds(r, 1)], sems.at[sl]).start()
            return carry
        lax.fori_loop(0, RB, issue, 0, unroll=DMA_UNROLL)

    @pl.when(t == 0)
    def _():
        start_block(pos_ref, 0)

    @pl.when(t + 1 < nsteps)
    def _():
        start_block(posn_ref, 1 - slot)

    for e in range(2):
        pltpu.make_async_copy(y_ref.at[pl.ds(0, RB)], ybuf.at[slot, e], sems.at[slot]).wait()
    route = route_ref[...]
    y = route[:, 2:3] * ybuf[slot, 0] + route[:, 3:4] * ybuf[slot, 1]
    mod = mod_ref[0]
    x2 = x_ref[0] + mod[5:6] * y
    x2_ref[0] = x2
    modn = modn_ref[0]
    hn_ref[0] = _norm_mod(x2, gn_ref[...], modn[1:2], modn[0:1]).astype(hn_ref.dtype)


def _combine(pos, ys, route, x1, mod, gn, modn, latent_only, hn_dtype):
    nrb = NRB - 1 if latent_only else NRB
    off = 1 if latent_only else 0
    rows = nrb * RB
    full = lambda b, i: (0, 0)
    modspec = pl.BlockSpec((1, 6, D), lambda b, i: (_mod_row(b, i + off), 0, 0))
    last = BATCH * nrb - 1
    return pl.pallas_call(
        _combine_kernel,
        grid=(BATCH, nrb),
        in_specs=[pl.BlockSpec((1, 1, 2 * RB), lambda b, i: (b * nrb + i, 0, 0), memory_space=pltpu.SMEM),
                  pl.BlockSpec((1, 1, 2 * RB), lambda b, i: (jnp.minimum(b * nrb + i + 1, last), 0, 0),
                               memory_space=pltpu.SMEM),
                  pl.BlockSpec(memory_space=pl.ANY),
                  pl.BlockSpec((RB, LANES), lambda b, i: (b * nrb + i, 0)),
                  pl.BlockSpec((1, RB, D), lambda b, i: (b, i + off, 0)),
                  modspec,
                  pl.BlockSpec((1, D), full),
                  modspec],
        out_specs=[pl.BlockSpec((1, RB, D), lambda b, i: (b, i, 0)),
                   pl.BlockSpec((1, RB, D), lambda b, i: (b, i, 0))],
        out_shape=[jax.ShapeDtypeStruct((BATCH, rows, D), F32),
                   jax.ShapeDtypeStruct((BATCH, rows, D), hn_dtype)],
        scratch_shapes=[pltpu.VMEM((2, 2, RB, D), F32), pltpu.SemaphoreType.DMA((2,))],
        compiler_params=_cp(("arbitrary", "arbitrary")),
        name="moe_combine",
    )(pos, pos, ys, route, x1, mod, gn, modn)


def _moe_ffn(h2, x1, mod, wr, w1, w3, w2, j, gn, modn, latent_only, hn_dtype):
    h2_flat = h2.reshape(NTOK, D)
    if latent_only:
        nrb = NRB - 1
        blk_of = lambda i: (i // nrb) * NRB + (i % nrb) + 1
    else:
        nrb = NRB
        blk_of = lambda i: i
    nblk = BATCH * nrb
    n_pairs = 2 * nblk * RB
    nbmax = n_pairs // MOE_TM + N_EXPERTS
    n_sorted = nbmax * MOE_TM

    route, counts = _router(h2_flat, wr, nblk, blk_of)
    cnt = counts[0, :N_EXPERTS].astype(jnp.int32)
    gsz = ((cnt + MOE_TM - 1) // MOE_TM) * MOE_TM
    gend = jnp.cumsum(gsz)
    goff = gend - gsz
    e12 = route[:, 0:2].astype(jnp.int32)
    rank = route[:, 4:6].astype(jnp.int32)
    pos = goff[e12] + rank
    pos = pos.reshape(nblk, RB, 2).transpose(0, 2, 1).reshape(nblk, 1, 2 * RB)
    nb = (gend[-1] // MOE_TM).reshape(1)
    blk_start = jnp.arange(nbmax, dtype=jnp.int32) * MOE_TM
    blk_e = jnp.minimum(jnp.sum(blk_start[:, None] >= gend[None, :], axis=1), N_EXPERTS - 1).astype(jnp.int32)

    trail = (nbmax - N_EXPERTS + jnp.arange(N_EXPERTS, dtype=jnp.int32)) * MOE_TM
    zstart = jnp.concatenate([jnp.where(cnt > 0, gend - MOE_TM, -1),
                              jnp.where(trail >= gend[-1], trail, -1)]).astype(jnp.int32)

    hs = _dispatch(pos, zstart, h2_flat, nblk, blk_of, n_sorted)
    ys = _expert_ffn(blk_e, nb, hs, w1, w3, w2, j)
    return _combine(pos, ys, route, x1, mod, gn, modn, latent_only, hn_dtype)


def _rope_tables():
    quarter = A_DQK // 4
    inv = 1.0 / (ROPE_BASE ** (jnp.arange(quarter, dtype=F32) / quarter))
    pos = jnp.arange(SEQ)
    rows = (pos // GRID_W).astype(F32)
    cols = (pos % GRID_W).astype(F32)
    ang_r = rows[:, None] * inv[None, :]
    ang_c = cols[:, None] * inv[None, :]
    cos = jnp.concatenate([jnp.cos(ang_r)] * 2 + [jnp.cos(ang_c)] * 2, axis=1)
    sin = jnp.concatenate([-jnp.sin(ang_r), jnp.sin(ang_r), -jnp.sin(ang_c), jnp.sin(ang_c)], axis=1)
    cos = jnp.concatenate([jnp.ones((CTX, A_DQK), F32), cos], axis=0)
    sin = jnp.concatenate([jnp.zeros((CTX, A_DQK), F32), sin], axis=0)
    kscale = A_DQK ** -0.5
    return (jnp.concatenate([cos, cos * kscale], axis=1),
            jnp.concatenate([sin, sin * kscale], axis=1))


def _na_bias_tables(rpb):
    rs = np.clip(np.arange(GRID_H) - NA_ROWS // 2, 0, GRID_H - NA_ROWS)
    cs = np.clip(np.arange(GRID_W) - NA_COLS // 2, 0, GRID_W - NA_COLS)
    rows = [0, 1, 2, 3, NA_ROWS // 2, GRID_H - 3, GRID_H - 2, GRID_H - 1]
    col = np.arange(GRID_W)
    dc = np.clip(col[None, :] - col[:, None] + NA_COLS - 1, 0, 2 * NA_COLS - 2)
    valid_c = (col[None, :] >= cs[:, None]) & (col[None, :] < cs[:, None] + NA_COLS)
    sel_c = np.eye(2 * NA_COLS - 1, dtype=np.float32)[dc]
    t = jnp.einsum('lhab,uvb->lhuav', rpb * LOG2E, jnp.asarray(sel_c),
                   precision=lax.Precision.HIGHEST)
    neg = np.where(valid_c, 0.0, -np.inf).astype(np.float32)[:, None, :]
    nl = rpb.shape[0]
    t = (t + jnp.asarray(neg)).reshape(nl, NPAIR, 2 * NA_NQ, (2 * NA_ROWS - 1) * GRID_W)
    first = [rs[r] - r + NA_ROWS - 1 for r in rows]
    return jnp.stack([t[..., d0 * GRID_W:d0 * GRID_W + NA_NK] for d0 in first], axis=1)


def kernel(x, c, ctx, c_ctx, w_mod, b_mod, g_norm1, g_norm2, w_in, a_conv, a_gate_b, a_hnorm_g, na_rpb,
           w_br_a, w_br_b, w_out, ffn_w1, ffn_w3, ffn_w2, moe_router, moe_w1, moe_w3, moe_w2, g_final):
    cc = jnp.concatenate([c, c_ctx[None, :], jnp.zeros((16 - BATCH - 1, D), F32)], axis=0)
    mod_all = _modulation(cc, w_mod, b_mod).reshape(DEPTH, 16, 6, D)[:, :BATCH + 1]
    mod_zero = jnp.zeros((BATCH + 1, 6, D), F32)
    rope_c, rope_s = _rope_tables()
    na_bias = _na_bias_tables(na_rpb)

    xs = jnp.concatenate([ctx, x], axis=1)
    h1 = _first_norm(xs, g_norm1[0][None, :], mod_all[0])
    out = None
    for l in range(DEPTH):
        last = l == DEPTH - 1
        mod = mod_all[l]
        wl = w_in[l]
        g0 = 3 * D
        g1 = g0 + NGATE
        wp = jnp.concatenate([wl[:, :g0], wl[:, g1:g1 + D] * (B_DH ** -0.5 * LOG2E), wl[:, g1 + D:]],
                             axis=1).astype(BF16)
        wg = jnp.pad(wl[:, g0:g1], ((0, 0), (0, LANES - NGATE))).astype(BF16)
        bg = jnp.pad(a_gate_b[l], (0, LANES - NGATE))[None, :]

        h1_flat = h1.reshape(NTOK, D)
        p3 = _in_proj(h1_flat, wp).reshape(BATCH, TT, P_COLS)
        gates, lfp = _gate_proj(h1_flat, wg, bg)

        qk = _mlstm_prep(p3, a_conv[l], rope_c, rope_s)
        hf, hbw = _mlstm_scan(qk, p3, gates.reshape(BATCH, TT, LANES), lfp.reshape(BATCH, TT, LANES))
        hb_lat = _na_attention(p3, na_bias, l)
        hb_ctx = _ctx_attention(p3)

        moe = l % 2 == 1
        x1, h2 = _merge(hf, hbw, p3, hb_lat, hb_ctx, xs, mod, a_hnorm_g[l][None, :],
                        w_br_a[l].astype(BF16), w_br_b[l].astype(BF16), w_out[l].astype(BF16),
                        g_norm2[l][None, :], F32 if moe else BF16)
        if last:
            gn, modn = g_final[None, :], mod_zero
        else:
            gn, modn = g_norm1[l + 1][None, :], mod_all[l + 1]
        j = l // 2
        if not moe:
            xs, h1 = _dense_ffn(h2, x1, mod, ffn_w1[j].astype(BF16), ffn_w3[j].astype(BF16),
                                ffn_w2[j].astype(BF16), gn, modn)
        else:
            wr = jnp.pad(moe_router[j], ((0, 0), (0, LANES - N_EXPERTS))).astype(BF16)
            xs, h1 = _moe_ffn(h2, x1, mod, wr, moe_w1, moe_w3, moe_w2, j, gn, modn, last,
                              F32 if last else BF16)
            if last:
                out = h1
    return out
```

```python
import functools

import numpy as np
import jax
import jax.numpy as jnp
from jax import lax
from jax.experimental import pallas as pl
from jax.experimental.pallas import tpu as pltpu

F32 = jnp.float32
BF16 = jnp.bfloat16

D = 1024
BATCH = 8
SEQ = 2048
CTX = 256
TT = CTX + SEQ
NTOK = BATCH * TT
DEPTH = 4
GRID_W = 64
GRID_H = SEQ // GRID_W

A_HEADS = 4
A_DQK = 128
A_DV = 256
A_QK = A_HEADS * A_DQK
A_V = A_HEADS * A_DV
ROPE_BASE = 10000.0
LCH = 256
NCH = TT // LCH

B_HEADS = 16
B_DH = 64
NA_ROWS = 8
NA_COLS = 16
NA_QROWS = 1
NA_KROWS = NA_QROWS + NA_ROWS - 1
NA_NQ = NA_QROWS * GRID_W
NA_NK = NA_KROWS * GRID_W
NA_STEPS = GRID_H // NA_QROWS

D_FF = 2816
N_EXPERTS = 8
D_FF_EXPERT = 3584
EPS = 1e-6
LOG2E = 1.4426950408889634

RB = 256
NRB = TT // RB
MM_TM = 2048
MOE_TM = 1024
MOE_FC = 512
MOE_SUB = 256
LANES = 128

PB_QK, PB_V, PB_O, PB_NQ, PB_NK, PB_NV, PB_GA, PB_GB = range(8)
P_COLS = 8 * D

VMEM_LIMIT = 56 * 1024 * 1024


def _cp(sem, vmem=VMEM_LIMIT):
    return pltpu.CompilerParams(dimension_semantics=sem, vmem_limit_bytes=vmem)


def _sigmoid(x):
    return 1.0 / (1.0 + jnp.exp(-x))


def _silu(x):
    return x * _sigmoid(x)


def _log_sigmoid(x):
    return jnp.minimum(x, 0.0) - jnp.log(1.0 + jnp.exp(-jnp.abs(x)))


def _norm_mod(x, g, sc, sh):
    ms = jnp.mean(x * x, axis=-1, keepdims=True)
    y = x * lax.rsqrt(ms + EPS)
    return (y * g) * (1.0 + sc) + sh


def _mod_row(b, i):
    return jnp.where(i == 0, BATCH, b)


def _mod_kernel(c_ref, w_ref, b_ref, o_ref):
    c = c_ref[...]
    s = _silu(c).astype(BF16)
    o_ref[0] = jnp.dot(s, w_ref[0].astype(BF16), preferred_element_type=F32) + b_ref[0]


def _modulation(cc, w_mod, b_mod):
    tn = 2048
    nl = w_mod.shape[0]
    return pl.pallas_call(
        _mod_kernel,
        grid=(nl, 6 * D // tn),
        in_specs=[pl.BlockSpec((16, D), lambda l, j: (0, 0)),
                  pl.BlockSpec((1, D, tn), lambda l, j: (l, 0, j)),
                  pl.BlockSpec((1, 1, tn), lambda l, j: (l, 0, j))],
        out_specs=pl.BlockSpec((1, 16, tn), lambda l, j: (l, 0, j)),
        out_shape=jax.ShapeDtypeStruct((nl, 16, 6 * D), F32),
        compiler_params=_cp(("parallel", "parallel")),
        name="modulation",
    )(cc, w_mod, b_mod.reshape(nl, 1, 6 * D))


def _norm_kernel(x_ref, g_ref, mod_ref, o_ref):
    mod = mod_ref[0]
    o_ref[0] = _norm_mod(x_ref[0], g_ref[...], mod[1:2], mod[0:1]).astype(o_ref.dtype)


def _first_norm(x, g, mod):
    return pl.pallas_call(
        _norm_kernel,
        grid=(BATCH, NRB),
        in_specs=[pl.BlockSpec((1, RB, D), lambda b, i: (b, i, 0)),
                  pl.BlockSpec((1, D), lambda b, i: (0, 0)),
                  pl.BlockSpec((1, 6, D), lambda b, i: (_mod_row(b, i), 0, 0))],
        out_specs=pl.BlockSpec((1, RB, D), lambda b, i: (b, i, 0)),
        out_shape=jax.ShapeDtypeStruct((BATCH, TT, D), BF16),
        compiler_params=_cp(("parallel", "parallel")),
        name="first_norm",
    )(x, g, mod)


def _mm_kernel(a_ref, w_ref, o_ref):
    o_ref[...] = jnp.dot(a_ref[...], w_ref[...], preferred_element_type=F32).astype(o_ref.dtype)


def _in_proj(h, w):
    tn = 1024
    return pl.pallas_call(
        _mm_kernel,
        grid=(P_COLS // tn, NTOK // MM_TM),
        in_specs=[pl.BlockSpec((MM_TM, D), lambda j, i: (i, 0)),
                  pl.BlockSpec((D, tn), lambda j, i: (0, j))],
        out_specs=pl.BlockSpec((MM_TM, tn), lambda j, i: (i, j)),
        out_shape=jax.ShapeDtypeStruct((NTOK, P_COLS), BF16),
        compiler_params=_cp(("parallel", "parallel")),
        name="in_proj",
    )(h, w)


NGATE = 4 * A_HEADS


def _gate_kernel(a_ref, w_ref, b_ref, g_ref, lf_ref):
    g = jnp.dot(a_ref[...], w_ref[...], preferred_element_type=F32) + b_ref[...]
    g_ref[...] = g
    lf = _log_sigmoid(g)
    p0 = lf.astype(BF16).astype(F32)
    r1 = lf - p0
    p1 = r1.astype(BF16).astype(F32)
    p2 = (r1 - p1).astype(BF16).astype(F32)
    lane = lax.broadcasted_iota(jnp.int32, (1, LANES), 1)
    parts = jnp.where(lane < NGATE, p0,
                      jnp.where(lane < 2 * NGATE, pltpu.roll(p1, NGATE, 1),
                                jnp.where(lane < 3 * NGATE, pltpu.roll(p2, 2 * NGATE, 1), 0.0)))
    lf_ref[...] = parts.astype(BF16)


def _gate_proj(h, wg, bg):
    return pl.pallas_call(
        _gate_kernel,
        grid=(NTOK // MM_TM,),
        in_specs=[pl.BlockSpec((MM_TM, D), lambda i: (i, 0)),
                  pl.BlockSpec((D, LANES), lambda i: (0, 0)),
                  pl.BlockSpec((1, LANES), lambda i: (0, 0))],
        out_specs=[pl.BlockSpec((MM_TM, LANES), lambda i: (i, 0)),
                   pl.BlockSpec((MM_TM, LANES), lambda i: (i, 0))],
        out_shape=[jax.ShapeDtypeStruct((NTOK, LANES), F32),
                   jax.ShapeDtypeStruct((NTOK, LANES), BF16)],
        compiler_params=_cp(("parallel",)),
        name="gate_proj",
    )(h, wg, bg)


def _prep_kernel(u_ref, up_ref, un_ref, w_ref, c_ref, s_ref, o_ref):
    i = pl.program_id(1)
    u = u_ref[0].astype(F32)
    prev_row = jnp.where(i >= 2, up_ref[0, 15:16, :].astype(F32), 0.0)
    next_row = jnp.where((i >= 1) & (i <= NRB - 2), un_ref[0, 0:1, :].astype(F32), 0.0)
    rid = lax.broadcasted_iota(jnp.int32, (RB, 1), 0)
    u_m1 = jnp.where(rid == 0, prev_row, pltpu.roll(u, 1, 0))
    u_p1 = jnp.where(rid == RB - 1, next_row, pltpu.roll(u, RB - 1, 0))
    w = w_ref[...]
    y = w[0:1] * u_m1 + w[1:2] * u + w[2:3] * u_p1
    y = _silu(y)
    c = c_ref[...]
    s = s_ref[...]
    cfull = jnp.concatenate([c[:, :A_DQK]] * A_HEADS + [c[:, A_DQK:]] * A_HEADS, axis=1)
    sfull = jnp.concatenate([s[:, :A_DQK]] * A_HEADS + [s[:, A_DQK:]] * A_HEADS, axis=1)
    lane = lax.broadcasted_iota(jnp.int32, (1, 2 * A_QK), 1)
    partner = jnp.where((lane & 32) == 0,
                        pltpu.roll(y, 2 * A_QK - 32, 1), pltpu.roll(y, 32, 1))
    o_ref[0] = (y * cfull + partner * sfull).astype(o_ref.dtype)


def _mlstm_prep(p3, conv_w, rope_c, rope_s):
    nb16 = TT // 16
    return pl.pallas_call(
        _prep_kernel,
        grid=(BATCH, NRB),
        in_specs=[pl.BlockSpec((1, RB, D), lambda b, i: (b, i, PB_QK)),
                  pl.BlockSpec((1, 16, D), lambda b, i: (b, jnp.maximum(i * (RB // 16) - 1, 0), PB_QK)),
                  pl.BlockSpec((1, 16, D), lambda b, i: (b, jnp.minimum((i + 1) * (RB // 16), nb16 - 1), PB_QK)),
                  pl.BlockSpec((3, D), lambda b, i: (0, 0)),
                  pl.BlockSpec((RB, 2 * A_DQK), lambda b, i: (i, 0)),
                  pl.BlockSpec((RB, 2 * A_DQK), lambda b, i: (i, 0))],
        out_specs=pl.BlockSpec((1, RB, D), lambda b, i: (b, i, 0)),
        out_shape=jax.ShapeDtypeStruct((BATCH, TT, D), BF16),
        compiler_params=_cp(("parallel", "parallel")),
        name="mlstm_prep",
    )(p3, p3, p3, conv_w, rope_c, rope_s)


def _mlstm_kernel(qkf_ref, vf_ref, gf_ref, lff_ref, qkb_ref, vb_ref, gb_ref, lfb_ref,
                  of_ref, ob_ref, ct_ref, n_ref, m_ref):
    @pl.when(pl.program_id(1) == 0)
    def _():
        ct_ref[...] = jnp.zeros_like(ct_ref)
        n_ref[...] = jnp.zeros_like(n_ref)
        m_ref[...] = jnp.zeros_like(m_ref)

    r = lax.broadcasted_iota(jnp.int32, (LCH, LCH), 0)
    c = lax.broadcasted_iota(jnp.int32, (LCH, LCH), 1)
    masks = [c <= r, c >= r]
    ins = [(qkf_ref, vf_ref, gf_ref, lff_ref, of_ref), (qkb_ref, vb_ref, gb_ref, lfb_ref, ob_ref)]
    b_all, g_all, b_t, g_t, b_end = [], [], [], [], []
    for d in range(2):
        tri = jnp.where(masks[d], 1.0, 0.0).astype(BF16)
        bc = jnp.dot(tri, ins[d][3][0], preferred_element_type=F32)
        ba = bc + pltpu.roll(bc, LANES - NGATE, 1) + pltpu.roll(bc, LANES - 2 * NGATE, 1)
        ga = ins[d][2][0]
        if d == 1:
            ba = pltpu.roll(ba, LANES - 2 * A_HEADS, 1)
            ga = pltpu.roll(ga, LANES - 2 * A_HEADS, 1)
        b_all.append(ba)
        g_all.append(ga)
        b_t.append(ba.T)
        g_t.append(ga.T)
        b_end.append(ba[LCH - 1:LCH, :] if d == 0 else ba[0:1, :])
    ones = jnp.ones((LCH, LANES), BF16)
    tn_dims = (((0,), (0,)), ((), ()))
    combos = [(d, h) for d in range(2) for h in range(A_HEADS)]
    idx = range(len(combos))
    qs = [ins[d][0][0, :, h * A_DQK:(h + 1) * A_DQK] for d, h in combos]
    ks = [ins[d][0][0, :, A_QK + h * A_DQK:A_QK + (h + 1) * A_DQK] for d, h in combos]
    qk = [lax.dot_general(qs[i], ks[i], NT_DIMS, preferred_element_type=F32) for i in idx]
    b_col = [b_all[d][:, A_HEADS + h:A_HEADS + h + 1] for d, h in combos]
    b_last = [b_end[d][:, A_HEADS + h:A_HEADS + h + 1] for d, h in combos]
    m_old = [m_ref[i][:, 0:1] for i in idx]
    r_row = [b_t[d][A_HEADS + h:A_HEADS + h + 1, :] - g_t[d][h:h + 1, :] for d, h in combos]
    m_row = [jnp.maximum(b_col[i] + m_old[i],
                         jnp.max(jnp.where(masks[d], b_col[i] - r_row[i], -jnp.inf), axis=1, keepdims=True))
             for i, (d, h) in enumerate(combos)]
    s = [(qk[i] * jnp.exp(jnp.where(masks[d], (b_col[i] - m_row[i]) - r_row[i], -jnp.inf))).astype(BF16)
         for i, (d, h) in enumerate(combos)]
    wq = [(jnp.exp(b_col[i] + m_old[i] - m_row[i]) * qs[i].astype(F32)).astype(BF16) for i in idx]
    g_col = [b_last[i] - b_col[i] + g_all[d][:, h:h + 1] for i, (d, h) in enumerate(combos)]
    m_new = [jnp.maximum(b_last[i] + m_old[i], jnp.max(g_col[i], axis=0, keepdims=True)) for i in idx]
    kw = [(jnp.exp(g_col[i] - m_new[i]) * ks[i].astype(F32)).astype(BF16) for i in idx]
    decay = [jnp.exp(b_last[i] + m_old[i] - m_new[i]) for i in idx]
    for i, (d, h) in enumerate(combos):
        v = ins[d][1][0, :, h * A_DV:(h + 1) * A_DV]
        lhs = jnp.concatenate([s[i], wq[i]], axis=1)
        den = jnp.dot(lhs, jnp.concatenate([ones, n_ref[i].astype(BF16)], axis=0),
                      preferred_element_type=F32)[:, 0:1]
        rinv = 1.0 / jnp.maximum(jnp.abs(den), jnp.exp(-m_row[i]))
        num = jnp.dot(lhs, jnp.concatenate([v, ct_ref[i].astype(BF16)], axis=0), preferred_element_type=F32)
        ins[d][4][0, :, h * A_DV:(h + 1) * A_DV] = (num * rinv).astype(of_ref.dtype)
    for i, (d, h) in enumerate(combos):
        v = ins[d][1][0, :, h * A_DV:(h + 1) * A_DV]
        ct_ref[i] = decay[i] * ct_ref[i] + lax.dot_general(kw[i], v, tn_dims, preferred_element_type=F32)
        n_ref[i] = decay[i] * n_ref[i] + lax.dot_general(kw[i], ones, tn_dims, preferred_element_type=F32)
        m_ref[i] = jnp.broadcast_to(m_new[i], (1, LANES))


def _rev_chunk(s):
    ncc = CTX // LCH
    return jnp.where(s < ncc, ncc - 1 - s, NCH + ncc - 1 - s)


def _mlstm_scan(qk, p3, gates, lfp):
    fwd = lambda b, s: (b, s, 0)
    bwd = lambda b, s: (b, _rev_chunk(s), 0)
    specs = lambda im, imv: [pl.BlockSpec((1, LCH, D), im), pl.BlockSpec((1, LCH, A_V), imv),
                             pl.BlockSpec((1, LCH, LANES), im), pl.BlockSpec((1, LCH, LANES), im)]
    nstate = 2 * A_HEADS
    return pl.pallas_call(
        _mlstm_kernel,
        grid=(BATCH, NCH),
        in_specs=(specs(fwd, lambda b, s: (b, s, PB_V))
                  + specs(bwd, lambda b, s: (b, _rev_chunk(s), PB_V))),
        out_specs=[pl.BlockSpec((1, LCH, A_V), fwd), pl.BlockSpec((1, LCH, A_V), bwd)],
        out_shape=[jax.ShapeDtypeStruct((BATCH, TT, A_V), BF16)] * 2,
        scratch_shapes=[pltpu.VMEM((nstate, A_DQK, A_DV), F32),
                        pltpu.VMEM((nstate, A_DQK, LANES), F32),
                        pltpu.VMEM((nstate, 1, LANES), F32)],
        compiler_params=_cp(("parallel", "arbitrary")),
        name="mlstm_scan",
    )(qk, p3, gates, lfp, qk, p3, gates, lfp)


NPAIR = B_HEADS // 2
NT_DIMS = (((1,), (1,)), ((), ()))


def _stack_pair(q, low):
    zero = jnp.zeros_like(q)
    return jnp.concatenate([jnp.where(low, q, zero), jnp.where(low, zero, q)], axis=0)


def _softmax_rows(s_ref, p_ref):
    s = s_ref[...]
    p = jnp.exp2(s - jnp.max(s, axis=2, keepdims=True))
    p_ref[...] = p.astype(p_ref.dtype)
    return 1.0 / jnp.sum(p, axis=2, keepdims=True)


def _na_kernel(q_ref, k_ref, v_ref, bias_ref, o_ref, s_ref, p_ref):
    row = pl.program_id(1)
    rs = jnp.clip(row - NA_ROWS // 2, 0, GRID_H - NA_ROWS)
    start = pl.multiple_of(CTX + rs * GRID_W, GRID_W)
    low = lax.broadcasted_iota(jnp.int32, (1, 2 * B_DH), 1) < B_DH
    for hp in range(NPAIR):
        lo, hi = hp * 2 * B_DH, (hp + 1) * 2 * B_DH
        q2 = _stack_pair(q_ref[0, :, lo:hi], low)
        s_ref[hp, :, 0:NA_NK] = lax.dot_general(q2, k_ref[0, pl.ds(start, NA_NK), lo:hi], NT_DIMS,
                                                preferred_element_type=F32) + bias_ref[0, 0, hp]
        s_ref[hp, :, NA_NK:] = lax.dot_general(q2, k_ref[0, 0:CTX, lo:hi], NT_DIMS,
                                               preferred_element_type=F32)
    rinv = _softmax_rows(s_ref, p_ref)
    for hp in range(NPAIR):
        lo, hi = hp * 2 * B_DH, (hp + 1) * 2 * B_DH
        o2 = (jnp.dot(p_ref[hp, :, 0:NA_NK], v_ref[0, pl.ds(start, NA_NK), lo:hi],
                      preferred_element_type=F32)
              + jnp.dot(p_ref[hp, :, NA_NK:], v_ref[0, 0:CTX, lo:hi], preferred_element_type=F32))
        o2 = o2 * rinv[hp]
        o_ref[0, :, lo:hi] = jnp.where(low, o2[0:NA_NQ], o2[NA_NQ:]).astype(o_ref.dtype)


def _na_pattern(row):
    edge = NA_ROWS // 2
    return jnp.where(row < edge, row, jnp.where(row <= GRID_H - edge, edge, row - (GRID_H - 2 * edge)))


NA_NPAT = NA_ROWS


def _na_attention(p3, bias, layer):
    qoff = CTX // NA_NQ
    return pl.pallas_call(
        _na_kernel,
        grid=(BATCH, GRID_H),
        in_specs=[pl.BlockSpec((1, NA_NQ, D), lambda b, i: (b, i + qoff, PB_NQ)),
                  pl.BlockSpec((1, TT, D), lambda b, i: (b, 0, PB_NK)),
                  pl.BlockSpec((1, TT, D), lambda b, i: (b, 0, PB_NV)),
                  pl.BlockSpec((1, 1, NPAIR, 2 * NA_NQ, NA_NK),
                               lambda b, i: (layer, _na_pattern(i), 0, 0, 0))],
        out_specs=pl.BlockSpec((1, NA_NQ, D), lambda b, i: (b, i, 0)),
        out_shape=jax.ShapeDtypeStruct((BATCH, SEQ, D), BF16),
        scratch_shapes=[pltpu.VMEM((NPAIR, 2 * NA_NQ, NA_NK + CTX), F32),
                        pltpu.VMEM((NPAIR, 2 * NA_NQ, NA_NK + CTX), BF16)],
        compiler_params=_cp(("parallel", "arbitrary")),
        name="na_attention",
    )(p3, p3, p3, bias)


def _ctx_attn_kernel(q_ref, k_ref, v_ref, o_ref, s_ref, p_ref):
    low = lax.broadcasted_iota(jnp.int32, (1, 2 * B_DH), 1) < B_DH
    for hp in range(NPAIR):
        lo, hi = hp * 2 * B_DH, (hp + 1) * 2 * B_DH
        s_ref[hp] = lax.dot_general(_stack_pair(q_ref[0, :, lo:hi], low), k_ref[0, :, lo:hi], NT_DIMS,
                                    preferred_element_type=F32)
    rinv = _softmax_rows(s_ref, p_ref)
    for hp in range(NPAIR):
        lo, hi = hp * 2 * B_DH, (hp + 1) * 2 * B_DH
        o2 = jnp.dot(p_ref[hp], v_ref[0, :, lo:hi], preferred_element_type=F32) * rinv[hp]
        o_ref[0, :, lo:hi] = jnp.where(low, o2[0:CTX], o2[CTX:]).astype(o_ref.dtype)


def _ctx_attention(p3):
    return pl.pallas_call(
        _ctx_attn_kernel,
        grid=(BATCH,),
        in_specs=[pl.BlockSpec((1, CTX, D), lambda b: (b, 0, PB_NQ)),
                  pl.BlockSpec((1, CTX, D), lambda b: (b, 0, PB_NK)),
                  pl.BlockSpec((1, CTX, D), lambda b: (b, 0, PB_NV))],
        out_specs=pl.BlockSpec((1, CTX, D), lambda b: (b, 0, 0)),
        out_shape=jax.ShapeDtypeStruct((BATCH, CTX, D), BF16),
        scratch_shapes=[pltpu.VMEM((NPAIR, 2 * CTX, CTX), F32),
                        pltpu.VMEM((NPAIR, 2 * CTX, CTX), BF16)],
        compiler_params=_cp(("parallel",)),
        name="ctx_attention",
    )(p3, p3, p3)


def _merge_kernel(hf_ref, hbw_ref, o_ref, ga_ref, gb_ref, hnl_ref, hnc_ref, x_ref, mod_ref, ghn_ref,
                  wa_ref, wb_ref, wo_ref, g2_ref, x1_ref, h2_ref):
    hn = jnp.where(pl.program_id(1) == 0, hnc_ref[0], hnl_ref[0])
    bm = jnp.dot(hn, wb_ref[...], preferred_element_type=F32)
    hs = hf_ref[0].astype(F32) + hbw_ref[0].astype(F32)
    parts = []
    for h in range(A_HEADS):
        seg = hs[:, h * A_DV:(h + 1) * A_DV]
        mu = jnp.mean(seg, axis=-1, keepdims=True)
        cen = seg - mu
        var = jnp.mean(cen * cen, axis=-1, keepdims=True)
        parts.append(cen * lax.rsqrt(var + EPS))
    ya = jnp.concatenate(parts, axis=1) * ghn_ref[...] * _sigmoid(o_ref[0].astype(F32))
    a = jnp.dot(ya.astype(BF16), wa_ref[...], preferred_element_type=F32)
    mrg = _sigmoid(ga_ref[0].astype(F32)) * a + _sigmoid(gb_ref[0].astype(F32)) * bm
    y = jnp.dot(mrg.astype(BF16), wo_ref[...], preferred_element_type=F32)
    mod = mod_ref[0]
    x1 = x_ref[0] + mod[2:3] * y
    x1_ref[0] = x1
    h2_ref[0] = _norm_mod(x1, g2_ref[...], mod[4:5], mod[3:4]).astype(h2_ref.dtype)


def _merge(hf, hbw, p3, hb_lat, hb_ctx, x, mod, ghn, wa, wb, wo, g2, h2_dtype):
    row = lambda b, i: (b, i, 0)
    full = lambda b, i: (0, 0)
    return pl.pallas_call(
        _merge_kernel,
        grid=(BATCH, NRB),
        in_specs=[pl.BlockSpec((1, RB, A_V), row),
                  pl.BlockSpec((1, RB, A_V), row),
                  pl.BlockSpec((1, RB, D), lambda b, i: (b, i, PB_O)),
                  pl.BlockSpec((1, RB, D), lambda b, i: (b, i, PB_GA)),
                  pl.BlockSpec((1, RB, D), lambda b, i: (b, i, PB_GB)),
                  pl.BlockSpec((1, RB, D), lambda b, i: (b, jnp.maximum(i - 1, 0), 0)),
                  pl.BlockSpec((1, CTX, D), lambda b, i: (b, 0, 0)),
                  pl.BlockSpec((1, RB, D), row),
                  pl.BlockSpec((1, 6, D), lambda b, i: (_mod_row(b, i), 0, 0)),
                  pl.BlockSpec((1, A_V), full),
                  pl.BlockSpec((A_V, D), full),
                  pl.BlockSpec((D, D), full),
                  pl.BlockSpec((D, D), full),
                  pl.BlockSpec((1, D), full)],
        out_specs=[pl.BlockSpec((1, RB, D), row), pl.BlockSpec((1, RB, D), row)],
        out_shape=[jax.ShapeDtypeStruct((BATCH, TT, D), F32),
                   jax.ShapeDtypeStruct((BATCH, TT, D), h2_dtype)],
        compiler_params=_cp(("parallel", "parallel")),
        name="merge",
    )(hf, hbw, p3, p3, p3, hb_lat, hb_ctx, x, mod, ghn, wa, wb, wo, g2)


FFN_CH = D_FF // 2


def _ffn_kernel(h_ref, x_ref, mod_ref, w1_ref, w3_ref, w2_ref, gn_ref, modn_ref, x2_ref, hn_ref):
    h = h_ref[0]
    y = None
    for cidx in range(D_FF // FFN_CH):
        lo, hi = cidx * FFN_CH, (cidx + 1) * FFN_CH
        a = jnp.dot(h, w1_ref[:, lo:hi], preferred_element_type=F32)
        g = jnp.dot(h, w3_ref[:, lo:hi], preferred_element_type=F32)
        part = jnp.dot((_silu(a) * g).astype(BF16), w2_ref[lo:hi, :], preferred_element_type=F32)
        y = part if y is None else y + part
    mod = mod_ref[0]
    x2 = x_ref[0] + mod[5:6] * y
    x2_ref[0] = x2
    modn = modn_ref[0]
    hn_ref[0] = _norm_mod(x2, gn_ref[...], modn[1:2], modn[0:1]).astype(hn_ref.dtype)


def _dense_ffn(h2, x1, mod, w1, w3, w2, gn, modn):
    row = lambda b, i: (b, i, 0)
    full = lambda b, i: (0, 0)
    modspec = pl.BlockSpec((1, 6, D), lambda b, i: (_mod_row(b, i), 0, 0))
    return pl.pallas_call(
        _ffn_kernel,
        grid=(BATCH, NRB),
        in_specs=[pl.BlockSpec((1, RB, D), row), pl.BlockSpec((1, RB, D), row), modspec,
                  pl.BlockSpec((D, D_FF), full), pl.BlockSpec((D, D_FF), full),
                  pl.BlockSpec((D_FF, D), full), pl.BlockSpec((1, D), full), modspec],
        out_specs=[pl.BlockSpec((1, RB, D), row), pl.BlockSpec((1, RB, D), row)],
        out_shape=[jax.ShapeDtypeStruct((BATCH, TT, D), F32),
                   jax.ShapeDtypeStruct((BATCH, TT, D), BF16)],
        compiler_params=_cp(("parallel", "parallel")),
        name="dense_ffn",
    )(h2, x1, mod, w1, w3, w2, gn, modn)


def _router_kernel(h_ref, wr_ref, route_ref, cnt_ref, run_ref):
    i = pl.program_id(0)

    @pl.when(i == 0)
    def _():
        run_ref[...] = jnp.zeros_like(run_ref)

    logits = jnp.dot(h_ref[...].astype(BF16), wr_ref[...], preferred_element_type=F32)
    lane = lax.broadcasted_iota(jnp.int32, (RB, LANES), 1).astype(F32)
    lg = jnp.where(lane < N_EXPERTS, logits, -jnp.inf)
    v1 = jnp.max(lg, axis=1, keepdims=True)
    i1 = jnp.min(jnp.where(lg == v1, lane, float(LANES)), axis=1, keepdims=True)
    lg2 = jnp.where(lane == i1, -jnp.inf, lg)
    v2 = jnp.max(lg2, axis=1, keepdims=True)
    i2 = jnp.min(jnp.where(lg2 == v2, lane, float(LANES)), axis=1, keepdims=True)
    e = jnp.exp(v2 - v1)
    w1 = 1.0 / (1.0 + e)
    w2 = e / (1.0 + e)
    oh1 = (lane == i1).astype(F32)
    oh2 = (lane == i2).astype(F32)
    r = lax.broadcasted_iota(jnp.int32, (RB, RB), 0)
    c = lax.broadcasted_iota(jnp.int32, (RB, RB), 1)
    tri = (r > c).astype(BF16)
    cs1 = jnp.dot(tri, oh1.astype(BF16), preferred_element_type=F32)
    cs2 = jnp.dot(tri, oh2.astype(BF16), preferred_element_type=F32)
    tot1 = jnp.sum(oh1, axis=0, keepdims=True)
    tot2 = jnp.sum(oh2, axis=0, keepdims=True)
    run = run_ref[...]
    rank1 = jnp.sum(oh1 * (run + cs1), axis=1, keepdims=True)
    rank2 = jnp.sum(oh2 * (run + tot1 + cs2), axis=1, keepdims=True)
    new_run = run + tot1 + tot2
    run_ref[...] = new_run
    cnt_ref[...] = new_run
    out = jnp.where(lane == 0, i1,
          jnp.where(lane == 1, i2,
          jnp.where(lane == 2, w1,
          jnp.where(lane == 3, w2,
          jnp.where(lane == 4, rank1,
          jnp.where(lane == 5, rank2, 0.0))))))
    route_ref[...] = out


def _router(h2_flat, wr, nblk, blk_of):
    return pl.pallas_call(
        _router_kernel,
        grid=(nblk,),
        in_specs=[pl.BlockSpec((RB, D), lambda i: (blk_of(i), 0)),
                  pl.BlockSpec((D, LANES), lambda i: (0, 0))],
        out_specs=[pl.BlockSpec((RB, LANES), lambda i: (i, 0)),
                   pl.BlockSpec((1, LANES), lambda i: (0, 0))],
        out_shape=[jax.ShapeDtypeStruct((nblk * RB, LANES), F32),
                   jax.ShapeDtypeStruct((1, LANES), F32)],
        scratch_shapes=[pltpu.VMEM((1, LANES), F32)],
        compiler_params=_cp(("arbitrary",)),
        name="moe_router",
    )(h2_flat, wr)


DMA_UNROLL = 8
N_ZERO = 2 * N_EXPERTS


def _dispatch_kernel(pos_ref, zstart_ref, h_ref, o_ref, zbuf, stage, sems, zsem):
    @pl.when(pl.program_id(0) == 0)
    def _():
        zbuf[...] = jnp.zeros_like(zbuf)
        for z in range(N_ZERO):
            @pl.when(zstart_ref[z] >= 0)
            def _():
                zs = pl.multiple_of(zstart_ref[z], MOE_TM)
                pltpu.make_async_copy(zbuf, o_ref.at[pl.ds(zs, MOE_TM)], zsem).start()
        for z in range(N_ZERO):
            @pl.when(zstart_ref[z] >= 0)
            def _():
                pltpu.make_async_copy(zbuf, o_ref.at[pl.ds(0, MOE_TM)], zsem).wait()

    step = pl.program_id(0)
    slot = step % 2
    stage[slot] = h_ref[...]

    def issue(t, carry):
        src = stage.at[slot, pl.ds(t, 1)]
        pltpu.make_async_copy(src, o_ref.at[pl.ds(pos_ref[0, 0, t], 1)], sems.at[slot]).start(priority=0)
        pltpu.make_async_copy(src, o_ref.at[pl.ds(pos_ref[0, 0, RB + t], 1)], sems.at[slot]).start(priority=1)
        return carry

    lax.fori_loop(0, RB, issue, 0, unroll=DMA_UNROLL)

    def drain(sl):
        for _ in range(2):
            pltpu.make_async_copy(stage.at[sl], o_ref.at[pl.ds(0, RB)], sems.at[sl]).wait()

    @pl.when(step > 0)
    def _():
        drain(1 - slot)

    @pl.when(step == pl.num_programs(0) - 1)
    def _():
        drain(slot)


def _dispatch(pos, zstart, h2_flat, nblk, blk_of, n_sorted):
    return pl.pallas_call(
        _dispatch_kernel,
        grid=(nblk,),
        in_specs=[pl.BlockSpec((1, 1, 2 * RB), lambda i: (i, 0, 0), memory_space=pltpu.SMEM),
                  pl.BlockSpec(memory_space=pltpu.SMEM),
                  pl.BlockSpec((RB, D), lambda i: (blk_of(i), 0))],
        out_specs=pl.BlockSpec(memory_space=pl.ANY),
        out_shape=jax.ShapeDtypeStruct((n_sorted, D), F32),
        scratch_shapes=[pltpu.VMEM((MOE_TM, D), F32), pltpu.VMEM((2, RB, D), F32),
                        pltpu.SemaphoreType.DMA((2,)), pltpu.SemaphoreType.DMA(())],
        compiler_params=_cp(("arbitrary",)),
        name="moe_dispatch",
    )(pos, zstart, h2_flat)


def _gmm_kernel(be_ref, nb_ref, a_ref, w1_ref, w3_ref, w2_ref, o_ref, abf_ref):
    del be_ref
    i = pl.program_id(0)
    f = pl.program_id(1)

    @pl.when((i < nb_ref[0]) & (f == 0))
    def _():
        abf_ref[...] = a_ref[...].astype(BF16)

    @pl.when(i < nb_ref[0])
    def _():
        a = abf_ref[...]
        gs = []
        for c in range(MOE_FC // MOE_SUB):
            lo, hi = c * MOE_SUB, (c + 1) * MOE_SUB
            h1 = jnp.dot(a, w1_ref[0, :, lo:hi].astype(BF16), preferred_element_type=F32)
            h3 = jnp.dot(a, w3_ref[0, :, lo:hi].astype(BF16), preferred_element_type=F32)
            gs.append((_silu(h1) * h3).astype(BF16))
        part = jnp.dot(jnp.concatenate(gs, axis=1), w2_ref[0].astype(BF16), preferred_element_type=F32)

        @pl.when(f == 0)
        def _():
            o_ref[...] = part

        @pl.when(f > 0)
        def _():
            o_ref[...] += part

    @pl.when((i >= nb_ref[0]) & (f == 0))
    def _():
        o_ref[...] = jnp.zeros_like(o_ref)


def _expert_ffn(blk_e, nb, hs, w1, w3, w2, j):
    nbmax = hs.shape[0] // MOE_TM
    nf = D_FF_EXPERT // MOE_FC

    def ieff(i, nb_ref):
        return jnp.minimum(i, nb_ref[0] - 1)

    def feff(i, f, nb_ref):
        return jnp.where(i < nb_ref[0], f, nf - 1)

    grid_spec = pltpu.PrefetchScalarGridSpec(
        num_scalar_prefetch=2,
        grid=(nbmax, nf),
        in_specs=[pl.BlockSpec((MOE_TM, D), lambda i, f, be, nbr: (ieff(i, nbr), 0)),
                  pl.BlockSpec((None, 1, D, MOE_FC),
                               lambda i, f, be, nbr: (j, be[ieff(i, nbr)], 0, feff(i, f, nbr))),
                  pl.BlockSpec((None, 1, D, MOE_FC),
                               lambda i, f, be, nbr: (j, be[ieff(i, nbr)], 0, feff(i, f, nbr))),
                  pl.BlockSpec((None, 1, MOE_FC, D),
                               lambda i, f, be, nbr: (j, be[ieff(i, nbr)], feff(i, f, nbr), 0))],
        out_specs=pl.BlockSpec((MOE_TM, D), lambda i, f, be, nbr: (i, 0)),
        scratch_shapes=[pltpu.VMEM((MOE_TM, D), BF16)],
    )
    return pl.pallas_call(
        _gmm_kernel,
        grid_spec=grid_spec,
        out_shape=jax.ShapeDtypeStruct(hs.shape, F32),
        compiler_params=_cp(("arbitrary", "arbitrary")),
        name="moe_expert_ffn",
    )(blk_e, nb, hs, w1, w3, w2)


def _combine_kernel(pos_ref, posn_ref, y_ref, route_ref, x_ref, mod_ref, gn_ref, modn_ref,
                    x2_ref, hn_ref, ybuf, sems):
    t = pl.program_id(0) * pl.num_programs(1) + pl.program_id(1)
    nsteps = pl.num_programs(0) * pl.num_programs(1)
    slot = t % 2

    def start_block(p_ref, sl):
        def issue(r, carry):
            pltpu.make_async_copy(y_ref.at[pl.ds(p_ref[0, 0, r], 1)],
                                  ybuf.at[sl, 0, pl.ds(r, 1)], sems.at[sl]).start(priority=0)
            pltpu.make_async_copy(y_ref.at[pl.ds(p_ref[0, 0, RB + r], 1)],
                                  ybuf.at[sl, 1, pl.ds(r, 1)], sems.at[sl]).start(priority=1)
            return carry
        lax.fori_loop(0, RB, issue, 0, unroll=DMA_UNROLL)

    @pl.when(t == 0)
    def _():
        start_block(pos_ref, 0)

    @pl.when(t + 1 < nsteps)
    def _():
        start_block(posn_ref, 1 - slot)

    for e in range(2):
        pltpu.make_async_copy(y_ref.at[pl.ds(0, RB)], ybuf.at[slot, e], sems.at[slot]).wait()
    route = route_ref[...]
    y = route[:, 2:3] * ybuf[slot, 0] + route[:, 3:4] * ybuf[slot, 1]
    mod = mod_ref[0]
    x2 = x_ref[0] + mod[5:6] * y
    x2_ref[0] = x2
    modn = modn_ref[0]
    hn_ref[0] = _norm_mod(x2, gn_ref[...], modn[1:2], modn[0:1]).astype(hn_ref.dtype)


def _combine(pos, ys, route, x1, mod, gn, modn, latent_only, hn_dtype):
    nrb = NRB - 1 if latent_only else NRB
    off = 1 if latent_only else 0
    rows = nrb * RB
    full = lambda b, i: (0, 0)
    modspec = pl.BlockSpec((1, 6, D), lambda b, i: (_mod_row(b, i + off), 0, 0))
    last = BATCH * nrb - 1
    return pl.pallas_call(
        _combine_kernel,
        grid=(BATCH, nrb),
        in_specs=[pl.BlockSpec((1, 1, 2 * RB), lambda b, i: (b * nrb + i, 0, 0), memory_space=pltpu.SMEM),
                  pl.BlockSpec((1, 1, 2 * RB), lambda b, i: (jnp.minimum(b * nrb + i + 1, last), 0, 0),
                               memory_space=pltpu.SMEM),
                  pl.BlockSpec(memory_space=pl.ANY),
                  pl.BlockSpec((RB, LANES), lambda b, i: (b * nrb + i, 0)),
                  pl.BlockSpec((1, RB, D), lambda b, i: (b, i + off, 0)),
                  modspec,
                  pl.BlockSpec((1, D), full),
                  modspec],
        out_specs=[pl.BlockSpec((1, RB, D), lambda b, i: (b, i, 0)),
                   pl.BlockSpec((1, RB, D), lambda b, i: (b, i, 0))],
        out_shape=[jax.ShapeDtypeStruct((BATCH, rows, D), F32),
                   jax.ShapeDtypeStruct((BATCH, rows, D), hn_dtype)],
        scratch_shapes=[pltpu.VMEM((2, 2, RB, D), F32), pltpu.SemaphoreType.DMA((2,))],
        compiler_params=_cp(("arbitrary", "arbitrary")),
        name="moe_combine",
    )(pos, pos, ys, route, x1, mod, gn, modn)


def _moe_ffn(h2, x1, mod, wr, w1, w3, w2, j, gn, modn, latent_only, hn_dtype):
    h2_flat = h2.reshape(NTOK, D)
    if latent_only:
        nrb = NRB - 1
        blk_of = lambda i: (i // nrb) * NRB + (i % nrb) + 1
    else:
        nrb = NRB
        blk_of = lambda i: i
    nblk = BATCH * nrb
    n_pairs = 2 * nblk * RB
    nbmax = n_pairs // MOE_TM + N_EXPERTS
    n_sorted = nbmax * MOE_TM

    route, counts = _router(h2_flat, wr, nblk, blk_of)
    cnt = counts[0, :N_EXPERTS].astype(jnp.int32)
    gsz = ((cnt + MOE_TM - 1) // MOE_TM) * MOE_TM
    gend = jnp.cumsum(gsz)
    goff = gend - gsz
    e12 = route[:, 0:2].astype(jnp.int32)
    rank = route[:, 4:6].astype(jnp.int32)
    pos = goff[e12] + rank
    pos = pos.reshape(nblk, RB, 2).transpose(0, 2, 1).reshape(nblk, 1, 2 * RB)
    nb = (gend[-1] // MOE_TM).reshape(1)
    blk_start = jnp.arange(nbmax, dtype=jnp.int32) * MOE_TM
    blk_e = jnp.minimum(jnp.sum(blk_start[:, None] >= gend[None, :], axis=1), N_EXPERTS - 1).astype(jnp.int32)

    trail = (nbmax - N_EXPERTS + jnp.arange(N_EXPERTS, dtype=jnp.int32)) * MOE_TM
    zstart = jnp.concatenate([jnp.where(cnt > 0, gend - MOE_TM, -1),
                              jnp.where(trail >= gend[-1], trail, -1)]).astype(jnp.int32)

    hs = _dispatch(pos, zstart, h2_flat, nblk, blk_of, n_sorted)
    ys = _expert_ffn(blk_e, nb, hs, w1, w3, w2, j)
    return _combine(pos, ys, route, x1, mod, gn, modn, latent_only, hn_dtype)


def _rope_tables():
    quarter = A_DQK // 4
    inv = 1.0 / (ROPE_BASE ** (jnp.arange(quarter, dtype=F32) / quarter))
    pos = jnp.arange(SEQ)
    rows = (pos // GRID_W).astype(F32)
    cols = (pos % GRID_W).astype(F32)
    ang_r = rows[:, None] * inv[None, :]
    ang_c = cols[:, None] * inv[None, :]
    cos = jnp.concatenate([jnp.cos(ang_r)] * 2 + [jnp.cos(ang_c)] * 2, axis=1)
    sin = jnp.concatenate([-jnp.sin(ang_r), jnp.sin(ang_r), -jnp.sin(ang_c), jnp.sin(ang_c)], axis=1)
    cos = jnp.concatenate([jnp.ones((CTX, A_DQK), F32), cos], axis=0)
    sin = jnp.concatenate([jnp.zeros((CTX, A_DQK), F32), sin], axis=0)
    kscale = A_DQK ** -0.5
    return (jnp.concatenate([cos, cos * kscale], axis=1),
            jnp.concatenate([sin, sin * kscale], axis=1))


def _na_bias_tables(rpb):
    rs = np.clip(np.arange(GRID_H) - NA_ROWS // 2, 0, GRID_H - NA_ROWS)
    cs = np.clip(np.arange(GRID_W) - NA_COLS // 2, 0, GRID_W - NA_COLS)
    rows = [0, 1, 2, 3, NA_ROWS // 2, GRID_H - 3, GRID_H - 2, GRID_H - 1]
    col = np.arange(GRID_W)
    dc = np.clip(col[None, :] - col[:, None] + NA_COLS - 1, 0, 2 * NA_COLS - 2)
    valid_c = (col[None, :] >= cs[:, None]) & (col[None, :] < cs[:, None] + NA_COLS)
    sel_c = np.eye(2 * NA_COLS - 1, dtype=np.float32)[dc]
    t = jnp.einsum('lhab,uvb->lhuav', rpb * LOG2E, jnp.asarray(sel_c),
                   precision=lax.Precision.HIGHEST)
    neg = np.where(valid_c, 0.0, -np.inf).astype(np.float32)[:, None, :]
    nl = rpb.shape[0]
    t = (t + jnp.asarray(neg)).reshape(nl, NPAIR, 2 * NA_NQ, (2 * NA_ROWS - 1) * GRID_W)
    first = [rs[r] - r + NA_ROWS - 1 for r in rows]
    return jnp.stack([t[..., d0 * GRID_W:d0 * GRID_W + NA_NK] for d0 in first], axis=1)


def kernel(x, c, ctx, c_ctx, w_mod, b_mod, g_norm1, g_norm2, w_in, a_conv, a_gate_b, a_hnorm_g, na_rpb,
           w_br_a, w_br_b, w_out, ffn_w1, ffn_w3, ffn_w2, moe_router, moe_w1, moe_w3, moe_w2, g_final):
    cc = jnp.concatenate([c, c_ctx[None, :], jnp.zeros((16 - BATCH - 1, D), F32)], axis=0)
    mod_all = _modulation(cc, w_mod, b_mod).reshape(DEPTH, 16, 6, D)[:, :BATCH + 1]
    mod_zero = jnp.zeros((BATCH + 1, 6, D), F32)
    rope_c, rope_s = _rope_tables()
    na_bias = _na_bias_tables(na_rpb)

    xs = jnp.concatenate([ctx, x], axis=1)
    h1 = _first_norm(xs, g_norm1[0][None, :], mod_all[0])
    out = None
    for l in range(DEPTH):
        last = l == DEPTH - 1
        mod = mod_all[l]
        wl = w_in[l]
        g0 = 3 * D
        g1 = g0 + NGATE
        wp = jnp.concatenate([wl[:, :g0], wl[:, g1:g1 + D] * (B_DH ** -0.5 * LOG2E), wl[:, g1 + D:]],
                             axis=1).astype(BF16)
        wg = jnp.pad(wl[:, g0:g1], ((0, 0), (0, LANES - NGATE))).astype(BF16)
        bg = jnp.pad(a_gate_b[l], (0, LANES - NGATE))[None, :]

        h1_flat = h1.reshape(NTOK, D)
        p3 = _in_proj(h1_flat, wp).reshape(BATCH, TT, P_COLS)
        gates, lfp = _gate_proj(h1_flat, wg, bg)

        qk = _mlstm_prep(p3, a_conv[l], rope_c, rope_s)
        hf, hbw = _mlstm_scan(qk, p3, gates.reshape(BATCH, TT, LANES), lfp.reshape(BATCH, TT, LANES))
        hb_lat = _na_attention(p3, na_bias, l)
        hb_ctx = _ctx_attention(p3)

        moe = l % 2 == 1
        x1, h2 = _merge(hf, hbw, p3, hb_lat, hb_ctx, xs, mod, a_hnorm_g[l][None, :],
                        w_br_a[l].astype(BF16), w_br_b[l].astype(BF16), w_out[l].astype(BF16),
                        g_norm2[l][None, :], F32 if moe else BF16)
        if last:
            gn, modn = g_final[None, :], mod_zero
        else:
            gn, modn = g_norm1[l + 1][None, :], mod_all[l + 1]
        j = l // 2
        if not moe:
            xs, h1 = _dense_ffn(h2, x1, mod, ffn_w1[j].astype(BF16), ffn_w3[j].astype(BF16),
                                ffn_w2[j].astype(BF16), gn, modn)
        else:
            wr = jnp.pad(moe_router[j], ((0, 0), (0, LANES - N_EXPERTS))).astype(BF16)
            xs, h1 = _moe_ffn(h2, x1, mod, wr, moe_w1, moe_w3, moe_w2, j, gn, modn, last,
                              F32 if last else BF16)
            if last:
                out = h1
    return out
```

```python
import functools

import numpy as np
import jax
import jax.numpy as jnp
from jax import lax
from jax.experimental import pallas as pl
from jax.experimental.pallas import tpu as pltpu

F32 = jnp.float32
BF16 = jnp.bfloat16

D = 1024
BATCH = 8
SEQ = 2048
CTX = 256
TT = CTX + SEQ
NTOK = BATCH * TT
DEPTH = 4
GRID_W = 64
GRID_H = SEQ // GRID_W

A_HEADS = 4
A_DQK = 128
A_DV = 256
A_QK = A_HEADS * A_DQK
A_V = A_HEADS * A_DV
ROPE_BASE = 10000.0
LCH = 256
NCH = TT // LCH

B_HEADS = 16
B_DH = 64
NA_ROWS = 8
NA_COLS = 16
NA_QROWS = 1
NA_KROWS = NA_QROWS + NA_ROWS - 1
NA_NQ = NA_QROWS * GRID_W
NA_NK = NA_KROWS * GRID_W
NA_STEPS = GRID_H // NA_QROWS

D_FF = 2816
N_EXPERTS = 8
D_FF_EXPERT = 3584
EPS = 1e-6
LOG2E = 1.4426950408889634

RB = 256
NRB = TT // RB
MM_TM = 2048
MOE_TM = 1024
MOE_FC = 512
MOE_SUB = 256
LANES = 128

PB_QK, PB_V, PB_O, PB_NQ, PB_NK, PB_NV, PB_GA, PB_GB = range(8)
P_COLS = 8 * D

VMEM_LIMIT = 56 * 1024 * 1024


def _cp(sem, vmem=VMEM_LIMIT):
    return pltpu.CompilerParams(dimension_semantics=sem, vmem_limit_bytes=vmem)


def _sigmoid(x):
    return 1.0 / (1.0 + jnp.exp(-x))


def _silu(x):
    return x * _sigmoid(x)


def _log_sigmoid(x):
    return jnp.minimum(x, 0.0) - jnp.log(1.0 + jnp.exp(-jnp.abs(x)))


def _norm_mod(x, g, sc, sh):
    ms = jnp.mean(x * x, axis=-1, keepdims=True)
    y = x * lax.rsqrt(ms + EPS)
    return (y * g) * (1.0 + sc) + sh


def _mod_row(b, i):
    return jnp.where(i == 0, BATCH, b)


def _mod_kernel(c_ref, w_ref, b_ref, o_ref):
    c = c_ref[...]
    s = _silu(c).astype(BF16)
    o_ref[0] = jnp.dot(s, w_ref[0].astype(BF16), preferred_element_type=F32) + b_ref[0]


def _modulation(cc, w_mod, b_mod):
    tn = 2048
    nl = w_mod.shape[0]
    return pl.pallas_call(
        _mod_kernel,
        grid=(nl, 6 * D // tn),
        in_specs=[pl.BlockSpec((16, D), lambda l, j: (0, 0)),
                  pl.BlockSpec((1, D, tn), lambda l, j: (l, 0, j)),
                  pl.BlockSpec((1, 1, tn), lambda l, j: (l, 0, j))],
        out_specs=pl.BlockSpec((1, 16, tn), lambda l, j: (l, 0, j)),
        out_shape=jax.ShapeDtypeStruct((nl, 16, 6 * D), F32),
        compiler_params=_cp(("parallel", "parallel")),
        name="modulation",
    )(cc, w_mod, b_mod.reshape(nl, 1, 6 * D))


def _norm_kernel(x_ref, g_ref, mod_ref, o_ref):
    mod = mod_ref[0]
    o_ref[0] = _norm_mod(x_ref[0], g_ref[...], mod[1:2], mod[0:1]).astype(o_ref.dtype)


def _first_norm(x, g, mod):
    return pl.pallas_call(
        _norm_kernel,
        grid=(BATCH, NRB),
        in_specs=[pl.BlockSpec((1, RB, D), lambda b, i: (b, i, 0)),
                  pl.BlockSpec((1, D), lambda b, i: (0, 0)),
                  pl.BlockSpec((1, 6, D), lambda b, i: (_mod_row(b, i), 0, 0))],
        out_specs=pl.BlockSpec((1, RB, D), lambda b, i: (b, i, 0)),
        out_shape=jax.ShapeDtypeStruct((BATCH, TT, D), BF16),
        compiler_params=_cp(("parallel", "parallel")),
        name="first_norm",
    )(x, g, mod)


def _mm_kernel(a_ref, w_ref, o_ref):
    o_ref[...] = jnp.dot(a_ref[...], w_ref[...], preferred_element_type=F32).astype(o_ref.dtype)


def _in_proj(h, w):
    tn = 1024
    return pl.pallas_call(
        _mm_kernel,
        grid=(P_COLS // tn, NTOK // MM_TM),
        in_specs=[pl.BlockSpec((MM_TM, D), lambda j, i: (i, 0)),
                  pl.BlockSpec((D, tn), lambda j, i: (0, j))],
        out_specs=pl.BlockSpec((MM_TM, tn), lambda j, i: (i, j)),
        out_shape=jax.ShapeDtypeStruct((NTOK, P_COLS), BF16),
        compiler_params=_cp(("parallel", "parallel")),
        name="in_proj",
    )(h, w)


NGATE = 4 * A_HEADS


def _gate_kernel(a_ref, w_ref, b_ref, g_ref, lf_ref):
    g = jnp.dot(a_ref[...], w_ref[...], preferred_element_type=F32) + b_ref[...]
    g_ref[...] = g
    lf = _log_sigmoid(g)
    p0 = lf.astype(BF16).astype(F32)
    r1 = lf - p0
    p1 = r1.astype(BF16).astype(F32)
    p2 = (r1 - p1).astype(BF16).astype(F32)
    lane = lax.broadcasted_iota(jnp.int32, (1, LANES), 1)
    parts = jnp.where(lane < NGATE, p0,
                      jnp.where(lane < 2 * NGATE, pltpu.roll(p1, NGATE, 1),
                                jnp.where(lane < 3 * NGATE, pltpu.roll(p2, 2 * NGATE, 1), 0.0)))
    lf_ref[...] = parts.astype(BF16)


def _gate_proj(h, wg, bg):
    return pl.pallas_call(
        _gate_kernel,
        grid=(NTOK // MM_TM,),
        in_specs=[pl.BlockSpec((MM_TM, D), lambda i: (i, 0)),
                  pl.BlockSpec((D, LANES), lambda i: (0, 0)),
                  pl.BlockSpec((1, LANES), lambda i: (0, 0))],
        out_specs=[pl.BlockSpec((MM_TM, LANES), lambda i: (i, 0)),
                   pl.BlockSpec((MM_TM, LANES), lambda i: (i, 0))],
        out_shape=[jax.ShapeDtypeStruct((NTOK, LANES), F32),
                   jax.ShapeDtypeStruct((NTOK, LANES), BF16)],
        compiler_params=_cp(("parallel",)),
        name="gate_proj",
    )(h, wg, bg)


def _prep_kernel(u_ref, up_ref, un_ref, w_ref, c_ref, s_ref, o_ref):
    i = pl.program_id(1)
    u = u_ref[0].astype(F32)
    prev_row = jnp.where(i >= 2, up_ref[0, 15:16, :].astype(F32), 0.0)
    next_row = jnp.where((i >= 1) & (i <= NRB - 2), un_ref[0, 0:1, :].astype(F32), 0.0)
    rid = lax.broadcasted_iota(jnp.int32, (RB, 1), 0)
    u_m1 = jnp.where(rid == 0, prev_row, pltpu.roll(u, 1, 0))
    u_p1 = jnp.where(rid == RB - 1, next_row, pltpu.roll(u, RB - 1, 0))
    w = w_ref[...]
    y = w[0:1] * u_m1 + w[1:2] * u + w[2:3] * u_p1
    y = _silu(y)
    c = c_ref[...]
    s = s_ref[...]
    cfull = jnp.concatenate([c[:, :A_DQK]] * A_HEADS + [c[:, A_DQK:]] * A_HEADS, axis=1)
    sfull = jnp.concatenate([s[:, :A_DQK]] * A_HEADS + [s[:, A_DQK:]] * A_HEADS, axis=1)
    lane = lax.broadcasted_iota(jnp.int32, (1, 2 * A_QK), 1)
    partner = jnp.where((lane & 32) == 0,
                        pltpu.roll(y, 2 * A_QK - 32, 1), pltpu.roll(y, 32, 1))
    o_ref[0] = (y * cfull + partner * sfull).astype(o_ref.dtype)


def _mlstm_prep(p3, conv_w, rope_c, rope_s):
    nb16 = TT // 16
    return pl.pallas_call(
        _prep_kernel,
        grid=(BATCH, NRB),
        in_specs=[pl.BlockSpec((1, RB, D), lambda b, i: (b, i, PB_QK)),
                  pl.BlockSpec((1, 16, D), lambda b, i: (b, jnp.maximum(i * (RB // 16) - 1, 0), PB_QK)),
                  pl.BlockSpec((1, 16, D), lambda b, i: (b, jnp.minimum((i + 1) * (RB // 16), nb16 - 1), PB_QK)),
                  pl.BlockSpec((3, D), lambda b, i: (0, 0)),
                  pl.BlockSpec((RB, 2 * A_DQK), lambda b, i: (i, 0)),
                  pl.BlockSpec((RB, 2 * A_DQK), lambda b, i: (i, 0))],
        out_specs=pl.BlockSpec((1, RB, D), lambda b, i: (b, i, 0)),
        out_shape=jax.ShapeDtypeStruct((BATCH, TT, D), BF16),
        compiler_params=_cp(("parallel", "parallel")),
        name="mlstm_prep",
    )(p3, p3, p3, conv_w, rope_c, rope_s)


def _mlstm_kernel(qkf_ref, vf_ref, gf_ref, lff_ref, qkb_ref, vb_ref, gb_ref, lfb_ref,
                  of_ref, ob_ref, ct_ref, n_ref, m_ref):
    @pl.when(pl.program_id(1) == 0)
    def _():
        ct_ref[...] = jnp.zeros_like(ct_ref)
        n_ref[...] = jnp.zeros_like(n_ref)
        m_ref[...] = jnp.zeros_like(m_ref)

    r = lax.broadcasted_iota(jnp.int32, (LCH, LCH), 0)
    c = lax.broadcasted_iota(jnp.int32, (LCH, LCH), 1)
    masks = [c <= r, c >= r]
    ins = [(qkf_ref, vf_ref, gf_ref, lff_ref, of_ref), (qkb_ref, vb_ref, gb_ref, lfb_ref, ob_ref)]
    b_all, g_all, b_t, g_t, b_end = [], [], [], [], []
    for d in range(2):
        tri = jnp.where(masks[d], 1.0, 0.0).astype(BF16)
        bc = jnp.dot(tri, ins[d][3][0], preferred_element_type=F32)
        ba = bc + pltpu.roll(bc, LANES - NGATE, 1) + pltpu.roll(bc, LANES - 2 * NGATE, 1)
        ga = ins[d][2][0]
        if d == 1:
            ba = pltpu.roll(ba, LANES - 2 * A_HEADS, 1)
            ga = pltpu.roll(ga, LANES - 2 * A_HEADS, 1)
        b_all.append(ba)
        g_all.append(ga)
        b_t.append(ba.T)
        g_t.append(ga.T)
        b_end.append(ba[LCH - 1:LCH, :] if d == 0 else ba[0:1, :])
    ones = jnp.ones((LCH, LANES), BF16)
    tn_dims = (((0,), (0,)), ((), ()))
    combos = [(d, h) for d in range(2) for h in range(A_HEADS)]
    idx = range(len(combos))
    qs = [ins[d][0][0, :, h * A_DQK:(h + 1) * A_DQK] for d, h in combos]
    ks = [ins[d][0][0, :, A_QK + h * A_DQK:A_QK + (h + 1) * A_DQK] for d, h in combos]
    qk = [lax.dot_general(qs[i], ks[i], NT_DIMS, preferred_element_type=F32) for i in idx]
    b_col = [b_all[d][:, A_HEADS + h:A_HEADS + h + 1] for d, h in combos]
    b_last = [b_end[d][:, A_HEADS + h:A_HEADS + h + 1] for d, h in combos]
    m_old = [m_ref[i][:, 0:1] for i in idx]
    r_row = [b_t[d][A_HEADS + h:A_HEADS + h + 1, :] - g_t[d][h:h + 1, :] for d, h in combos]
    m_row = [jnp.maximum(b_col[i] + m_old[i],
                         jnp.max(jnp.where(masks[d], b_col[i] - r_row[i], -jnp.inf), axis=1, keepdims=True))
             for i, (d, h) in enumerate(combos)]
    s = [(qk[i] * jnp.exp(jnp.where(masks[d], (b_col[i] - m_row[i]) - r_row[i], -jnp.inf))).astype(BF16)
         for i, (d, h) in enumerate(combos)]
    wq = [(jnp.exp(b_col[i] + m_old[i] - m_row[i]) * qs[i].astype(F32)).astype(BF16) for i in idx]
    g_col = [b_last[i] - b_col[i] + g_all[d][:, h:h + 1] for i, (d, h) in enumerate(combos)]
    m_new = [jnp.maximum(b_last[i] + m_old[i], jnp.max(g_col[i], axis=0, keepdims=True)) for i in idx]
    kw = [(jnp.exp(g_col[i] - m_new[i]) * ks[i].astype(F32)).astype(BF16) for i in idx]
    decay = [jnp.exp(b_last[i] + m_old[i] - m_new[i]) for i in idx]
    for i, (d, h) in enumerate(combos):
        v = ins[d][1][0, :, h * A_DV:(h + 1) * A_DV]
        lhs = jnp.concatenate([s[i], wq[i]], axis=1)
        den = jnp.dot(lhs, jnp.concatenate([ones, n_ref[i].astype(BF16)], axis=0),
                      preferred_element_type=F32)[:, 0:1]
        rinv = 1.0 / jnp.maximum(jnp.abs(den), jnp.exp(-m_row[i]))
        num = jnp.dot(lhs, jnp.concatenate([v, ct_ref[i].astype(BF16)], axis=0), preferred_element_type=F32)
        ins[d][4][0, :, h * A_DV:(h + 1) * A_DV] = (num * rinv).astype(of_ref.dtype)
    for i, (d, h) in enumerate(combos):
        v = ins[d][1][0, :, h * A_DV:(h + 1) * A_DV]
        ct_ref[i] = decay[i] * ct_ref[i] + lax.dot_general(kw[i], v, tn_dims, preferred_element_type=F32)
        n_ref[i] = decay[i] * n_ref[i] + lax.dot_general(kw[i], ones, tn_dims, preferred_element_type=F32)
        m_ref[i] = jnp.broadcast_to(m_new[i], (1, LANES))


def _rev_chunk(s):
    ncc = CTX // LCH
    return jnp.where(s < ncc, ncc - 1 - s, NCH + ncc - 1 - s)


def _mlstm_scan(qk, p3, gates, lfp):
    fwd = lambda b, s: (b, s, 0)
    bwd = lambda b, s: (b, _rev_chunk(s), 0)
    specs = lambda im, imv: [pl.BlockSpec((1, LCH, D), im), pl.BlockSpec((1, LCH, A_V), imv),
                             pl.BlockSpec((1, LCH, LANES), im), pl.BlockSpec((1, LCH, LANES), im)]
    nstate = 2 * A_HEADS
    return pl.pallas_call(
        _mlstm_kernel,
        grid=(BATCH, NCH),
        in_specs=(specs(fwd, lambda b, s: (b, s, PB_V))
                  + specs(bwd, lambda b, s: (b, _rev_chunk(s), PB_V))),
        out_specs=[pl.BlockSpec((1, LCH, A_V), fwd), pl.BlockSpec((1, LCH, A_V), bwd)],
        out_shape=[jax.ShapeDtypeStruct((BATCH, TT, A_V), BF16)] * 2,
        scratch_shapes=[pltpu.VMEM((nstate, A_DQK, A_DV), F32),
                        pltpu.VMEM((nstate, A_DQK, LANES), F32),
                        pltpu.VMEM((nstate, 1, LANES), F32)],
        compiler_params=_cp(("parallel", "arbitrary")),
        name="mlstm_scan",
    )(qk, p3, gates, lfp, qk, p3, gates, lfp)


NPAIR = B_HEADS // 2
NT_DIMS = (((1,), (1,)), ((), ()))


def _stack_pair(q, low):
    zero = jnp.zeros_like(q)
    return jnp.concatenate([jnp.where(low, q, zero), jnp.where(low, zero, q)], axis=0)


def _softmax_rows(s_ref, p_ref):
    s = s_ref[...]
    p = jnp.exp2(s - jnp.max(s, axis=2, keepdims=True))
    p_ref[...] = p.astype(p_ref.dtype)
    return 1.0 / jnp.sum(p, axis=2, keepdims=True)


def _na_kernel(q_ref, k_ref, v_ref, *rest):
    bias_refs, (o_ref, s_ref, p_ref) = rest[:NA_SUB], rest[NA_SUB:]
    low = lax.broadcasted_iota(jnp.int32, (1, 2 * B_DH), 1) < B_DH
    starts = []
    for sub in range(NA_SUB):
        row = pl.program_id(1) * NA_SUB + sub
        rs = jnp.clip(row - NA_ROWS // 2, 0, GRID_H - NA_ROWS)
        starts.append(pl.multiple_of(CTX + rs * GRID_W, GRID_W))
    for sub in range(NA_SUB):
        qrows = pl.ds(sub * NA_NQ, NA_NQ)
        for hp in range(NPAIR):
            lo, hi = hp * 2 * B_DH, (hp + 1) * 2 * B_DH
            q2 = _stack_pair(q_ref[0, qrows, lo:hi], low)
            idx = sub * NPAIR + hp
            s_ref[idx, :, 0:NA_NK] = lax.dot_general(q2, k_ref[0, pl.ds(starts[sub], NA_NK), lo:hi], NT_DIMS,
                                                     preferred_element_type=F32) + bias_refs[sub][0, 0, hp]
            s_ref[idx, :, NA_NK:] = lax.dot_general(q2, k_ref[0, 0:CTX, lo:hi], NT_DIMS,
                                                    preferred_element_type=F32)
    rinv = _softmax_rows(s_ref, p_ref)
    for sub in range(NA_SUB):
        qrows = pl.ds(sub * NA_NQ, NA_NQ)
        for hp in range(NPAIR):
            lo, hi = hp * 2 * B_DH, (hp + 1) * 2 * B_DH
            idx = sub * NPAIR + hp
            o2 = (jnp.dot(p_ref[idx, :, 0:NA_NK], v_ref[0, pl.ds(starts[sub], NA_NK), lo:hi],
                          preferred_element_type=F32)
                  + jnp.dot(p_ref[idx, :, NA_NK:], v_ref[0, 0:CTX, lo:hi], preferred_element_type=F32))
            o2 = o2 * rinv[idx]
            o_ref[0, qrows, lo:hi] = jnp.where(low, o2[0:NA_NQ], o2[NA_NQ:]).astype(o_ref.dtype)


def _na_pattern(row):
    edge = NA_ROWS // 2
    return jnp.where(row < edge, row, jnp.where(row <= GRID_H - edge, edge, row - (GRID_H - 2 * edge)))


NA_NPAT = NA_ROWS
NA_SUB = 2


def _na_attention(p3, bias, layer):
    nq = NA_SUB * NA_NQ
    qoff = CTX // nq
    bias_specs = [pl.BlockSpec((1, 1, NPAIR, 2 * NA_NQ, NA_NK),
                               lambda b, i, sub=sub: (layer, _na_pattern(i * NA_SUB + sub), 0, 0, 0))
                  for sub in range(NA_SUB)]
    return pl.pallas_call(
        _na_kernel,
        grid=(BATCH, GRID_H // NA_SUB),
        in_specs=[pl.BlockSpec((1, nq, D), lambda b, i: (b, i + qoff, PB_NQ)),
                  pl.BlockSpec((1, TT, D), lambda b, i: (b, 0, PB_NK)),
                  pl.BlockSpec((1, TT, D), lambda b, i: (b, 0, PB_NV))] + bias_specs,
        out_specs=pl.BlockSpec((1, nq, D), lambda b, i: (b, i, 0)),
        out_shape=jax.ShapeDtypeStruct((BATCH, SEQ, D), BF16),
        scratch_shapes=[pltpu.VMEM((NA_SUB * NPAIR, 2 * NA_NQ, NA_NK + CTX), F32),
                        pltpu.VMEM((NA_SUB * NPAIR, 2 * NA_NQ, NA_NK + CTX), BF16)],
        compiler_params=_cp(("parallel", "arbitrary")),
        name="na_attention",
    )(p3, p3, p3, *([bias] * NA_SUB))


def _ctx_attn_kernel(q_ref, k_ref, v_ref, o_ref, s_ref, p_ref):
    low = lax.broadcasted_iota(jnp.int32, (1, 2 * B_DH), 1) < B_DH
    for hp in range(NPAIR):
        lo, hi = hp * 2 * B_DH, (hp + 1) * 2 * B_DH
        s_ref[hp] = lax.dot_general(_stack_pair(q_ref[0, :, lo:hi], low), k_ref[0, :, lo:hi], NT_DIMS,
                                    preferred_element_type=F32)
    rinv = _softmax_rows(s_ref, p_ref)
    for hp in range(NPAIR):
        lo, hi = hp * 2 * B_DH, (hp + 1) * 2 * B_DH
        o2 = jnp.dot(p_ref[hp], v_ref[0, :, lo:hi], preferred_element_type=F32) * rinv[hp]
        o_ref[0, :, lo:hi] = jnp.where(low, o2[0:CTX], o2[CTX:]).astype(o_ref.dtype)


def _ctx_attention(p3):
    return pl.pallas_call(
        _ctx_attn_kernel,
        grid=(BATCH,),
        in_specs=[pl.BlockSpec((1, CTX, D), lambda b: (b, 0, PB_NQ)),
                  pl.BlockSpec((1, CTX, D), lambda b: (b, 0, PB_NK)),
                  pl.BlockSpec((1, CTX, D), lambda b: (b, 0, PB_NV))],
        out_specs=pl.BlockSpec((1, CTX, D), lambda b: (b, 0, 0)),
        out_shape=jax.ShapeDtypeStruct((BATCH, CTX, D), BF16),
        scratch_shapes=[pltpu.VMEM((NPAIR, 2 * CTX, CTX), F32),
                        pltpu.VMEM((NPAIR, 2 * CTX, CTX), BF16)],
        compiler_params=_cp(("parallel",)),
        name="ctx_attention",
    )(p3, p3, p3)


def _merge_kernel(hf_ref, hbw_ref, o_ref, ga_ref, gb_ref, hnl_ref, hnc_ref, x_ref, mod_ref, ghn_ref,
                  wa_ref, wb_ref, wo_ref, g2_ref, x1_ref, h2_ref):
    hn = jnp.where(pl.program_id(1) == 0, hnc_ref[0], hnl_ref[0])
    bm = jnp.dot(hn, wb_ref[...], preferred_element_type=F32)
    hs = hf_ref[0].astype(F32) + hbw_ref[0].astype(F32)
    parts = []
    for h in range(A_HEADS):
        seg = hs[:, h * A_DV:(h + 1) * A_DV]
        mu = jnp.mean(seg, axis=-1, keepdims=True)
        cen = seg - mu
        var = jnp.mean(cen * cen, axis=-1, keepdims=True)
        parts.append(cen * lax.rsqrt(var + EPS))
    ya = jnp.concatenate(parts, axis=1) * ghn_ref[...] * _sigmoid(o_ref[0].astype(F32))
    a = jnp.dot(ya.astype(BF16), wa_ref[...], preferred_element_type=F32)
    mrg = _sigmoid(ga_ref[0].astype(F32)) * a + _sigmoid(gb_ref[0].astype(F32)) * bm
    y = jnp.dot(mrg.astype(BF16), wo_ref[...], preferred_element_type=F32)
    mod = mod_ref[0]
    x1 = x_ref[0] + mod[2:3] * y
    x1_ref[0] = x1
    h2_ref[0] = _norm_mod(x1, g2_ref[...], mod[4:5], mod[3:4]).astype(h2_ref.dtype)


def _merge(hf, hbw, p3, hb_lat, hb_ctx, x, mod, ghn, wa, wb, wo, g2, h2_dtype):
    row = lambda b, i: (b, i, 0)
    full = lambda b, i: (0, 0)
    return pl.pallas_call(
        _merge_kernel,
        grid=(BATCH, NRB),
        in_specs=[pl.BlockSpec((1, RB, A_V), row),
                  pl.BlockSpec((1, RB, A_V), row),
                  pl.BlockSpec((1, RB, D), lambda b, i: (b, i, PB_O)),
                  pl.BlockSpec((1, RB, D), lambda b, i: (b, i, PB_GA)),
                  pl.BlockSpec((1, RB, D), lambda b, i: (b, i, PB_GB)),
                  pl.BlockSpec((1, RB, D), lambda b, i: (b, jnp.maximum(i - 1, 0), 0)),
                  pl.BlockSpec((1, CTX, D), lambda b, i: (b, 0, 0)),
                  pl.BlockSpec((1, RB, D), row),
                  pl.BlockSpec((1, 6, D), lambda b, i: (_mod_row(b, i), 0, 0)),
                  pl.BlockSpec((1, A_V), full),
                  pl.BlockSpec((A_V, D), full),
                  pl.BlockSpec((D, D), full),
                  pl.BlockSpec((D, D), full),
                  pl.BlockSpec((1, D), full)],
        out_specs=[pl.BlockSpec((1, RB, D), row), pl.BlockSpec((1, RB, D), row)],
        out_shape=[jax.ShapeDtypeStruct((BATCH, TT, D), F32),
                   jax.ShapeDtypeStruct((BATCH, TT, D), h2_dtype)],
        compiler_params=_cp(("parallel", "parallel")),
        name="merge",
    )(hf, hbw, p3, p3, p3, hb_lat, hb_ctx, x, mod, ghn, wa, wb, wo, g2)


FFN_CH = D_FF // 2


def _ffn_kernel(h_ref, x_ref, mod_ref, w1_ref, w3_ref, w2_ref, gn_ref, modn_ref, x2_ref, hn_ref):
    h = h_ref[0]
    y = None
    for cidx in range(D_FF // FFN_CH):
        lo, hi = cidx * FFN_CH, (cidx + 1) * FFN_CH
        a = jnp.dot(h, w1_ref[:, lo:hi], preferred_element_type=F32)
        g = jnp.dot(h, w3_ref[:, lo:hi], preferred_element_type=F32)
        part = jnp.dot((_silu(a) * g).astype(BF16), w2_ref[lo:hi, :], preferred_element_type=F32)
        y = part if y is None else y + part
    mod = mod_ref[0]
    x2 = x_ref[0] + mod[5:6] * y
    x2_ref[0] = x2
    modn = modn_ref[0]
    hn_ref[0] = _norm_mod(x2, gn_ref[...], modn[1:2], modn[0:1]).astype(hn_ref.dtype)


def _dense_ffn(h2, x1, mod, w1, w3, w2, gn, modn):
    row = lambda b, i: (b, i, 0)
    full = lambda b, i: (0, 0)
    modspec = pl.BlockSpec((1, 6, D), lambda b, i: (_mod_row(b, i), 0, 0))
    return pl.pallas_call(
        _ffn_kernel,
        grid=(BATCH, NRB),
        in_specs=[pl.BlockSpec((1, RB, D), row), pl.BlockSpec((1, RB, D), row), modspec,
                  pl.BlockSpec((D, D_FF), full), pl.BlockSpec((D, D_FF), full),
                  pl.BlockSpec((D_FF, D), full), pl.BlockSpec((1, D), full), modspec],
        out_specs=[pl.BlockSpec((1, RB, D), row), pl.BlockSpec((1, RB, D), row)],
        out_shape=[jax.ShapeDtypeStruct((BATCH, TT, D), F32),
                   jax.ShapeDtypeStruct((BATCH, TT, D), BF16)],
        compiler_params=_cp(("parallel", "parallel")),
        name="dense_ffn",
    )(h2, x1, mod, w1, w3, w2, gn, modn)


def _router_kernel(h_ref, wr_ref, route_ref, cnt_ref, run_ref):
    i = pl.program_id(0)

    @pl.when(i == 0)
    def _():
        run_ref[...] = jnp.zeros_like(run_ref)

    logits = jnp.dot(h_ref[...].astype(BF16), wr_ref[...], preferred_element_type=F32)
    lane = lax.broadcasted_iota(jnp.int32, (RB, LANES), 1).astype(F32)
    lg = jnp.where(lane < N_EXPERTS, logits, -jnp.inf)
    v1 = jnp.max(lg, axis=1, keepdims=True)
    i1 = jnp.min(jnp.where(lg == v1, lane, float(LANES)), axis=1, keepdims=True)
    lg2 = jnp.where(lane == i1, -jnp.inf, lg)
    v2 = jnp.max(lg2, axis=1, keepdims=True)
    i2 = jnp.min(jnp.where(lg2 == v2, lane, float(LANES)), axis=1, keepdims=True)
    e = jnp.exp(v2 - v1)
    w1 = 1.0 / (1.0 + e)
    w2 = e / (1.0 + e)
    oh1 = (lane == i1).astype(F32)
    oh2 = (lane == i2).astype(F32)
    r = lax.broadcasted_iota(jnp.int32, (RB, RB), 0)
    c = lax.broadcasted_iota(jnp.int32, (RB, RB), 1)
    tri = (r > c).astype(BF16)
    cs1 = jnp.dot(tri, oh1.astype(BF16), preferred_element_type=F32)
    cs2 = jnp.dot(tri, oh2.astype(BF16), preferred_element_type=F32)
    tot1 = jnp.sum(oh1, axis=0, keepdims=True)
    tot2 = jnp.sum(oh2, axis=0, keepdims=True)
    run = run_ref[...]
    rank1 = jnp.sum(oh1 * (run + cs1), axis=1, keepdims=True)
    rank2 = jnp.sum(oh2 * (run + tot1 + cs2), axis=1, keepdims=True)
    new_run = run + tot1 + tot2
    run_ref[...] = new_run
    cnt_ref[...] = new_run
    out = jnp.where(lane == 0, i1,
          jnp.where(lane == 1, i2,
          jnp.where(lane == 2, w1,
          jnp.where(lane == 3, w2,
          jnp.where(lane == 4, rank1,
          jnp.where(lane == 5, rank2, 0.0))))))
    route_ref[...] = out


def _router(h2_flat, wr, nblk, blk_of):
    return pl.pallas_call(
        _router_kernel,
        grid=(nblk,),
        in_specs=[pl.BlockSpec((RB, D), lambda i: (blk_of(i), 0)),
                  pl.BlockSpec((D, LANES), lambda i: (0, 0))],
        out_specs=[pl.BlockSpec((RB, LANES), lambda i: (i, 0)),
                   pl.BlockSpec((1, LANES), lambda i: (0, 0))],
        out_shape=[jax.ShapeDtypeStruct((nblk * RB, LANES), F32),
                   jax.ShapeDtypeStruct((1, LANES), F32)],
        scratch_shapes=[pltpu.VMEM((1, LANES), F32)],
        compiler_params=_cp(("arbitrary",)),
        name="moe_router",
    )(h2_flat, wr)


DMA_UNROLL = 8
N_ZERO = 2 * N_EXPERTS


def _dispatch_kernel(pos_ref, zstart_ref, h_ref, o_ref, zbuf, stage, sems, zsem):
    @pl.when(pl.program_id(0) == 0)
    def _():
        zbuf[...] = jnp.zeros_like(zbuf)
        for z in range(N_ZERO):
            @pl.when(zstart_ref[z] >= 0)
            def _():
                zs = pl.multiple_of(zstart_ref[z], MOE_TM)
                pltpu.make_async_copy(zbuf, o_ref.at[pl.ds(zs, MOE_TM)], zsem).start()
        for z in range(N_ZERO):
            @pl.when(zstart_ref[z] >= 0)
            def _():
                pltpu.make_async_copy(zbuf, o_ref.at[pl.ds(0, MOE_TM)], zsem).wait()

    step = pl.program_id(0)
    slot = step % 2
    stage[slot] = h_ref[...]

    def issue(t, carry):
        src = stage.at[slot, pl.ds(t, 1)]
        pltpu.make_async_copy(src, o_ref.at[pl.ds(pos_ref[0, 0, t], 1)], sems.at[slot]).start(priority=0)
        pltpu.make_async_copy(src, o_ref.at[pl.ds(pos_ref[0, 0, RB + t], 1)], sems.at[slot]).start(priority=1)
        return carry

    lax.fori_loop(0, RB, issue, 0, unroll=DMA_UNROLL)

    def drain(sl):
        for _ in range(2):
            pltpu.make_async_copy(stage.at[sl], o_ref.at[pl.ds(0, RB)], sems.at[sl]).wait()

    @pl.when(step > 0)
    def _():
        drain(1 - slot)

    @pl.when(step == pl.num_programs(0) - 1)
    def _():
        drain(slot)


def _dispatch(pos, zstart, h2_flat, nblk, blk_of, n_sorted):
    return pl.pallas_call(
        _dispatch_kernel,
        grid=(nblk,),
        in_specs=[pl.BlockSpec((1, 1, 2 * RB), lambda i: (i, 0, 0), memory_space=pltpu.SMEM),
                  pl.BlockSpec(memory_space=pltpu.SMEM),
                  pl.BlockSpec((RB, D), lambda i: (blk_of(i), 0))],
        out_specs=pl.BlockSpec(memory_space=pl.ANY),
        out_shape=jax.ShapeDtypeStruct((n_sorted, D), F32),
        scratch_shapes=[pltpu.VMEM((MOE_TM, D), F32), pltpu.VMEM((2, RB, D), F32),
                        pltpu.SemaphoreType.DMA((2,)), pltpu.SemaphoreType.DMA(())],
        compiler_params=_cp(("arbitrary",)),
        name="moe_dispatch",
    )(pos, zstart, h2_flat)


def _gmm_kernel(be_ref, nb_ref, a_ref, w1_ref, w3_ref, w2_ref, o_ref, abf_ref):
    del be_ref
    i = pl.program_id(0)
    f = pl.program_id(1)

    @pl.when((i < nb_ref[0]) & (f == 0))
    def _():
        abf_ref[...] = a_ref[...].astype(BF16)

    @pl.when(i < nb_ref[0])
    def _():
        a = abf_ref[...]
        gs = []
        for c in range(MOE_FC // MOE_SUB):
            lo, hi = c * MOE_SUB, (c + 1) * MOE_SUB
            h1 = jnp.dot(a, w1_ref[0, :, lo:hi].astype(BF16), preferred_element_type=F32)
            h3 = jnp.dot(a, w3_ref[0, :, lo:hi].astype(BF16), preferred_element_type=F32)
            gs.append((_silu(h1) * h3).astype(BF16))
        part = jnp.dot(jnp.concatenate(gs, axis=1), w2_ref[0].astype(BF16), preferred_element_type=F32)

        @pl.when(f == 0)
        def _():
            o_ref[...] = part

        @pl.when(f > 0)
        def _():
            o_ref[...] += part

    @pl.when((i >= nb_ref[0]) & (f == 0))
    def _():
        o_ref[...] = jnp.zeros_like(o_ref)


def _expert_ffn(blk_e, nb, hs, w1, w3, w2, j):
    nbmax = hs.shape[0] // MOE_TM
    nf = D_FF_EXPERT // MOE_FC

    def ieff(i, nb_ref):
        return jnp.minimum(i, nb_ref[0] - 1)

    def feff(i, f, nb_ref):
        return jnp.where(i < nb_ref[0], f, nf - 1)

    grid_spec = pltpu.PrefetchScalarGridSpec(
        num_scalar_prefetch=2,
        grid=(nbmax, nf),
        in_specs=[pl.BlockSpec((MOE_TM, D), lambda i, f, be, nbr: (ieff(i, nbr), 0)),
                  pl.BlockSpec((None, 1, D, MOE_FC),
                               lambda i, f, be, nbr: (j, be[ieff(i, nbr)], 0, feff(i, f, nbr))),
                  pl.BlockSpec((None, 1, D, MOE_FC),
                               lambda i, f, be, nbr: (j, be[ieff(i, nbr)], 0, feff(i, f, nbr))),
                  pl.BlockSpec((None, 1, MOE_FC, D),
                               lambda i, f, be, nbr: (j, be[ieff(i, nbr)], feff(i, f, nbr), 0))],
        out_specs=pl.BlockSpec((MOE_TM, D), lambda i, f, be, nbr: (i, 0)),
        scratch_shapes=[pltpu.VMEM((MOE_TM, D), BF16)],
    )
    return pl.pallas_call(
        _gmm_kernel,
        grid_spec=grid_spec,
        out_shape=jax.ShapeDtypeStruct(hs.shape, F32),
        compiler_params=_cp(("arbitrary", "arbitrary")),
        name="moe_expert_ffn",
    )(blk_e, nb, hs, w1, w3, w2)


def _combine_kernel(pos_ref, posn_ref, y_ref, route_ref, x_ref, mod_ref, gn_ref, modn_ref,
                    x2_ref, hn_ref, ybuf, sems):
    t = pl.program_id(0) * pl.num_programs(1) + pl.program_id(1)
    nsteps = pl.num_programs(0) * pl.num_programs(1)
    slot = t % 2

    def start_block(p_ref, sl):
        def issue(r, carry):
            pltpu.make_async_copy(y_ref.at[pl.ds(p_ref[0, 0, r], 1)],
                                  ybuf.at[sl, 0, pl.ds(r, 1)], sems.at[sl]).start(priority=0)
            pltpu.make_async_copy(y_ref.at[pl.ds(p_ref[0, 0, RB + r], 1)],
                                  ybuf.at[sl, 1, pl.ds(r, 1)], sems.at[sl]).start(priority=1)
            return carry
        lax.fori_loop(0, RB, issue, 0, unroll=DMA_UNROLL)

    @pl.when(t == 0)
    def _():
        start_block(pos_ref, 0)

    @pl.when(t + 1 < nsteps)
    def _():
        start_block(posn_ref, 1 - slot)

    for e in range(2):
        pltpu.make_async_copy(y_ref.at[pl.ds(0, RB)], ybuf.at[slot, e], sems.at[slot]).wait()
    route = route_ref[...]
    y = route[:, 2:3] * ybuf[slot, 0] + route[:, 3:4] * ybuf[slot, 1]
    mod = mod_ref[0]
    x2 = x_ref[0] + mod[5:6] * y
    x2_ref[0] = x2
    modn = modn_ref[0]
    hn_ref[0] = _norm_mod(x2, gn_ref[...], modn[1:2], modn[0:1]).astype(hn_ref.dtype)


def _combine(pos, ys, route, x1, mod, gn, modn, latent_only, hn_dtype):
    nrb = NRB - 1 if latent_only else NRB
    off = 1 if latent_only else 0
    rows = nrb * RB
    full = lambda b, i: (0, 0)
    modspec = pl.BlockSpec((1, 6, D), lambda b, i: (_mod_row(b, i + off), 0, 0))
    last = BATCH * nrb - 1
    return pl.pallas_call(
        _combine_kernel,
        grid=(BATCH, nrb),
        in_specs=[pl.BlockSpec((1, 1, 2 * RB), lambda b, i: (b * nrb + i, 0, 0), memory_space=pltpu.SMEM),
                  pl.BlockSpec((1, 1, 2 * RB), lambda b, i: (jnp.minimum(b * nrb + i + 1, last), 0, 0),
                               memory_space=pltpu.SMEM),
                  pl.BlockSpec(memory_space=pl.ANY),
                  pl.BlockSpec((RB, LANES), lambda b, i: (b * nrb + i, 0)),
                  pl.BlockSpec((1, RB, D), lambda b, i: (b, i + off, 0)),
                  modspec,
                  pl.BlockSpec((1, D), full),
                  modspec],
        out_specs=[pl.BlockSpec((1, RB, D), lambda b, i: (b, i, 0)),
                   pl.BlockSpec((1, RB, D), lambda b, i: (b, i, 0))],
        out_shape=[jax.ShapeDtypeStruct((BATCH, rows, D), F32),
                   jax.ShapeDtypeStruct((BATCH, rows, D), hn_dtype)],
        scratch_shapes=[pltpu.VMEM((2, 2, RB, D), F32), pltpu.SemaphoreType.DMA((2,))],
        compiler_params=_cp(("arbitrary", "arbitrary")),
        name="moe_combine",
    )(pos, pos, ys, route, x1, mod, gn, modn)


def _moe_ffn(h2, x1, mod, wr, w1, w3, w2, j, gn, modn, latent_only, hn_dtype):
    h2_flat = h2.reshape(NTOK, D)
    if latent_only:
        nrb = NRB - 1
        blk_of = lambda i: (i // nrb) * NRB + (i % nrb) + 1
    else:
        nrb = NRB
        blk_of = lambda i: i
    nblk = BATCH * nrb
    n_pairs = 2 * nblk * RB
    nbmax = n_pairs // MOE_TM + N_EXPERTS
    n_sorted = nbmax * MOE_TM

    route, counts = _router(h2_flat, wr, nblk, blk_of)
    cnt = counts[0, :N_EXPERTS].astype(jnp.int32)
    gsz = ((cnt + MOE_TM - 1) // MOE_TM) * MOE_TM
    gend = jnp.cumsum(gsz)
    goff = gend - gsz
    e12 = route[:, 0:2].astype(jnp.int32)
    rank = route[:, 4:6].astype(jnp.int32)
    pos = goff[e12] + rank
    pos = pos.reshape(nblk, RB, 2).transpose(0, 2, 1).reshape(nblk, 1, 2 * RB)
    nb = (gend[-1] // MOE_TM).reshape(1)
    blk_start = jnp.arange(nbmax, dtype=jnp.int32) * MOE_TM
    blk_e = jnp.minimum(jnp.sum(blk_start[:, None] >= gend[None, :], axis=1), N_EXPERTS - 1).astype(jnp.int32)

    trail = (nbmax - N_EXPERTS + jnp.arange(N_EXPERTS, dtype=jnp.int32)) * MOE_TM
    zstart = jnp.concatenate([jnp.where(cnt > 0, gend - MOE_TM, -1),
                              jnp.where(trail >= gend[-1], trail, -1)]).astype(jnp.int32)

    hs = _dispatch(pos, zstart, h2_flat, nblk, blk_of, n_sorted)
    ys = _expert_ffn(blk_e, nb, hs, w1, w3, w2, j)
    return _combine(pos, ys, route, x1, mod, gn, modn, latent_only, hn_dtype)


def _rope_tables():
    quarter = A_DQK // 4
    inv = 1.0 / (ROPE_BASE ** (jnp.arange(quarter, dtype=F32) / quarter))
    pos = jnp.arange(SEQ)
    rows = (pos // GRID_W).astype(F32)
    cols = (pos % GRID_W).astype(F32)
    ang_r = rows[:, None] * inv[None, :]
    ang_c = cols[:, None] * inv[None, :]
    cos = jnp.concatenate([jnp.cos(ang_r)] * 2 + [jnp.cos(ang_c)] * 2, axis=1)
    sin = jnp.concatenate([-jnp.sin(ang_r), jnp.sin(ang_r), -jnp.sin(ang_c), jnp.sin(ang_c)], axis=1)
    cos = jnp.concatenate([jnp.ones((CTX, A_DQK), F32), cos], axis=0)
    sin = jnp.concatenate([jnp.zeros((CTX, A_DQK), F32), sin], axis=0)
    kscale = A_DQK ** -0.5
    return (jnp.concatenate([cos, cos * kscale], axis=1),
            jnp.concatenate([sin, sin * kscale], axis=1))


def _na_bias_tables(rpb):
    rs = np.clip(np.arange(GRID_H) - NA_ROWS // 2, 0, GRID_H - NA_ROWS)
    cs = np.clip(np.arange(GRID_W) - NA_COLS // 2, 0, GRID_W - NA_COLS)
    rows = [0, 1, 2, 3, NA_ROWS // 2, GRID_H - 3, GRID_H - 2, GRID_H - 1]
    col = np.arange(GRID_W)
    dc = np.clip(col[None, :] - col[:, None] + NA_COLS - 1, 0, 2 * NA_COLS - 2)
    valid_c = (col[None, :] >= cs[:, None]) & (col[None, :] < cs[:, None] + NA_COLS)
    sel_c = np.eye(2 * NA_COLS - 1, dtype=np.float32)[dc]
    t = jnp.einsum('lhab,uvb->lhuav', rpb * LOG2E, jnp.asarray(sel_c),
                   precision=lax.Precision.HIGHEST)
    neg = np.where(valid_c, 0.0, -np.inf).astype(np.float32)[:, None, :]
    nl = rpb.shape[0]
    t = (t + jnp.asarray(neg)).reshape(nl, NPAIR, 2 * NA_NQ, (2 * NA_ROWS - 1) * GRID_W)
    first = [rs[r] - r + NA_ROWS - 1 for r in rows]
    return jnp.stack([t[..., d0 * GRID_W:d0 * GRID_W + NA_NK] for d0 in first], axis=1)


def kernel(x, c, ctx, c_ctx, w_mod, b_mod, g_norm1, g_norm2, w_in, a_conv, a_gate_b, a_hnorm_g, na_rpb,
           w_br_a, w_br_b, w_out, ffn_w1, ffn_w3, ffn_w2, moe_router, moe_w1, moe_w3, moe_w2, g_final):
    cc = jnp.concatenate([c, c_ctx[None, :], jnp.zeros((16 - BATCH - 1, D), F32)], axis=0)
    mod_all = _modulation(cc, w_mod, b_mod).reshape(DEPTH, 16, 6, D)[:, :BATCH + 1]
    mod_zero = jnp.zeros((BATCH + 1, 6, D), F32)
    rope_c, rope_s = _rope_tables()
    na_bias = _na_bias_tables(na_rpb)

    xs = jnp.concatenate([ctx, x], axis=1)
    h1 = _first_norm(xs, g_norm1[0][None, :], mod_all[0])
    out = None
    for l in range(DEPTH):
        last = l == DEPTH - 1
        mod = mod_all[l]
        wl = w_in[l]
        g0 = 3 * D
        g1 = g0 + NGATE
        wp = jnp.concatenate([wl[:, :g0], wl[:, g1:g1 + D] * (B_DH ** -0.5 * LOG2E), wl[:, g1 + D:]],
                             axis=1).astype(BF16)
        wg = jnp.pad(wl[:, g0:g1], ((0, 0), (0, LANES - NGATE))).astype(BF16)
        bg = jnp.pad(a_gate_b[l], (0, LANES - NGATE))[None, :]

        h1_flat = h1.reshape(NTOK, D)
        p3 = _in_proj(h1_flat, wp).reshape(BATCH, TT, P_COLS)
        gates, lfp = _gate_proj(h1_flat, wg, bg)

        qk = _mlstm_prep(p3, a_conv[l], rope_c, rope_s)
        hf, hbw = _mlstm_scan(qk, p3, gates.reshape(BATCH, TT, LANES), lfp.reshape(BATCH, TT, LANES))
        hb_lat = _na_attention(p3, na_bias, l)
        hb_ctx = _ctx_attention(p3)

        moe = l % 2 == 1
        x1, h2 = _merge(hf, hbw, p3, hb_lat, hb_ctx, xs, mod, a_hnorm_g[l][None, :],
                        w_br_a[l].astype(BF16), w_br_b[l].astype(BF16), w_out[l].astype(BF16),
                        g_norm2[l][None, :], F32 if moe else BF16)
        if last:
            gn, modn = g_final[None, :], mod_zero
        else:
            gn, modn = g_norm1[l + 1][None, :], mod_all[l + 1]
        j = l // 2
        if not moe:
            xs, h1 = _dense_ffn(h2, x1, mod, ffn_w1[j].astype(BF16), ffn_w3[j].astype(BF16),
                                ffn_w2[j].astype(BF16), gn, modn)
        else:
            wr = jnp.pad(moe_router[j], ((0, 0), (0, LANES - N_EXPERTS))).astype(BF16)
            xs, h1 = _moe_ffn(h2, x1, mod, wr, moe_w1, moe_w3, moe_w2, j, gn, modn, last,
                              F32 if last else BF16)
            if last:
                out = h1
    return out
```

```python
import functools

import numpy as np
import jax
import jax.numpy as jnp
from jax import lax
from jax.experimental import pallas as pl
from jax.experimental.pallas import tpu as pltpu

F32 = jnp.float32
BF16 = jnp.bfloat16

D = 1024
BATCH = 8
SEQ = 2048
CTX = 256
TT = CTX + SEQ
NTOK = BATCH * TT
DEPTH = 4
GRID_W = 64
GRID_H = SEQ // GRID_W

A_HEADS = 4
A_DQK = 128
A_DV = 256
A_QK = A_HEADS * A_DQK
A_V = A_HEADS * A_DV
ROPE_BASE = 10000.0
LCH = 256
NCH = TT // LCH
SCAN_MB = 4

B_HEADS = 16
B_DH = 64
NA_ROWS = 8
NA_COLS = 16
NA_QROWS = 1
NA_KROWS = NA_QROWS + NA_ROWS - 1
NA_NQ = NA_QROWS * GRID_W
NA_NK = NA_KROWS * GRID_W
NA_STEPS = GRID_H // NA_QROWS

D_FF = 2816
N_EXPERTS = 8
D_FF_EXPERT = 3584
EPS = 1e-6
LOG2E = 1.4426950408889634

RB = 256
NRB = TT // RB
MM_TM = 2048
MOE_TM = 1024
MOE_FC = 512
MOE_SUB = 256
LANES = 128

PB_QK, PB_V, PB_O, PB_NQ, PB_NK, PB_NV, PB_GA, PB_GB = range(8)
P_COLS = 8 * D

VMEM_LIMIT = 56 * 1024 * 1024


def _cp(sem, vmem=VMEM_LIMIT):
    return pltpu.CompilerParams(dimension_semantics=sem, vmem_limit_bytes=vmem)


def _sigmoid(x):
    return 1.0 / (1.0 + jnp.exp(-x))


def _silu(x):
    return x * _sigmoid(x)


def _log_sigmoid(x):
    return jnp.minimum(x, 0.0) - jnp.log(1.0 + jnp.exp(-jnp.abs(x)))


def _norm_mod(x, g, sc, sh):
    ms = jnp.mean(x * x, axis=-1, keepdims=True)
    y = x * lax.rsqrt(ms + EPS)
    return (y * g) * (1.0 + sc) + sh


def _mod_row(b, i):
    return jnp.where(i == 0, BATCH, b)


def _mod_kernel(c_ref, w_ref, b_ref, o_ref):
    c = c_ref[...]
    s = _silu(c).astype(BF16)
    o_ref[0] = jnp.dot(s, w_ref[0].astype(BF16), preferred_element_type=F32) + b_ref[0]


def _modulation(cc, w_mod, b_mod):
    tn = 2048
    nl = w_mod.shape[0]
    return pl.pallas_call(
        _mod_kernel,
        grid=(nl, 6 * D // tn),
        in_specs=[pl.BlockSpec((16, D), lambda l, j: (0, 0)),
                  pl.BlockSpec((1, D, tn), lambda l, j: (l, 0, j)),
                  pl.BlockSpec((1, 1, tn), lambda l, j: (l, 0, j))],
        out_specs=pl.BlockSpec((1, 16, tn), lambda l, j: (l, 0, j)),
        out_shape=jax.ShapeDtypeStruct((nl, 16, 6 * D), F32),
        compiler_params=_cp(("parallel", "parallel")),
        name="modulation",
    )(cc, w_mod, b_mod.reshape(nl, 1, 6 * D))


def _norm_kernel(x_ref, g_ref, mod_ref, o_ref):
    mod = mod_ref[0]
    o_ref[0] = _norm_mod(x_ref[0], g_ref[...], mod[1:2], mod[0:1]).astype(o_ref.dtype)


def _first_norm(x, g, mod):
    return pl.pallas_call(
        _norm_kernel,
        grid=(BATCH, NRB),
        in_specs=[pl.BlockSpec((1, RB, D), lambda b, i: (b, i, 0)),
                  pl.BlockSpec((1, D), lambda b, i: (0, 0)),
                  pl.BlockSpec((1, 6, D), lambda b, i: (_mod_row(b, i), 0, 0))],
        out_specs=pl.BlockSpec((1, RB, D), lambda b, i: (b, i, 0)),
        out_shape=jax.ShapeDtypeStruct((BATCH, TT, D), BF16),
        compiler_params=_cp(("parallel", "parallel")),
        name="first_norm",
    )(x, g, mod)


def _mm_kernel(a_ref, w_ref, o_ref):
    o_ref[...] = jnp.dot(a_ref[...], w_ref[...], preferred_element_type=F32).astype(o_ref.dtype)


def _in_proj(h, w):
    tn = 1024
    return pl.pallas_call(
        _mm_kernel,
        grid=(P_COLS // tn, NTOK // MM_TM),
        in_specs=[pl.BlockSpec((MM_TM, D), lambda j, i: (i, 0)),
                  pl.BlockSpec((D, tn), lambda j, i: (0, j))],
        out_specs=pl.BlockSpec((MM_TM, tn), lambda j, i: (i, j)),
        out_shape=jax.ShapeDtypeStruct((NTOK, P_COLS), BF16),
        compiler_params=_cp(("parallel", "parallel")),
        name="in_proj",
    )(h, w)


NGATE = 4 * A_HEADS


def _gate_kernel(a_ref, w_ref, b_ref, g_ref, lf_ref):
    g = jnp.dot(a_ref[...], w_ref[...], preferred_element_type=F32) + b_ref[...]
    g_ref[...] = g
    lf = _log_sigmoid(g)
    p0 = lf.astype(BF16).astype(F32)
    r1 = lf - p0
    p1 = r1.astype(BF16).astype(F32)
    p2 = (r1 - p1).astype(BF16).astype(F32)
    lane = lax.broadcasted_iota(jnp.int32, (1, LANES), 1)
    parts = jnp.where(lane < NGATE, p0,
                      jnp.where(lane < 2 * NGATE, pltpu.roll(p1, NGATE, 1),
                                jnp.where(lane < 3 * NGATE, pltpu.roll(p2, 2 * NGATE, 1), 0.0)))
    lf_ref[...] = parts.astype(BF16)


def _gate_proj(h, wg, bg):
    return pl.pallas_call(
        _gate_kernel,
        grid=(NTOK // MM_TM,),
        in_specs=[pl.BlockSpec((MM_TM, D), lambda i: (i, 0)),
                  pl.BlockSpec((D, LANES), lambda i: (0, 0)),
                  pl.BlockSpec((1, LANES), lambda i: (0, 0))],
        out_specs=[pl.BlockSpec((MM_TM, LANES), lambda i: (i, 0)),
                   pl.BlockSpec((MM_TM, LANES), lambda i: (i, 0))],
        out_shape=[jax.ShapeDtypeStruct((NTOK, LANES), F32),
                   jax.ShapeDtypeStruct((NTOK, LANES), BF16)],
        compiler_params=_cp(("parallel",)),
        name="gate_proj",
    )(h, wg, bg)


def _prep_kernel(u_ref, up_ref, un_ref, w_ref, c_ref, s_ref, o_ref):
    i = pl.program_id(1)
    u = u_ref[0].astype(F32)
    prev_row = jnp.where(i >= 2, up_ref[0, 15:16, :].astype(F32), 0.0)
    next_row = jnp.where((i >= 1) & (i <= NRB - 2), un_ref[0, 0:1, :].astype(F32), 0.0)
    rid = lax.broadcasted_iota(jnp.int32, (RB, 1), 0)
    u_m1 = jnp.where(rid == 0, prev_row, pltpu.roll(u, 1, 0))
    u_p1 = jnp.where(rid == RB - 1, next_row, pltpu.roll(u, RB - 1, 0))
    w = w_ref[...]
    y = w[0:1] * u_m1 + w[1:2] * u + w[2:3] * u_p1
    y = _silu(y)
    c = c_ref[...]
    s = s_ref[...]
    cfull = jnp.concatenate([c[:, :A_DQK]] * A_HEADS + [c[:, A_DQK:]] * A_HEADS, axis=1)
    sfull = jnp.concatenate([s[:, :A_DQK]] * A_HEADS + [s[:, A_DQK:]] * A_HEADS, axis=1)
    lane = lax.broadcasted_iota(jnp.int32, (1, 2 * A_QK), 1)
    partner = jnp.where((lane & 32) == 0,
                        pltpu.roll(y, 2 * A_QK - 32, 1), pltpu.roll(y, 32, 1))
    o_ref[0] = (y * cfull + partner * sfull).astype(o_ref.dtype)


def _mlstm_prep(p3, conv_w, rope_c, rope_s):
    nb16 = TT // 16
    return pl.pallas_call(
        _prep_kernel,
        grid=(BATCH, NRB),
        in_specs=[pl.BlockSpec((1, RB, D), lambda b, i: (b, i, PB_QK)),
                  pl.BlockSpec((1, 16, D), lambda b, i: (b, jnp.maximum(i * (RB // 16) - 1, 0), PB_QK)),
                  pl.BlockSpec((1, 16, D), lambda b, i: (b, jnp.minimum((i + 1) * (RB // 16), nb16 - 1), PB_QK)),
                  pl.BlockSpec((3, D), lambda b, i: (0, 0)),
                  pl.BlockSpec((RB, 2 * A_DQK), lambda b, i: (i, 0)),
                  pl.BlockSpec((RB, 2 * A_DQK), lambda b, i: (i, 0))],
        out_specs=pl.BlockSpec((1, RB, D), lambda b, i: (b, i, 0)),
        out_shape=jax.ShapeDtypeStruct((BATCH, TT, D), BF16),
        compiler_params=_cp(("parallel", "parallel")),
        name="mlstm_prep",
    )(p3, p3, p3, conv_w, rope_c, rope_s)


def _mlstm_kernel(qkf_ref, vf_ref, gf_ref, lff_ref, qkb_ref, vb_ref, gb_ref, lfb_ref,
                  of_ref, ob_ref, ct_ref, n_ref, m_ref):
    nscan = 2 * SCAN_MB
    per_dir = [(qkf_ref, vf_ref, gf_ref, lff_ref, of_ref), (qkb_ref, vb_ref, gb_ref, lfb_ref, ob_ref)]
    ins = [per_dir[d % 2] + (d // 2,) for d in range(nscan)]

    @pl.when(pl.program_id(1) == 0)
    def _():
        ct_ref[...] = jnp.zeros_like(ct_ref)
        n_ref[...] = jnp.zeros_like(n_ref)
        m_ref[...] = jnp.zeros_like(m_ref)

    r = lax.broadcasted_iota(jnp.int32, (LCH, LCH), 0)
    c = lax.broadcasted_iota(jnp.int32, (LCH, LCH), 1)
    dir_masks = [c <= r, c >= r]
    masks = [dir_masks[d % 2] for d in range(nscan)]
    b_all, g_all, b_t, g_t, b_end = [], [], [], [], []
    for d in range(nscan):
        tri = jnp.where(masks[d], 1.0, 0.0).astype(BF16)
        bc = jnp.dot(tri, ins[d][3][ins[d][5]], preferred_element_type=F32)
        ba = bc + pltpu.roll(bc, LANES - NGATE, 1) + pltpu.roll(bc, LANES - 2 * NGATE, 1)
        ga = ins[d][2][ins[d][5]]
        if d % 2 == 1:
            ba = pltpu.roll(ba, LANES - 2 * A_HEADS, 1)
            ga = pltpu.roll(ga, LANES - 2 * A_HEADS, 1)
        b_all.append(ba)
        g_all.append(ga)
        b_t.append(ba.T)
        g_t.append(ga.T)
        b_end.append(ba[LCH - 1:LCH, :] if d % 2 == 0 else ba[0:1, :])
    ones = jnp.ones((LCH, LANES), BF16)
    tn_dims = (((0,), (0,)), ((), ()))
    combos = [(d, h) for d in range(nscan) for h in range(A_HEADS)]
    idx = range(len(combos))
    qs = [ins[d][0][ins[d][5], :, h * A_DQK:(h + 1) * A_DQK] for d, h in combos]
    ks = [ins[d][0][ins[d][5], :, A_QK + h * A_DQK:A_QK + (h + 1) * A_DQK] for d, h in combos]
    qk = [lax.dot_general(qs[i], ks[i], NT_DIMS, preferred_element_type=F32) for i in idx]
    b_col = [b_all[d][:, A_HEADS + h:A_HEADS + h + 1] for d, h in combos]
    b_last = [b_end[d][:, A_HEADS + h:A_HEADS + h + 1] for d, h in combos]
    m_old = [m_ref[i][:, 0:1] for i in idx]
    r_row = [b_t[d][A_HEADS + h:A_HEADS + h + 1, :] - g_t[d][h:h + 1, :] for d, h in combos]
    m_row = [jnp.maximum(b_col[i] + m_old[i],
                         jnp.max(jnp.where(masks[d], b_col[i] - r_row[i], -jnp.inf), axis=1, keepdims=True))
             for i, (d, h) in enumerate(combos)]
    s = [(qk[i] * jnp.exp(jnp.where(masks[d], (b_col[i] - m_row[i]) - r_row[i], -jnp.inf))).astype(BF16)
         for i, (d, h) in enumerate(combos)]
    wq = [(jnp.exp(b_col[i] + m_old[i] - m_row[i]) * qs[i].astype(F32)).astype(BF16) for i in idx]
    g_col = [b_last[i] - b_col[i] + g_all[d][:, h:h + 1] for i, (d, h) in enumerate(combos)]
    m_new = [jnp.maximum(b_last[i] + m_old[i], jnp.max(g_col[i], axis=0, keepdims=True)) for i in idx]
    kw = [(jnp.exp(g_col[i] - m_new[i]) * ks[i].astype(F32)).astype(BF16) for i in idx]
    decay = [jnp.exp(b_last[i] + m_old[i] - m_new[i]) for i in idx]
    for i, (d, h) in enumerate(combos):
        v = ins[d][1][ins[d][5], :, h * A_DV:(h + 1) * A_DV]
        lhs = jnp.concatenate([s[i], wq[i]], axis=1)
        den = jnp.dot(lhs, jnp.concatenate([ones, n_ref[i].astype(BF16)], axis=0),
                      preferred_element_type=F32)[:, 0:1]
        rinv = 1.0 / jnp.maximum(jnp.abs(den), jnp.exp(-m_row[i]))
        num = jnp.dot(lhs, jnp.concatenate([v, ct_ref[i].astype(BF16)], axis=0), preferred_element_type=F32)
        ins[d][4][ins[d][5], :, h * A_DV:(h + 1) * A_DV] = (num * rinv).astype(ins[d][4].dtype)
    for i, (d, h) in enumerate(combos):
        v = ins[d][1][ins[d][5], :, h * A_DV:(h + 1) * A_DV]
        ct_ref[i] = decay[i] * ct_ref[i] + lax.dot_general(kw[i], v, tn_dims, preferred_element_type=F32)
        n_ref[i] = decay[i] * n_ref[i] + lax.dot_general(kw[i], ones, tn_dims, preferred_element_type=F32)
        m_ref[i] = jnp.broadcast_to(m_new[i], (1, LANES))


def _rev_chunk(s):
    ncc = CTX // LCH
    return jnp.where(s < ncc, ncc - 1 - s, NCH + ncc - 1 - s)


def _mlstm_scan(qk, p3, gates, lfp):
    fwd = lambda b, s: (b, s, 0)
    bwd = lambda b, s: (b, _rev_chunk(s), 0)
    specs = lambda im, imv: [pl.BlockSpec((SCAN_MB, LCH, D), im), pl.BlockSpec((SCAN_MB, LCH, A_V), imv),
                             pl.BlockSpec((SCAN_MB, LCH, LANES), im), pl.BlockSpec((SCAN_MB, LCH, LANES), im)]
    nstate = 2 * SCAN_MB * A_HEADS
    return pl.pallas_call(
        _mlstm_kernel,
        grid=(BATCH // SCAN_MB, NCH),
        in_specs=(specs(fwd, lambda b, s: (b, s, PB_V))
                  + specs(bwd, lambda b, s: (b, _rev_chunk(s), PB_V))),
        out_specs=[pl.BlockSpec((SCAN_MB, LCH, A_V), fwd), pl.BlockSpec((SCAN_MB, LCH, A_V), bwd)],
        out_shape=[jax.ShapeDtypeStruct((BATCH, TT, A_V), BF16)] * 2,
        scratch_shapes=[pltpu.VMEM((nstate, A_DQK, A_DV), F32),
                        pltpu.VMEM((nstate, A_DQK, LANES), F32),
                        pltpu.VMEM((nstate, 1, LANES), F32)],
        compiler_params=_cp(("parallel", "arbitrary")),
        name="mlstm_scan",
    )(qk, p3, gates, lfp, qk, p3, gates, lfp)


NPAIR = B_HEADS // 2
NT_DIMS = (((1,), (1,)), ((), ()))


def _stack_pair(q, low):
    zero = jnp.zeros_like(q)
    return jnp.concatenate([jnp.where(low, q, zero), jnp.where(low, zero, q)], axis=0)


def _softmax_rows(s_ref, p_ref):
    s = s_ref[...]
    p = jnp.exp2(s - jnp.max(s, axis=2, keepdims=True))
    p_ref[...] = p.astype(p_ref.dtype)
    return 1.0 / jnp.sum(p, axis=2, keepdims=True)


def _na_kernel(q_ref, k_ref, v_ref, *rest):
    bias_refs, (o_ref, s_ref, p_ref) = rest[:NA_SUB], rest[NA_SUB:]
    low = lax.broadcasted_iota(jnp.int32, (1, 2 * B_DH), 1) < B_DH
    starts = []
    for sub in range(NA_SUB):
        row = pl.program_id(1) * NA_SUB + sub
        rs = jnp.clip(row - NA_ROWS // 2, 0, GRID_H - NA_ROWS)
        starts.append(pl.multiple_of(CTX + rs * GRID_W, GRID_W))
    for sub in range(NA_SUB):
        qrows = pl.ds(sub * NA_NQ, NA_NQ)
        for hp in range(NPAIR):
            lo, hi = hp * 2 * B_DH, (hp + 1) * 2 * B_DH
            q2 = _stack_pair(q_ref[0, qrows, lo:hi], low)
            idx = sub * NPAIR + hp
            s_ref[idx, :, 0:NA_NK] = lax.dot_general(q2, k_ref[0, pl.ds(starts[sub], NA_NK), lo:hi], NT_DIMS,
                                                     preferred_element_type=F32) + bias_refs[sub][0, 0, hp]
            s_ref[idx, :, NA_NK:] = lax.dot_general(q2, k_ref[0, 0:CTX, lo:hi], NT_DIMS,
                                                    preferred_element_type=F32)
    rinv = _softmax_rows(s_ref, p_ref)
    for sub in range(NA_SUB):
        qrows = pl.ds(sub * NA_NQ, NA_NQ)
        for hp in range(NPAIR):
            lo, hi = hp * 2 * B_DH, (hp + 1) * 2 * B_DH
            idx = sub * NPAIR + hp
            o2 = (jnp.dot(p_ref[idx, :, 0:NA_NK], v_ref[0, pl.ds(starts[sub], NA_NK), lo:hi],
                          preferred_element_type=F32)
                  + jnp.dot(p_ref[idx, :, NA_NK:], v_ref[0, 0:CTX, lo:hi], preferred_element_type=F32))
            o2 = o2 * rinv[idx]
            o_ref[0, qrows, lo:hi] = jnp.where(low, o2[0:NA_NQ], o2[NA_NQ:]).astype(o_ref.dtype)


def _na_pattern(row):
    edge = NA_ROWS // 2
    return jnp.where(row < edge, row, jnp.where(row <= GRID_H - edge, edge, row - (GRID_H - 2 * edge)))


NA_NPAT = NA_ROWS
NA_SUB = 2


def _na_attention(p3, bias, layer):
    nq = NA_SUB * NA_NQ
    qoff = CTX // nq
    bias_specs = [pl.BlockSpec((1, 1, NPAIR, 2 * NA_NQ, NA_NK),
                               lambda b, i, sub=sub: (layer, _na_pattern(i * NA_SUB + sub), 0, 0, 0))
                  for sub in range(NA_SUB)]
    return pl.pallas_call(
        _na_kernel,
        grid=(BATCH, GRID_H // NA_SUB),
        in_specs=[pl.BlockSpec((1, nq, D), lambda b, i: (b, i + qoff, PB_NQ)),
                  pl.BlockSpec((1, TT, D), lambda b, i: (b, 0, PB_NK)),
                  pl.BlockSpec((1, TT, D), lambda b, i: (b, 0, PB_NV))] + bias_specs,
        out_specs=pl.BlockSpec((1, nq, D), lambda b, i: (b, i, 0)),
        out_shape=jax.ShapeDtypeStruct((BATCH, SEQ, D), BF16),
        scratch_shapes=[pltpu.VMEM((NA_SUB * NPAIR, 2 * NA_NQ, NA_NK + CTX), F32),
                        pltpu.VMEM((NA_SUB * NPAIR, 2 * NA_NQ, NA_NK + CTX), BF16)],
        compiler_params=_cp(("parallel", "arbitrary")),
        name="na_attention",
    )(p3, p3, p3, *([bias] * NA_SUB))


def _ctx_attn_kernel(q_ref, k_ref, v_ref, o_ref, s_ref, p_ref):
    low = lax.broadcasted_iota(jnp.int32, (1, 2 * B_DH), 1) < B_DH
    for hp in range(NPAIR):
        lo, hi = hp * 2 * B_DH, (hp + 1) * 2 * B_DH
        s_ref[hp] = lax.dot_general(_stack_pair(q_ref[0, :, lo:hi], low), k_ref[0, :, lo:hi], NT_DIMS,
                                    preferred_element_type=F32)
    rinv = _softmax_rows(s_ref, p_ref)
    for hp in range(NPAIR):
        lo, hi = hp * 2 * B_DH, (hp + 1) * 2 * B_DH
        o2 = jnp.dot(p_ref[hp], v_ref[0, :, lo:hi], preferred_element_type=F32) * rinv[hp]
        o_ref[0, :, lo:hi] = jnp.where(low, o2[0:CTX], o2[CTX:]).astype(o_ref.dtype)


def _ctx_attention(p3):
    return pl.pallas_call(
        _ctx_attn_kernel,
        grid=(BATCH,),
        in_specs=[pl.BlockSpec((1, CTX, D), lambda b: (b, 0, PB_NQ)),
                  pl.BlockSpec((1, CTX, D), lambda b: (b, 0, PB_NK)),
                  pl.BlockSpec((1, CTX, D), lambda b: (b, 0, PB_NV))],
        out_specs=pl.BlockSpec((1, CTX, D), lambda b: (b, 0, 0)),
        out_shape=jax.ShapeDtypeStruct((BATCH, CTX, D), BF16),
        scratch_shapes=[pltpu.VMEM((NPAIR, 2 * CTX, CTX), F32),
                        pltpu.VMEM((NPAIR, 2 * CTX, CTX), BF16)],
        compiler_params=_cp(("parallel",)),
        name="ctx_attention",
    )(p3, p3, p3)


def _merge_kernel(hf_ref, hbw_ref, o_ref, ga_ref, gb_ref, hnl_ref, hnc_ref, x_ref, mod_ref, ghn_ref,
                  wa_ref, wb_ref, wo_ref, g2_ref, x1_ref, h2_ref):
    hn = jnp.where(pl.program_id(1) == 0, hnc_ref[0], hnl_ref[0])
    bm = jnp.dot(hn, wb_ref[...], preferred_element_type=F32)
    hs = hf_ref[0].astype(F32) + hbw_ref[0].astype(F32)
    parts = []
    for h in range(A_HEADS):
        seg = hs[:, h * A_DV:(h + 1) * A_DV]
        mu = jnp.mean(seg, axis=-1, keepdims=True)
        cen = seg - mu
        var = jnp.mean(cen * cen, axis=-1, keepdims=True)
        parts.append(cen * lax.rsqrt(var + EPS))
    ya = jnp.concatenate(parts, axis=1) * ghn_ref[...] * _sigmoid(o_ref[0].astype(F32))
    a = jnp.dot(ya.astype(BF16), wa_ref[...], preferred_element_type=F32)
    mrg = _sigmoid(ga_ref[0].astype(F32)) * a + _sigmoid(gb_ref[0].astype(F32)) * bm
    y = jnp.dot(mrg.astype(BF16), wo_ref[...], preferred_element_type=F32)
    mod = mod_ref[0]
    x1 = x_ref[0] + mod[2:3] * y
    x1_ref[0] = x1
    h2_ref[0] = _norm_mod(x1, g2_ref[...], mod[4:5], mod[3:4]).astype(h2_ref.dtype)


def _merge(hf, hbw, p3, hb_lat, hb_ctx, x, mod, ghn, wa, wb, wo, g2, h2_dtype):
    row = lambda b, i: (b, i, 0)
    full = lambda b, i: (0, 0)
    return pl.pallas_call(
        _merge_kernel,
        grid=(BATCH, NRB),
        in_specs=[pl.BlockSpec((1, RB, A_V), row),
                  pl.BlockSpec((1, RB, A_V), row),
                  pl.BlockSpec((1, RB, D), lambda b, i: (b, i, PB_O)),
                  pl.BlockSpec((1, RB, D), lambda b, i: (b, i, PB_GA)),
                  pl.BlockSpec((1, RB, D), lambda b, i: (b, i, PB_GB)),
                  pl.BlockSpec((1, RB, D), lambda b, i: (b, jnp.maximum(i - 1, 0), 0)),
                  pl.BlockSpec((1, CTX, D), lambda b, i: (b, 0, 0)),
                  pl.BlockSpec((1, RB, D), row),
                  pl.BlockSpec((1, 6, D), lambda b, i: (_mod_row(b, i), 0, 0)),
                  pl.BlockSpec((1, A_V), full),
                  pl.BlockSpec((A_V, D), full),
                  pl.BlockSpec((D, D), full),
                  pl.BlockSpec((D, D), full),
                  pl.BlockSpec((1, D), full)],
        out_specs=[pl.BlockSpec((1, RB, D), row), pl.BlockSpec((1, RB, D), row)],
        out_shape=[jax.ShapeDtypeStruct((BATCH, TT, D), F32),
                   jax.ShapeDtypeStruct((BATCH, TT, D), h2_dtype)],
        compiler_params=_cp(("parallel", "parallel")),
        name="merge",
    )(hf, hbw, p3, p3, p3, hb_lat, hb_ctx, x, mod, ghn, wa, wb, wo, g2)


FFN_CH = D_FF // 2


def _ffn_kernel(h_ref, x_ref, mod_ref, w1_ref, w3_ref, w2_ref, gn_ref, modn_ref, x2_ref, hn_ref):
    h = h_ref[0]
    y = None
    for cidx in range(D_FF // FFN_CH):
        lo, hi = cidx * FFN_CH, (cidx + 1) * FFN_CH
        a = jnp.dot(h, w1_ref[:, lo:hi], preferred_element_type=F32)
        g = jnp.dot(h, w3_ref[:, lo:hi], preferred_element_type=F32)
        part = jnp.dot((_silu(a) * g).astype(BF16), w2_ref[lo:hi, :], preferred_element_type=F32)
        y = part if y is None else y + part
    mod = mod_ref[0]
    x2 = x_ref[0] + mod[5:6] * y
    x2_ref[0] = x2
    modn = modn_ref[0]
    hn_ref[0] = _norm_mod(x2, gn_ref[...], modn[1:2], modn[0:1]).astype(hn_ref.dtype)


def _dense_ffn(h2, x1, mod, w1, w3, w2, gn, modn):
    row = lambda b, i: (b, i, 0)
    full = lambda b, i: (0, 0)
    modspec = pl.BlockSpec((1, 6, D), lambda b, i: (_mod_row(b, i), 0, 0))
    return pl.pallas_call(
        _ffn_kernel,
        grid=(BATCH, NRB),
        in_specs=[pl.BlockSpec((1, RB, D), row), pl.BlockSpec((1, RB, D), row), modspec,
                  pl.BlockSpec((D, D_FF), full), pl.BlockSpec((D, D_FF), full),
                  pl.BlockSpec((D_FF, D), full), pl.BlockSpec((1, D), full), modspec],
        out_specs=[pl.BlockSpec((1, RB, D), row), pl.BlockSpec((1, RB, D), row)],
        out_shape=[jax.ShapeDtypeStruct((BATCH, TT, D), F32),
                   jax.ShapeDtypeStruct((BATCH, TT, D), BF16)],
        compiler_params=_cp(("parallel", "parallel")),
        name="dense_ffn",
    )(h2, x1, mod, w1, w3, w2, gn, modn)


def _router_kernel(h_ref, wr_ref, route_ref, cnt_ref, run_ref):
    i = pl.program_id(0)

    @pl.when(i == 0)
    def _():
        run_ref[...] = jnp.zeros_like(run_ref)

    logits = jnp.dot(h_ref[...].astype(BF16), wr_ref[...], preferred_element_type=F32)
    lane = lax.broadcasted_iota(jnp.int32, (RB, LANES), 1).astype(F32)
    lg = jnp.where(lane < N_EXPERTS, logits, -jnp.inf)
    v1 = jnp.max(lg, axis=1, keepdims=True)
    i1 = jnp.min(jnp.where(lg == v1, lane, float(LANES)), axis=1, keepdims=True)
    lg2 = jnp.where(lane == i1, -jnp.inf, lg)
    v2 = jnp.max(lg2, axis=1, keepdims=True)
    i2 = jnp.min(jnp.where(lg2 == v2, lane, float(LANES)), axis=1, keepdims=True)
    e = jnp.exp(v2 - v1)
    w1 = 1.0 / (1.0 + e)
    w2 = e / (1.0 + e)
    oh1 = (lane == i1).astype(F32)
    oh2 = (lane == i2).astype(F32)
    r = lax.broadcasted_iota(jnp.int32, (RB, RB), 0)
    c = lax.broadcasted_iota(jnp.int32, (RB, RB), 1)
    tri = (r > c).astype(BF16)
    cs1 = jnp.dot(tri, oh1.astype(BF16), preferred_element_type=F32)
    cs2 = jnp.dot(tri, oh2.astype(BF16), preferred_element_type=F32)
    tot1 = jnp.sum(oh1, axis=0, keepdims=True)
    tot2 = jnp.sum(oh2, axis=0, keepdims=True)
    run = run_ref[...]
    rank1 = jnp.sum(oh1 * (run + cs1), axis=1, keepdims=True)
    rank2 = jnp.sum(oh2 * (run + tot1 + cs2), axis=1, keepdims=True)
    new_run = run + tot1 + tot2
    run_ref[...] = new_run
    cnt_ref[...] = new_run
    out = jnp.where(lane == 0, i1,
          jnp.where(lane == 1, i2,
          jnp.where(lane == 2, w1,
          jnp.where(lane == 3, w2,
          jnp.where(lane == 4, rank1,
          jnp.where(lane == 5, rank2, 0.0))))))
    route_ref[...] = out


def _router(h2_flat, wr, nblk, blk_of):
    return pl.pallas_call(
        _router_kernel,
        grid=(nblk,),
        in_specs=[pl.BlockSpec((RB, D), lambda i: (blk_of(i), 0)),
                  pl.BlockSpec((D, LANES), lambda i: (0, 0))],
        out_specs=[pl.BlockSpec((RB, LANES), lambda i: (i, 0)),
                   pl.BlockSpec((1, LANES), lambda i: (0, 0))],
        out_shape=[jax.ShapeDtypeStruct((nblk * RB, LANES), F32),
                   jax.ShapeDtypeStruct((1, LANES), F32)],
        scratch_shapes=[pltpu.VMEM((1, LANES), F32)],
        compiler_params=_cp(("arbitrary",)),
        name="moe_router",
    )(h2_flat, wr)


DMA_UNROLL = 8
N_ZERO = 2 * N_EXPERTS


def _dispatch_kernel(pos_ref, zstart_ref, h_ref, o_ref, zbuf, stage, sems, zsem):
    @pl.when(pl.program_id(0) == 0)
    def _():
        zbuf[...] = jnp.zeros_like(zbuf)
        for z in range(N_ZERO):
            @pl.when(zstart_ref[z] >= 0)
            def _():
                zs = pl.multiple_of(zstart_ref[z], MOE_TM)
                pltpu.make_async_copy(zbuf, o_ref.at[pl.ds(zs, MOE_TM)], zsem).start()
        for z in range(N_ZERO):
            @pl.when(zstart_ref[z] >= 0)
            def _():
                pltpu.make_async_copy(zbuf, o_ref.at[pl.ds(0, MOE_TM)], zsem).wait()

    step = pl.program_id(0)
    slot = step % 2
    stage[slot] = h_ref[...]

    def issue(t, carry):
        src = stage.at[slot, pl.ds(t, 1)]
        pltpu.make_async_copy(src, o_ref.at[pl.ds(pos_ref[0, 0, t], 1)], sems.at[slot]).start(priority=0)
        pltpu.make_async_copy(src, o_ref.at[pl.ds(pos_ref[0, 0, RB + t], 1)], sems.at[slot]).start(priority=1)
        return carry

    lax.fori_loop(0, RB, issue, 0, unroll=DMA_UNROLL)

    def drain(sl):
        for _ in range(2):
            pltpu.make_async_copy(stage.at[sl], o_ref.at[pl.ds(0, RB)], sems.at[sl]).wait()

    @pl.when(step > 0)
    def _():
        drain(1 - slot)

    @pl.when(step == pl.num_programs(0) - 1)
    def _():
        drain(slot)


def _dispatch(pos, zstart, h2_flat, nblk, blk_of, n_sorted):
    return pl.pallas_call(
        _dispatch_kernel,
        grid=(nblk,),
        in_specs=[pl.BlockSpec((1, 1, 2 * RB), lambda i: (i, 0, 0), memory_space=pltpu.SMEM),
                  pl.BlockSpec(memory_space=pltpu.SMEM),
                  pl.BlockSpec((RB, D), lambda i: (blk_of(i), 0))],
        out_specs=pl.BlockSpec(memory_space=pl.ANY),
        out_shape=jax.ShapeDtypeStruct((n_sorted, D), F32),
        scratch_shapes=[pltpu.VMEM((MOE_TM, D), F32), pltpu.VMEM((2, RB, D), F32),
                        pltpu.SemaphoreType.DMA((2,)), pltpu.SemaphoreType.DMA(())],
        compiler_params=_cp(("arbitrary",)),
        name="moe_dispatch",
    )(pos, zstart, h2_flat)


def _gmm_kernel(be_ref, nb_ref, a_ref, w1_ref, w3_ref, w2_ref, o_ref, abf_ref):
    del be_ref
    i = pl.program_id(0)
    f = pl.program_id(1)

    @pl.when((i < nb_ref[0]) & (f == 0))
    def _():
        abf_ref[...] = a_ref[...].astype(BF16)

    @pl.when(i < nb_ref[0])
    def _():
        a = abf_ref[...]
        gs = []
        for c in range(MOE_FC // MOE_SUB):
            lo, hi = c * MOE_SUB, (c + 1) * MOE_SUB
            h1 = jnp.dot(a, w1_ref[0, :, lo:hi].astype(BF16), preferred_element_type=F32)
            h3 = jnp.dot(a, w3_ref[0, :, lo:hi].astype(BF16), preferred_element_type=F32)
            gs.append((_silu(h1) * h3).astype(BF16))
        part = jnp.dot(jnp.concatenate(gs, axis=1), w2_ref[0].astype(BF16), preferred_element_type=F32)

        @pl.when(f == 0)
        def _():
            o_ref[...] = part

        @pl.when(f > 0)
        def _():
            o_ref[...] += part

    @pl.when((i >= nb_ref[0]) & (f == 0))
    def _():
        o_ref[...] = jnp.zeros_like(o_ref)


def _expert_ffn(blk_e, nb, hs, w1, w3, w2, j):
    nbmax = hs.shape[0] // MOE_TM
    nf = D_FF_EXPERT // MOE_FC

    def ieff(i, nb_ref):
        return jnp.minimum(i, nb_ref[0] - 1)

    def feff(i, f, nb_ref):
        return jnp.where(i < nb_ref[0], f, nf - 1)

    grid_spec = pltpu.PrefetchScalarGridSpec(
        num_scalar_prefetch=2,
        grid=(nbmax, nf),
        in_specs=[pl.BlockSpec((MOE_TM, D), lambda i, f, be, nbr: (ieff(i, nbr), 0)),
                  pl.BlockSpec((None, 1, D, MOE_FC),
                               lambda i, f, be, nbr: (j, be[ieff(i, nbr)], 0, feff(i, f, nbr))),
                  pl.BlockSpec((None, 1, D, MOE_FC),
                               lambda i, f, be, nbr: (j, be[ieff(i, nbr)], 0, feff(i, f, nbr))),
                  pl.BlockSpec((None, 1, MOE_FC, D),
                               lambda i, f, be, nbr: (j, be[ieff(i, nbr)], feff(i, f, nbr), 0))],
        out_specs=pl.BlockSpec((MOE_TM, D), lambda i, f, be, nbr: (i, 0)),
        scratch_shapes=[pltpu.VMEM((MOE_TM, D), BF16)],
    )
    return pl.pallas_call(
        _gmm_kernel,
        grid_spec=grid_spec,
        out_shape=jax.ShapeDtypeStruct(hs.shape, F32),
        compiler_params=_cp(("arbitrary", "arbitrary")),
        name="moe_expert_ffn",
    )(blk_e, nb, hs, w1, w3, w2)


def _combine_kernel(pos_ref, posn_ref, y_ref, route_ref, x_ref, mod_ref, gn_ref, modn_ref,
                    x2_ref, hn_ref, ybuf, sems):
    t = pl.program_id(0) * pl.num_programs(1) + pl.program_id(1)
    nsteps = pl.num_programs(0) * pl.num_programs(1)
    slot = t % 2

    def start_block(p_ref, sl):
        def issue(r, carry):
            pltpu.make_async_copy(y_ref.at[pl.ds(p_ref[0, 0, r], 1)],
                                  ybuf.at[sl, 0, pl.ds(r, 1)], sems.at[sl]).start(priority=0)
            pltpu.make_async_copy(y_ref.at[pl.ds(p_ref[0, 0, RB + r], 1)],
                                  ybuf.at[sl, 1, pl.ds(r, 1)], sems.at[sl]).start(priority=1)
            return carry
        lax.fori_loop(0, RB, issue, 0, unroll=DMA_UNROLL)

    @pl.when(t == 0)
    def _():
        start_block(pos_ref, 0)

    @pl.when(t + 1 < nsteps)
    def _():
        start_block(posn_ref, 1 - slot)

    for e in range(2):
        pltpu.make_async_copy(y_ref.at[pl.ds(0, RB)], ybuf.at[slot, e], sems.at[slot]).wait()
    route = route_ref[...]
    y = route[:, 2:3] * ybuf[slot, 0] + route[:, 3:4] * ybuf[slot, 1]
    mod = mod_ref[0]
    x2 = x_ref[0] + mod[5:6] * y
    x2_ref[0] = x2
    modn = modn_ref[0]
    hn_ref[0] = _norm_mod(x2, gn_ref[...], modn[1:2], modn[0:1]).astype(hn_ref.dtype)


def _combine(pos, ys, route, x1, mod, gn, modn, latent_only, hn_dtype):
    nrb = NRB - 1 if latent_only else NRB
    off = 1 if latent_only else 0
    rows = nrb * RB
    full = lambda b, i: (0, 0)
    modspec = pl.BlockSpec((1, 6, D), lambda b, i: (_mod_row(b, i + off), 0, 0))
    last = BATCH * nrb - 1
    return pl.pallas_call(
        _combine_kernel,
        grid=(BATCH, nrb),
        in_specs=[pl.BlockSpec((1, 1, 2 * RB), lambda b, i: (b * nrb + i, 0, 0), memory_space=pltpu.SMEM),
                  pl.BlockSpec((1, 1, 2 * RB), lambda b, i: (jnp.minimum(b * nrb + i + 1, last), 0, 0),
                               memory_space=pltpu.SMEM),
                  pl.BlockSpec(memory_space=pl.ANY),
                  pl.BlockSpec((RB, LANES), lambda b, i: (b * nrb + i, 0)),
                  pl.BlockSpec((1, RB, D), lambda b, i: (b, i + off, 0)),
                  modspec,
                  pl.BlockSpec((1, D), full),
                  modspec],
        out_specs=[pl.BlockSpec((1, RB, D), lambda b, i: (b, i, 0)),
                   pl.BlockSpec((1, RB, D), lambda b, i: (b, i, 0))],
        out_shape=[jax.ShapeDtypeStruct((BATCH, rows, D), F32),
                   jax.ShapeDtypeStruct((BATCH, rows, D), hn_dtype)],
        scratch_shapes=[pltpu.VMEM((2, 2, RB, D), F32), pltpu.SemaphoreType.DMA((2,))],
        compiler_params=_cp(("arbitrary", "arbitrary")),
        name="moe_combine",
    )(pos, pos, ys, route, x1, mod, gn, modn)


def _moe_ffn(h2, x1, mod, wr, w1, w3, w2, j, gn, modn, latent_only, hn_dtype):
    h2_flat = h2.reshape(NTOK, D)
    if latent_only:
        nrb = NRB - 1
        blk_of = lambda i: (i // nrb) * NRB + (i % nrb) + 1
    else:
        nrb = NRB
        blk_of = lambda i: i
    nblk = BATCH * nrb
    n_pairs = 2 * nblk * RB
    nbmax = n_pairs // MOE_TM + N_EXPERTS
    n_sorted = nbmax * MOE_TM

    route, counts = _router(h2_flat, wr, nblk, blk_of)
    cnt = counts[0, :N_EXPERTS].astype(jnp.int32)
    gsz = ((cnt + MOE_TM - 1) // MOE_TM) * MOE_TM
    gend = jnp.cumsum(gsz)
    goff = gend - gsz
    e12 = route[:, 0:2].astype(jnp.int32)
    rank = route[:, 4:6].astype(jnp.int32)
    pos = goff[e12] + rank
    pos = pos.reshape(nblk, RB, 2).transpose(0, 2, 1).reshape(nblk, 1, 2 * RB)
    nb = (gend[-1] // MOE_TM).reshape(1)
    blk_start = jnp.arange(nbmax, dtype=jnp.int32) * MOE_TM
    blk_e = jnp.minimum(jnp.sum(blk_start[:, None] >= gend[None, :], axis=1), N_EXPERTS - 1).astype(jnp.int32)

    trail = (nbmax - N_EXPERTS + jnp.arange(N_EXPERTS, dtype=jnp.int32)) * MOE_TM
    zstart = jnp.concatenate([jnp.where(cnt > 0, gend - MOE_TM, -1),
                              jnp.where(trail >= gend[-1], trail, -1)]).astype(jnp.int32)

    hs = _dispatch(pos, zstart, h2_flat, nblk, blk_of, n_sorted)
    ys = _expert_ffn(blk_e, nb, hs, w1, w3, w2, j)
    return _combine(pos, ys, route, x1, mod, gn, modn, latent_only, hn_dtype)


def _rope_tables():
    quarter = A_DQK // 4
    inv = 1.0 / (ROPE_BASE ** (jnp.arange(quarter, dtype=F32) / quarter))
    pos = jnp.arange(SEQ)
    rows = (pos // GRID_W).astype(F32)
    cols = (pos % GRID_W).astype(F32)
    ang_r = rows[:, None] * inv[None, :]
    ang_c = cols[:, None] * inv[None, :]
    cos = jnp.concatenate([jnp.cos(ang_r)] * 2 + [jnp.cos(ang_c)] * 2, axis=1)
    sin = jnp.concatenate([-jnp.sin(ang_r), jnp.sin(ang_r), -jnp.sin(ang_c), jnp.sin(ang_c)], axis=1)
    cos = jnp.concatenate([jnp.ones((CTX, A_DQK), F32), cos], axis=0)
    sin = jnp.concatenate([jnp.zeros((CTX, A_DQK), F32), sin], axis=0)
    kscale = A_DQK ** -0.5
    return (jnp.concatenate([cos, cos * kscale], axis=1),
            jnp.concatenate([sin, sin * kscale], axis=1))


def _na_bias_tables(rpb):
    rs = np.clip(np.arange(GRID_H) - NA_ROWS // 2, 0, GRID_H - NA_ROWS)
    cs = np.clip(np.arange(GRID_W) - NA_COLS // 2, 0, GRID_W - NA_COLS)
    rows = [0, 1, 2, 3, NA_ROWS // 2, GRID_H - 3, GRID_H - 2, GRID_H - 1]
    col = np.arange(GRID_W)
    dc = np.clip(col[None, :] - col[:, None] + NA_COLS - 1, 0, 2 * NA_COLS - 2)
    valid_c = (col[None, :] >= cs[:, None]) & (col[None, :] < cs[:, None] + NA_COLS)
    sel_c = np.eye(2 * NA_COLS - 1, dtype=np.float32)[dc]
    t = jnp.einsum('lhab,uvb->lhuav', rpb * LOG2E, jnp.asarray(sel_c),
                   precision=lax.Precision.HIGHEST)
    neg = np.where(valid_c, 0.0, -np.inf).astype(np.float32)[:, None, :]
    nl = rpb.shape[0]
    t = (t + jnp.asarray(neg)).reshape(nl, NPAIR, 2 * NA_NQ, (2 * NA_ROWS - 1) * GRID_W)
    first = [rs[r] - r + NA_ROWS - 1 for r in rows]
    return jnp.stack([t[..., d0 * GRID_W:d0 * GRID_W + NA_NK] for d0 in first], axis=1)


def kernel(x, c, ctx, c_ctx, w_mod, b_mod, g_norm1, g_norm2, w_in, a_conv, a_gate_b, a_hnorm_g, na_rpb,
           w_br_a, w_br_b, w_out, ffn_w1, ffn_w3, ffn_w2, moe_router, moe_w1, moe_w3, moe_w2, g_final):
    cc = jnp.concatenate([c, c_ctx[None, :], jnp.zeros((16 - BATCH - 1, D), F32)], axis=0)
    mod_all = _modulation(cc, w_mod, b_mod).reshape(DEPTH, 16, 6, D)[:, :BATCH + 1]
    mod_zero = jnp.zeros((BATCH + 1, 6, D), F32)
    rope_c, rope_s = _rope_tables()
    na_bias = _na_bias_tables(na_rpb)

    xs = jnp.concatenate([ctx, x], axis=1)
    h1 = _first_norm(xs, g_norm1[0][None, :], mod_all[0])
    out = None
    for l in range(DEPTH):
        last = l == DEPTH - 1
        mod = mod_all[l]
        wl = w_in[l]
        g0 = 3 * D
        g1 = g0 + NGATE
        wp = jnp.concatenate([wl[:, :g0], wl[:, g1:g1 + D] * (B_DH ** -0.5 * LOG2E), wl[:, g1 + D:]],
                             axis=1).astype(BF16)
        wg = jnp.pad(wl[:, g0:g1], ((0, 0), (0, LANES - NGATE))).astype(BF16)
        bg = jnp.pad(a_gate_b[l], (0, LANES - NGATE))[None, :]

        h1_flat = h1.reshape(NTOK, D)
        p3 = _in_proj(h1_flat, wp).reshape(BATCH, TT, P_COLS)
        gates, lfp = _gate_proj(h1_flat, wg, bg)

        qk = _mlstm_prep(p3, a_conv[l], rope_c, rope_s)
        hf, hbw = _mlstm_scan(qk, p3, gates.reshape(BATCH, TT, LANES), lfp.reshape(BATCH, TT, LANES))
        hb_lat = _na_attention(p3, na_bias, l)
        hb_ctx = _ctx_attention(p3)

        moe = l % 2 == 1
        x1, h2 = _merge(hf, hbw, p3, hb_lat, hb_ctx, xs, mod, a_hnorm_g[l][None, :],
                        w_br_a[l].astype(BF16), w_br_b[l].astype(BF16), w_out[l].astype(BF16),
                        g_norm2[l][None, :], F32 if moe else BF16)
        if last:
            gn, modn = g_final[None, :], mod_zero
        else:
            gn, modn = g_norm1[l + 1][None, :], mod_all[l + 1]
        j = l // 2
        if not moe:
            xs, h1 = _dense_ffn(h2, x1, mod, ffn_w1[j].astype(BF16), ffn_w3[j].astype(BF16),
                                ffn_w2[j].astype(BF16), gn, modn)
        else:
            wr = jnp.pad(moe_router[j], ((0, 0), (0, LANES - N_EXPERTS))).astype(BF16)
            xs, h1 = _moe_ffn(h2, x1, mod, wr, moe_w1, moe_w3, moe_w2, j, gn, modn, last,
                              F32 if last else BF16)
            if last:
                out = h1
    return out
```

```python
import functools

import numpy as np
import jax
import jax.numpy as jnp
from jax import lax
from jax.experimental import pallas as pl
from jax.experimental.pallas import tpu as pltpu

F32 = jnp.float32
BF16 = jnp.bfloat16

D = 1024
BATCH = 8
SEQ = 2048
CTX = 256
TT = CTX + SEQ
NTOK = BATCH * TT
DEPTH = 4
GRID_W = 64
GRID_H = SEQ // GRID_W

A_HEADS = 4
A_DQK = 128
A_DV = 256
A_QK = A_HEADS * A_DQK
A_V = A_HEADS * A_DV
ROPE_BASE = 10000.0
LCH = 256
NCH = TT // LCH
SCAN_MB = 4

B_HEADS = 16
B_DH = 64
NA_ROWS = 8
NA_COLS = 16
NA_QROWS = 1
NA_KROWS = NA_QROWS + NA_ROWS - 1
NA_NQ = NA_QROWS * GRID_W
NA_NK = NA_KROWS * GRID_W
NA_STEPS = GRID_H // NA_QROWS

D_FF = 2816
N_EXPERTS = 8
D_FF_EXPERT = 3584
EPS = 1e-6
LOG2E = 1.4426950408889634

RB = 256
NRB = TT // RB
MM_TM = 2048
MOE_TM = 1024
MOE_FC = 512
MOE_SUB = 256
LANES = 128

PB_QK, PB_V, PB_O, PB_NQ, PB_NK, PB_NV, PB_GA, PB_GB = range(8)
P_COLS = 8 * D

VMEM_LIMIT = 56 * 1024 * 1024


def _cp(sem, vmem=VMEM_LIMIT):
    return pltpu.CompilerParams(dimension_semantics=sem, vmem_limit_bytes=vmem)


def _sigmoid(x):
    return 1.0 / (1.0 + jnp.exp(-x))


def _silu(x):
    return x * _sigmoid(x)


def _log_sigmoid(x):
    return jnp.minimum(x, 0.0) - jnp.log(1.0 + jnp.exp(-jnp.abs(x)))


def _norm_mod(x, g, sc, sh):
    ms = jnp.mean(x * x, axis=-1, keepdims=True)
    y = x * lax.rsqrt(ms + EPS)
    return (y * g) * (1.0 + sc) + sh


def _mod_row(b, i):
    return jnp.where(i == 0, BATCH, b)


def _mod_kernel(c_ref, w_ref, b_ref, o_ref):
    c = c_ref[...]
    s = _silu(c).astype(BF16)
    o_ref[0] = jnp.dot(s, w_ref[0].astype(BF16), preferred_element_type=F32) + b_ref[0]


def _modulation(cc, w_mod, b_mod):
    tn = 2048
    nl = w_mod.shape[0]
    return pl.pallas_call(
        _mod_kernel,
        grid=(nl, 6 * D // tn),
        in_specs=[pl.BlockSpec((16, D), lambda l, j: (0, 0)),
                  pl.BlockSpec((1, D, tn), lambda l, j: (l, 0, j)),
                  pl.BlockSpec((1, 1, tn), lambda l, j: (l, 0, j))],
        out_specs=pl.BlockSpec((1, 16, tn), lambda l, j: (l, 0, j)),
        out_shape=jax.ShapeDtypeStruct((nl, 16, 6 * D), F32),
        compiler_params=_cp(("parallel", "parallel")),
        name="modulation",
    )(cc, w_mod, b_mod.reshape(nl, 1, 6 * D))


def _norm_kernel(x_ref, g_ref, mod_ref, o_ref):
    mod = mod_ref[0]
    o_ref[0] = _norm_mod(x_ref[0], g_ref[...], mod[1:2], mod[0:1]).astype(o_ref.dtype)


def _first_norm(x, g, mod):
    return pl.pallas_call(
        _norm_kernel,
        grid=(BATCH, NRB),
        in_specs=[pl.BlockSpec((1, RB, D), lambda b, i: (b, i, 0)),
                  pl.BlockSpec((1, D), lambda b, i: (0, 0)),
                  pl.BlockSpec((1, 6, D), lambda b, i: (_mod_row(b, i), 0, 0))],
        out_specs=pl.BlockSpec((1, RB, D), lambda b, i: (b, i, 0)),
        out_shape=jax.ShapeDtypeStruct((BATCH, TT, D), BF16),
        compiler_params=_cp(("parallel", "parallel")),
        name="first_norm",
    )(x, g, mod)


def _mm_kernel(a_ref, w_ref, o_ref):
    o_ref[...] = jnp.dot(a_ref[...], w_ref[...], preferred_element_type=F32).astype(o_ref.dtype)


def _in_proj(h, w):
    tn = 1024
    return pl.pallas_call(
        _mm_kernel,
        grid=(P_COLS // tn, NTOK // MM_TM),
        in_specs=[pl.BlockSpec((MM_TM, D), lambda j, i: (i, 0)),
                  pl.BlockSpec((D, tn), lambda j, i: (0, j))],
        out_specs=pl.BlockSpec((MM_TM, tn), lambda j, i: (i, j)),
        out_shape=jax.ShapeDtypeStruct((NTOK, P_COLS), BF16),
        compiler_params=_cp(("parallel", "parallel")),
        name="in_proj",
    )(h, w)


NGATE = 4 * A_HEADS


def _gate_kernel(a_ref, w_ref, b_ref, g_ref, lf_ref):
    g = jnp.dot(a_ref[...], w_ref[...], preferred_element_type=F32) + b_ref[...]
    g_ref[...] = g
    lf = _log_sigmoid(g)
    p0 = lf.astype(BF16).astype(F32)
    r1 = lf - p0
    p1 = r1.astype(BF16).astype(F32)
    p2 = (r1 - p1).astype(BF16).astype(F32)
    lane = lax.broadcasted_iota(jnp.int32, (1, LANES), 1)
    parts = jnp.where(lane < NGATE, p0,
                      jnp.where(lane < 2 * NGATE, pltpu.roll(p1, NGATE, 1),
                                jnp.where(lane < 3 * NGATE, pltpu.roll(p2, 2 * NGATE, 1), 0.0)))
    lf_ref[...] = parts.astype(BF16)


def _gate_proj(h, wg, bg):
    return pl.pallas_call(
        _gate_kernel,
        grid=(NTOK // MM_TM,),
        in_specs=[pl.BlockSpec((MM_TM, D), lambda i: (i, 0)),
                  pl.BlockSpec((D, LANES), lambda i: (0, 0)),
                  pl.BlockSpec((1, LANES), lambda i: (0, 0))],
        out_specs=[pl.BlockSpec((MM_TM, LANES), lambda i: (i, 0)),
                   pl.BlockSpec((MM_TM, LANES), lambda i: (i, 0))],
        out_shape=[jax.ShapeDtypeStruct((NTOK, LANES), F32),
                   jax.ShapeDtypeStruct((NTOK, LANES), BF16)],
        compiler_params=_cp(("parallel",)),
        name="gate_proj",
    )(h, wg, bg)


def _prep_kernel(u_ref, up_ref, un_ref, w_ref, c_ref, s_ref, o_ref):
    i = pl.program_id(1)
    u = u_ref[0].astype(F32)
    prev_row = jnp.where(i >= 2, up_ref[0, 15:16, :].astype(F32), 0.0)
    next_row = jnp.where((i >= 1) & (i <= NRB - 2), un_ref[0, 0:1, :].astype(F32), 0.0)
    rid = lax.broadcasted_iota(jnp.int32, (RB, 1), 0)
    u_m1 = jnp.where(rid == 0, prev_row, pltpu.roll(u, 1, 0))
    u_p1 = jnp.where(rid == RB - 1, next_row, pltpu.roll(u, RB - 1, 0))
    w = w_ref[...]
    y = w[0:1] * u_m1 + w[1:2] * u + w[2:3] * u_p1
    y = _silu(y)
    c = c_ref[...]
    s = s_ref[...]
    cfull = jnp.concatenate([c[:, :A_DQK]] * A_HEADS + [c[:, A_DQK:]] * A_HEADS, axis=1)
    sfull = jnp.concatenate([s[:, :A_DQK]] * A_HEADS + [s[:, A_DQK:]] * A_HEADS, axis=1)
    lane = lax.broadcasted_iota(jnp.int32, (1, 2 * A_QK), 1)
    partner = jnp.where((lane & 32) == 0,
                        pltpu.roll(y, 2 * A_QK - 32, 1), pltpu.roll(y, 32, 1))
    o_ref[0] = (y * cfull + partner * sfull).astype(o_ref.dtype)


def _mlstm_prep(p3, conv_w, rope_c, rope_s):
    nb16 = TT // 16
    return pl.pallas_call(
        _prep_kernel,
        grid=(BATCH, NRB),
        in_specs=[pl.BlockSpec((1, RB, D), lambda b, i: (b, i, PB_QK)),
                  pl.BlockSpec((1, 16, D), lambda b, i: (b, jnp.maximum(i * (RB // 16) - 1, 0), PB_QK)),
                  pl.BlockSpec((1, 16, D), lambda b, i: (b, jnp.minimum((i + 1) * (RB // 16), nb16 - 1), PB_QK)),
                  pl.BlockSpec((3, D), lambda b, i: (0, 0)),
                  pl.BlockSpec((RB, 2 * A_DQK), lambda b, i: (i, 0)),
                  pl.BlockSpec((RB, 2 * A_DQK), lambda b, i: (i, 0))],
        out_specs=pl.BlockSpec((1, RB, D), lambda b, i: (b, i, 0)),
        out_shape=jax.ShapeDtypeStruct((BATCH, TT, D), BF16),
        compiler_params=_cp(("parallel", "parallel")),
        name="mlstm_prep",
    )(p3, p3, p3, conv_w, rope_c, rope_s)


def _mlstm_kernel(qkf_ref, vf_ref, gf_ref, lff_ref, qkb_ref, vb_ref, gb_ref, lfb_ref,
                  of_ref, ob_ref, ct_ref, n_ref, m_ref):
    nscan = 2 * SCAN_MB
    per_dir = [(qkf_ref, vf_ref, gf_ref, lff_ref, of_ref), (qkb_ref, vb_ref, gb_ref, lfb_ref, ob_ref)]
    ins = [per_dir[d % 2] + (d // 2,) for d in range(nscan)]

    @pl.when(pl.program_id(1) == 0)
    def _():
        ct_ref[...] = jnp.zeros_like(ct_ref)
        n_ref[...] = jnp.zeros_like(n_ref)
        m_ref[...] = jnp.zeros_like(m_ref)

    r = lax.broadcasted_iota(jnp.int32, (LCH, LCH), 0)
    c = lax.broadcasted_iota(jnp.int32, (LCH, LCH), 1)
    dir_masks = [c <= r, c >= r]
    masks = [dir_masks[d % 2] for d in range(nscan)]
    b_all, g_all, b_t, g_t, b_end = [], [], [], [], []
    for d in range(nscan):
        tri = jnp.where(masks[d], 1.0, 0.0).astype(BF16)
        bc = jnp.dot(tri, ins[d][3][ins[d][5]], preferred_element_type=F32)
        ba = bc + pltpu.roll(bc, LANES - NGATE, 1) + pltpu.roll(bc, LANES - 2 * NGATE, 1)
        ga = ins[d][2][ins[d][5]]
        if d % 2 == 1:
            ba = pltpu.roll(ba, LANES - 2 * A_HEADS, 1)
            ga = pltpu.roll(ga, LANES - 2 * A_HEADS, 1)
        b_all.append(ba)
        g_all.append(ga)
        b_t.append(ba.T)
        g_t.append(ga.T)
        b_end.append(ba[LCH - 1:LCH, :] if d % 2 == 0 else ba[0:1, :])
    ones = jnp.ones((LCH, LANES), BF16)
    tn_dims = (((0,), (0,)), ((), ()))
    combos = [(d, h) for d in range(nscan) for h in range(A_HEADS)]
    idx = range(len(combos))
    qs = [ins[d][0][ins[d][5], :, h * A_DQK:(h + 1) * A_DQK] for d, h in combos]
    ks = [ins[d][0][ins[d][5], :, A_QK + h * A_DQK:A_QK + (h + 1) * A_DQK] for d, h in combos]
    qk = [lax.dot_general(qs[i], ks[i], NT_DIMS, preferred_element_type=F32) for i in idx]
    b_col = [b_all[d][:, A_HEADS + h:A_HEADS + h + 1] for d, h in combos]
    b_last = [b_end[d][:, A_HEADS + h:A_HEADS + h + 1] for d, h in combos]
    m_old = [m_ref[i][:, 0:1] for i in idx]
    r_row = [b_t[d][A_HEADS + h:A_HEADS + h + 1, :] - g_t[d][h:h + 1, :] for d, h in combos]
    m_row = [jnp.maximum(b_col[i] + m_old[i],
                         jnp.max(jnp.where(masks[d], b_col[i] - r_row[i], -jnp.inf), axis=1, keepdims=True))
             for i, (d, h) in enumerate(combos)]
    s = [(qk[i] * jnp.exp(jnp.where(masks[d], (b_col[i] - m_row[i]) - r_row[i], -jnp.inf))).astype(BF16)
         for i, (d, h) in enumerate(combos)]
    wq = [(jnp.exp(b_col[i] + m_old[i] - m_row[i]) * qs[i].astype(F32)).astype(BF16) for i in idx]
    g_col = [b_last[i] - b_col[i] + g_all[d][:, h:h + 1] for i, (d, h) in enumerate(combos)]
    m_new = [jnp.maximum(b_last[i] + m_old[i], jnp.max(g_col[i], axis=0, keepdims=True)) for i in idx]
    kw = [(jnp.exp(g_col[i] - m_new[i]) * ks[i].astype(F32)).astype(BF16) for i in idx]
    decay = [jnp.exp(b_last[i] + m_old[i] - m_new[i]) for i in idx]
    for i, (d, h) in enumerate(combos):
        v = ins[d][1][ins[d][5], :, h * A_DV:(h + 1) * A_DV]
        lhs = jnp.concatenate([s[i], wq[i]], axis=1)
        den = jnp.dot(lhs, jnp.concatenate([ones, n_ref[i].astype(BF16)], axis=0),
                      preferred_element_type=F32)[:, 0:1]
        rinv = 1.0 / jnp.maximum(jnp.abs(den), jnp.exp(-m_row[i]))
        num = jnp.dot(lhs, jnp.concatenate([v, ct_ref[i].astype(BF16)], axis=0), preferred_element_type=F32)
        ins[d][4][ins[d][5], :, h * A_DV:(h + 1) * A_DV] = (num * rinv).astype(ins[d][4].dtype)
    for i, (d, h) in enumerate(combos):
        v = ins[d][1][ins[d][5], :, h * A_DV:(h + 1) * A_DV]
        ct_ref[i] = decay[i] * ct_ref[i] + lax.dot_general(kw[i], v, tn_dims, preferred_element_type=F32)
        n_ref[i] = decay[i] * n_ref[i] + lax.dot_general(kw[i], ones, tn_dims, preferred_element_type=F32)
        m_ref[i] = jnp.broadcast_to(m_new[i], (1, LANES))


def _rev_chunk(s):
    ncc = CTX // LCH
    return jnp.where(s < ncc, ncc - 1 - s, NCH + ncc - 1 - s)


def _mlstm_scan(qk, p3, gates, lfp):
    fwd = lambda b, s: (b, s, 0)
    bwd = lambda b, s: (b, _rev_chunk(s), 0)
    specs = lambda im, imv: [pl.BlockSpec((SCAN_MB, LCH, D), im), pl.BlockSpec((SCAN_MB, LCH, A_V), imv),
                             pl.BlockSpec((SCAN_MB, LCH, LANES), im), pl.BlockSpec((SCAN_MB, LCH, LANES), im)]
    nstate = 2 * SCAN_MB * A_HEADS
    return pl.pallas_call(
        _mlstm_kernel,
        grid=(BATCH // SCAN_MB, NCH),
        in_specs=(specs(fwd, lambda b, s: (b, s, PB_V))
                  + specs(bwd, lambda b, s: (b, _rev_chunk(s), PB_V))),
        out_specs=[pl.BlockSpec((SCAN_MB, LCH, A_V), fwd), pl.BlockSpec((SCAN_MB, LCH, A_V), bwd)],
        out_shape=[jax.ShapeDtypeStruct((BATCH, TT, A_V), BF16)] * 2,
        scratch_shapes=[pltpu.VMEM((nstate, A_DQK, A_DV), F32),
                        pltpu.VMEM((nstate, A_DQK, LANES), F32),
                        pltpu.VMEM((nstate, 1, LANES), F32)],
        compiler_params=_cp(("parallel", "arbitrary")),
        name="mlstm_scan",
    )(qk, p3, gates, lfp, qk, p3, gates, lfp)


NPAIR = B_HEADS // 2
NT_DIMS = (((1,), (1,)), ((), ()))


def _stack_pair(q, low):
    zero = jnp.zeros_like(q)
    return jnp.concatenate([jnp.where(low, q, zero), jnp.where(low, zero, q)], axis=0)


def _softmax_rows(s_ref, p_ref):
    s = s_ref[...]
    p = jnp.exp2(s - jnp.max(s, axis=2, keepdims=True))
    p_ref[...] = p.astype(p_ref.dtype)
    return 1.0 / jnp.sum(p, axis=2, keepdims=True)


def _na_kernel(q_ref, k_ref, v_ref, *rest):
    bias_refs, (o_ref, s_ref, p_ref) = rest[:NA_SUB], rest[NA_SUB:]
    low = lax.broadcasted_iota(jnp.int32, (1, 2 * B_DH), 1) < B_DH
    starts = []
    for sub in range(NA_SUB):
        row = pl.program_id(1) * NA_SUB + sub
        rs = jnp.clip(row - NA_ROWS // 2, 0, GRID_H - NA_ROWS)
        starts.append(pl.multiple_of(CTX + rs * GRID_W, GRID_W))
    for sub in range(NA_SUB):
        qrows = pl.ds(sub * NA_NQ, NA_NQ)
        for hp in range(NPAIR):
            lo, hi = hp * 2 * B_DH, (hp + 1) * 2 * B_DH
            q2 = _stack_pair(q_ref[0, qrows, lo:hi], low)
            idx = sub * NPAIR + hp
            s_ref[idx, :, 0:NA_NK] = lax.dot_general(q2, k_ref[0, pl.ds(starts[sub], NA_NK), lo:hi], NT_DIMS,
                                                     preferred_element_type=F32
                                                     ) + bias_refs[sub][0, 0, hp].astype(F32)
            s_ref[idx, :, NA_NK:] = lax.dot_general(q2, k_ref[0, 0:CTX, lo:hi], NT_DIMS,
                                                    preferred_element_type=F32)
    rinv = _softmax_rows(s_ref, p_ref)
    for sub in range(NA_SUB):
        qrows = pl.ds(sub * NA_NQ, NA_NQ)
        for hp in range(NPAIR):
            lo, hi = hp * 2 * B_DH, (hp + 1) * 2 * B_DH
            idx = sub * NPAIR + hp
            o2 = (jnp.dot(p_ref[idx, :, 0:NA_NK], v_ref[0, pl.ds(starts[sub], NA_NK), lo:hi],
                          preferred_element_type=F32)
                  + jnp.dot(p_ref[idx, :, NA_NK:], v_ref[0, 0:CTX, lo:hi], preferred_element_type=F32))
            o2 = o2 * rinv[idx]
            o_ref[0, qrows, lo:hi] = jnp.where(low, o2[0:NA_NQ], o2[NA_NQ:]).astype(o_ref.dtype)


def _na_pattern(row):
    edge = NA_ROWS // 2
    return jnp.where(row < edge, row, jnp.where(row <= GRID_H - edge, edge, row - (GRID_H - 2 * edge)))


NA_NPAT = NA_ROWS
NA_SUB = 4


def _na_attention(p3, bias, layer):
    nq = NA_SUB * NA_NQ
    qoff = CTX // nq
    bias_specs = [pl.BlockSpec((1, 1, NPAIR, 2 * NA_NQ, NA_NK),
                               lambda b, i, sub=sub: (layer, _na_pattern(i * NA_SUB + sub), 0, 0, 0))
                  for sub in range(NA_SUB)]
    return pl.pallas_call(
        _na_kernel,
        grid=(BATCH, GRID_H // NA_SUB),
        in_specs=[pl.BlockSpec((1, nq, D), lambda b, i: (b, i + qoff, PB_NQ)),
                  pl.BlockSpec((1, TT, D), lambda b, i: (b, 0, PB_NK)),
                  pl.BlockSpec((1, TT, D), lambda b, i: (b, 0, PB_NV))] + bias_specs,
        out_specs=pl.BlockSpec((1, nq, D), lambda b, i: (b, i, 0)),
        out_shape=jax.ShapeDtypeStruct((BATCH, SEQ, D), BF16),
        scratch_shapes=[pltpu.VMEM((NA_SUB * NPAIR, 2 * NA_NQ, NA_NK + CTX), F32),
                        pltpu.VMEM((NA_SUB * NPAIR, 2 * NA_NQ, NA_NK + CTX), BF16)],
        compiler_params=_cp(("parallel", "arbitrary")),
        name="na_attention",
    )(p3, p3, p3, *([bias] * NA_SUB))


def _ctx_attn_kernel(q_ref, k_ref, v_ref, o_ref, s_ref, p_ref):
    low = lax.broadcasted_iota(jnp.int32, (1, 2 * B_DH), 1) < B_DH
    for hp in range(NPAIR):
        lo, hi = hp * 2 * B_DH, (hp + 1) * 2 * B_DH
        s_ref[hp] = lax.dot_general(_stack_pair(q_ref[0, :, lo:hi], low), k_ref[0, :, lo:hi], NT_DIMS,
                                    preferred_element_type=F32)
    rinv = _softmax_rows(s_ref, p_ref)
    for hp in range(NPAIR):
        lo, hi = hp * 2 * B_DH, (hp + 1) * 2 * B_DH
        o2 = jnp.dot(p_ref[hp], v_ref[0, :, lo:hi], preferred_element_type=F32) * rinv[hp]
        o_ref[0, :, lo:hi] = jnp.where(low, o2[0:CTX], o2[CTX:]).astype(o_ref.dtype)


def _ctx_attention(p3):
    return pl.pallas_call(
        _ctx_attn_kernel,
        grid=(BATCH,),
        in_specs=[pl.BlockSpec((1, CTX, D), lambda b: (b, 0, PB_NQ)),
                  pl.BlockSpec((1, CTX, D), lambda b: (b, 0, PB_NK)),
                  pl.BlockSpec((1, CTX, D), lambda b: (b, 0, PB_NV))],
        out_specs=pl.BlockSpec((1, CTX, D), lambda b: (b, 0, 0)),
        out_shape=jax.ShapeDtypeStruct((BATCH, CTX, D), BF16),
        scratch_shapes=[pltpu.VMEM((NPAIR, 2 * CTX, CTX), F32),
                        pltpu.VMEM((NPAIR, 2 * CTX, CTX), BF16)],
        compiler_params=_cp(("parallel",)),
        name="ctx_attention",
    )(p3, p3, p3)


def _merge_kernel(hf_ref, hbw_ref, o_ref, ga_ref, gb_ref, hnl_ref, hnc_ref, x_ref, mod_ref, ghn_ref,
                  wa_ref, wb_ref, wo_ref, g2_ref, x1_ref, h2_ref):
    hn = jnp.where(pl.program_id(1) == 0, hnc_ref[0], hnl_ref[0])
    bm = jnp.dot(hn, wb_ref[...], preferred_element_type=F32)
    hs = hf_ref[0].astype(F32) + hbw_ref[0].astype(F32)
    parts = []
    for h in range(A_HEADS):
        seg = hs[:, h * A_DV:(h + 1) * A_DV]
        mu = jnp.mean(seg, axis=-1, keepdims=True)
        cen = seg - mu
        var = jnp.mean(cen * cen, axis=-1, keepdims=True)
        parts.append(cen * lax.rsqrt(var + EPS))
    ya = jnp.concatenate(parts, axis=1) * ghn_ref[...] * _sigmoid(o_ref[0].astype(F32))
    a = jnp.dot(ya.astype(BF16), wa_ref[...], preferred_element_type=F32)
    mrg = _sigmoid(ga_ref[0].astype(F32)) * a + _sigmoid(gb_ref[0].astype(F32)) * bm
    y = jnp.dot(mrg.astype(BF16), wo_ref[...], preferred_element_type=F32)
    mod = mod_ref[0]
    x1 = x_ref[0] + mod[2:3] * y
    x1_ref[0] = x1
    h2_ref[0] = _norm_mod(x1, g2_ref[...], mod[4:5], mod[3:4]).astype(h2_ref.dtype)


def _merge(hf, hbw, p3, hb_lat, hb_ctx, x, mod, ghn, wa, wb, wo, g2, h2_dtype):
    row = lambda b, i: (b, i, 0)
    full = lambda b, i: (0, 0)
    return pl.pallas_call(
        _merge_kernel,
        grid=(BATCH, NRB),
        in_specs=[pl.BlockSpec((1, RB, A_V), row),
                  pl.BlockSpec((1, RB, A_V), row),
                  pl.BlockSpec((1, RB, D), lambda b, i: (b, i, PB_O)),
                  pl.BlockSpec((1, RB, D), lambda b, i: (b, i, PB_GA)),
                  pl.BlockSpec((1, RB, D), lambda b, i: (b, i, PB_GB)),
                  pl.BlockSpec((1, RB, D), lambda b, i: (b, jnp.maximum(i - 1, 0), 0)),
                  pl.BlockSpec((1, CTX, D), lambda b, i: (b, 0, 0)),
                  pl.BlockSpec((1, RB, D), row),
                  pl.BlockSpec((1, 6, D), lambda b, i: (_mod_row(b, i), 0, 0)),
                  pl.BlockSpec((1, A_V), full),
                  pl.BlockSpec((A_V, D), full),
                  pl.BlockSpec((D, D), full),
                  pl.BlockSpec((D, D), full),
                  pl.BlockSpec((1, D), full)],
        out_specs=[pl.BlockSpec((1, RB, D), row), pl.BlockSpec((1, RB, D), row)],
        out_shape=[jax.ShapeDtypeStruct((BATCH, TT, D), F32),
                   jax.ShapeDtypeStruct((BATCH, TT, D), h2_dtype)],
        compiler_params=_cp(("parallel", "parallel")),
        name="merge",
    )(hf, hbw, p3, p3, p3, hb_lat, hb_ctx, x, mod, ghn, wa, wb, wo, g2)


FFN_CH = D_FF // 2


def _ffn_kernel(h_ref, x_ref, mod_ref, w1_ref, w3_ref, w2_ref, gn_ref, modn_ref, x2_ref, hn_ref):
    h = h_ref[0]
    y = None
    for cidx in range(D_FF // FFN_CH):
        lo, hi = cidx * FFN_CH, (cidx + 1) * FFN_CH
        a = jnp.dot(h, w1_ref[:, lo:hi], preferred_element_type=F32)
        g = jnp.dot(h, w3_ref[:, lo:hi], preferred_element_type=F32)
        part = jnp.dot((_silu(a) * g).astype(BF16), w2_ref[lo:hi, :], preferred_element_type=F32)
        y = part if y is None else y + part
    mod = mod_ref[0]
    x2 = x_ref[0] + mod[5:6] * y
    x2_ref[0] = x2
    modn = modn_ref[0]
    hn_ref[0] = _norm_mod(x2, gn_ref[...], modn[1:2], modn[0:1]).astype(hn_ref.dtype)


def _dense_ffn(h2, x1, mod, w1, w3, w2, gn, modn):
    row = lambda b, i: (b, i, 0)
    full = lambda b, i: (0, 0)
    modspec = pl.BlockSpec((1, 6, D), lambda b, i: (_mod_row(b, i), 0, 0))
    return pl.pallas_call(
        _ffn_kernel,
        grid=(BATCH, NRB),
        in_specs=[pl.BlockSpec((1, RB, D), row), pl.BlockSpec((1, RB, D), row), modspec,
                  pl.BlockSpec((D, D_FF), full), pl.BlockSpec((D, D_FF), full),
                  pl.BlockSpec((D_FF, D), full), pl.BlockSpec((1, D), full), modspec],
        out_specs=[pl.BlockSpec((1, RB, D), row), pl.BlockSpec((1, RB, D), row)],
        out_shape=[jax.ShapeDtypeStruct((BATCH, TT, D), F32),
                   jax.ShapeDtypeStruct((BATCH, TT, D), BF16)],
        compiler_params=_cp(("parallel", "parallel")),
        name="dense_ffn",
    )(h2, x1, mod, w1, w3, w2, gn, modn)


def _router_kernel(h_ref, wr_ref, route_ref, cnt_ref, run_ref):
    i = pl.program_id(0)

    @pl.when(i == 0)
    def _():
        run_ref[...] = jnp.zeros_like(run_ref)

    logits = jnp.dot(h_ref[...].astype(BF16), wr_ref[...], preferred_element_type=F32)
    lane = lax.broadcasted_iota(jnp.int32, (RB, LANES), 1).astype(F32)
    lg = jnp.where(lane < N_EXPERTS, logits, -jnp.inf)
    v1 = jnp.max(lg, axis=1, keepdims=True)
    i1 = jnp.min(jnp.where(lg == v1, lane, float(LANES)), axis=1, keepdims=True)
    lg2 = jnp.where(lane == i1, -jnp.inf, lg)
    v2 = jnp.max(lg2, axis=1, keepdims=True)
    i2 = jnp.min(jnp.where(lg2 == v2, lane, float(LANES)), axis=1, keepdims=True)
    e = jnp.exp(v2 - v1)
    w1 = 1.0 / (1.0 + e)
    w2 = e / (1.0 + e)
    oh1 = (lane == i1).astype(F32)
    oh2 = (lane == i2).astype(F32)
    r = lax.broadcasted_iota(jnp.int32, (RB, RB), 0)
    c = lax.broadcasted_iota(jnp.int32, (RB, RB), 1)
    tri = (r > c).astype(BF16)
    cs1 = jnp.dot(tri, oh1.astype(BF16), preferred_element_type=F32)
    cs2 = jnp.dot(tri, oh2.astype(BF16), preferred_element_type=F32)
    tot1 = jnp.sum(oh1, axis=0, keepdims=True)
    tot2 = jnp.sum(oh2, axis=0, keepdims=True)
    run = run_ref[...]
    rank1 = jnp.sum(oh1 * (run + cs1), axis=1, keepdims=True)
    rank2 = jnp.sum(oh2 * (run + tot1 + cs2), axis=1, keepdims=True)
    new_run = run + tot1 + tot2
    run_ref[...] = new_run
    cnt_ref[...] = new_run
    out = jnp.where(lane == 0, i1,
          jnp.where(lane == 1, i2,
          jnp.where(lane == 2, w1,
          jnp.where(lane == 3, w2,
          jnp.where(lane == 4, rank1,
          jnp.where(lane == 5, rank2, 0.0))))))
    route_ref[...] = out


def _router(h2_flat, wr, nblk, blk_of):
    return pl.pallas_call(
        _router_kernel,
        grid=(nblk,),
        in_specs=[pl.BlockSpec((RB, D), lambda i: (blk_of(i), 0)),
                  pl.BlockSpec((D, LANES), lambda i: (0, 0))],
        out_specs=[pl.BlockSpec((RB, LANES), lambda i: (i, 0)),
                   pl.BlockSpec((1, LANES), lambda i: (0, 0))],
        out_shape=[jax.ShapeDtypeStruct((nblk * RB, LANES), F32),
                   jax.ShapeDtypeStruct((1, LANES), F32)],
        scratch_shapes=[pltpu.VMEM((1, LANES), F32)],
        compiler_params=_cp(("arbitrary",)),
        name="moe_router",
    )(h2_flat, wr)


DMA_UNROLL = 8
N_ZERO = 2 * N_EXPERTS


def _dispatch_kernel(pos_ref, zstart_ref, h_ref, o_ref, zbuf, stage, sems, zsem):
    @pl.when(pl.program_id(0) == 0)
    def _():
        zbuf[...] = jnp.zeros_like(zbuf)
        for z in range(N_ZERO):
            @pl.when(zstart_ref[z] >= 0)
            def _():
                zs = pl.multiple_of(zstart_ref[z], MOE_TM)
                pltpu.make_async_copy(zbuf, o_ref.at[pl.ds(zs, MOE_TM)], zsem).start()
        for z in range(N_ZERO):
            @pl.when(zstart_ref[z] >= 0)
            def _():
                pltpu.make_async_copy(zbuf, o_ref.at[pl.ds(0, MOE_TM)], zsem).wait()

    step = pl.program_id(0)
    slot = step % 2
    stage[slot] = h_ref[...]

    def issue(t, carry):
        src = stage.at[slot, pl.ds(t, 1)]
        pltpu.make_async_copy(src, o_ref.at[pl.ds(pos_ref[0, 0, t], 1)], sems.at[slot]).start(priority=0)
        pltpu.make_async_copy(src, o_ref.at[pl.ds(pos_ref[0, 0, RB + t], 1)], sems.at[slot]).start(priority=1)
        return carry

    lax.fori_loop(0, RB, issue, 0, unroll=DMA_UNROLL)

    def drain(sl):
        for _ in range(2):
            pltpu.make_async_copy(stage.at[sl], o_ref.at[pl.ds(0, RB)], sems.at[sl]).wait()

    @pl.when(step > 0)
    def _():
        drain(1 - slot)

    @pl.when(step == pl.num_programs(0) - 1)
    def _():
        drain(slot)


def _dispatch(pos, zstart, h2_flat, nblk, blk_of, n_sorted):
    return pl.pallas_call(
        _dispatch_kernel,
        grid=(nblk,),
        in_specs=[pl.BlockSpec((1, 1, 2 * RB), lambda i: (i, 0, 0), memory_space=pltpu.SMEM),
                  pl.BlockSpec(memory_space=pltpu.SMEM),
                  pl.BlockSpec((RB, D), lambda i: (blk_of(i), 0))],
        out_specs=pl.BlockSpec(memory_space=pl.ANY),
        out_shape=jax.ShapeDtypeStruct((n_sorted, D), F32),
        scratch_shapes=[pltpu.VMEM((MOE_TM, D), F32), pltpu.VMEM((2, RB, D), F32),
                        pltpu.SemaphoreType.DMA((2,)), pltpu.SemaphoreType.DMA(())],
        compiler_params=_cp(("arbitrary",)),
        name="moe_dispatch",
    )(pos, zstart, h2_flat)


def _gmm_kernel(be_ref, nb_ref, a_ref, w1_ref, w3_ref, w2_ref, o_ref, abf_ref):
    del be_ref
    i = pl.program_id(0)
    f = pl.program_id(1)

    @pl.when((i < nb_ref[0]) & (f == 0))
    def _():
        abf_ref[...] = a_ref[...].astype(BF16)

    @pl.when(i < nb_ref[0])
    def _():
        a = abf_ref[...]
        gs = []
        for c in range(MOE_FC // MOE_SUB):
            lo, hi = c * MOE_SUB, (c + 1) * MOE_SUB
            h1 = jnp.dot(a, w1_ref[0, :, lo:hi].astype(BF16), preferred_element_type=F32)
            h3 = jnp.dot(a, w3_ref[0, :, lo:hi].astype(BF16), preferred_element_type=F32)
            gs.append((_silu(h1) * h3).astype(BF16))
        part = jnp.dot(jnp.concatenate(gs, axis=1), w2_ref[0].astype(BF16), preferred_element_type=F32)

        @pl.when(f == 0)
        def _():
            o_ref[...] = part

        @pl.when(f > 0)
        def _():
            o_ref[...] += part

    @pl.when((i >= nb_ref[0]) & (f == 0))
    def _():
        o_ref[...] = jnp.zeros_like(o_ref)


def _expert_ffn(blk_e, nb, hs, w1, w3, w2, j):
    nbmax = hs.shape[0] // MOE_TM
    nf = D_FF_EXPERT // MOE_FC

    def ieff(i, nb_ref):
        return jnp.minimum(i, nb_ref[0] - 1)

    def feff(i, f, nb_ref):
        return jnp.where(i < nb_ref[0], f, nf - 1)

    grid_spec = pltpu.PrefetchScalarGridSpec(
        num_scalar_prefetch=2,
        grid=(nbmax, nf),
        in_specs=[pl.BlockSpec((MOE_TM, D), lambda i, f, be, nbr: (ieff(i, nbr), 0)),
                  pl.BlockSpec((None, 1, D, MOE_FC),
                               lambda i, f, be, nbr: (j, be[ieff(i, nbr)], 0, feff(i, f, nbr))),
                  pl.BlockSpec((None, 1, D, MOE_FC),
                               lambda i, f, be, nbr: (j, be[ieff(i, nbr)], 0, feff(i, f, nbr))),
                  pl.BlockSpec((None, 1, MOE_FC, D),
                               lambda i, f, be, nbr: (j, be[ieff(i, nbr)], feff(i, f, nbr), 0))],
        out_specs=pl.BlockSpec((MOE_TM, D), lambda i, f, be, nbr: (i, 0)),
        scratch_shapes=[pltpu.VMEM((MOE_TM, D), BF16)],
    )
    return pl.pallas_call(
        _gmm_kernel,
        grid_spec=grid_spec,
        out_shape=jax.ShapeDtypeStruct(hs.shape, F32),
        compiler_params=_cp(("arbitrary", "arbitrary")),
        name="moe_expert_ffn",
    )(blk_e, nb, hs, w1, w3, w2)


def _combine_kernel(pos_ref, posn_ref, y_ref, route_ref, x_ref, mod_ref, gn_ref, modn_ref,
                    x2_ref, hn_ref, ybuf, sems):
    t = pl.program_id(0) * pl.num_programs(1) + pl.program_id(1)
    nsteps = pl.num_programs(0) * pl.num_programs(1)
    slot = t % 2

    def start_block(p_ref, sl):
        def issue(r, carry):
            pltpu.make_async_copy(y_ref.at[pl.ds(p_ref[0, 0, r], 1)],
                                  ybuf.at[sl, 0, pl.ds(r, 1)], sems.at[sl]).start(priority=0)
            pltpu.make_async_copy(y_ref.at[pl.ds(p_ref[0, 0, RB + r], 1)],
                                  ybuf.at[sl, 1, pl.ds(r, 1)], sems.at[sl]).start(priority=1)
            return carry
        lax.fori_loop(0, RB, issue, 0, unroll=DMA_UNROLL)

    @pl.when(t == 0)
    def _():
        start_block(pos_ref, 0)

    @pl.when(t + 1 < nsteps)
    def _():
        start_block(posn_ref, 1 - slot)

    for e in range(2):
        pltpu.make_async_copy(y_ref.at[pl.ds(0, RB)], ybuf.at[slot, e], sems.at[slot]).wait()
    route = route_ref[...]
    y = route[:, 2:3] * ybuf[slot, 0] + route[:, 3:4] * ybuf[slot, 1]
    mod = mod_ref[0]
    x2 = x_ref[0] + mod[5:6] * y
    x2_ref[0] = x2
    modn = modn_ref[0]
    hn_ref[0] = _norm_mod(x2, gn_ref[...], modn[1:2], modn[0:1]).astype(hn_ref.dtype)


def _combine(pos, ys, route, x1, mod, gn, modn, latent_only, hn_dtype):
    nrb = NRB - 1 if latent_only else NRB
    off = 1 if latent_only else 0
    rows = nrb * RB
    full = lambda b, i: (0, 0)
    modspec = pl.BlockSpec((1, 6, D), lambda b, i: (_mod_row(b, i + off), 0, 0))
    last = BATCH * nrb - 1
    return pl.pallas_call(
        _combine_kernel,
        grid=(BATCH, nrb),
        in_specs=[pl.BlockSpec((1, 1, 2 * RB), lambda b, i: (b * nrb + i, 0, 0), memory_space=pltpu.SMEM),
                  pl.BlockSpec((1, 1, 2 * RB), lambda b, i: (jnp.minimum(b * nrb + i + 1, last), 0, 0),
                               memory_space=pltpu.SMEM),
                  pl.BlockSpec(memory_space=pl.ANY),
                  pl.BlockSpec((RB, LANES), lambda b, i: (b * nrb + i, 0)),
                  pl.BlockSpec((1, RB, D), lambda b, i: (b, i + off, 0)),
                  modspec,
                  pl.BlockSpec((1, D), full),
                  modspec],
        out_specs=[pl.BlockSpec((1, RB, D), lambda b, i: (b, i, 0)),
                   pl.BlockSpec((1, RB, D), lambda b, i: (b, i, 0))],
        out_shape=[jax.ShapeDtypeStruct((BATCH, rows, D), F32),
                   jax.ShapeDtypeStruct((BATCH, rows, D), hn_dtype)],
        scratch_shapes=[pltpu.VMEM((2, 2, RB, D), F32), pltpu.SemaphoreType.DMA((2,))],
        compiler_params=_cp(("arbitrary", "arbitrary")),
        name="moe_combine",
    )(pos, pos, ys, route, x1, mod, gn, modn)


def _moe_ffn(h2, x1, mod, wr, w1, w3, w2, j, gn, modn, latent_only, hn_dtype):
    h2_flat = h2.reshape(NTOK, D)
    if latent_only:
        nrb = NRB - 1
        blk_of = lambda i: (i // nrb) * NRB + (i % nrb) + 1
    else:
        nrb = NRB
        blk_of = lambda i: i
    nblk = BATCH * nrb
    n_pairs = 2 * nblk * RB
    nbmax = n_pairs // MOE_TM + N_EXPERTS
    n_sorted = nbmax * MOE_TM

    route, counts = _router(h2_flat, wr, nblk, blk_of)
    cnt = counts[0, :N_EXPERTS].astype(jnp.int32)
    gsz = ((cnt + MOE_TM - 1) // MOE_TM) * MOE_TM
    gend = jnp.cumsum(gsz)
    goff = gend - gsz
    e12 = route[:, 0:2].astype(jnp.int32)
    rank = route[:, 4:6].astype(jnp.int32)
    pos = goff[e12] + rank
    pos = pos.reshape(nblk, RB, 2).transpose(0, 2, 1).reshape(nblk, 1, 2 * RB)
    nb = (gend[-1] // MOE_TM).reshape(1)
    blk_start = jnp.arange(nbmax, dtype=jnp.int32) * MOE_TM
    blk_e = jnp.minimum(jnp.sum(blk_start[:, None] >= gend[None, :], axis=1), N_EXPERTS - 1).astype(jnp.int32)

    trail = (nbmax - N_EXPERTS + jnp.arange(N_EXPERTS, dtype=jnp.int32)) * MOE_TM
    zstart = jnp.concatenate([jnp.where(cnt > 0, gend - MOE_TM, -1),
                              jnp.where(trail >= gend[-1], trail, -1)]).astype(jnp.int32)

    hs = _dispatch(pos, zstart, h2_flat, nblk, blk_of, n_sorted)
    ys = _expert_ffn(blk_e, nb, hs, w1, w3, w2, j)
    return _combine(pos, ys, route, x1, mod, gn, modn, latent_only, hn_dtype)


def _rope_tables():
    quarter = A_DQK // 4
    inv = 1.0 / (ROPE_BASE ** (jnp.arange(quarter, dtype=F32) / quarter))
    pos = jnp.arange(SEQ)
    rows = (pos // GRID_W).astype(F32)
    cols = (pos % GRID_W).astype(F32)
    ang_r = rows[:, None] * inv[None, :]
    ang_c = cols[:, None] * inv[None, :]
    cos = jnp.concatenate([jnp.cos(ang_r)] * 2 + [jnp.cos(ang_c)] * 2, axis=1)
    sin = jnp.concatenate([-jnp.sin(ang_r), jnp.sin(ang_r), -jnp.sin(ang_c), jnp.sin(ang_c)], axis=1)
    cos = jnp.concatenate([jnp.ones((CTX, A_DQK), F32), cos], axis=0)
    sin = jnp.concatenate([jnp.zeros((CTX, A_DQK), F32), sin], axis=0)
    kscale = A_DQK ** -0.5
    return (jnp.concatenate([cos, cos * kscale], axis=1),
            jnp.concatenate([sin, sin * kscale], axis=1))


def _na_bias_tables(rpb):
    rs = np.clip(np.arange(GRID_H) - NA_ROWS // 2, 0, GRID_H - NA_ROWS)
    cs = np.clip(np.arange(GRID_W) - NA_COLS // 2, 0, GRID_W - NA_COLS)
    rows = [0, 1, 2, 3, NA_ROWS // 2, GRID_H - 3, GRID_H - 2, GRID_H - 1]
    col = np.arange(GRID_W)
    dc = np.clip(col[None, :] - col[:, None] + NA_COLS - 1, 0, 2 * NA_COLS - 2)
    valid_c = (col[None, :] >= cs[:, None]) & (col[None, :] < cs[:, None] + NA_COLS)
    sel_c = np.eye(2 * NA_COLS - 1, dtype=np.float32)[dc]
    t = jnp.einsum('lhab,uvb->lhuav', rpb * LOG2E, jnp.asarray(sel_c),
                   precision=lax.Precision.HIGHEST)
    neg = np.where(valid_c, 0.0, -np.inf).astype(np.float32)[:, None, :]
    nl = rpb.shape[0]
    t = (t + jnp.asarray(neg)).reshape(nl, NPAIR, 2 * NA_NQ, (2 * NA_ROWS - 1) * GRID_W)
    first = [rs[r] - r + NA_ROWS - 1 for r in rows]
    return jnp.stack([t[..., d0 * GRID_W:d0 * GRID_W + NA_NK] for d0 in first], axis=1).astype(BF16)


def kernel(x, c, ctx, c_ctx, w_mod, b_mod, g_norm1, g_norm2, w_in, a_conv, a_gate_b, a_hnorm_g, na_rpb,
           w_br_a, w_br_b, w_out, ffn_w1, ffn_w3, ffn_w2, moe_router, moe_w1, moe_w3, moe_w2, g_final):
    cc = jnp.concatenate([c, c_ctx[None, :], jnp.zeros((16 - BATCH - 1, D), F32)], axis=0)
    mod_all = _modulation(cc, w_mod, b_mod).reshape(DEPTH, 16, 6, D)[:, :BATCH + 1]
    mod_zero = jnp.zeros((BATCH + 1, 6, D), F32)
    rope_c, rope_s = _rope_tables()
    na_bias = _na_bias_tables(na_rpb)

    xs = jnp.concatenate([ctx, x], axis=1)
    h1 = _first_norm(xs, g_norm1[0][None, :], mod_all[0])
    out = None
    for l in range(DEPTH):
        last = l == DEPTH - 1
        mod = mod_all[l]
        wl = w_in[l]
        g0 = 3 * D
        g1 = g0 + NGATE
        wp = jnp.concatenate([wl[:, :g0], wl[:, g1:g1 + D] * (B_DH ** -0.5 * LOG2E), wl[:, g1 + D:]],
                             axis=1).astype(BF16)
        wg = jnp.pad(wl[:, g0:g1], ((0, 0), (0, LANES - NGATE))).astype(BF16)
        bg = jnp.pad(a_gate_b[l], (0, LANES - NGATE))[None, :]

        h1_flat = h1.reshape(NTOK, D)
        p3 = _in_proj(h1_flat, wp).reshape(BATCH, TT, P_COLS)
        gates, lfp = _gate_proj(h1_flat, wg, bg)

        qk = _mlstm_prep(p3, a_conv[l], rope_c, rope_s)
        hf, hbw = _mlstm_scan(qk, p3, gates.reshape(BATCH, TT, LANES), lfp.reshape(BATCH, TT, LANES))
        hb_lat = _na_attention(p3, na_bias, l)
        hb_ctx = _ctx_attention(p3)

        moe = l % 2 == 1
        x1, h2 = _merge(hf, hbw, p3, hb_lat, hb_ctx, xs, mod, a_hnorm_g[l][None, :],
                        w_br_a[l].astype(BF16), w_br_b[l].astype(BF16), w_out[l].astype(BF16),
                        g_norm2[l][None, :], F32 if moe else BF16)
        if last:
            gn, modn = g_final[None, :], mod_zero
        else:
            gn, modn = g_norm1[l + 1][None, :], mod_all[l + 1]
        j = l // 2
        if not moe:
            xs, h1 = _dense_ffn(h2, x1, mod, ffn_w1[j].astype(BF16), ffn_w3[j].astype(BF16),
                                ffn_w2[j].astype(BF16), gn, modn)
        else:
            wr = jnp.pad(moe_router[j], ((0, 0), (0, LANES - N_EXPERTS))).astype(BF16)
            xs, h1 = _moe_ffn(h2, x1, mod, wr, moe_w1, moe_w3, moe_w2, j, gn, modn, last,
                              F32 if last else BF16)
            if last:
                out = h1
    return out
```

```python
import functools

import numpy as np
import jax
import jax.numpy as jnp
from jax import lax
from jax.experimental import pallas as pl
from jax.experimental.pallas import tpu as pltpu

F32 = jnp.float32
BF16 = jnp.bfloat16

D = 1024
BATCH = 8
SEQ = 2048
CTX = 256
TT = CTX + SEQ
NTOK = BATCH * TT
DEPTH = 4
GRID_W = 64
GRID_H = SEQ // GRID_W

A_HEADS = 4
A_DQK = 128
A_DV = 256
A_QK = A_HEADS * A_DQK
A_V = A_HEADS * A_DV
ROPE_BASE = 10000.0
LCH = 256
NCH = TT // LCH
SCAN_MB = 4

B_HEADS = 16
B_DH = 64
NA_ROWS = 8
NA_COLS = 16
NA_QROWS = 1
NA_KROWS = NA_QROWS + NA_ROWS - 1
NA_NQ = NA_QROWS * GRID_W
NA_NK = NA_KROWS * GRID_W
NA_STEPS = GRID_H // NA_QROWS

D_FF = 2816
N_EXPERTS = 8
D_FF_EXPERT = 3584
EPS = 1e-6
LOG2E = 1.4426950408889634

RB = 256
NRB = TT // RB
ROW_MB = 2
MM_TM = 2048
MOE_TM = 1024
MOE_FC = 512
MOE_SUB = 256
LANES = 128

PB_QK, PB_V, PB_O, PB_NQ, PB_NK, PB_NV, PB_GA, PB_GB = range(8)
P_COLS = 8 * D

VMEM_LIMIT = 56 * 1024 * 1024


def _cp(sem, vmem=VMEM_LIMIT):
    return pltpu.CompilerParams(dimension_semantics=sem, vmem_limit_bytes=vmem)


def _sigmoid(x):
    return 1.0 / (1.0 + jnp.exp(-x))


def _silu(x):
    return x * _sigmoid(x)


def _log_sigmoid(x):
    return jnp.minimum(x, 0.0) - jnp.log(1.0 + jnp.exp(-jnp.abs(x)))


def _norm_mod(x, g, sc, sh):
    ms = jnp.mean(x * x, axis=-1, keepdims=True)
    y = x * lax.rsqrt(ms + EPS)
    return (y * g) * (1.0 + sc) + sh


def _mod_row(b, i):
    return jnp.where(i == 0, BATCH, b)


def _mod_kernel(c_ref, w_ref, b_ref, o_ref):
    c = c_ref[...]
    s = _silu(c).astype(BF16)
    o_ref[0] = jnp.dot(s, w_ref[0].astype(BF16), preferred_element_type=F32) + b_ref[0]


def _modulation(cc, w_mod, b_mod):
    tn = 2048
    nl = w_mod.shape[0]
    return pl.pallas_call(
        _mod_kernel,
        grid=(nl, 6 * D // tn),
        in_specs=[pl.BlockSpec((16, D), lambda l, j: (0, 0)),
                  pl.BlockSpec((1, D, tn), lambda l, j: (l, 0, j)),
                  pl.BlockSpec((1, 1, tn), lambda l, j: (l, 0, j))],
        out_specs=pl.BlockSpec((1, 16, tn), lambda l, j: (l, 0, j)),
        out_shape=jax.ShapeDtypeStruct((nl, 16, 6 * D), F32),
        compiler_params=_cp(("parallel", "parallel")),
        name="modulation",
    )(cc, w_mod, b_mod.reshape(nl, 1, 6 * D))


def _norm_kernel(x_ref, g_ref, mod_ref, o_ref):
    mod = mod_ref[0]
    o_ref[0] = _norm_mod(x_ref[0], g_ref[...], mod[1:2], mod[0:1]).astype(o_ref.dtype)


def _first_norm(x, g, mod):
    return pl.pallas_call(
        _norm_kernel,
        grid=(BATCH, NRB),
        in_specs=[pl.BlockSpec((1, RB, D), lambda b, i: (b, i, 0)),
                  pl.BlockSpec((1, D), lambda b, i: (0, 0)),
                  pl.BlockSpec((1, 6, D), lambda b, i: (_mod_row(b, i), 0, 0))],
        out_specs=pl.BlockSpec((1, RB, D), lambda b, i: (b, i, 0)),
        out_shape=jax.ShapeDtypeStruct((BATCH, TT, D), BF16),
        compiler_params=_cp(("parallel", "parallel")),
        name="first_norm",
    )(x, g, mod)


def _mm_kernel(a_ref, w_ref, o_ref):
    o_ref[...] = jnp.dot(a_ref[...], w_ref[...], preferred_element_type=F32).astype(o_ref.dtype)


def _in_proj(h, w):
    tn = 1024
    return pl.pallas_call(
        _mm_kernel,
        grid=(P_COLS // tn, NTOK // MM_TM),
        in_specs=[pl.BlockSpec((MM_TM, D), lambda j, i: (i, 0)),
                  pl.BlockSpec((D, tn), lambda j, i: (0, j))],
        out_specs=pl.BlockSpec((MM_TM, tn), lambda j, i: (i, j)),
        out_shape=jax.ShapeDtypeStruct((NTOK, P_COLS), BF16),
        compiler_params=_cp(("parallel", "parallel")),
        name="in_proj",
    )(h, w)


NGATE = 4 * A_HEADS


def _gate_kernel(a_ref, w_ref, b_ref, g_ref, lf_ref):
    g = jnp.dot(a_ref[...], w_ref[...], preferred_element_type=F32) + b_ref[...]
    g_ref[...] = g
    lf = _log_sigmoid(g)
    p0 = lf.astype(BF16).astype(F32)
    r1 = lf - p0
    p1 = r1.astype(BF16).astype(F32)
    p2 = (r1 - p1).astype(BF16).astype(F32)
    lane = lax.broadcasted_iota(jnp.int32, (1, LANES), 1)
    parts = jnp.where(lane < NGATE, p0,
                      jnp.where(lane < 2 * NGATE, pltpu.roll(p1, NGATE, 1),
                                jnp.where(lane < 3 * NGATE, pltpu.roll(p2, 2 * NGATE, 1), 0.0)))
    lf_ref[...] = parts.astype(BF16)


def _gate_proj(h, wg, bg):
    return pl.pallas_call(
        _gate_kernel,
        grid=(NTOK // MM_TM,),
        in_specs=[pl.BlockSpec((MM_TM, D), lambda i: (i, 0)),
                  pl.BlockSpec((D, LANES), lambda i: (0, 0)),
                  pl.BlockSpec((1, LANES), lambda i: (0, 0))],
        out_specs=[pl.BlockSpec((MM_TM, LANES), lambda i: (i, 0)),
                   pl.BlockSpec((MM_TM, LANES), lambda i: (i, 0))],
        out_shape=[jax.ShapeDtypeStruct((NTOK, LANES), F32),
                   jax.ShapeDtypeStruct((NTOK, LANES), BF16)],
        compiler_params=_cp(("parallel",)),
        name="gate_proj",
    )(h, wg, bg)


def _prep_kernel(u_ref, up_ref, un_ref, w_ref, c_ref, s_ref, o_ref):
    i = pl.program_id(1)
    u = u_ref[0].astype(F32)
    prev_row = jnp.where(i >= 2, up_ref[0, 15:16, :].astype(F32), 0.0)
    next_row = jnp.where((i >= 1) & (i <= NRB - 2), un_ref[0, 0:1, :].astype(F32), 0.0)
    rid = lax.broadcasted_iota(jnp.int32, (RB, 1), 0)
    u_m1 = jnp.where(rid == 0, prev_row, pltpu.roll(u, 1, 0))
    u_p1 = jnp.where(rid == RB - 1, next_row, pltpu.roll(u, RB - 1, 0))
    w = w_ref[...]
    y = w[0:1] * u_m1 + w[1:2] * u + w[2:3] * u_p1
    y = _silu(y)
    c = c_ref[...]
    s = s_ref[...]
    cfull = jnp.concatenate([c[:, :A_DQK]] * A_HEADS + [c[:, A_DQK:]] * A_HEADS, axis=1)
    sfull = jnp.concatenate([s[:, :A_DQK]] * A_HEADS + [s[:, A_DQK:]] * A_HEADS, axis=1)
    lane = lax.broadcasted_iota(jnp.int32, (1, 2 * A_QK), 1)
    partner = jnp.where((lane & 32) == 0,
                        pltpu.roll(y, 2 * A_QK - 32, 1), pltpu.roll(y, 32, 1))
    o_ref[0] = (y * cfull + partner * sfull).astype(o_ref.dtype)


def _mlstm_prep(p3, conv_w, rope_c, rope_s):
    nb16 = TT // 16
    return pl.pallas_call(
        _prep_kernel,
        grid=(BATCH, NRB),
        in_specs=[pl.BlockSpec((1, RB, D), lambda b, i: (b, i, PB_QK)),
                  pl.BlockSpec((1, 16, D), lambda b, i: (b, jnp.maximum(i * (RB // 16) - 1, 0), PB_QK)),
                  pl.BlockSpec((1, 16, D), lambda b, i: (b, jnp.minimum((i + 1) * (RB // 16), nb16 - 1), PB_QK)),
                  pl.BlockSpec((3, D), lambda b, i: (0, 0)),
                  pl.BlockSpec((RB, 2 * A_DQK), lambda b, i: (i, 0)),
                  pl.BlockSpec((RB, 2 * A_DQK), lambda b, i: (i, 0))],
        out_specs=pl.BlockSpec((1, RB, D), lambda b, i: (b, i, 0)),
        out_shape=jax.ShapeDtypeStruct((BATCH, TT, D), BF16),
        compiler_params=_cp(("parallel", "parallel")),
        name="mlstm_prep",
    )(p3, p3, p3, conv_w, rope_c, rope_s)


def _mlstm_kernel(qkf_ref, vf_ref, gf_ref, lff_ref, qkb_ref, vb_ref, gb_ref, lfb_ref,
                  of_ref, ob_ref, ct_ref, n_ref, m_ref):
    nscan = 2 * SCAN_MB
    per_dir = [(qkf_ref, vf_ref, gf_ref, lff_ref, of_ref), (qkb_ref, vb_ref, gb_ref, lfb_ref, ob_ref)]
    ins = [per_dir[d % 2] + (d // 2,) for d in range(nscan)]

    @pl.when(pl.program_id(1) == 0)
    def _():
        ct_ref[...] = jnp.zeros_like(ct_ref)
        n_ref[...] = jnp.zeros_like(n_ref)
        m_ref[...] = jnp.zeros_like(m_ref)

    r = lax.broadcasted_iota(jnp.int32, (LCH, LCH), 0)
    c = lax.broadcasted_iota(jnp.int32, (LCH, LCH), 1)
    dir_masks = [c <= r, c >= r]
    masks = [dir_masks[d % 2] for d in range(nscan)]
    b_all, g_all, b_t, g_t, b_end = [], [], [], [], []
    for d in range(nscan):
        tri = jnp.where(masks[d], 1.0, 0.0).astype(BF16)
        bc = jnp.dot(tri, ins[d][3][ins[d][5]], preferred_element_type=F32)
        ba = bc + pltpu.roll(bc, LANES - NGATE, 1) + pltpu.roll(bc, LANES - 2 * NGATE, 1)
        ga = ins[d][2][ins[d][5]]
        if d % 2 == 1:
            ba = pltpu.roll(ba, LANES - 2 * A_HEADS, 1)
            ga = pltpu.roll(ga, LANES - 2 * A_HEADS, 1)
        b_all.append(ba)
        g_all.append(ga)
        b_t.append(ba.T)
        g_t.append(ga.T)
        b_end.append(ba[LCH - 1:LCH, :] if d % 2 == 0 else ba[0:1, :])
    ones = jnp.ones((LCH, LANES), BF16)
    tn_dims = (((0,), (0,)), ((), ()))
    combos = [(d, h) for d in range(nscan) for h in range(A_HEADS)]
    idx = range(len(combos))
    qs = [ins[d][0][ins[d][5], :, h * A_DQK:(h + 1) * A_DQK] for d, h in combos]
    ks = [ins[d][0][ins[d][5], :, A_QK + h * A_DQK:A_QK + (h + 1) * A_DQK] for d, h in combos]
    qk = [lax.dot_general(qs[i], ks[i], NT_DIMS, preferred_element_type=F32) for i in idx]
    b_col = [b_all[d][:, A_HEADS + h:A_HEADS + h + 1] for d, h in combos]
    b_last = [b_end[d][:, A_HEADS + h:A_HEADS + h + 1] for d, h in combos]
    m_old = [m_ref[i][:, 0:1] for i in idx]
    r_row = [b_t[d][A_HEADS + h:A_HEADS + h + 1, :] - g_t[d][h:h + 1, :] for d, h in combos]
    m_row = [jnp.maximum(b_col[i] + m_old[i],
                         jnp.max(jnp.where(masks[d], b_col[i] - r_row[i], -jnp.inf), axis=1, keepdims=True))
             for i, (d, h) in enumerate(combos)]
    s = [(qk[i] * jnp.exp(jnp.where(masks[d], (b_col[i] - m_row[i]) - r_row[i], -jnp.inf))).astype(BF16)
         for i, (d, h) in enumerate(combos)]
    wq = [(jnp.exp(b_col[i] + m_old[i] - m_row[i]) * qs[i].astype(F32)).astype(BF16) for i in idx]
    g_col = [b_last[i] - b_col[i] + g_all[d][:, h:h + 1] for i, (d, h) in enumerate(combos)]
    m_new = [jnp.maximum(b_last[i] + m_old[i], jnp.max(g_col[i], axis=0, keepdims=True)) for i in idx]
    kw = [(jnp.exp(g_col[i] - m_new[i]) * ks[i].astype(F32)).astype(BF16) for i in idx]
    decay = [jnp.exp(b_last[i] + m_old[i] - m_new[i]) for i in idx]
    for i, (d, h) in enumerate(combos):
        v = ins[d][1][ins[d][5], :, h * A_DV:(h + 1) * A_DV]
        lhs = jnp.concatenate([s[i], wq[i]], axis=1)
        den = jnp.dot(lhs, jnp.concatenate([ones, n_ref[i].astype(BF16)], axis=0),
                      preferred_element_type=F32)[:, 0:1]
        rinv = 1.0 / jnp.maximum(jnp.abs(den), jnp.exp(-m_row[i]))
        num = jnp.dot(lhs, jnp.concatenate([v, ct_ref[i].astype(BF16)], axis=0), preferred_element_type=F32)
        ins[d][4][ins[d][5], :, h * A_DV:(h + 1) * A_DV] = (num * rinv).astype(ins[d][4].dtype)
    for i, (d, h) in enumerate(combos):
        v = ins[d][1][ins[d][5], :, h * A_DV:(h + 1) * A_DV]
        ct_ref[i] = decay[i] * ct_ref[i] + lax.dot_general(kw[i], v, tn_dims, preferred_element_type=F32)
        n_ref[i] = decay[i] * n_ref[i] + lax.dot_general(kw[i], ones, tn_dims, preferred_element_type=F32)
        m_ref[i] = jnp.broadcast_to(m_new[i], (1, LANES))


def _rev_chunk(s):
    ncc = CTX // LCH
    return jnp.where(s < ncc, ncc - 1 - s, NCH + ncc - 1 - s)


def _mlstm_scan(qk, p3, gates, lfp):
    fwd = lambda b, s: (b, s, 0)
    bwd = lambda b, s: (b, _rev_chunk(s), 0)
    specs = lambda im, imv: [pl.BlockSpec((SCAN_MB, LCH, D), im), pl.BlockSpec((SCAN_MB, LCH, A_V), imv),
                             pl.BlockSpec((SCAN_MB, LCH, LANES), im), pl.BlockSpec((SCAN_MB, LCH, LANES), im)]
    nstate = 2 * SCAN_MB * A_HEADS
    return pl.pallas_call(
        _mlstm_kernel,
        grid=(BATCH // SCAN_MB, NCH),
        in_specs=(specs(fwd, lambda b, s: (b, s, PB_V))
                  + specs(bwd, lambda b, s: (b, _rev_chunk(s), PB_V))),
        out_specs=[pl.BlockSpec((SCAN_MB, LCH, A_V), fwd), pl.BlockSpec((SCAN_MB, LCH, A_V), bwd)],
        out_shape=[jax.ShapeDtypeStruct((BATCH, TT, A_V), BF16)] * 2,
        scratch_shapes=[pltpu.VMEM((nstate, A_DQK, A_DV), F32),
                        pltpu.VMEM((nstate, A_DQK, LANES), F32),
                        pltpu.VMEM((nstate, 1, LANES), F32)],
        compiler_params=_cp(("parallel", "arbitrary")),
        name="mlstm_scan",
    )(qk, p3, gates, lfp, qk, p3, gates, lfp)


NPAIR = B_HEADS // 2
NT_DIMS = (((1,), (1,)), ((), ()))


def _stack_pair(q, low):
    zero = jnp.zeros_like(q)
    return jnp.concatenate([jnp.where(low, q, zero), jnp.where(low, zero, q)], axis=0)


def _softmax_rows(s_ref, p_ref):
    s = s_ref[...]
    p = jnp.exp2(s - jnp.max(s, axis=2, keepdims=True))
    p_ref[...] = p.astype(p_ref.dtype)
    return 1.0 / jnp.sum(p, axis=2, keepdims=True)


def _na_kernel(q_ref, k_ref, v_ref, *rest):
    bias_refs, (o_ref, s_ref, p_ref) = rest[:NA_SUB], rest[NA_SUB:]
    low = lax.broadcasted_iota(jnp.int32, (1, 2 * B_DH), 1) < B_DH
    starts = []
    for sub in range(NA_SUB):
        row = pl.program_id(1) * NA_SUB + sub
        rs = jnp.clip(row - NA_ROWS // 2, 0, GRID_H - NA_ROWS)
        starts.append(pl.multiple_of(CTX + rs * GRID_W, GRID_W))
    for sub in range(NA_SUB):
        qrows = pl.ds(sub * NA_NQ, NA_NQ)
        for hp in range(NPAIR):
            lo, hi = hp * 2 * B_DH, (hp + 1) * 2 * B_DH
            q2 = _stack_pair(q_ref[0, qrows, lo:hi], low)
            idx = sub * NPAIR + hp
            s_ref[idx, :, 0:NA_NK] = lax.dot_general(q2, k_ref[0, pl.ds(starts[sub], NA_NK), lo:hi], NT_DIMS,
                                                     preferred_element_type=F32
                                                     ) + bias_refs[sub][0, 0, hp].astype(F32)
            s_ref[idx, :, NA_NK:] = lax.dot_general(q2, k_ref[0, 0:CTX, lo:hi], NT_DIMS,
                                                    preferred_element_type=F32)
    rinv = _softmax_rows(s_ref, p_ref)
    for sub in range(NA_SUB):
        qrows = pl.ds(sub * NA_NQ, NA_NQ)
        for hp in range(NPAIR):
            lo, hi = hp * 2 * B_DH, (hp + 1) * 2 * B_DH
            idx = sub * NPAIR + hp
            o2 = (jnp.dot(p_ref[idx, :, 0:NA_NK], v_ref[0, pl.ds(starts[sub], NA_NK), lo:hi],
                          preferred_element_type=F32)
                  + jnp.dot(p_ref[idx, :, NA_NK:], v_ref[0, 0:CTX, lo:hi], preferred_element_type=F32))
            o2 = o2 * rinv[idx]
            o_ref[0, qrows, lo:hi] = jnp.where(low, o2[0:NA_NQ], o2[NA_NQ:]).astype(o_ref.dtype)


def _na_pattern(row):
    edge = NA_ROWS // 2
    return jnp.where(row < edge, row, jnp.where(row <= GRID_H - edge, edge, row - (GRID_H - 2 * edge)))


NA_NPAT = NA_ROWS
NA_SUB = 4


def _na_attention(p3, bias, layer):
    nq = NA_SUB * NA_NQ
    qoff = CTX // nq
    bias_specs = [pl.BlockSpec((1, 1, NPAIR, 2 * NA_NQ, NA_NK),
                               lambda b, i, sub=sub: (layer, _na_pattern(i * NA_SUB + sub), 0, 0, 0))
                  for sub in range(NA_SUB)]
    return pl.pallas_call(
        _na_kernel,
        grid=(BATCH, GRID_H // NA_SUB),
        in_specs=[pl.BlockSpec((1, nq, D), lambda b, i: (b, i + qoff, PB_NQ)),
                  pl.BlockSpec((1, TT, D), lambda b, i: (b, 0, PB_NK)),
                  pl.BlockSpec((1, TT, D), lambda b, i: (b, 0, PB_NV))] + bias_specs,
        out_specs=pl.BlockSpec((1, nq, D), lambda b, i: (b, i, 0)),
        out_shape=jax.ShapeDtypeStruct((BATCH, SEQ, D), BF16),
        scratch_shapes=[pltpu.VMEM((NA_SUB * NPAIR, 2 * NA_NQ, NA_NK + CTX), F32),
                        pltpu.VMEM((NA_SUB * NPAIR, 2 * NA_NQ, NA_NK + CTX), BF16)],
        compiler_params=_cp(("parallel", "arbitrary")),
        name="na_attention",
    )(p3, p3, p3, *([bias] * NA_SUB))


def _ctx_attn_kernel(q_ref, k_ref, v_ref, o_ref, s_ref, p_ref):
    low = lax.broadcasted_iota(jnp.int32, (1, 2 * B_DH), 1) < B_DH
    for hp in range(NPAIR):
        lo, hi = hp * 2 * B_DH, (hp + 1) * 2 * B_DH
        s_ref[hp] = lax.dot_general(_stack_pair(q_ref[0, :, lo:hi], low), k_ref[0, :, lo:hi], NT_DIMS,
                                    preferred_element_type=F32)
    rinv = _softmax_rows(s_ref, p_ref)
    for hp in range(NPAIR):
        lo, hi = hp * 2 * B_DH, (hp + 1) * 2 * B_DH
        o2 = jnp.dot(p_ref[hp], v_ref[0, :, lo:hi], preferred_element_type=F32) * rinv[hp]
        o_ref[0, :, lo:hi] = jnp.where(low, o2[0:CTX], o2[CTX:]).astype(o_ref.dtype)


def _ctx_attention(p3):
    return pl.pallas_call(
        _ctx_attn_kernel,
        grid=(BATCH,),
        in_specs=[pl.BlockSpec((1, CTX, D), lambda b: (b, 0, PB_NQ)),
                  pl.BlockSpec((1, CTX, D), lambda b: (b, 0, PB_NK)),
                  pl.BlockSpec((1, CTX, D), lambda b: (b, 0, PB_NV))],
        out_specs=pl.BlockSpec((1, CTX, D), lambda b: (b, 0, 0)),
        out_shape=jax.ShapeDtypeStruct((BATCH, CTX, D), BF16),
        scratch_shapes=[pltpu.VMEM((NPAIR, 2 * CTX, CTX), F32),
                        pltpu.VMEM((NPAIR, 2 * CTX, CTX), BF16)],
        compiler_params=_cp(("parallel",)),
        name="ctx_attention",
    )(p3, p3, p3)


def _merge_kernel(hf_ref, hbw_ref, o_ref, ga_ref, gb_ref, hnl_ref, hnc_ref, x_ref, mod_ref, ghn_ref,
                  wa_ref, wb_ref, wo_ref, g2_ref, x1_ref, h2_ref):
    rows = ROW_MB * RB
    flat = lambda ref: ref[...].reshape(rows, ref.shape[-1])
    hn = jnp.where(pl.program_id(1) == 0, flat(hnc_ref), flat(hnl_ref))
    bm = jnp.dot(hn, wb_ref[...], preferred_element_type=F32)
    hs = flat(hf_ref).astype(F32) + flat(hbw_ref).astype(F32)
    parts = []
    for h in range(A_HEADS):
        seg = hs[:, h * A_DV:(h + 1) * A_DV]
        mu = jnp.mean(seg, axis=-1, keepdims=True)
        cen = seg - mu
        var = jnp.mean(cen * cen, axis=-1, keepdims=True)
        parts.append(cen * lax.rsqrt(var + EPS))
    ya = jnp.concatenate(parts, axis=1) * ghn_ref[...] * _sigmoid(flat(o_ref).astype(F32))
    a = jnp.dot(ya.astype(BF16), wa_ref[...], preferred_element_type=F32)
    mrg = _sigmoid(flat(ga_ref).astype(F32)) * a + _sigmoid(flat(gb_ref).astype(F32)) * bm
    y = jnp.dot(mrg.astype(BF16), wo_ref[...], preferred_element_type=F32)
    for m in range(ROW_MB):
        mod = mod_ref[m]
        x1 = x_ref[m] + mod[2:3] * y[m * RB:(m + 1) * RB]
        x1_ref[m] = x1
        h2_ref[m] = _norm_mod(x1, g2_ref[...], mod[4:5], mod[3:4]).astype(h2_ref.dtype)


def _merge(hf, hbw, p3, hb_lat, hb_ctx, x, mod, ghn, wa, wb, wo, g2, h2_dtype):
    row = lambda b, i: (b, i, 0)
    full = lambda b, i: (0, 0)
    once = pl.Buffered(1)
    return pl.pallas_call(
        _merge_kernel,
        grid=(BATCH // ROW_MB, NRB),
        in_specs=[pl.BlockSpec((ROW_MB, RB, A_V), row),
                  pl.BlockSpec((ROW_MB, RB, A_V), row),
                  pl.BlockSpec((ROW_MB, RB, D), lambda b, i: (b, i, PB_O)),
                  pl.BlockSpec((ROW_MB, RB, D), lambda b, i: (b, i, PB_GA)),
                  pl.BlockSpec((ROW_MB, RB, D), lambda b, i: (b, i, PB_GB)),
                  pl.BlockSpec((ROW_MB, RB, D), lambda b, i: (b, jnp.maximum(i - 1, 0), 0)),
                  pl.BlockSpec((ROW_MB, CTX, D), lambda b, i: (b, 0, 0)),
                  pl.BlockSpec((ROW_MB, RB, D), row),
                  pl.BlockSpec((ROW_MB, 6, D), lambda b, i: (_mod_block(b, i), 0, 0)),
                  pl.BlockSpec((1, A_V), full),
                  pl.BlockSpec((A_V, D), full, pipeline_mode=once),
                  pl.BlockSpec((D, D), full, pipeline_mode=once),
                  pl.BlockSpec((D, D), full, pipeline_mode=once),
                  pl.BlockSpec((1, D), full)],
        out_specs=[pl.BlockSpec((ROW_MB, RB, D), row), pl.BlockSpec((ROW_MB, RB, D), row)],
        out_shape=[jax.ShapeDtypeStruct((BATCH, TT, D), F32),
                   jax.ShapeDtypeStruct((BATCH, TT, D), h2_dtype)],
        compiler_params=_cp(("parallel", "parallel")),
        name="merge",
    )(hf, hbw, p3, p3, p3, hb_lat, hb_ctx, x, _pad_mod(mod), ghn, wa, wb, wo, g2)


FFN_CH = D_FF // 2


def _ffn_kernel(h_ref, x_ref, mod_ref, w1_ref, w3_ref, w2_ref, gn_ref, modn_ref, x2_ref, hn_ref):
    h = h_ref[...].reshape(ROW_MB * RB, D)
    y = None
    for cidx in range(D_FF // FFN_CH):
        lo, hi = cidx * FFN_CH, (cidx + 1) * FFN_CH
        a = jnp.dot(h, w1_ref[:, lo:hi], preferred_element_type=F32)
        g = jnp.dot(h, w3_ref[:, lo:hi], preferred_element_type=F32)
        part = jnp.dot((_silu(a) * g).astype(BF16), w2_ref[lo:hi, :], preferred_element_type=F32)
        y = part if y is None else y + part
    for m in range(ROW_MB):
        mod = mod_ref[m]
        x2 = x_ref[m] + mod[5:6] * y[m * RB:(m + 1) * RB]
        x2_ref[m] = x2
        modn = modn_ref[m]
        hn_ref[m] = _norm_mod(x2, gn_ref[...], modn[1:2], modn[0:1]).astype(hn_ref.dtype)


def _pad_mod(mod):
    return jnp.concatenate([mod] + [mod[BATCH:]] * (ROW_MB - 1), axis=0)


def _mod_block(b, i):
    return jnp.where(i == 0, BATCH // ROW_MB, b)


def _dense_ffn(h2, x1, mod, w1, w3, w2, gn, modn):
    row = lambda b, i: (b, i, 0)
    full = lambda b, i: (0, 0)
    modspec = pl.BlockSpec((ROW_MB, 6, D), lambda b, i: (_mod_block(b, i), 0, 0))
    once = pl.Buffered(1)
    return pl.pallas_call(
        _ffn_kernel,
        grid=(BATCH // ROW_MB, NRB),
        in_specs=[pl.BlockSpec((ROW_MB, RB, D), row), pl.BlockSpec((ROW_MB, RB, D), row), modspec,
                  pl.BlockSpec((D, D_FF), full, pipeline_mode=once),
                  pl.BlockSpec((D, D_FF), full, pipeline_mode=once),
                  pl.BlockSpec((D_FF, D), full, pipeline_mode=once),
                  pl.BlockSpec((1, D), full), modspec],
        out_specs=[pl.BlockSpec((ROW_MB, RB, D), row), pl.BlockSpec((ROW_MB, RB, D), row)],
        out_shape=[jax.ShapeDtypeStruct((BATCH, TT, D), F32),
                   jax.ShapeDtypeStruct((BATCH, TT, D), BF16)],
        compiler_params=_cp(("parallel", "parallel")),
        name="dense_ffn",
    )(h2, x1, _pad_mod(mod), w1, w3, w2, gn, _pad_mod(modn))


def _router_kernel(h_ref, wr_ref, route_ref, cnt_ref, run_ref):
    i = pl.program_id(0)

    @pl.when(i == 0)
    def _():
        run_ref[...] = jnp.zeros_like(run_ref)

    logits = jnp.dot(h_ref[...].astype(BF16), wr_ref[...], preferred_element_type=F32)
    lane = lax.broadcasted_iota(jnp.int32, (RB, LANES), 1).astype(F32)
    lg = jnp.where(lane < N_EXPERTS, logits, -jnp.inf)
    v1 = jnp.max(lg, axis=1, keepdims=True)
    i1 = jnp.min(jnp.where(lg == v1, lane, float(LANES)), axis=1, keepdims=True)
    lg2 = jnp.where(lane == i1, -jnp.inf, lg)
    v2 = jnp.max(lg2, axis=1, keepdims=True)
    i2 = jnp.min(jnp.where(lg2 == v2, lane, float(LANES)), axis=1, keepdims=True)
    e = jnp.exp(v2 - v1)
    w1 = 1.0 / (1.0 + e)
    w2 = e / (1.0 + e)
    oh1 = (lane == i1).astype(F32)
    oh2 = (lane == i2).astype(F32)
    r = lax.broadcasted_iota(jnp.int32, (RB, RB), 0)
    c = lax.broadcasted_iota(jnp.int32, (RB, RB), 1)
    tri = (r > c).astype(BF16)
    cs1 = jnp.dot(tri, oh1.astype(BF16), preferred_element_type=F32)
    cs2 = jnp.dot(tri, oh2.astype(BF16), preferred_element_type=F32)
    tot1 = jnp.sum(oh1, axis=0, keepdims=True)
    tot2 = jnp.sum(oh2, axis=0, keepdims=True)
    run = run_ref[...]
    rank1 = jnp.sum(oh1 * (run + cs1), axis=1, keepdims=True)
    rank2 = jnp.sum(oh2 * (run + tot1 + cs2), axis=1, keepdims=True)
    new_run = run + tot1 + tot2
    run_ref[...] = new_run
    cnt_ref[...] = new_run
    out = jnp.where(lane == 0, i1,
          jnp.where(lane == 1, i2,
          jnp.where(lane == 2, w1,
          jnp.where(lane == 3, w2,
          jnp.where(lane == 4, rank1,
          jnp.where(lane == 5, rank2, 0.0))))))
    route_ref[...] = out


def _router(h2_flat, wr, nblk, blk_of):
    return pl.pallas_call(
        _router_kernel,
        grid=(nblk,),
        in_specs=[pl.BlockSpec((RB, D), lambda i: (blk_of(i), 0)),
                  pl.BlockSpec((D, LANES), lambda i: (0, 0))],
        out_specs=[pl.BlockSpec((RB, LANES), lambda i: (i, 0)),
                   pl.BlockSpec((1, LANES), lambda i: (0, 0))],
        out_shape=[jax.ShapeDtypeStruct((nblk * RB, LANES), F32),
                   jax.ShapeDtypeStruct((1, LANES), F32)],
        scratch_shapes=[pltpu.VMEM((1, LANES), F32)],
        compiler_params=_cp(("arbitrary",)),
        name="moe_router",
    )(h2_flat, wr)


DMA_UNROLL = 8
N_ZERO = 2 * N_EXPERTS


def _dispatch_kernel(pos_ref, zstart_ref, h_ref, o_ref, zbuf, stage, sems, zsem):
    @pl.when(pl.program_id(0) == 0)
    def _():
        zbuf[...] = jnp.zeros_like(zbuf)
        for z in range(N_ZERO):
            @pl.when(zstart_ref[z] >= 0)
            def _():
                zs = pl.multiple_of(zstart_ref[z], MOE_TM)
                pltpu.make_async_copy(zbuf, o_ref.at[pl.ds(zs, MOE_TM)], zsem).start()
        for z in range(N_ZERO):
            @pl.when(zstart_ref[z] >= 0)
            def _():
                pltpu.make_async_copy(zbuf, o_ref.at[pl.ds(0, MOE_TM)], zsem).wait()

    step = pl.program_id(0)
    slot = step % 2
    stage[slot] = h_ref[...]

    def issue(t, carry):
        src = stage.at[slot, pl.ds(t, 1)]
        pltpu.make_async_copy(src, o_ref.at[pl.ds(pos_ref[0, 0, t], 1)], sems.at[slot]).start(priority=0)
        pltpu.make_async_copy(src, o_ref.at[pl.ds(pos_ref[0, 0, RB + t], 1)], sems.at[slot]).start(priority=1)
        return carry

    lax.fori_loop(0, RB, issue, 0, unroll=DMA_UNROLL)

    def drain(sl):
        for _ in range(2):
            pltpu.make_async_copy(stage.at[sl], o_ref.at[pl.ds(0, RB)], sems.at[sl]).wait()

    @pl.when(step > 0)
    def _():
        drain(1 - slot)

    @pl.when(step == pl.num_programs(0) - 1)
    def _():
        drain(slot)


def _dispatch(pos, zstart, h2_flat, nblk, blk_of, n_sorted):
    return pl.pallas_call(
        _dispatch_kernel,
        grid=(nblk,),
        in_specs=[pl.BlockSpec((1, 1, 2 * RB), lambda i: (i, 0, 0), memory_space=pltpu.SMEM),
                  pl.BlockSpec(memory_space=pltpu.SMEM),
                  pl.BlockSpec((RB, D), lambda i: (blk_of(i), 0))],
        out_specs=pl.BlockSpec(memory_space=pl.ANY),
        out_shape=jax.ShapeDtypeStruct((n_sorted, D), F32),
        scratch_shapes=[pltpu.VMEM((MOE_TM, D), F32), pltpu.VMEM((2, RB, D), F32),
                        pltpu.SemaphoreType.DMA((2,)), pltpu.SemaphoreType.DMA(())],
        compiler_params=_cp(("arbitrary",)),
        name="moe_dispatch",
    )(pos, zstart, h2_flat)


def _gmm_kernel(be_ref, nb_ref, a_ref, w1_ref, w3_ref, w2_ref, o_ref, abf_ref):
    del be_ref
    i = pl.program_id(0)
    f = pl.program_id(1)

    @pl.when((i < nb_ref[0]) & (f == 0))
    def _():
        abf_ref[...] = a_ref[...].astype(BF16)

    @pl.when(i < nb_ref[0])
    def _():
        a = abf_ref[...]
        gs = []
        for c in range(MOE_FC // MOE_SUB):
            lo, hi = c * MOE_SUB, (c + 1) * MOE_SUB
            h1 = jnp.dot(a, w1_ref[0, :, lo:hi].astype(BF16), preferred_element_type=F32)
            h3 = jnp.dot(a, w3_ref[0, :, lo:hi].astype(BF16), preferred_element_type=F32)
            gs.append((_silu(h1) * h3).astype(BF16))
        part = jnp.dot(jnp.concatenate(gs, axis=1), w2_ref[0].astype(BF16), preferred_element_type=F32)

        @pl.when(f == 0)
        def _():
            o_ref[...] = part

        @pl.when(f > 0)
        def _():
            o_ref[...] += part

    @pl.when((i >= nb_ref[0]) & (f == 0))
    def _():
        o_ref[...] = jnp.zeros_like(o_ref)


def _expert_ffn(blk_e, nb, hs, w1, w3, w2, j):
    nbmax = hs.shape[0] // MOE_TM
    nf = D_FF_EXPERT // MOE_FC

    def ieff(i, nb_ref):
        return jnp.minimum(i, nb_ref[0] - 1)

    def feff(i, f, nb_ref):
        return jnp.where(i < nb_ref[0], f, nf - 1)

    grid_spec = pltpu.PrefetchScalarGridSpec(
        num_scalar_prefetch=2,
        grid=(nbmax, nf),
        in_specs=[pl.BlockSpec((MOE_TM, D), lambda i, f, be, nbr: (ieff(i, nbr), 0)),
                  pl.BlockSpec((None, 1, D, MOE_FC),
                               lambda i, f, be, nbr: (j, be[ieff(i, nbr)], 0, feff(i, f, nbr))),
                  pl.BlockSpec((None, 1, D, MOE_FC),
                               lambda i, f, be, nbr: (j, be[ieff(i, nbr)], 0, feff(i, f, nbr))),
                  pl.BlockSpec((None, 1, MOE_FC, D),
                               lambda i, f, be, nbr: (j, be[ieff(i, nbr)], feff(i, f, nbr), 0))],
        out_specs=pl.BlockSpec((MOE_TM, D), lambda i, f, be, nbr: (i, 0)),
        scratch_shapes=[pltpu.VMEM((MOE_TM, D), BF16)],
    )
    return pl.pallas_call(
        _gmm_kernel,
        grid_spec=grid_spec,
        out_shape=jax.ShapeDtypeStruct(hs.shape, F32),
        compiler_params=_cp(("arbitrary", "arbitrary")),
        name="moe_expert_ffn",
    )(blk_e, nb, hs, w1, w3, w2)


def _combine_kernel(pos_ref, posn_ref, y_ref, route_ref, x_ref, mod_ref, gn_ref, modn_ref,
                    x2_ref, hn_ref, ybuf, sems):
    t = pl.program_id(0) * pl.num_programs(1) + pl.program_id(1)
    nsteps = pl.num_programs(0) * pl.num_programs(1)
    slot = t % 2

    def start_block(p_ref, sl):
        def issue(r, carry):
            pltpu.make_async_copy(y_ref.at[pl.ds(p_ref[0, 0, r], 1)],
                                  ybuf.at[sl, 0, pl.ds(r, 1)], sems.at[sl]).start(priority=0)
            pltpu.make_async_copy(y_ref.at[pl.ds(p_ref[0, 0, RB + r], 1)],
                                  ybuf.at[sl, 1, pl.ds(r, 1)], sems.at[sl]).start(priority=1)
            return carry
        lax.fori_loop(0, RB, issue, 0, unroll=DMA_UNROLL)

    @pl.when(t == 0)
    def _():
        start_block(pos_ref, 0)

    @pl.when(t + 1 < nsteps)
    def _():
        start_block(posn_ref, 1 - slot)

    for e in range(2):
        pltpu.make_async_copy(y_ref.at[pl.ds(0, RB)], ybuf.at[slot, e], sems.at[slot]).wait()
    route = route_ref[...]
    y = route[:, 2:3] * ybuf[slot, 0] + route[:, 3:4] * ybuf[slot, 1]
    mod = mod_ref[0]
    x2 = x_ref[0] + mod[5:6] * y
    x2_ref[0] = x2
    modn = modn_ref[0]
    hn_ref[0] = _norm_mod(x2, gn_ref[...], modn[1:2], modn[0:1]).astype(hn_ref.dtype)


def _combine(pos, ys, route, x1, mod, gn, modn, latent_only, hn_dtype):
    nrb = NRB - 1 if latent_only else NRB
    off = 1 if latent_only else 0
    rows = nrb * RB
    full = lambda b, i: (0, 0)
    modspec = pl.BlockSpec((1, 6, D), lambda b, i: (_mod_row(b, i + off), 0, 0))
    last = BATCH * nrb - 1
    return pl.pallas_call(
        _combine_kernel,
        grid=(BATCH, nrb),
        in_specs=[pl.BlockSpec((1, 1, 2 * RB), lambda b, i: (b * nrb + i, 0, 0), memory_space=pltpu.SMEM),
                  pl.BlockSpec((1, 1, 2 * RB), lambda b, i: (jnp.minimum(b * nrb + i + 1, last), 0, 0),
                               memory_space=pltpu.SMEM),
                  pl.BlockSpec(memory_space=pl.ANY),
                  pl.BlockSpec((RB, LANES), lambda b, i: (b * nrb + i, 0)),
                  pl.BlockSpec((1, RB, D), lambda b, i: (b, i + off, 0)),
                  modspec,
                  pl.BlockSpec((1, D), full),
                  modspec],
        out_specs=[pl.BlockSpec((1, RB, D), lambda b, i: (b, i, 0)),
                   pl.BlockSpec((1, RB, D), lambda b, i: (b, i, 0))],
        out_shape=[jax.ShapeDtypeStruct((BATCH, rows, D), F32),
                   jax.ShapeDtypeStruct((BATCH, rows, D), hn_dtype)],
        scratch_shapes=[pltpu.VMEM((2, 2, RB, D), F32), pltpu.SemaphoreType.DMA((2,))],
        compiler_params=_cp(("arbitrary", "arbitrary")),
        name="moe_combine",
    )(pos, pos, ys, route, x1, mod, gn, modn)


def _moe_ffn(h2, x1, mod, wr, w1, w3, w2, j, gn, modn, latent_only, hn_dtype):
    h2_flat = h2.reshape(NTOK, D)
    if latent_only:
        nrb = NRB - 1
        blk_of = lambda i: (i // nrb) * NRB + (i % nrb) + 1
    else:
        nrb = NRB
        blk_of = lambda i: i
    nblk = BATCH * nrb
    n_pairs = 2 * nblk * RB
    nbmax = n_pairs // MOE_TM + N_EXPERTS
    n_sorted = nbmax * MOE_TM

    route, counts = _router(h2_flat, wr, nblk, blk_of)
    cnt = counts[0, :N_EXPERTS].astype(jnp.int32)
    gsz = ((cnt + MOE_TM - 1) // MOE_TM) * MOE_TM
    gend = jnp.cumsum(gsz)
    goff = gend - gsz
    e12 = route[:, 0:2].astype(jnp.int32)
    rank = route[:, 4:6].astype(jnp.int32)
    pos = goff[e12] + rank
    pos = pos.reshape(nblk, RB, 2).transpose(0, 2, 1).reshape(nblk, 1, 2 * RB)
    nb = (gend[-1] // MOE_TM).reshape(1)
    blk_start = jnp.arange(nbmax, dtype=jnp.int32) * MOE_TM
    blk_e = jnp.minimum(jnp.sum(blk_start[:, None] >= gend[None, :], axis=1), N_EXPERTS - 1).astype(jnp.int32)

    trail = (nbmax - N_EXPERTS + jnp.arange(N_EXPERTS, dtype=jnp.int32)) * MOE_TM
    zstart = jnp.concatenate([jnp.where(cnt > 0, gend - MOE_TM, -1),
                              jnp.where(trail >= gend[-1], trail, -1)]).astype(jnp.int32)

    hs = _dispatch(pos, zstart, h2_flat, nblk, blk_of, n_sorted)
    ys = _expert_ffn(blk_e, nb, hs, w1, w3, w2, j)
    return _combine(pos, ys, route, x1, mod, gn, modn, latent_only, hn_dtype)


def _rope_tables():
    quarter = A_DQK // 4
    inv = 1.0 / (ROPE_BASE ** (jnp.arange(quarter, dtype=F32) / quarter))
    pos = jnp.arange(SEQ)
    rows = (pos // GRID_W).astype(F32)
    cols = (pos % GRID_W).astype(F32)
    ang_r = rows[:, None] * inv[None, :]
    ang_c = cols[:, None] * inv[None, :]
    cos = jnp.concatenate([jnp.cos(ang_r)] * 2 + [jnp.cos(ang_c)] * 2, axis=1)
    sin = jnp.concatenate([-jnp.sin(ang_r), jnp.sin(ang_r), -jnp.sin(ang_c), jnp.sin(ang_c)], axis=1)
    cos = jnp.concatenate([jnp.ones((CTX, A_DQK), F32), cos], axis=0)
    sin = jnp.concatenate([jnp.zeros((CTX, A_DQK), F32), sin], axis=0)
    kscale = A_DQK ** -0.5
    return (jnp.concatenate([cos, cos * kscale], axis=1),
            jnp.concatenate([sin, sin * kscale], axis=1))


def _na_bias_tables(rpb):
    rs = np.clip(np.arange(GRID_H) - NA_ROWS // 2, 0, GRID_H - NA_ROWS)
    cs = np.clip(np.arange(GRID_W) - NA_COLS // 2, 0, GRID_W - NA_COLS)
    rows = [0, 1, 2, 3, NA_ROWS // 2, GRID_H - 3, GRID_H - 2, GRID_H - 1]
    col = np.arange(GRID_W)
    dc = np.clip(col[None, :] - col[:, None] + NA_COLS - 1, 0, 2 * NA_COLS - 2)
    valid_c = (col[None, :] >= cs[:, None]) & (col[None, :] < cs[:, None] + NA_COLS)
    sel_c = np.eye(2 * NA_COLS - 1, dtype=np.float32)[dc]
    t = jnp.einsum('lhab,uvb->lhuav', rpb * LOG2E, jnp.asarray(sel_c),
                   precision=lax.Precision.HIGHEST)
    neg = np.where(valid_c, 0.0, -np.inf).astype(np.float32)[:, None, :]
    nl = rpb.shape[0]
    t = (t + jnp.asarray(neg)).reshape(nl, NPAIR, 2 * NA_NQ, (2 * NA_ROWS - 1) * GRID_W)
    first = [rs[r] - r + NA_ROWS - 1 for r in rows]
    return jnp.stack([t[..., d0 * GRID_W:d0 * GRID_W + NA_NK] for d0 in first], axis=1).astype(BF16)


def kernel(x, c, ctx, c_ctx, w_mod, b_mod, g_norm1, g_norm2, w_in, a_conv, a_gate_b, a_hnorm_g, na_rpb,
           w_br_a, w_br_b, w_out, ffn_w1, ffn_w3, ffn_w2, moe_router, moe_w1, moe_w3, moe_w2, g_final):
    cc = jnp.concatenate([c, c_ctx[None, :], jnp.zeros((16 - BATCH - 1, D), F32)], axis=0)
    mod_all = _modulation(cc, w_mod, b_mod).reshape(DEPTH, 16, 6, D)[:, :BATCH + 1]
    mod_zero = jnp.zeros((BATCH + 1, 6, D), F32)
    rope_c, rope_s = _rope_tables()
    na_bias = _na_bias_tables(na_rpb)

    xs = jnp.concatenate([ctx, x], axis=1)
    h1 = _first_norm(xs, g_norm1[0][None, :], mod_all[0])
    out = None
    for l in range(DEPTH):
        last = l == DEPTH - 1
        mod = mod_all[l]
        wl = w_in[l]
        g0 = 3 * D
        g1 = g0 + NGATE
        wp = jnp.concatenate([wl[:, :g0], wl[:, g1:g1 + D] * (B_DH ** -0.5 * LOG2E), wl[:, g1 + D:]],
                             axis=1).astype(BF16)
        wg = jnp.pad(wl[:, g0:g1], ((0, 0), (0, LANES - NGATE))).astype(BF16)
        bg = jnp.pad(a_gate_b[l], (0, LANES - NGATE))[None, :]

        h1_flat = h1.reshape(NTOK, D)
        p3 = _in_proj(h1_flat, wp).reshape(BATCH, TT, P_COLS)
        gates, lfp = _gate_proj(h1_flat, wg, bg)

        qk = _mlstm_prep(p3, a_conv[l], rope_c, rope_s)
        hf, hbw = _mlstm_scan(qk, p3, gates.reshape(BATCH, TT, LANES), lfp.reshape(BATCH, TT, LANES))
        hb_lat = _na_attention(p3, na_bias, l)
        hb_ctx = _ctx_attention(p3)

        moe = l % 2 == 1
        x1, h2 = _merge(hf, hbw, p3, hb_lat, hb_ctx, xs, mod, a_hnorm_g[l][None, :],
                        w_br_a[l].astype(BF16), w_br_b[l].astype(BF16), w_out[l].astype(BF16),
                        g_norm2[l][None, :], F32 if moe else BF16)
        if last:
            gn, modn = g_final[None, :], mod_zero
        else:
            gn, modn = g_norm1[l + 1][None, :], mod_all[l + 1]
        j = l // 2
        if not moe:
            xs, h1 = _dense_ffn(h2, x1, mod, ffn_w1[j].astype(BF16), ffn_w3[j].astype(BF16),
                                ffn_w2[j].astype(BF16), gn, modn)
        else:
            wr = jnp.pad(moe_router[j], ((0, 0), (0, LANES - N_EXPERTS))).astype(BF16)
            xs, h1 = _moe_ffn(h2, x1, mod, wr, moe_w1, moe_w3, moe_w2, j, gn, modn, last,
                              F32 if last else BF16)
            if last:
                out = h1
    return out
```

```python
import functools

import numpy as np
import jax
import jax.numpy as jnp
from jax import lax
from jax.experimental import pallas as pl
from jax.experimental.pallas import tpu as pltpu

F32 = jnp.float32
BF16 = jnp.bfloat16

D = 1024
BATCH = 8
SEQ = 2048
CTX = 256
TT = CTX + SEQ
NTOK = BATCH * TT
DEPTH = 4
GRID_W = 64
GRID_H = SEQ // GRID_W

A_HEADS = 4
A_DQK = 128
A_DV = 256
A_QK = A_HEADS * A_DQK
A_V = A_HEADS * A_DV
ROPE_BASE = 10000.0
LCH = 256
NCH = TT // LCH
SCAN_MB = 4

B_HEADS = 16
B_DH = 64
NA_ROWS = 8
NA_COLS = 16
NA_QROWS = 1
NA_KROWS = NA_QROWS + NA_ROWS - 1
NA_NQ = NA_QROWS * GRID_W
NA_NK = NA_KROWS * GRID_W
NA_STEPS = GRID_H // NA_QROWS

D_FF = 2816
N_EXPERTS = 8
D_FF_EXPERT = 3584
EPS = 1e-6
LOG2E = 1.4426950408889634

RB = 256
NRB = TT // RB
ROW_MB = 2
MM_TM = 2048
MOE_TM = 1024
MOE_FC = 512
MOE_SUB = 256
LANES = 128

PB_QK, PB_V, PB_O, PB_NQ, PB_NK, PB_NV, PB_GA, PB_GB = range(8)
P_COLS = 8 * D

VMEM_LIMIT = 56 * 1024 * 1024


def _cp(sem, vmem=VMEM_LIMIT):
    return pltpu.CompilerParams(dimension_semantics=sem, vmem_limit_bytes=vmem)


def _sigmoid(x):
    return 1.0 / (1.0 + jnp.exp(-x))


def _silu(x):
    return x * _sigmoid(x)


def _log_sigmoid(x):
    return jnp.minimum(x, 0.0) - jnp.log(1.0 + jnp.exp(-jnp.abs(x)))


def _norm_mod(x, g, sc, sh):
    ms = jnp.mean(x * x, axis=-1, keepdims=True)
    y = x * lax.rsqrt(ms + EPS)
    return (y * g) * (1.0 + sc) + sh


def _mod_row(b, i):
    return jnp.where(i == 0, BATCH, b)


def _mod_kernel(c_ref, w_ref, b_ref, o_ref):
    c = c_ref[...]
    s = _silu(c).astype(BF16)
    o_ref[0] = jnp.dot(s, w_ref[0].astype(BF16), preferred_element_type=F32) + b_ref[0]


def _modulation(cc, w_mod, b_mod):
    tn = 2048
    nl = w_mod.shape[0]
    return pl.pallas_call(
        _mod_kernel,
        grid=(nl, 6 * D // tn),
        in_specs=[pl.BlockSpec((16, D), lambda l, j: (0, 0)),
                  pl.BlockSpec((1, D, tn), lambda l, j: (l, 0, j)),
                  pl.BlockSpec((1, 1, tn), lambda l, j: (l, 0, j))],
        out_specs=pl.BlockSpec((1, 16, tn), lambda l, j: (l, 0, j)),
        out_shape=jax.ShapeDtypeStruct((nl, 16, 6 * D), F32),
        compiler_params=_cp(("parallel", "parallel")),
        name="modulation",
    )(cc, w_mod, b_mod.reshape(nl, 1, 6 * D))


def _norm_kernel(x_ref, g_ref, mod_ref, o_ref):
    mod = mod_ref[0]
    o_ref[0] = _norm_mod(x_ref[0], g_ref[...], mod[1:2], mod[0:1]).astype(o_ref.dtype)


def _first_norm(x, g, mod):
    return pl.pallas_call(
        _norm_kernel,
        grid=(BATCH, NRB),
        in_specs=[pl.BlockSpec((1, RB, D), lambda b, i: (b, i, 0)),
                  pl.BlockSpec((1, D), lambda b, i: (0, 0)),
                  pl.BlockSpec((1, 6, D), lambda b, i: (_mod_row(b, i), 0, 0))],
        out_specs=pl.BlockSpec((1, RB, D), lambda b, i: (b, i, 0)),
        out_shape=jax.ShapeDtypeStruct((BATCH, TT, D), BF16),
        compiler_params=_cp(("parallel", "parallel")),
        name="first_norm",
    )(x, g, mod)


def _mm_kernel(a_ref, w_ref, o_ref):
    o_ref[...] = jnp.dot(a_ref[...], w_ref[...], preferred_element_type=F32).astype(o_ref.dtype)


def _in_proj(h, w):
    tn = 1024
    return pl.pallas_call(
        _mm_kernel,
        grid=(P_COLS // tn, NTOK // MM_TM),
        in_specs=[pl.BlockSpec((MM_TM, D), lambda j, i: (i, 0)),
                  pl.BlockSpec((D, tn), lambda j, i: (0, j))],
        out_specs=pl.BlockSpec((MM_TM, tn), lambda j, i: (i, j)),
        out_shape=jax.ShapeDtypeStruct((NTOK, P_COLS), BF16),
        compiler_params=_cp(("parallel", "parallel")),
        name="in_proj",
    )(h, w)


NGATE = 4 * A_HEADS


def _gate_kernel(a_ref, w_ref, b_ref, g_ref, lf_ref):
    g = jnp.dot(a_ref[...], w_ref[...], preferred_element_type=F32) + b_ref[...]
    g_ref[...] = g
    lf = _log_sigmoid(g)
    p0 = lf.astype(BF16).astype(F32)
    r1 = lf - p0
    p1 = r1.astype(BF16).astype(F32)
    p2 = (r1 - p1).astype(BF16).astype(F32)
    lane = lax.broadcasted_iota(jnp.int32, (1, LANES), 1)
    parts = jnp.where(lane < NGATE, p0,
                      jnp.where(lane < 2 * NGATE, pltpu.roll(p1, NGATE, 1),
                                jnp.where(lane < 3 * NGATE, pltpu.roll(p2, 2 * NGATE, 1), 0.0)))
    lf_ref[...] = parts.astype(BF16)


def _gate_proj(h, wg, bg):
    return pl.pallas_call(
        _gate_kernel,
        grid=(NTOK // MM_TM,),
        in_specs=[pl.BlockSpec((MM_TM, D), lambda i: (i, 0)),
                  pl.BlockSpec((D, LANES), lambda i: (0, 0)),
                  pl.BlockSpec((1, LANES), lambda i: (0, 0))],
        out_specs=[pl.BlockSpec((MM_TM, LANES), lambda i: (i, 0)),
                   pl.BlockSpec((MM_TM, LANES), lambda i: (i, 0))],
        out_shape=[jax.ShapeDtypeStruct((NTOK, LANES), F32),
                   jax.ShapeDtypeStruct((NTOK, LANES), BF16)],
        compiler_params=_cp(("parallel",)),
        name="gate_proj",
    )(h, wg, bg)


def _prep_kernel(u_ref, up_ref, un_ref, w_ref, c_ref, s_ref, o_ref):
    i = pl.program_id(1)
    u = u_ref[0].astype(F32)
    prev_row = jnp.where(i >= 2, up_ref[0, 15:16, :].astype(F32), 0.0)
    next_row = jnp.where((i >= 1) & (i <= NRB - 2), un_ref[0, 0:1, :].astype(F32), 0.0)
    rid = lax.broadcasted_iota(jnp.int32, (RB, 1), 0)
    u_m1 = jnp.where(rid == 0, prev_row, pltpu.roll(u, 1, 0))
    u_p1 = jnp.where(rid == RB - 1, next_row, pltpu.roll(u, RB - 1, 0))
    w = w_ref[...]
    y = w[0:1] * u_m1 + w[1:2] * u + w[2:3] * u_p1
    y = _silu(y)
    c = c_ref[...]
    s = s_ref[...]
    cfull = jnp.concatenate([c[:, :A_DQK]] * A_HEADS + [c[:, A_DQK:]] * A_HEADS, axis=1)
    sfull = jnp.concatenate([s[:, :A_DQK]] * A_HEADS + [s[:, A_DQK:]] * A_HEADS, axis=1)
    lane = lax.broadcasted_iota(jnp.int32, (1, 2 * A_QK), 1)
    partner = jnp.where((lane & 32) == 0,
                        pltpu.roll(y, 2 * A_QK - 32, 1), pltpu.roll(y, 32, 1))
    o_ref[0] = (y * cfull + partner * sfull).astype(o_ref.dtype)


def _mlstm_prep(p3, conv_w, rope_c, rope_s):
    nb16 = TT // 16
    return pl.pallas_call(
        _prep_kernel,
        grid=(BATCH, NRB),
        in_specs=[pl.BlockSpec((1, RB, D), lambda b, i: (b, i, PB_QK)),
                  pl.BlockSpec((1, 16, D), lambda b, i: (b, jnp.maximum(i * (RB // 16) - 1, 0), PB_QK)),
                  pl.BlockSpec((1, 16, D), lambda b, i: (b, jnp.minimum((i + 1) * (RB // 16), nb16 - 1), PB_QK)),
                  pl.BlockSpec((3, D), lambda b, i: (0, 0)),
                  pl.BlockSpec((RB, 2 * A_DQK), lambda b, i: (i, 0)),
                  pl.BlockSpec((RB, 2 * A_DQK), lambda b, i: (i, 0))],
        out_specs=pl.BlockSpec((1, RB, D), lambda b, i: (b, i, 0)),
        out_shape=jax.ShapeDtypeStruct((BATCH, TT, D), BF16),
        compiler_params=_cp(("parallel", "parallel")),
        name="mlstm_prep",
    )(p3, p3, p3, conv_w, rope_c, rope_s)


def _mlstm_kernel(qkf_ref, vf_ref, gf_ref, lff_ref, qkb_ref, vb_ref, gb_ref, lfb_ref,
                  of_ref, ob_ref, ct_ref, n_ref, m_ref):
    nscan = 2 * SCAN_MB
    per_dir = [(qkf_ref, vf_ref, gf_ref, lff_ref, of_ref), (qkb_ref, vb_ref, gb_ref, lfb_ref, ob_ref)]
    ins = [per_dir[d % 2] + (d // 2,) for d in range(nscan)]

    @pl.when(pl.program_id(1) == 0)
    def _():
        ct_ref[...] = jnp.zeros_like(ct_ref)
        n_ref[...] = jnp.zeros_like(n_ref)
        m_ref[...] = jnp.zeros_like(m_ref)

    r = lax.broadcasted_iota(jnp.int32, (LCH, LCH), 0)
    c = lax.broadcasted_iota(jnp.int32, (LCH, LCH), 1)
    dir_masks = [c <= r, c >= r]
    masks = [dir_masks[d % 2] for d in range(nscan)]
    b_all, g_all, b_t, g_t, b_end = [], [], [], [], []
    for d in range(nscan):
        tri = jnp.where(masks[d], 1.0, 0.0).astype(BF16)
        bc = jnp.dot(tri, ins[d][3][ins[d][5]], preferred_element_type=F32)
        ba = bc + pltpu.roll(bc, LANES - NGATE, 1) + pltpu.roll(bc, LANES - 2 * NGATE, 1)
        ga = ins[d][2][ins[d][5]]
        if d % 2 == 1:
            ba = pltpu.roll(ba, LANES - 2 * A_HEADS, 1)
            ga = pltpu.roll(ga, LANES - 2 * A_HEADS, 1)
        b_all.append(ba)
        g_all.append(ga)
        b_t.append(ba.T)
        g_t.append(ga.T)
        b_end.append(ba[LCH - 1:LCH, :] if d % 2 == 0 else ba[0:1, :])
    ones = jnp.ones((LCH, LANES), BF16)
    tn_dims = (((0,), (0,)), ((), ()))
    combos = [(d, h) for d in range(nscan) for h in range(A_HEADS)]
    idx = range(len(combos))
    qs = [ins[d][0][ins[d][5], :, h * A_DQK:(h + 1) * A_DQK] for d, h in combos]
    ks = [ins[d][0][ins[d][5], :, A_QK + h * A_DQK:A_QK + (h + 1) * A_DQK] for d, h in combos]
    qk = [lax.dot_general(qs[i], ks[i], NT_DIMS, preferred_element_type=F32) for i in idx]
    b_col = [b_all[d][:, A_HEADS + h:A_HEADS + h + 1] for d, h in combos]
    b_last = [b_end[d][:, A_HEADS + h:A_HEADS + h + 1] for d, h in combos]
    m_old = [m_ref[i][:, 0:1] for i in idx]
    r_row = [b_t[d][A_HEADS + h:A_HEADS + h + 1, :] - g_t[d][h:h + 1, :] for d, h in combos]
    m_row = [jnp.maximum(b_col[i] + m_old[i],
                         jnp.max(jnp.where(masks[d], b_col[i] - r_row[i], -jnp.inf), axis=1, keepdims=True))
             for i, (d, h) in enumerate(combos)]
    s = [(qk[i] * jnp.exp(jnp.where(masks[d], (b_col[i] - m_row[i]) - r_row[i], -jnp.inf))).astype(BF16)
         for i, (d, h) in enumerate(combos)]
    wq = [(jnp.exp(b_col[i] + m_old[i] - m_row[i]) * qs[i].astype(F32)).astype(BF16) for i in idx]
    g_col = [b_last[i] - b_col[i] + g_all[d][:, h:h + 1] for i, (d, h) in enumerate(combos)]
    m_new = [jnp.maximum(b_last[i] + m_old[i], jnp.max(g_col[i], axis=0, keepdims=True)) for i in idx]
    kw = [(jnp.exp(g_col[i] - m_new[i]) * ks[i].astype(F32)).astype(BF16) for i in idx]
    decay = [jnp.exp(b_last[i] + m_old[i] - m_new[i]) for i in idx]
    for i, (d, h) in enumerate(combos):
        v = ins[d][1][ins[d][5], :, h * A_DV:(h + 1) * A_DV]
        lhs = jnp.concatenate([s[i], wq[i]], axis=1)
        den = jnp.dot(lhs, jnp.concatenate([ones, n_ref[i].astype(BF16)], axis=0),
                      preferred_element_type=F32)[:, 0:1]
        rinv = 1.0 / jnp.maximum(jnp.abs(den), jnp.exp(-m_row[i]))
        num = jnp.dot(lhs, jnp.concatenate([v, ct_ref[i].astype(BF16)], axis=0), preferred_element_type=F32)
        ins[d][4][ins[d][5], :, h * A_DV:(h + 1) * A_DV] = (num * rinv).astype(ins[d][4].dtype)
    for i, (d, h) in enumerate(combos):
        v = ins[d][1][ins[d][5], :, h * A_DV:(h + 1) * A_DV]
        ct_ref[i] = decay[i] * ct_ref[i] + lax.dot_general(kw[i], v, tn_dims, preferred_element_type=F32)
        n_ref[i] = decay[i] * n_ref[i] + lax.dot_general(kw[i], ones, tn_dims, preferred_element_type=F32)
        m_ref[i] = jnp.broadcast_to(m_new[i], (1, LANES))


def _rev_chunk(s):
    ncc = CTX // LCH
    return jnp.where(s < ncc, ncc - 1 - s, NCH + ncc - 1 - s)


def _mlstm_scan(qk, p3, gates, lfp):
    fwd = lambda b, s: (b, s, 0)
    bwd = lambda b, s: (b, _rev_chunk(s), 0)
    specs = lambda im, imv: [pl.BlockSpec((SCAN_MB, LCH, D), im), pl.BlockSpec((SCAN_MB, LCH, A_V), imv),
                             pl.BlockSpec((SCAN_MB, LCH, LANES), im), pl.BlockSpec((SCAN_MB, LCH, LANES), im)]
    nstate = 2 * SCAN_MB * A_HEADS
    return pl.pallas_call(
        _mlstm_kernel,
        grid=(BATCH // SCAN_MB, NCH),
        in_specs=(specs(fwd, lambda b, s: (b, s, PB_V))
                  + specs(bwd, lambda b, s: (b, _rev_chunk(s), PB_V))),
        out_specs=[pl.BlockSpec((SCAN_MB, LCH, A_V), fwd), pl.BlockSpec((SCAN_MB, LCH, A_V), bwd)],
        out_shape=[jax.ShapeDtypeStruct((BATCH, TT, A_V), BF16)] * 2,
        scratch_shapes=[pltpu.VMEM((nstate, A_DQK, A_DV), F32),
                        pltpu.VMEM((nstate, A_DQK, LANES), F32),
                        pltpu.VMEM((nstate, 1, LANES), F32)],
        compiler_params=_cp(("parallel", "arbitrary")),
        name="mlstm_scan",
    )(qk, p3, gates, lfp, qk, p3, gates, lfp)


NPAIR = B_HEADS // 2
NT_DIMS = (((1,), (1,)), ((), ()))


def _stack_pair(q, low):
    zero = jnp.zeros_like(q)
    return jnp.concatenate([jnp.where(low, q, zero), jnp.where(low, zero, q)], axis=0)


def _softmax_rows(s_ref, p_ref):
    s = s_ref[...]
    p = jnp.exp2(s - jnp.max(s, axis=2, keepdims=True))
    p_ref[...] = p.astype(p_ref.dtype)
    return 1.0 / jnp.sum(p, axis=2, keepdims=True)


def _na_kernel(q_ref, k_ref, v_ref, *rest):
    bias_refs, (o_ref, s_ref, p_ref) = rest[:NA_SUB], rest[NA_SUB:]
    low = lax.broadcasted_iota(jnp.int32, (1, 2 * B_DH), 1) < B_DH
    starts = []
    for sub in range(NA_SUB):
        row = pl.program_id(1) * NA_SUB + sub
        rs = jnp.clip(row - NA_ROWS // 2, 0, GRID_H - NA_ROWS)
        starts.append(pl.multiple_of(CTX + rs * GRID_W, GRID_W))
    for sub in range(NA_SUB):
        qrows = pl.ds(sub * NA_NQ, NA_NQ)
        for hp in range(NPAIR):
            lo, hi = hp * 2 * B_DH, (hp + 1) * 2 * B_DH
            q2 = _stack_pair(q_ref[0, qrows, lo:hi], low)
            idx = sub * NPAIR + hp
            s_ref[idx, :, 0:NA_NK] = lax.dot_general(q2, k_ref[0, pl.ds(starts[sub], NA_NK), lo:hi], NT_DIMS,
                                                     preferred_element_type=F32
                                                     ) + bias_refs[sub][0, 0, hp].astype(F32)
            s_ref[idx, :, NA_NK:] = lax.dot_general(q2, k_ref[0, 0:CTX, lo:hi], NT_DIMS,
                                                    preferred_element_type=F32)
    rinv = _softmax_rows(s_ref, p_ref)
    for sub in range(NA_SUB):
        qrows = pl.ds(sub * NA_NQ, NA_NQ)
        for hp in range(NPAIR):
            lo, hi = hp * 2 * B_DH, (hp + 1) * 2 * B_DH
            idx = sub * NPAIR + hp
            o2 = (jnp.dot(p_ref[idx, :, 0:NA_NK], v_ref[0, pl.ds(starts[sub], NA_NK), lo:hi],
                          preferred_element_type=F32)
                  + jnp.dot(p_ref[idx, :, NA_NK:], v_ref[0, 0:CTX, lo:hi], preferred_element_type=F32))
            o2 = o2 * rinv[idx]
            o_ref[0, qrows, lo:hi] = jnp.where(low, o2[0:NA_NQ], o2[NA_NQ:]).astype(o_ref.dtype)


def _na_pattern(row):
    edge = NA_ROWS // 2
    return jnp.where(row < edge, row, jnp.where(row <= GRID_H - edge, edge, row - (GRID_H - 2 * edge)))


NA_NPAT = NA_ROWS
NA_SUB = 4


def _na_attention(p3, bias, layer):
    nq = NA_SUB * NA_NQ
    qoff = CTX // nq
    bias_specs = [pl.BlockSpec((1, 1, NPAIR, 2 * NA_NQ, NA_NK),
                               lambda b, i, sub=sub: (layer, _na_pattern(i * NA_SUB + sub), 0, 0, 0))
                  for sub in range(NA_SUB)]
    return pl.pallas_call(
        _na_kernel,
        grid=(BATCH, GRID_H // NA_SUB),
        in_specs=[pl.BlockSpec((1, nq, D), lambda b, i: (b, i + qoff, PB_NQ)),
                  pl.BlockSpec((1, TT, D), lambda b, i: (b, 0, PB_NK)),
                  pl.BlockSpec((1, TT, D), lambda b, i: (b, 0, PB_NV))] + bias_specs,
        out_specs=pl.BlockSpec((1, nq, D), lambda b, i: (b, i, 0)),
        out_shape=jax.ShapeDtypeStruct((BATCH, SEQ, D), BF16),
        scratch_shapes=[pltpu.VMEM((NA_SUB * NPAIR, 2 * NA_NQ, NA_NK + CTX), F32),
                        pltpu.VMEM((NA_SUB * NPAIR, 2 * NA_NQ, NA_NK + CTX), BF16)],
        compiler_params=_cp(("parallel", "arbitrary")),
        name="na_attention",
    )(p3, p3, p3, *([bias] * NA_SUB))


def _ctx_attn_kernel(q_ref, k_ref, v_ref, o_ref, s_ref, p_ref):
    low = lax.broadcasted_iota(jnp.int32, (1, 2 * B_DH), 1) < B_DH
    for hp in range(NPAIR):
        lo, hi = hp * 2 * B_DH, (hp + 1) * 2 * B_DH
        s_ref[hp] = lax.dot_general(_stack_pair(q_ref[0, :, lo:hi], low), k_ref[0, :, lo:hi], NT_DIMS,
                                    preferred_element_type=F32)
    rinv = _softmax_rows(s_ref, p_ref)
    for hp in range(NPAIR):
        lo, hi = hp * 2 * B_DH, (hp + 1) * 2 * B_DH
        o2 = jnp.dot(p_ref[hp], v_ref[0, :, lo:hi], preferred_element_type=F32) * rinv[hp]
        o_ref[0, :, lo:hi] = jnp.where(low, o2[0:CTX], o2[CTX:]).astype(o_ref.dtype)


def _ctx_attention(p3):
    return pl.pallas_call(
        _ctx_attn_kernel,
        grid=(BATCH,),
        in_specs=[pl.BlockSpec((1, CTX, D), lambda b: (b, 0, PB_NQ)),
                  pl.BlockSpec((1, CTX, D), lambda b: (b, 0, PB_NK)),
                  pl.BlockSpec((1, CTX, D), lambda b: (b, 0, PB_NV))],
        out_specs=pl.BlockSpec((1, CTX, D), lambda b: (b, 0, 0)),
        out_shape=jax.ShapeDtypeStruct((BATCH, CTX, D), BF16),
        scratch_shapes=[pltpu.VMEM((NPAIR, 2 * CTX, CTX), F32),
                        pltpu.VMEM((NPAIR, 2 * CTX, CTX), BF16)],
        compiler_params=_cp(("parallel",)),
        name="ctx_attention",
    )(p3, p3, p3)


def _merge_kernel(hf_ref, hbw_ref, o_ref, ga_ref, gb_ref, hnl_ref, hnc_ref, x_ref, mod_ref, ghn_ref,
                  wa_ref, wb_ref, wo_ref, g2_ref, x1_ref, h2_ref):
    rows = ROW_MB * RB
    flat = lambda ref: ref[...].reshape(rows, ref.shape[-1])
    hn = jnp.where(pl.program_id(1) == 0, flat(hnc_ref), flat(hnl_ref))
    bm = jnp.dot(hn, wb_ref[...], preferred_element_type=F32)
    hs = flat(hf_ref).astype(F32) + flat(hbw_ref).astype(F32)
    parts = []
    for h in range(A_HEADS):
        seg = hs[:, h * A_DV:(h + 1) * A_DV]
        mu = jnp.mean(seg, axis=-1, keepdims=True)
        cen = seg - mu
        var = jnp.mean(cen * cen, axis=-1, keepdims=True)
        parts.append(cen * lax.rsqrt(var + EPS))
    ya = jnp.concatenate(parts, axis=1) * ghn_ref[...] * _sigmoid(flat(o_ref).astype(F32))
    a = jnp.dot(ya.astype(BF16), wa_ref[...], preferred_element_type=F32)
    mrg = _sigmoid(flat(ga_ref).astype(F32)) * a + _sigmoid(flat(gb_ref).astype(F32)) * bm
    y = jnp.dot(mrg.astype(BF16), wo_ref[...], preferred_element_type=F32)
    for m in range(ROW_MB):
        mod = mod_ref[m]
        x1 = x_ref[m] + mod[2:3] * y[m * RB:(m + 1) * RB]
        x1_ref[m] = x1
        h2_ref[m] = _norm_mod(x1, g2_ref[...], mod[4:5], mod[3:4]).astype(h2_ref.dtype)


def _merge(hf, hbw, p3, hb_lat, hb_ctx, x, mod, ghn, wa, wb, wo, g2, h2_dtype):
    row = lambda b, i: (b, i, 0)
    full = lambda b, i: (0, 0)
    once = pl.Buffered(1)
    return pl.pallas_call(
        _merge_kernel,
        grid=(BATCH // ROW_MB, NRB),
        in_specs=[pl.BlockSpec((ROW_MB, RB, A_V), row),
                  pl.BlockSpec((ROW_MB, RB, A_V), row),
                  pl.BlockSpec((ROW_MB, RB, D), lambda b, i: (b, i, PB_O)),
                  pl.BlockSpec((ROW_MB, RB, D), lambda b, i: (b, i, PB_GA)),
                  pl.BlockSpec((ROW_MB, RB, D), lambda b, i: (b, i, PB_GB)),
                  pl.BlockSpec((ROW_MB, RB, D), lambda b, i: (b, jnp.maximum(i - 1, 0), 0)),
                  pl.BlockSpec((ROW_MB, CTX, D), lambda b, i: (b, 0, 0)),
                  pl.BlockSpec((ROW_MB, RB, D), row),
                  pl.BlockSpec((ROW_MB, 6, D), lambda b, i: (_mod_block(b, i), 0, 0)),
                  pl.BlockSpec((1, A_V), full),
                  pl.BlockSpec((A_V, D), full, pipeline_mode=once),
                  pl.BlockSpec((D, D), full, pipeline_mode=once),
                  pl.BlockSpec((D, D), full, pipeline_mode=once),
                  pl.BlockSpec((1, D), full)],
        out_specs=[pl.BlockSpec((ROW_MB, RB, D), row), pl.BlockSpec((ROW_MB, RB, D), row)],
        out_shape=[jax.ShapeDtypeStruct((BATCH, TT, D), F32),
                   jax.ShapeDtypeStruct((BATCH, TT, D), h2_dtype)],
        compiler_params=_cp(("parallel", "parallel")),
        name="merge",
    )(hf, hbw, p3, p3, p3, hb_lat, hb_ctx, x, _pad_mod(mod), ghn, wa, wb, wo, g2)


def _ffn_kernel(h_ref, x_ref, mod_ref, w1_ref, w3_ref, w2_ref, gn_ref, modn_ref, x2_ref, hn_ref):
    h = h_ref[...].reshape(ROW_MB * RB, D)
    a = jnp.dot(h, w1_ref[...], preferred_element_type=F32)
    g = jnp.dot(h, w3_ref[...], preferred_element_type=F32)
    y = jnp.dot((_silu(a) * g).astype(BF16), w2_ref[...], preferred_element_type=F32)
    for m in range(ROW_MB):
        mod = mod_ref[m]
        x2 = x_ref[m] + mod[5:6] * y[m * RB:(m + 1) * RB]
        x2_ref[m] = x2
        modn = modn_ref[m]
        hn_ref[m] = _norm_mod(x2, gn_ref[...], modn[1:2], modn[0:1]).astype(hn_ref.dtype)


def _pad_mod(mod):
    return jnp.concatenate([mod] + [mod[BATCH:]] * (ROW_MB - 1), axis=0)


def _mod_block(b, i):
    return jnp.where(i == 0, BATCH // ROW_MB, b)


def _dense_ffn(h2, x1, mod, w1, w3, w2, gn, modn):
    row = lambda b, i: (b, i, 0)
    full = lambda b, i: (0, 0)
    modspec = pl.BlockSpec((ROW_MB, 6, D), lambda b, i: (_mod_block(b, i), 0, 0))
    once = pl.Buffered(1)
    return pl.pallas_call(
        _ffn_kernel,
        grid=(BATCH // ROW_MB, NRB),
        in_specs=[pl.BlockSpec((ROW_MB, RB, D), row), pl.BlockSpec((ROW_MB, RB, D), row), modspec,
                  pl.BlockSpec((D, D_FF), full, pipeline_mode=once),
                  pl.BlockSpec((D, D_FF), full, pipeline_mode=once),
                  pl.BlockSpec((D_FF, D), full, pipeline_mode=once),
                  pl.BlockSpec((1, D), full), modspec],
        out_specs=[pl.BlockSpec((ROW_MB, RB, D), row), pl.BlockSpec((ROW_MB, RB, D), row)],
        out_shape=[jax.ShapeDtypeStruct((BATCH, TT, D), F32),
                   jax.ShapeDtypeStruct((BATCH, TT, D), BF16)],
        compiler_params=_cp(("parallel", "parallel")),
        name="dense_ffn",
    )(h2, x1, _pad_mod(mod), w1, w3, w2, gn, _pad_mod(modn))


def _router_kernel(h_ref, wr_ref, route_ref, cnt_ref, run_ref):
    i = pl.program_id(0)

    @pl.when(i == 0)
    def _():
        run_ref[...] = jnp.zeros_like(run_ref)

    logits = jnp.dot(h_ref[...].astype(BF16), wr_ref[...], preferred_element_type=F32)
    lane = lax.broadcasted_iota(jnp.int32, (RB, LANES), 1).astype(F32)
    lg = jnp.where(lane < N_EXPERTS, logits, -jnp.inf)
    v1 = jnp.max(lg, axis=1, keepdims=True)
    i1 = jnp.min(jnp.where(lg == v1, lane, float(LANES)), axis=1, keepdims=True)
    lg2 = jnp.where(lane == i1, -jnp.inf, lg)
    v2 = jnp.max(lg2, axis=1, keepdims=True)
    i2 = jnp.min(jnp.where(lg2 == v2, lane, float(LANES)), axis=1, keepdims=True)
    e = jnp.exp(v2 - v1)
    w1 = 1.0 / (1.0 + e)
    w2 = e / (1.0 + e)
    oh1 = (lane == i1).astype(F32)
    oh2 = (lane == i2).astype(F32)
    r = lax.broadcasted_iota(jnp.int32, (RB, RB), 0)
    c = lax.broadcasted_iota(jnp.int32, (RB, RB), 1)
    tri = (r > c).astype(BF16)
    cs1 = jnp.dot(tri, oh1.astype(BF16), preferred_element_type=F32)
    cs2 = jnp.dot(tri, oh2.astype(BF16), preferred_element_type=F32)
    tot1 = jnp.sum(oh1, axis=0, keepdims=True)
    tot2 = jnp.sum(oh2, axis=0, keepdims=True)
    run = run_ref[...]
    rank1 = jnp.sum(oh1 * (run + cs1), axis=1, keepdims=True)
    rank2 = jnp.sum(oh2 * (run + tot1 + cs2), axis=1, keepdims=True)
    new_run = run + tot1 + tot2
    run_ref[...] = new_run
    cnt_ref[...] = new_run
    out = jnp.where(lane == 0, i1,
          jnp.where(lane == 1, i2,
          jnp.where(lane == 2, w1,
          jnp.where(lane == 3, w2,
          jnp.where(lane == 4, rank1,
          jnp.where(lane == 5, rank2, 0.0))))))
    route_ref[...] = out


def _router(h2_flat, wr, nblk, blk_of):
    return pl.pallas_call(
        _router_kernel,
        grid=(nblk,),
        in_specs=[pl.BlockSpec((RB, D), lambda i: (blk_of(i), 0)),
                  pl.BlockSpec((D, LANES), lambda i: (0, 0))],
        out_specs=[pl.BlockSpec((RB, LANES), lambda i: (i, 0)),
                   pl.BlockSpec((1, LANES), lambda i: (0, 0))],
        out_shape=[jax.ShapeDtypeStruct((nblk * RB, LANES), F32),
                   jax.ShapeDtypeStruct((1, LANES), F32)],
        scratch_shapes=[pltpu.VMEM((1, LANES), F32)],
        compiler_params=_cp(("arbitrary",)),
        name="moe_router",
    )(h2_flat, wr)


DMA_UNROLL = 8
N_ZERO = 2 * N_EXPERTS


def _dispatch_kernel(pos_ref, zstart_ref, h_ref, o_ref, zbuf, stage, sems, zsem):
    @pl.when(pl.program_id(0) == 0)
    def _():
        zbuf[...] = jnp.zeros_like(zbuf)
        for z in range(N_ZERO):
            @pl.when(zstart_ref[z] >= 0)
            def _():
                zs = pl.multiple_of(zstart_ref[z], MOE_TM)
                pltpu.make_async_copy(zbuf, o_ref.at[pl.ds(zs, MOE_TM)], zsem).start()
        for z in range(N_ZERO):
            @pl.when(zstart_ref[z] >= 0)
            def _():
                pltpu.make_async_copy(zbuf, o_ref.at[pl.ds(0, MOE_TM)], zsem).wait()

    step = pl.program_id(0)
    slot = step % 2
    stage[slot] = h_ref[...]

    def issue(t, carry):
        src = stage.at[slot, pl.ds(t, 1)]
        pltpu.make_async_copy(src, o_ref.at[pl.ds(pos_ref[0, 0, t], 1)], sems.at[slot]).start(priority=0)
        pltpu.make_async_copy(src, o_ref.at[pl.ds(pos_ref[0, 0, RB + t], 1)], sems.at[slot]).start(priority=1)
        return carry

    lax.fori_loop(0, RB, issue, 0, unroll=DMA_UNROLL)

    def drain(sl):
        for _ in range(2):
            pltpu.make_async_copy(stage.at[sl], o_ref.at[pl.ds(0, RB)], sems.at[sl]).wait()

    @pl.when(step > 0)
    def _():
        drain(1 - slot)

    @pl.when(step == pl.num_programs(0) - 1)
    def _():
        drain(slot)


def _dispatch(pos, zstart, h2_flat, nblk, blk_of, n_sorted):
    return pl.pallas_call(
        _dispatch_kernel,
        grid=(nblk,),
        in_specs=[pl.BlockSpec((1, 1, 2 * RB), lambda i: (i, 0, 0), memory_space=pltpu.SMEM),
                  pl.BlockSpec(memory_space=pltpu.SMEM),
                  pl.BlockSpec((RB, D), lambda i: (blk_of(i), 0))],
        out_specs=pl.BlockSpec(memory_space=pl.ANY),
        out_shape=jax.ShapeDtypeStruct((n_sorted, D), F32),
        scratch_shapes=[pltpu.VMEM((MOE_TM, D), F32), pltpu.VMEM((2, RB, D), F32),
                        pltpu.SemaphoreType.DMA((2,)), pltpu.SemaphoreType.DMA(())],
        compiler_params=_cp(("arbitrary",)),
        name="moe_dispatch",
    )(pos, zstart, h2_flat)


def _gmm_kernel(be_ref, nb_ref, a_ref, w1_ref, w3_ref, w2_ref, o_ref, abf_ref):
    del be_ref
    i = pl.program_id(0)
    f = pl.program_id(1)

    @pl.when((i < nb_ref[0]) & (f == 0))
    def _():
        abf_ref[...] = a_ref[...].astype(BF16)

    @pl.when(i < nb_ref[0])
    def _():
        a = abf_ref[...]
        gs = []
        for c in range(MOE_FC // MOE_SUB):
            lo, hi = c * MOE_SUB, (c + 1) * MOE_SUB
            h1 = jnp.dot(a, w1_ref[0, :, lo:hi].astype(BF16), preferred_element_type=F32)
            h3 = jnp.dot(a, w3_ref[0, :, lo:hi].astype(BF16), preferred_element_type=F32)
            gs.append((_silu(h1) * h3).astype(BF16))
        part = jnp.dot(jnp.concatenate(gs, axis=1), w2_ref[0].astype(BF16), preferred_element_type=F32)

        @pl.when(f == 0)
        def _():
            o_ref[...] = part

        @pl.when(f > 0)
        def _():
            o_ref[...] += part

    @pl.when((i >= nb_ref[0]) & (f == 0))
    def _():
        o_ref[...] = jnp.zeros_like(o_ref)


def _expert_ffn(blk_e, nb, hs, w1, w3, w2, j):
    nbmax = hs.shape[0] // MOE_TM
    nf = D_FF_EXPERT // MOE_FC

    def ieff(i, nb_ref):
        return jnp.minimum(i, nb_ref[0] - 1)

    def feff(i, f, nb_ref):
        return jnp.where(i < nb_ref[0], f, nf - 1)

    grid_spec = pltpu.PrefetchScalarGridSpec(
        num_scalar_prefetch=2,
        grid=(nbmax, nf),
        in_specs=[pl.BlockSpec((MOE_TM, D), lambda i, f, be, nbr: (ieff(i, nbr), 0)),
                  pl.BlockSpec((None, 1, D, MOE_FC),
                               lambda i, f, be, nbr: (j, be[ieff(i, nbr)], 0, feff(i, f, nbr))),
                  pl.BlockSpec((None, 1, D, MOE_FC),
                               lambda i, f, be, nbr: (j, be[ieff(i, nbr)], 0, feff(i, f, nbr))),
                  pl.BlockSpec((None, 1, MOE_FC, D),
                               lambda i, f, be, nbr: (j, be[ieff(i, nbr)], feff(i, f, nbr), 0))],
        out_specs=pl.BlockSpec((MOE_TM, D), lambda i, f, be, nbr: (i, 0)),
        scratch_shapes=[pltpu.VMEM((MOE_TM, D), BF16)],
    )
    return pl.pallas_call(
        _gmm_kernel,
        grid_spec=grid_spec,
        out_shape=jax.ShapeDtypeStruct(hs.shape, F32),
        compiler_params=_cp(("arbitrary", "arbitrary")),
        name="moe_expert_ffn",
    )(blk_e, nb, hs, w1, w3, w2)


def _combine_kernel(pos_ref, posn_ref, y_ref, route_ref, x_ref, mod_ref, gn_ref, modn_ref,
                    x2_ref, hn_ref, ybuf, sems):
    t = pl.program_id(0) * pl.num_programs(1) + pl.program_id(1)
    nsteps = pl.num_programs(0) * pl.num_programs(1)
    slot = t % 2

    def start_block(p_ref, sl):
        def issue(r, carry):
            pltpu.make_async_copy(y_ref.at[pl.ds(p_ref[0, 0, r], 1)],
                                  ybuf.at[sl, 0, pl.ds(r, 1)], sems.at[sl]).start(priority=0)
            pltpu.make_async_copy(y_ref.at[pl.ds(p_ref[0, 0, RB + r], 1)],
                                  ybuf.at[sl, 1, pl.ds(r, 1)], sems.at[sl]).start(priority=1)
            return carry
        lax.fori_loop(0, RB, issue, 0, unroll=DMA_UNROLL)

    @pl.when(t == 0)
    def _():
        start_block(pos_ref, 0)

    @pl.when(t + 1 < nsteps)
    def _():
        start_block(posn_ref, 1 - slot)

    for e in range(2):
        pltpu.make_async_copy(y_ref.at[pl.ds(0, RB)], ybuf.at[slot, e], sems.at[slot]).wait()
    route = route_ref[...]
    y = route[:, 2:3] * ybuf[slot, 0] + route[:, 3:4] * ybuf[slot, 1]
    mod = mod_ref[0]
    x2 = x_ref[0] + mod[5:6] * y
    x2_ref[0] = x2
    modn = modn_ref[0]
    hn_ref[0] = _norm_mod(x2, gn_ref[...], modn[1:2], modn[0:1]).astype(hn_ref.dtype)


def _combine(pos, ys, route, x1, mod, gn, modn, latent_only, hn_dtype):
    nrb = NRB - 1 if latent_only else NRB
    off = 1 if latent_only else 0
    rows = nrb * RB
    full = lambda b, i: (0, 0)
    modspec = pl.BlockSpec((1, 6, D), lambda b, i: (_mod_row(b, i + off), 0, 0))
    last = BATCH * nrb - 1
    return pl.pallas_call(
        _combine_kernel,
        grid=(BATCH, nrb),
        in_specs=[pl.BlockSpec((1, 1, 2 * RB), lambda b, i: (b * nrb + i, 0, 0), memory_space=pltpu.SMEM),
                  pl.BlockSpec((1, 1, 2 * RB), lambda b, i: (jnp.minimum(b * nrb + i + 1, last), 0, 0),
                               memory_space=pltpu.SMEM),
                  pl.BlockSpec(memory_space=pl.ANY),
                  pl.BlockSpec((RB, LANES), lambda b, i: (b * nrb + i, 0)),
                  pl.BlockSpec((1, RB, D), lambda b, i: (b, i + off, 0)),
                  modspec,
                  pl.BlockSpec((1, D), full),
                  modspec],
        out_specs=[pl.BlockSpec((1, RB, D), lambda b, i: (b, i, 0)),
                   pl.BlockSpec((1, RB, D), lambda b, i: (b, i, 0))],
        out_shape=[jax.ShapeDtypeStruct((BATCH, rows, D), F32),
                   jax.ShapeDtypeStruct((BATCH, rows, D), hn_dtype)],
        scratch_shapes=[pltpu.VMEM((2, 2, RB, D), F32), pltpu.SemaphoreType.DMA((2,))],
        compiler_params=_cp(("arbitrary", "arbitrary")),
        name="moe_combine",
    )(pos, pos, ys, route, x1, mod, gn, modn)


def _moe_ffn(h2, x1, mod, wr, w1, w3, w2, j, gn, modn, latent_only, hn_dtype):
    h2_flat = h2.reshape(NTOK, D)
    if latent_only:
        nrb = NRB - 1
        blk_of = lambda i: (i // nrb) * NRB + (i % nrb) + 1
    else:
        nrb = NRB
        blk_of = lambda i: i
    nblk = BATCH * nrb
    n_pairs = 2 * nblk * RB
    nbmax = n_pairs // MOE_TM + N_EXPERTS
    n_sorted = nbmax * MOE_TM

    route, counts = _router(h2_flat, wr, nblk, blk_of)
    cnt = counts[0, :N_EXPERTS].astype(jnp.int32)
    gsz = ((cnt + MOE_TM - 1) // MOE_TM) * MOE_TM
    gend = jnp.cumsum(gsz)
    goff = gend - gsz
    e12 = route[:, 0:2].astype(jnp.int32)
    rank = route[:, 4:6].astype(jnp.int32)
    pos = goff[e12] + rank
    pos = pos.reshape(nblk, RB, 2).transpose(0, 2, 1).reshape(nblk, 1, 2 * RB)
    nb = (gend[-1] // MOE_TM).reshape(1)
    blk_start = jnp.arange(nbmax, dtype=jnp.int32) * MOE_TM
    blk_e = jnp.minimum(jnp.sum(blk_start[:, None] >= gend[None, :], axis=1), N_EXPERTS - 1).astype(jnp.int32)

    trail = (nbmax - N_EXPERTS + jnp.arange(N_EXPERTS, dtype=jnp.int32)) * MOE_TM
    zstart = jnp.concatenate([jnp.where(cnt > 0, gend - MOE_TM, -1),
                              jnp.where(trail >= gend[-1], trail, -1)]).astype(jnp.int32)

    hs = _dispatch(pos, zstart, h2_flat, nblk, blk_of, n_sorted)
    ys = _expert_ffn(blk_e, nb, hs, w1, w3, w2, j)
    return _combine(pos, ys, route, x1, mod, gn, modn, latent_only, hn_dtype)


def _rope_tables():
    quarter = A_DQK // 4
    inv = 1.0 / (ROPE_BASE ** (jnp.arange(quarter, dtype=F32) / quarter))
    pos = jnp.arange(SEQ)
    rows = (pos // GRID_W).astype(F32)
    cols = (pos % GRID_W).astype(F32)
    ang_r = rows[:, None] * inv[None, :]
    ang_c = cols[:, None] * inv[None, :]
    cos = jnp.concatenate([jnp.cos(ang_r)] * 2 + [jnp.cos(ang_c)] * 2, axis=1)
    sin = jnp.concatenate([-jnp.sin(ang_r), jnp.sin(ang_r), -jnp.sin(ang_c), jnp.sin(ang_c)], axis=1)
    cos = jnp.concatenate([jnp.ones((CTX, A_DQK), F32), cos], axis=0)
    sin = jnp.concatenate([jnp.zeros((CTX, A_DQK), F32), sin], axis=0)
    kscale = A_DQK ** -0.5
    return (jnp.concatenate([cos, cos * kscale], axis=1),
            jnp.concatenate([sin, sin * kscale], axis=1))


def _na_bias_tables(rpb):
    rs = np.clip(np.arange(GRID_H) - NA_ROWS // 2, 0, GRID_H - NA_ROWS)
    cs = np.clip(np.arange(GRID_W) - NA_COLS // 2, 0, GRID_W - NA_COLS)
    rows = [0, 1, 2, 3, NA_ROWS // 2, GRID_H - 3, GRID_H - 2, GRID_H - 1]
    col = np.arange(GRID_W)
    dc = np.clip(col[None, :] - col[:, None] + NA_COLS - 1, 0, 2 * NA_COLS - 2)
    valid_c = (col[None, :] >= cs[:, None]) & (col[None, :] < cs[:, None] + NA_COLS)
    sel_c = np.eye(2 * NA_COLS - 1, dtype=np.float32)[dc]
    t = jnp.einsum('lhab,uvb->lhuav', rpb * LOG2E, jnp.asarray(sel_c),
                   precision=lax.Precision.HIGHEST)
    neg = np.where(valid_c, 0.0, -np.inf).astype(np.float32)[:, None, :]
    nl = rpb.shape[0]
    t = (t + jnp.asarray(neg)).reshape(nl, NPAIR, 2 * NA_NQ, (2 * NA_ROWS - 1) * GRID_W)
    first = [rs[r] - r + NA_ROWS - 1 for r in rows]
    return jnp.stack([t[..., d0 * GRID_W:d0 * GRID_W + NA_NK] for d0 in first], axis=1).astype(BF16)


def kernel(x, c, ctx, c_ctx, w_mod, b_mod, g_norm1, g_norm2, w_in, a_conv, a_gate_b, a_hnorm_g, na_rpb,
           w_br_a, w_br_b, w_out, ffn_w1, ffn_w3, ffn_w2, moe_router, moe_w1, moe_w3, moe_w2, g_final):
    cc = jnp.concatenate([c, c_ctx[None, :], jnp.zeros((16 - BATCH - 1, D), F32)], axis=0)
    mod_all = _modulation(cc, w_mod, b_mod).reshape(DEPTH, 16, 6, D)[:, :BATCH + 1]
    mod_zero = jnp.zeros((BATCH + 1, 6, D), F32)
    rope_c, rope_s = _rope_tables()
    na_bias = _na_bias_tables(na_rpb)

    xs = jnp.concatenate([ctx, x], axis=1)
    h1 = _first_norm(xs, g_norm1[0][None, :], mod_all[0])
    out = None
    for l in range(DEPTH):
        last = l == DEPTH - 1
        mod = mod_all[l]
        wl = w_in[l]
        g0 = 3 * D
        g1 = g0 + NGATE
        wp = jnp.concatenate([wl[:, :g0], wl[:, g1:g1 + D] * (B_DH ** -0.5 * LOG2E), wl[:, g1 + D:]],
                             axis=1).astype(BF16)
        wg = jnp.pad(wl[:, g0:g1], ((0, 0), (0, LANES - NGATE))).astype(BF16)
        bg = jnp.pad(a_gate_b[l], (0, LANES - NGATE))[None, :]

        h1_flat = h1.reshape(NTOK, D)
        p3 = _in_proj(h1_flat, wp).reshape(BATCH, TT, P_COLS)
        gates, lfp = _gate_proj(h1_flat, wg, bg)

        qk = _mlstm_prep(p3, a_conv[l], rope_c, rope_s)
        hf, hbw = _mlstm_scan(qk, p3, gates.reshape(BATCH, TT, LANES), lfp.reshape(BATCH, TT, LANES))
        hb_lat = _na_attention(p3, na_bias, l)
        hb_ctx = _ctx_attention(p3)

        moe = l % 2 == 1
        x1, h2 = _merge(hf, hbw, p3, hb_lat, hb_ctx, xs, mod, a_hnorm_g[l][None, :],
                        w_br_a[l].astype(BF16), w_br_b[l].astype(BF16), w_out[l].astype(BF16),
                        g_norm2[l][None, :], F32 if moe else BF16)
        if last:
            gn, modn = g_final[None, :], mod_zero
        else:
            gn, modn = g_norm1[l + 1][None, :], mod_all[l + 1]
        j = l // 2
        if not moe:
            xs, h1 = _dense_ffn(h2, x1, mod, ffn_w1[j].astype(BF16), ffn_w3[j].astype(BF16),
                                ffn_w2[j].astype(BF16), gn, modn)
        else:
            wr = jnp.pad(moe_router[j], ((0, 0), (0, LANES - N_EXPERTS))).astype(BF16)
            xs, h1 = _moe_ffn(h2, x1, mod, wr, moe_w1, moe_w3, moe_w2, j, gn, modn, last,
                              F32 if last else BF16)
            if last:
                out = h1
    return out
```

```python
import numpy as np
import jax
import jax.numpy as jnp
from jax import lax
from jax.experimental import pallas as pl
from jax.experimental.pallas import tpu as pltpu

F32 = jnp.float32
BF16 = jnp.bfloat16

D = 1024
BATCH = 8
SEQ = 2048
CTX = 256
TT = CTX + SEQ
NTOK = BATCH * TT
DEPTH = 4
GRID_W = 64
GRID_H = SEQ // GRID_W

A_HEADS = 4
A_DQK = 128
A_DV = 256
A_QK = A_HEADS * A_DQK
A_V = A_HEADS * A_DV
ROPE_BASE = 10000.0
LCH = 256
NCH = TT // LCH
SCAN_MB = 4

B_HEADS = 16
B_DH = 64
NA_ROWS = 8
NA_COLS = 16
NA_QROWS = 1
NA_KROWS = NA_QROWS + NA_ROWS - 1
NA_NQ = NA_QROWS * GRID_W
NA_NK = NA_KROWS * GRID_W

D_FF = 2816
N_EXPERTS = 8
D_FF_EXPERT = 3584
EPS = 1e-6
LOG2E = 1.4426950408889634

RB = 256
NRB = TT // RB
ROW_MB = 2
MM_TM = 2048
MOE_TM = 1024
MOE_FC = 512
MOE_SUB = 256
LANES = 128

PB_QK, PB_V, PB_O, PB_NQ, PB_NK, PB_NV, PB_GA, PB_GB = range(8)
P_COLS = 8 * D

V7X_VMEM_BYTES = 64 * 1024 * 1024
VMEM_LIMIT = V7X_VMEM_BYTES * 7 // 8


def _cp(sem, vmem=VMEM_LIMIT):
    return pltpu.CompilerParams(dimension_semantics=sem, vmem_limit_bytes=vmem)


def _sigmoid(x):
    return 1.0 / (1.0 + jnp.exp(-x))


def _silu(x):
    return x * _sigmoid(x)


def _log_sigmoid(x):
    return jnp.minimum(x, 0.0) - jnp.log(1.0 + jnp.exp(-jnp.abs(x)))


def _norm_mod(x, g, sc, sh):
    ms = jnp.mean(x * x, axis=-1, keepdims=True)
    y = x * lax.rsqrt(ms + EPS)
    return (y * g) * (1.0 + sc) + sh


def _mod_row(b, i):
    return jnp.where(i == 0, BATCH, b)


def _mod_kernel(c_ref, w_ref, b_ref, o_ref):
    c = c_ref[...]
    s = _silu(c).astype(BF16)
    o_ref[0] = jnp.dot(s, w_ref[0].astype(BF16), preferred_element_type=F32) + b_ref[0]


def _modulation(cc, w_mod, b_mod):
    tn = 2048
    nl = w_mod.shape[0]
    return pl.pallas_call(
        _mod_kernel,
        grid=(nl, 6 * D // tn),
        in_specs=[pl.BlockSpec((16, D), lambda l, j: (0, 0)),
                  pl.BlockSpec((1, D, tn), lambda l, j: (l, 0, j)),
                  pl.BlockSpec((1, 1, tn), lambda l, j: (l, 0, j))],
        out_specs=pl.BlockSpec((1, 16, tn), lambda l, j: (l, 0, j)),
        out_shape=jax.ShapeDtypeStruct((nl, 16, 6 * D), F32),
        compiler_params=_cp(("parallel", "parallel")),
        name="modulation",
    )(cc, w_mod, b_mod.reshape(nl, 1, 6 * D))


def _norm_kernel(x_ref, g_ref, mod_ref, o_ref):
    mod = mod_ref[0]
    o_ref[0] = _norm_mod(x_ref[0], g_ref[...], mod[1:2], mod[0:1]).astype(o_ref.dtype)


def _first_norm(x, g, mod):
    return pl.pallas_call(
        _norm_kernel,
        grid=(BATCH, NRB),
        in_specs=[pl.BlockSpec((1, RB, D), lambda b, i: (b, i, 0)),
                  pl.BlockSpec((1, D), lambda b, i: (0, 0)),
                  pl.BlockSpec((1, 6, D), lambda b, i: (_mod_row(b, i), 0, 0))],
        out_specs=pl.BlockSpec((1, RB, D), lambda b, i: (b, i, 0)),
        out_shape=jax.ShapeDtypeStruct((BATCH, TT, D), BF16),
        compiler_params=_cp(("parallel", "parallel")),
        name="first_norm",
    )(x, g, mod)


def _mm_kernel(a_ref, w_ref, o_ref):
    o_ref[...] = jnp.dot(a_ref[...], w_ref[...], preferred_element_type=F32).astype(o_ref.dtype)


def _in_proj(h, w):
    tn = 1024
    return pl.pallas_call(
        _mm_kernel,
        grid=(P_COLS // tn, NTOK // MM_TM),
        in_specs=[pl.BlockSpec((MM_TM, D), lambda j, i: (i, 0)),
                  pl.BlockSpec((D, tn), lambda j, i: (0, j))],
        out_specs=pl.BlockSpec((MM_TM, tn), lambda j, i: (i, j)),
        out_shape=jax.ShapeDtypeStruct((NTOK, P_COLS), BF16),
        compiler_params=_cp(("parallel", "parallel")),
        name="in_proj",
    )(h, w)


NGATE = 4 * A_HEADS


def _gate_kernel(a_ref, w_ref, b_ref, g_ref, lf_ref):
    g = jnp.dot(a_ref[...], w_ref[...], preferred_element_type=F32) + b_ref[...]
    g_ref[...] = g
    lf = _log_sigmoid(g)
    p0 = lf.astype(BF16).astype(F32)
    r1 = lf - p0
    p1 = r1.astype(BF16).astype(F32)
    p2 = (r1 - p1).astype(BF16).astype(F32)
    lane = lax.broadcasted_iota(jnp.int32, (1, LANES), 1)
    parts = jnp.where(lane < NGATE, p0,
                      jnp.where(lane < 2 * NGATE, pltpu.roll(p1, NGATE, 1),
                                jnp.where(lane < 3 * NGATE, pltpu.roll(p2, 2 * NGATE, 1), 0.0)))
    lf_ref[...] = parts.astype(BF16)


def _gate_proj(h, wg, bg):
    return pl.pallas_call(
        _gate_kernel,
        grid=(NTOK // MM_TM,),
        in_specs=[pl.BlockSpec((MM_TM, D), lambda i: (i, 0)),
                  pl.BlockSpec((D, LANES), lambda i: (0, 0)),
                  pl.BlockSpec((1, LANES), lambda i: (0, 0))],
        out_specs=[pl.BlockSpec((MM_TM, LANES), lambda i: (i, 0)),
                   pl.BlockSpec((MM_TM, LANES), lambda i: (i, 0))],
        out_shape=[jax.ShapeDtypeStruct((NTOK, LANES), F32),
                   jax.ShapeDtypeStruct((NTOK, LANES), BF16)],
        compiler_params=_cp(("parallel",)),
        name="gate_proj",
    )(h, wg, bg)


def _prep_kernel(u_ref, up_ref, un_ref, w_ref, c_ref, s_ref, o_ref):
    i = pl.program_id(1)
    u = u_ref[0].astype(F32)
    prev_row = jnp.where(i >= 2, up_ref[0, 15:16, :].astype(F32), 0.0)
    next_row = jnp.where((i >= 1) & (i <= NRB - 2), un_ref[0, 0:1, :].astype(F32), 0.0)
    rid = lax.broadcasted_iota(jnp.int32, (RB, 1), 0)
    u_m1 = jnp.where(rid == 0, prev_row, pltpu.roll(u, 1, 0))
    u_p1 = jnp.where(rid == RB - 1, next_row, pltpu.roll(u, RB - 1, 0))
    w = w_ref[...]
    y = w[0:1] * u_m1 + w[1:2] * u + w[2:3] * u_p1
    y = _silu(y)
    c = c_ref[...]
    s = s_ref[...]
    cfull = jnp.concatenate([c[:, :A_DQK]] * A_HEADS + [c[:, A_DQK:]] * A_HEADS, axis=1)
    sfull = jnp.concatenate([s[:, :A_DQK]] * A_HEADS + [s[:, A_DQK:]] * A_HEADS, axis=1)
    lane = lax.broadcasted_iota(jnp.int32, (1, 2 * A_QK), 1)
    partner = jnp.where((lane & 32) == 0,
                        pltpu.roll(y, 2 * A_QK - 32, 1), pltpu.roll(y, 32, 1))
    o_ref[0] = (y * cfull + partner * sfull).astype(o_ref.dtype)


def _mlstm_prep(p3, conv_w, rope_c, rope_s):
    nb16 = TT // 16
    return pl.pallas_call(
        _prep_kernel,
        grid=(BATCH, NRB),
        in_specs=[pl.BlockSpec((1, RB, D), lambda b, i: (b, i, PB_QK)),
                  pl.BlockSpec((1, 16, D), lambda b, i: (b, jnp.maximum(i * (RB // 16) - 1, 0), PB_QK)),
                  pl.BlockSpec((1, 16, D), lambda b, i: (b, jnp.minimum((i + 1) * (RB // 16), nb16 - 1), PB_QK)),
                  pl.BlockSpec((3, D), lambda b, i: (0, 0)),
                  pl.BlockSpec((RB, 2 * A_DQK), lambda b, i: (i, 0)),
                  pl.BlockSpec((RB, 2 * A_DQK), lambda b, i: (i, 0))],
        out_specs=pl.BlockSpec((1, RB, D), lambda b, i: (b, i, 0)),
        out_shape=jax.ShapeDtypeStruct((BATCH, TT, D), BF16),
        compiler_params=_cp(("parallel", "parallel")),
        name="mlstm_prep",
    )(p3, p3, p3, conv_w, rope_c, rope_s)


def _mlstm_kernel(qkf_ref, vf_ref, gf_ref, lff_ref, qkb_ref, vb_ref, gb_ref, lfb_ref,
                  of_ref, ob_ref, ct_ref, n_ref, m_ref):
    nscan = 2 * SCAN_MB
    per_dir = [(qkf_ref, vf_ref, gf_ref, lff_ref, of_ref), (qkb_ref, vb_ref, gb_ref, lfb_ref, ob_ref)]
    ins = [per_dir[d % 2] + (d // 2,) for d in range(nscan)]

    @pl.when(pl.program_id(1) == 0)
    def _():
        ct_ref[...] = jnp.zeros_like(ct_ref)
        n_ref[...] = jnp.zeros_like(n_ref)
        m_ref[...] = jnp.zeros_like(m_ref)

    r = lax.broadcasted_iota(jnp.int32, (LCH, LCH), 0)
    c = lax.broadcasted_iota(jnp.int32, (LCH, LCH), 1)
    dir_masks = [c <= r, c >= r]
    masks = [dir_masks[d % 2] for d in range(nscan)]
    b_all, g_all, b_t, g_t, b_end = [], [], [], [], []
    for d in range(nscan):
        tri = jnp.where(masks[d], 1.0, 0.0).astype(BF16)
        bc = jnp.dot(tri, ins[d][3][ins[d][5]], preferred_element_type=F32)
        ba = bc + pltpu.roll(bc, LANES - NGATE, 1) + pltpu.roll(bc, LANES - 2 * NGATE, 1)
        ga = ins[d][2][ins[d][5]]
        if d % 2 == 1:
            ba = pltpu.roll(ba, LANES - 2 * A_HEADS, 1)
            ga = pltpu.roll(ga, LANES - 2 * A_HEADS, 1)
        b_all.append(ba)
        g_all.append(ga)
        b_t.append(ba.T)
        g_t.append(ga.T)
        b_end.append(ba[LCH - 1:LCH, :] if d % 2 == 0 else ba[0:1, :])
    ones = jnp.ones((LCH, LANES), BF16)
    tn_dims = (((0,), (0,)), ((), ()))
    combos = [(d, h) for d in range(nscan) for h in range(A_HEADS)]
    idx = range(len(combos))
    qs = [ins[d][0][ins[d][5], :, h * A_DQK:(h + 1) * A_DQK] for d, h in combos]
    ks = [ins[d][0][ins[d][5], :, A_QK + h * A_DQK:A_QK + (h + 1) * A_DQK] for d, h in combos]
    qk = [lax.dot_general(qs[i], ks[i], NT_DIMS, preferred_element_type=F32) for i in idx]
    b_col = [b_all[d][:, A_HEADS + h:A_HEADS + h + 1] for d, h in combos]
    b_last = [b_end[d][:, A_HEADS + h:A_HEADS + h + 1] for d, h in combos]
    m_old = [m_ref[i][:, 0:1] for i in idx]
    r_row = [b_t[d][A_HEADS + h:A_HEADS + h + 1, :] - g_t[d][h:h + 1, :] for d, h in combos]
    m_row = [jnp.maximum(b_col[i] + m_old[i],
                         jnp.max(jnp.where(masks[d], b_col[i] - r_row[i], -jnp.inf), axis=1, keepdims=True))
             for i, (d, h) in enumerate(combos)]
    s = [(qk[i] * jnp.exp(jnp.where(masks[d], (b_col[i] - m_row[i]) - r_row[i], -jnp.inf))).astype(BF16)
         for i, (d, h) in enumerate(combos)]
    wq = [(jnp.exp(b_col[i] + m_old[i] - m_row[i]) * qs[i].astype(F32)).astype(BF16) for i in idx]
    g_col = [b_last[i] - b_col[i] + g_all[d][:, h:h + 1] for i, (d, h) in enumerate(combos)]
    m_new = [jnp.maximum(b_last[i] + m_old[i], jnp.max(g_col[i], axis=0, keepdims=True)) for i in idx]
    kw = [(jnp.exp(g_col[i] - m_new[i]) * ks[i].astype(F32)).astype(BF16) for i in idx]
    decay = [jnp.exp(b_last[i] + m_old[i] - m_new[i]) for i in idx]
    for i, (d, h) in enumerate(combos):
        v = ins[d][1][ins[d][5], :, h * A_DV:(h + 1) * A_DV]
        lhs = jnp.concatenate([s[i], wq[i]], axis=1)
        den = jnp.dot(lhs, jnp.concatenate([ones, n_ref[i].astype(BF16)], axis=0),
                      preferred_element_type=F32)[:, 0:1]
        rinv = 1.0 / jnp.maximum(jnp.abs(den), jnp.exp(-m_row[i]))
        num = jnp.dot(lhs, jnp.concatenate([v, ct_ref[i].astype(BF16)], axis=0), preferred_element_type=F32)
        ins[d][4][ins[d][5], :, h * A_DV:(h + 1) * A_DV] = (num * rinv).astype(ins[d][4].dtype)
    for i, (d, h) in enumerate(combos):
        v = ins[d][1][ins[d][5], :, h * A_DV:(h + 1) * A_DV]
        ct_ref[i] = decay[i] * ct_ref[i] + lax.dot_general(kw[i], v, tn_dims, preferred_element_type=F32)
        n_ref[i] = decay[i] * n_ref[i] + lax.dot_general(kw[i], ones, tn_dims, preferred_element_type=F32)
        m_ref[i] = jnp.broadcast_to(m_new[i], (1, LANES))


def _rev_chunk(s):
    ncc = CTX // LCH
    return jnp.where(s < ncc, ncc - 1 - s, NCH + ncc - 1 - s)


def _mlstm_scan(qk, p3, gates, lfp):
    fwd = lambda b, s: (b, s, 0)
    bwd = lambda b, s: (b, _rev_chunk(s), 0)
    specs = lambda im, imv: [pl.BlockSpec((SCAN_MB, LCH, D), im), pl.BlockSpec((SCAN_MB, LCH, A_V), imv),
                             pl.BlockSpec((SCAN_MB, LCH, LANES), im), pl.BlockSpec((SCAN_MB, LCH, LANES), im)]
    nstate = 2 * SCAN_MB * A_HEADS
    return pl.pallas_call(
        _mlstm_kernel,
        grid=(BATCH // SCAN_MB, NCH),
        in_specs=(specs(fwd, lambda b, s: (b, s, PB_V))
                  + specs(bwd, lambda b, s: (b, _rev_chunk(s), PB_V))),
        out_specs=[pl.BlockSpec((SCAN_MB, LCH, A_V), fwd), pl.BlockSpec((SCAN_MB, LCH, A_V), bwd)],
        out_shape=[jax.ShapeDtypeStruct((BATCH, TT, A_V), BF16)] * 2,
        scratch_shapes=[pltpu.VMEM((nstate, A_DQK, A_DV), F32),
                        pltpu.VMEM((nstate, A_DQK, LANES), F32),
                        pltpu.VMEM((nstate, 1, LANES), F32)],
        compiler_params=_cp(("parallel", "arbitrary")),
        name="mlstm_scan",
    )(qk, p3, gates, lfp, qk, p3, gates, lfp)


NPAIR = B_HEADS // 2
NT_DIMS = (((1,), (1,)), ((), ()))


def _stack_pair(q, low):
    zero = jnp.zeros_like(q)
    return jnp.concatenate([jnp.where(low, q, zero), jnp.where(low, zero, q)], axis=0)


def _softmax_rows(s_ref, p_ref):
    s = s_ref[...]
    p = jnp.exp2(s - jnp.max(s, axis=2, keepdims=True))
    p_ref[...] = p.astype(p_ref.dtype)
    return 1.0 / jnp.sum(p, axis=2, keepdims=True)


def _na_kernel(q_ref, k_ref, v_ref, *rest):
    bias_refs, (o_ref, s_ref, p_ref) = rest[:NA_SUB], rest[NA_SUB:]
    low = lax.broadcasted_iota(jnp.int32, (1, 2 * B_DH), 1) < B_DH
    starts = []
    for sub in range(NA_SUB):
        row = pl.program_id(1) * NA_SUB + sub
        rs = jnp.clip(row - NA_ROWS // 2, 0, GRID_H - NA_ROWS)
        starts.append(pl.multiple_of(CTX + rs * GRID_W, GRID_W))
    for sub in range(NA_SUB):
        qrows = pl.ds(sub * NA_NQ, NA_NQ)
        for hp in range(NPAIR):
            lo, hi = hp * 2 * B_DH, (hp + 1) * 2 * B_DH
            q2 = _stack_pair(q_ref[0, qrows, lo:hi], low)
            idx = sub * NPAIR + hp
            s_ref[idx, :, 0:NA_NK] = lax.dot_general(q2, k_ref[0, pl.ds(starts[sub], NA_NK), lo:hi], NT_DIMS,
                                                     preferred_element_type=F32
                                                     ) + bias_refs[sub][0, 0, hp].astype(F32)
            s_ref[idx, :, NA_NK:] = lax.dot_general(q2, k_ref[0, 0:CTX, lo:hi], NT_DIMS,
                                                    preferred_element_type=F32)
    rinv = _softmax_rows(s_ref, p_ref)
    for sub in range(NA_SUB):
        qrows = pl.ds(sub * NA_NQ, NA_NQ)
        for hp in range(NPAIR):
            lo, hi = hp * 2 * B_DH, (hp + 1) * 2 * B_DH
            idx = sub * NPAIR + hp
            o2 = (jnp.dot(p_ref[idx, :, 0:NA_NK], v_ref[0, pl.ds(starts[sub], NA_NK), lo:hi],
                          preferred_element_type=F32)
                  + jnp.dot(p_ref[idx, :, NA_NK:], v_ref[0, 0:CTX, lo:hi], preferred_element_type=F32))
            o2 = o2 * rinv[idx]
            o_ref[0, qrows, lo:hi] = jnp.where(low, o2[0:NA_NQ], o2[NA_NQ:]).astype(o_ref.dtype)


def _na_pattern(row):
    edge = NA_ROWS // 2
    return jnp.where(row < edge, row, jnp.where(row <= GRID_H - edge, edge, row - (GRID_H - 2 * edge)))


NA_NPAT = NA_ROWS
NA_SUB = 4


def _na_attention(p3, bias, layer):
    nq = NA_SUB * NA_NQ
    qoff = CTX // nq
    bias_specs = [pl.BlockSpec((1, 1, NPAIR, 2 * NA_NQ, NA_NK),
                               lambda b, i, sub=sub: (layer, _na_pattern(i * NA_SUB + sub), 0, 0, 0))
                  for sub in range(NA_SUB)]
    return pl.pallas_call(
        _na_kernel,
        grid=(BATCH, GRID_H // NA_SUB),
        in_specs=[pl.BlockSpec((1, nq, D), lambda b, i: (b, i + qoff, PB_NQ)),
                  pl.BlockSpec((1, TT, D), lambda b, i: (b, 0, PB_NK)),
                  pl.BlockSpec((1, TT, D), lambda b, i: (b, 0, PB_NV))] + bias_specs,
        out_specs=pl.BlockSpec((1, nq, D), lambda b, i: (b, i, 0)),
        out_shape=jax.ShapeDtypeStruct((BATCH, SEQ, D), BF16),
        scratch_shapes=[pltpu.VMEM((NA_SUB * NPAIR, 2 * NA_NQ, NA_NK + CTX), F32),
                        pltpu.VMEM((NA_SUB * NPAIR, 2 * NA_NQ, NA_NK + CTX), BF16)],
        compiler_params=_cp(("parallel", "arbitrary")),
        name="na_attention",
    )(p3, p3, p3, *([bias] * NA_SUB))


def _ctx_attn_kernel(q_ref, k_ref, v_ref, o_ref, s_ref, p_ref):
    low = lax.broadcasted_iota(jnp.int32, (1, 2 * B_DH), 1) < B_DH
    for hp in range(NPAIR):
        lo, hi = hp * 2 * B_DH, (hp + 1) * 2 * B_DH
        s_ref[hp] = lax.dot_general(_stack_pair(q_ref[0, :, lo:hi], low), k_ref[0, :, lo:hi], NT_DIMS,
                                    preferred_element_type=F32)
    rinv = _softmax_rows(s_ref, p_ref)
    for hp in range(NPAIR):
        lo, hi = hp * 2 * B_DH, (hp + 1) * 2 * B_DH
        o2 = jnp.dot(p_ref[hp], v_ref[0, :, lo:hi], preferred_element_type=F32) * rinv[hp]
        o_ref[0, :, lo:hi] = jnp.where(low, o2[0:CTX], o2[CTX:]).astype(o_ref.dtype)


def _ctx_attention(p3):
    return pl.pallas_call(
        _ctx_attn_kernel,
        grid=(BATCH,),
        in_specs=[pl.BlockSpec((1, CTX, D), lambda b: (b, 0, PB_NQ)),
                  pl.BlockSpec((1, CTX, D), lambda b: (b, 0, PB_NK)),
                  pl.BlockSpec((1, CTX, D), lambda b: (b, 0, PB_NV))],
        out_specs=pl.BlockSpec((1, CTX, D), lambda b: (b, 0, 0)),
        out_shape=jax.ShapeDtypeStruct((BATCH, CTX, D), BF16),
        scratch_shapes=[pltpu.VMEM((NPAIR, 2 * CTX, CTX), F32),
                        pltpu.VMEM((NPAIR, 2 * CTX, CTX), BF16)],
        compiler_params=_cp(("parallel",)),
        name="ctx_attention",
    )(p3, p3, p3)


def _merge_kernel(hf_ref, hbw_ref, o_ref, ga_ref, gb_ref, hnl_ref, hnc_ref, x_ref, mod_ref, ghn_ref,
                  wa_ref, wb_ref, wo_ref, g2_ref, x1_ref, h2_ref):
    rows = ROW_MB * RB
    flat = lambda ref: ref[...].reshape(rows, ref.shape[-1])
    hn = jnp.where(pl.program_id(1) == 0, flat(hnc_ref), flat(hnl_ref))
    bm = jnp.dot(hn, wb_ref[...], preferred_element_type=F32)
    hs = flat(hf_ref).astype(F32) + flat(hbw_ref).astype(F32)
    parts = []
    for h in range(A_HEADS):
        seg = hs[:, h * A_DV:(h + 1) * A_DV]
        mu = jnp.mean(seg, axis=-1, keepdims=True)
        cen = seg - mu
        var = jnp.mean(cen * cen, axis=-1, keepdims=True)
        parts.append(cen * lax.rsqrt(var + EPS))
    ya = jnp.concatenate(parts, axis=1) * ghn_ref[...] * _sigmoid(flat(o_ref).astype(F32))
    a = jnp.dot(ya.astype(BF16), wa_ref[...], preferred_element_type=F32)
    mrg = _sigmoid(flat(ga_ref).astype(F32)) * a + _sigmoid(flat(gb_ref).astype(F32)) * bm
    y = jnp.dot(mrg.astype(BF16), wo_ref[...], preferred_element_type=F32)
    for m in range(ROW_MB):
        mod = mod_ref[m]
        x1 = x_ref[m] + mod[2:3] * y[m * RB:(m + 1) * RB]
        x1_ref[m] = x1
        h2_ref[m] = _norm_mod(x1, g2_ref[...], mod[4:5], mod[3:4]).astype(h2_ref.dtype)


def _merge(hf, hbw, p3, hb_lat, hb_ctx, x, mod, ghn, wa, wb, wo, g2, h2_dtype):
    row = lambda b, i: (b, i, 0)
    full = lambda b, i: (0, 0)
    once = pl.Buffered(1)
    return pl.pallas_call(
        _merge_kernel,
        grid=(BATCH // ROW_MB, NRB),
        in_specs=[pl.BlockSpec((ROW_MB, RB, A_V), row),
                  pl.BlockSpec((ROW_MB, RB, A_V), row),
                  pl.BlockSpec((ROW_MB, RB, D), lambda b, i: (b, i, PB_O)),
                  pl.BlockSpec((ROW_MB, RB, D), lambda b, i: (b, i, PB_GA)),
                  pl.BlockSpec((ROW_MB, RB, D), lambda b, i: (b, i, PB_GB)),
                  pl.BlockSpec((ROW_MB, RB, D), lambda b, i: (b, jnp.maximum(i - 1, 0), 0)),
                  pl.BlockSpec((ROW_MB, CTX, D), lambda b, i: (b, 0, 0)),
                  pl.BlockSpec((ROW_MB, RB, D), row),
                  pl.BlockSpec((ROW_MB, 6, D), lambda b, i: (_mod_block(b, i), 0, 0)),
                  pl.BlockSpec((1, A_V), full),
                  pl.BlockSpec((A_V, D), full, pipeline_mode=once),
                  pl.BlockSpec((D, D), full, pipeline_mode=once),
                  pl.BlockSpec((D, D), full, pipeline_mode=once),
                  pl.BlockSpec((1, D), full)],
        out_specs=[pl.BlockSpec((ROW_MB, RB, D), row), pl.BlockSpec((ROW_MB, RB, D), row)],
        out_shape=[jax.ShapeDtypeStruct((BATCH, TT, D), F32),
                   jax.ShapeDtypeStruct((BATCH, TT, D), h2_dtype)],
        compiler_params=_cp(("parallel", "parallel")),
        name="merge",
    )(hf, hbw, p3, p3, p3, hb_lat, hb_ctx, x, _pad_mod(mod), ghn, wa, wb, wo, g2)


def _ffn_kernel(h_ref, x_ref, mod_ref, w1_ref, w3_ref, w2_ref, gn_ref, modn_ref, x2_ref, hn_ref):
    h = h_ref[...].reshape(ROW_MB * RB, D)
    a = jnp.dot(h, w1_ref[...], preferred_element_type=F32)
    g = jnp.dot(h, w3_ref[...], preferred_element_type=F32)
    y = jnp.dot((_silu(a) * g).astype(BF16), w2_ref[...], preferred_element_type=F32)
    for m in range(ROW_MB):
        mod = mod_ref[m]
        x2 = x_ref[m] + mod[5:6] * y[m * RB:(m + 1) * RB]
        x2_ref[m] = x2
        modn = modn_ref[m]
        hn_ref[m] = _norm_mod(x2, gn_ref[...], modn[1:2], modn[0:1]).astype(hn_ref.dtype)


def _pad_mod(mod):
    return jnp.concatenate([mod] + [mod[BATCH:]] * (ROW_MB - 1), axis=0)


def _mod_block(b, i):
    return jnp.where(i == 0, BATCH // ROW_MB, b)


def _dense_ffn(h2, x1, mod, w1, w3, w2, gn, modn):
    row = lambda b, i: (b, i, 0)
    full = lambda b, i: (0, 0)
    modspec = pl.BlockSpec((ROW_MB, 6, D), lambda b, i: (_mod_block(b, i), 0, 0))
    once = pl.Buffered(1)
    return pl.pallas_call(
        _ffn_kernel,
        grid=(BATCH // ROW_MB, NRB),
        in_specs=[pl.BlockSpec((ROW_MB, RB, D), row), pl.BlockSpec((ROW_MB, RB, D), row), modspec,
                  pl.BlockSpec((D, D_FF), full, pipeline_mode=once),
                  pl.BlockSpec((D, D_FF), full, pipeline_mode=once),
                  pl.BlockSpec((D_FF, D), full, pipeline_mode=once),
                  pl.BlockSpec((1, D), full), modspec],
        out_specs=[pl.BlockSpec((ROW_MB, RB, D), row), pl.BlockSpec((ROW_MB, RB, D), row)],
        out_shape=[jax.ShapeDtypeStruct((BATCH, TT, D), F32),
                   jax.ShapeDtypeStruct((BATCH, TT, D), BF16)],
        compiler_params=_cp(("parallel", "parallel")),
        name="dense_ffn",
    )(h2, x1, _pad_mod(mod), w1, w3, w2, gn, _pad_mod(modn))


def _router_kernel(h_ref, wr_ref, route_ref, cnt_ref, run_ref):
    i = pl.program_id(0)

    @pl.when(i == 0)
    def _():
        run_ref[...] = jnp.zeros_like(run_ref)

    logits = jnp.dot(h_ref[...].astype(BF16), wr_ref[...], preferred_element_type=F32)
    lane = lax.broadcasted_iota(jnp.int32, (RB, LANES), 1).astype(F32)
    lg = jnp.where(lane < N_EXPERTS, logits, -jnp.inf)
    v1 = jnp.max(lg, axis=1, keepdims=True)
    i1 = jnp.min(jnp.where(lg == v1, lane, float(LANES)), axis=1, keepdims=True)
    lg2 = jnp.where(lane == i1, -jnp.inf, lg)
    v2 = jnp.max(lg2, axis=1, keepdims=True)
    i2 = jnp.min(jnp.where(lg2 == v2, lane, float(LANES)), axis=1, keepdims=True)
    e = jnp.exp(v2 - v1)
    w1 = 1.0 / (1.0 + e)
    w2 = e / (1.0 + e)
    oh1 = (lane == i1).astype(F32)
    oh2 = (lane == i2).astype(F32)
    r = lax.broadcasted_iota(jnp.int32, (RB, RB), 0)
    c = lax.broadcasted_iota(jnp.int32, (RB, RB), 1)
    tri = (r > c).astype(BF16)
    cs1 = jnp.dot(tri, oh1.astype(BF16), preferred_element_type=F32)
    cs2 = jnp.dot(tri, oh2.astype(BF16), preferred_element_type=F32)
    tot1 = jnp.sum(oh1, axis=0, keepdims=True)
    tot2 = jnp.sum(oh2, axis=0, keepdims=True)
    run = run_ref[...]
    rank1 = jnp.sum(oh1 * (run + cs1), axis=1, keepdims=True)
    rank2 = jnp.sum(oh2 * (run + tot1 + cs2), axis=1, keepdims=True)
    new_run = run + tot1 + tot2
    run_ref[...] = new_run
    cnt_ref[...] = new_run
    out = jnp.where(lane == 0, i1,
          jnp.where(lane == 1, i2,
          jnp.where(lane == 2, w1,
          jnp.where(lane == 3, w2,
          jnp.where(lane == 4, rank1,
          jnp.where(lane == 5, rank2, 0.0))))))
    route_ref[...] = out


def _router(h2_flat, wr, nblk, blk_of):
    return pl.pallas_call(
        _router_kernel,
        grid=(nblk,),
        in_specs=[pl.BlockSpec((RB, D), lambda i: (blk_of(i), 0)),
                  pl.BlockSpec((D, LANES), lambda i: (0, 0))],
        out_specs=[pl.BlockSpec((RB, LANES), lambda i: (i, 0)),
                   pl.BlockSpec((1, LANES), lambda i: (0, 0))],
        out_shape=[jax.ShapeDtypeStruct((nblk * RB, LANES), F32),
                   jax.ShapeDtypeStruct((1, LANES), F32)],
        scratch_shapes=[pltpu.VMEM((1, LANES), F32)],
        compiler_params=_cp(("arbitrary",)),
        name="moe_router",
    )(h2_flat, wr)


DMA_UNROLL = 8
N_ZERO = 2 * N_EXPERTS


def _dispatch_kernel(pos_ref, zstart_ref, h_ref, o_ref, zbuf, stage, sems, zsem):
    @pl.when(pl.program_id(0) == 0)
    def _():
        zbuf[...] = jnp.zeros_like(zbuf)
        for z in range(N_ZERO):
            @pl.when(zstart_ref[z] >= 0)
            def _():
                zs = pl.multiple_of(zstart_ref[z], MOE_TM)
                pltpu.make_async_copy(zbuf, o_ref.at[pl.ds(zs, MOE_TM)], zsem).start()
        for z in range(N_ZERO):
            @pl.when(zstart_ref[z] >= 0)
            def _():
                pltpu.make_async_copy(zbuf, o_ref.at[pl.ds(0, MOE_TM)], zsem).wait()

    step = pl.program_id(0)
    slot = step % 2
    stage[slot] = h_ref[...]

    def issue(t, carry):
        src = stage.at[slot, pl.ds(t, 1)]
        pltpu.make_async_copy(src, o_ref.at[pl.ds(pos_ref[0, 0, t], 1)], sems.at[slot]).start(priority=0)
        pltpu.make_async_copy(src, o_ref.at[pl.ds(pos_ref[0, 0, RB + t], 1)], sems.at[slot]).start(priority=1)
        return carry

    lax.fori_loop(0, RB, issue, 0, unroll=DMA_UNROLL)

    def drain(sl):
        for _ in range(2):
            pltpu.make_async_copy(stage.at[sl], o_ref.at[pl.ds(0, RB)], sems.at[sl]).wait()

    @pl.when(step > 0)
    def _():
        drain(1 - slot)

    @pl.when(step == pl.num_programs(0) - 1)
    def _():
        drain(slot)


def _dispatch(pos, zstart, h2_flat, nblk, blk_of, n_sorted):
    return pl.pallas_call(
        _dispatch_kernel,
        grid=(nblk,),
        in_specs=[pl.BlockSpec((1, 1, 2 * RB), lambda i: (i, 0, 0), memory_space=pltpu.SMEM),
                  pl.BlockSpec(memory_space=pltpu.SMEM),
                  pl.BlockSpec((RB, D), lambda i: (blk_of(i), 0))],
        out_specs=pl.BlockSpec(memory_space=pl.ANY),
        out_shape=jax.ShapeDtypeStruct((n_sorted, D), F32),
        scratch_shapes=[pltpu.VMEM((MOE_TM, D), F32), pltpu.VMEM((2, RB, D), F32),
                        pltpu.SemaphoreType.DMA((2,)), pltpu.SemaphoreType.DMA(())],
        compiler_params=_cp(("arbitrary",)),
        name="moe_dispatch",
    )(pos, zstart, h2_flat)


def _gmm_kernel(be_ref, nb_ref, a_ref, w1_ref, w3_ref, w2_ref, o_ref, abf_ref):
    del be_ref
    i = pl.program_id(0)
    f = pl.program_id(1)

    @pl.when((i < nb_ref[0]) & (f == 0))
    def _():
        abf_ref[...] = a_ref[...].astype(BF16)

    @pl.when(i < nb_ref[0])
    def _():
        a = abf_ref[...]
        gs = []
        for c in range(MOE_FC // MOE_SUB):
            lo, hi = c * MOE_SUB, (c + 1) * MOE_SUB
            h1 = jnp.dot(a, w1_ref[0, :, lo:hi].astype(BF16), preferred_element_type=F32)
            h3 = jnp.dot(a, w3_ref[0, :, lo:hi].astype(BF16), preferred_element_type=F32)
            gs.append((_silu(h1) * h3).astype(BF16))
        part = jnp.dot(jnp.concatenate(gs, axis=1), w2_ref[0].astype(BF16), preferred_element_type=F32)

        @pl.when(f == 0)
        def _():
            o_ref[...] = part

        @pl.when(f > 0)
        def _():
            o_ref[...] += part

    @pl.when((i >= nb_ref[0]) & (f == 0))
    def _():
        o_ref[...] = jnp.zeros_like(o_ref)


def _expert_ffn(blk_e, nb, hs, w1, w3, w2, j):
    nbmax = hs.shape[0] // MOE_TM
    nf = D_FF_EXPERT // MOE_FC

    def ieff(i, nb_ref):
        return jnp.minimum(i, nb_ref[0] - 1)

    def feff(i, f, nb_ref):
        return jnp.where(i < nb_ref[0], f, nf - 1)

    grid_spec = pltpu.PrefetchScalarGridSpec(
        num_scalar_prefetch=2,
        grid=(nbmax, nf),
        in_specs=[pl.BlockSpec((MOE_TM, D), lambda i, f, be, nbr: (ieff(i, nbr), 0)),
                  pl.BlockSpec((None, 1, D, MOE_FC),
                               lambda i, f, be, nbr: (j, be[ieff(i, nbr)], 0, feff(i, f, nbr))),
                  pl.BlockSpec((None, 1, D, MOE_FC),
                               lambda i, f, be, nbr: (j, be[ieff(i, nbr)], 0, feff(i, f, nbr))),
                  pl.BlockSpec((None, 1, MOE_FC, D),
                               lambda i, f, be, nbr: (j, be[ieff(i, nbr)], feff(i, f, nbr), 0))],
        out_specs=pl.BlockSpec((MOE_TM, D), lambda i, f, be, nbr: (i, 0)),
        scratch_shapes=[pltpu.VMEM((MOE_TM, D), BF16)],
    )
    return pl.pallas_call(
        _gmm_kernel,
        grid_spec=grid_spec,
        out_shape=jax.ShapeDtypeStruct(hs.shape, F32),
        compiler_params=_cp(("arbitrary", "arbitrary")),
        name="moe_expert_ffn",
    )(blk_e, nb, hs, w1, w3, w2)


def _combine_kernel(pos_ref, posn_ref, y_ref, route_ref, x_ref, mod_ref, gn_ref, modn_ref,
                    x2_ref, hn_ref, ybuf, sems):
    t = pl.program_id(0) * pl.num_programs(1) + pl.program_id(1)
    nsteps = pl.num_programs(0) * pl.num_programs(1)
    slot = t % 2

    def start_block(p_ref, sl):
        def issue(r, carry):
            pltpu.make_async_copy(y_ref.at[pl.ds(p_ref[0, 0, r], 1)],
                                  ybuf.at[sl, 0, pl.ds(r, 1)], sems.at[sl]).start(priority=0)
            pltpu.make_async_copy(y_ref.at[pl.ds(p_ref[0, 0, RB + r], 1)],
                                  ybuf.at[sl, 1, pl.ds(r, 1)], sems.at[sl]).start(priority=1)
            return carry
        lax.fori_loop(0, RB, issue, 0, unroll=DMA_UNROLL)

    @pl.when(t == 0)
    def _():
        start_block(pos_ref, 0)

    @pl.when(t + 1 < nsteps)
    def _():
        start_block(posn_ref, 1 - slot)

    for e in range(2):
        pltpu.make_async_copy(y_ref.at[pl.ds(0, RB)], ybuf.at[slot, e], sems.at[slot]).wait()
    route = route_ref[...]
    y = route[:, 2:3] * ybuf[slot, 0] + route[:, 3:4] * ybuf[slot, 1]
    mod = mod_ref[0]
    x2 = x_ref[0] + mod[5:6] * y
    x2_ref[0] = x2
    modn = modn_ref[0]
    hn_ref[0] = _norm_mod(x2, gn_ref[...], modn[1:2], modn[0:1]).astype(hn_ref.dtype)


def _combine(pos, ys, route, x1, mod, gn, modn, latent_only, hn_dtype):
    nrb = NRB - 1 if latent_only else NRB
    off = 1 if latent_only else 0
    rows = nrb * RB
    full = lambda b, i: (0, 0)
    modspec = pl.BlockSpec((1, 6, D), lambda b, i: (_mod_row(b, i + off), 0, 0))
    last = BATCH * nrb - 1
    return pl.pallas_call(
        _combine_kernel,
        grid=(BATCH, nrb),
        in_specs=[pl.BlockSpec((1, 1, 2 * RB), lambda b, i: (b * nrb + i, 0, 0), memory_space=pltpu.SMEM),
                  pl.BlockSpec((1, 1, 2 * RB), lambda b, i: (jnp.minimum(b * nrb + i + 1, last), 0, 0),
                               memory_space=pltpu.SMEM),
                  pl.BlockSpec(memory_space=pl.ANY),
                  pl.BlockSpec((RB, LANES), lambda b, i: (b * nrb + i, 0)),
                  pl.BlockSpec((1, RB, D), lambda b, i: (b, i + off, 0)),
                  modspec,
                  pl.BlockSpec((1, D), full),
                  modspec],
        out_specs=[pl.BlockSpec((1, RB, D), lambda b, i: (b, i, 0)),
                   pl.BlockSpec((1, RB, D), lambda b, i: (b, i, 0))],
        out_shape=[jax.ShapeDtypeStruct((BATCH, rows, D), F32),
                   jax.ShapeDtypeStruct((BATCH, rows, D), hn_dtype)],
        scratch_shapes=[pltpu.VMEM((2, 2, RB, D), F32), pltpu.SemaphoreType.DMA((2,))],
        compiler_params=_cp(("arbitrary", "arbitrary")),
        name="moe_combine",
    )(pos, pos, ys, route, x1, mod, gn, modn)


def _moe_ffn(h2, x1, mod, wr, w1, w3, w2, j, gn, modn, latent_only, hn_dtype):
    h2_flat = h2.reshape(NTOK, D)
    if latent_only:
        nrb = NRB - 1
        blk_of = lambda i: (i // nrb) * NRB + (i % nrb) + 1
    else:
        nrb = NRB
        blk_of = lambda i: i
    nblk = BATCH * nrb
    n_pairs = 2 * nblk * RB
    nbmax = n_pairs // MOE_TM + N_EXPERTS
    n_sorted = nbmax * MOE_TM

    route, counts = _router(h2_flat, wr, nblk, blk_of)
    cnt = counts[0, :N_EXPERTS].astype(jnp.int32)
    gsz = ((cnt + MOE_TM - 1) // MOE_TM) * MOE_TM
    gend = jnp.cumsum(gsz)
    goff = gend - gsz
    e12 = route[:, 0:2].astype(jnp.int32)
    rank = route[:, 4:6].astype(jnp.int32)
    pos = goff[e12] + rank
    pos = pos.reshape(nblk, RB, 2).transpose(0, 2, 1).reshape(nblk, 1, 2 * RB)
    nb = (gend[-1] // MOE_TM).reshape(1)
    blk_start = jnp.arange(nbmax, dtype=jnp.int32) * MOE_TM
    blk_e = jnp.minimum(jnp.sum(blk_start[:, None] >= gend[None, :], axis=1), N_EXPERTS - 1).astype(jnp.int32)

    trail = (nbmax - N_EXPERTS + jnp.arange(N_EXPERTS, dtype=jnp.int32)) * MOE_TM
    zstart = jnp.concatenate([jnp.where(cnt > 0, gend - MOE_TM, -1),
                              jnp.where(trail >= gend[-1], trail, -1)]).astype(jnp.int32)

    hs = _dispatch(pos, zstart, h2_flat, nblk, blk_of, n_sorted)
    ys = _expert_ffn(blk_e, nb, hs, w1, w3, w2, j)
    return _combine(pos, ys, route, x1, mod, gn, modn, latent_only, hn_dtype)


def _rope_tables():
    quarter = A_DQK // 4
    inv = 1.0 / (ROPE_BASE ** (jnp.arange(quarter, dtype=F32) / quarter))
    pos = jnp.arange(SEQ)
    rows = (pos // GRID_W).astype(F32)
    cols = (pos % GRID_W).astype(F32)
    ang_r = rows[:, None] * inv[None, :]
    ang_c = cols[:, None] * inv[None, :]
    cos = jnp.concatenate([jnp.cos(ang_r)] * 2 + [jnp.cos(ang_c)] * 2, axis=1)
    sin = jnp.concatenate([-jnp.sin(ang_r), jnp.sin(ang_r), -jnp.sin(ang_c), jnp.sin(ang_c)], axis=1)
    cos = jnp.concatenate([jnp.ones((CTX, A_DQK), F32), cos], axis=0)
    sin = jnp.concatenate([jnp.zeros((CTX, A_DQK), F32), sin], axis=0)
    kscale = A_DQK ** -0.5
    return (jnp.concatenate([cos, cos * kscale], axis=1),
            jnp.concatenate([sin, sin * kscale], axis=1))


def _na_bias_tables(rpb):
    rs = np.clip(np.arange(GRID_H) - NA_ROWS // 2, 0, GRID_H - NA_ROWS)
    cs = np.clip(np.arange(GRID_W) - NA_COLS // 2, 0, GRID_W - NA_COLS)
    rows = [0, 1, 2, 3, NA_ROWS // 2, GRID_H - 3, GRID_H - 2, GRID_H - 1]
    col = np.arange(GRID_W)
    dc = np.clip(col[None, :] - col[:, None] + NA_COLS - 1, 0, 2 * NA_COLS - 2)
    valid_c = (col[None, :] >= cs[:, None]) & (col[None, :] < cs[:, None] + NA_COLS)
    sel_c = np.eye(2 * NA_COLS - 1, dtype=np.float32)[dc]
    t = jnp.einsum('lhab,uvb->lhuav', rpb * LOG2E, jnp.asarray(sel_c),
                   precision=lax.Precision.HIGHEST)
    neg = np.where(valid_c, 0.0, -np.inf).astype(np.float32)[:, None, :]
    nl = rpb.shape[0]
    t = (t + jnp.asarray(neg)).reshape(nl, NPAIR, 2 * NA_NQ, (2 * NA_ROWS - 1) * GRID_W)
    first = [rs[r] - r + NA_ROWS - 1 for r in rows]
    return jnp.stack([t[..., d0 * GRID_W:d0 * GRID_W + NA_NK] for d0 in first], axis=1).astype(BF16)


def kernel(x, c, ctx, c_ctx, w_mod, b_mod, g_norm1, g_norm2, w_in, a_conv, a_gate_b, a_hnorm_g, na_rpb,
           w_br_a, w_br_b, w_out, ffn_w1, ffn_w3, ffn_w2, moe_router, moe_w1, moe_w3, moe_w2, g_final):
    cc = jnp.concatenate([c, c_ctx[None, :], jnp.zeros((16 - BATCH - 1, D), F32)], axis=0)
    mod_all = _modulation(cc, w_mod, b_mod).reshape(DEPTH, 16, 6, D)[:, :BATCH + 1]
    mod_zero = jnp.zeros((BATCH + 1, 6, D), F32)
    rope_c, rope_s = _rope_tables()
    na_bias = _na_bias_tables(na_rpb)

    xs = jnp.concatenate([ctx, x], axis=1)
    h1 = _first_norm(xs, g_norm1[0][None, :], mod_all[0])
    out = None
    for l in range(DEPTH):
        last = l == DEPTH - 1
        mod = mod_all[l]
        wl = w_in[l]
        g0 = 3 * D
        g1 = g0 + NGATE
        wp = jnp.concatenate([wl[:, :g0], wl[:, g1:g1 + D] * (B_DH ** -0.5 * LOG2E), wl[:, g1 + D:]],
                             axis=1).astype(BF16)
        wg = jnp.pad(wl[:, g0:g1], ((0, 0), (0, LANES - NGATE))).astype(BF16)
        bg = jnp.pad(a_gate_b[l], (0, LANES - NGATE))[None, :]

        h1_flat = h1.reshape(NTOK, D)
        p3 = _in_proj(h1_flat, wp).reshape(BATCH, TT, P_COLS)
        gates, lfp = _gate_proj(h1_flat, wg, bg)

        qk = _mlstm_prep(p3, a_conv[l], rope_c, rope_s)
        hf, hbw = _mlstm_scan(qk, p3, gates.reshape(BATCH, TT, LANES), lfp.reshape(BATCH, TT, LANES))
        hb_lat = _na_attention(p3, na_bias, l)
        hb_ctx = _ctx_attention(p3)

        moe = l % 2 == 1
        x1, h2 = _merge(hf, hbw, p3, hb_lat, hb_ctx, xs, mod, a_hnorm_g[l][None, :],
                        w_br_a[l].astype(BF16), w_br_b[l].astype(BF16), w_out[l].astype(BF16),
                        g_norm2[l][None, :], F32 if moe else BF16)
        if last:
            gn, modn = g_final[None, :], mod_zero
        else:
            gn, modn = g_norm1[l + 1][None, :], mod_all[l + 1]
        j = l // 2
        if not moe:
            xs, h1 = _dense_ffn(h2, x1, mod, ffn_w1[j].astype(BF16), ffn_w3[j].astype(BF16),
                                ffn_w2[j].astype(BF16), gn, modn)
        else:
            wr = jnp.pad(moe_router[j], ((0, 0), (0, LANES - N_EXPERTS))).astype(BF16)
            xs, h1 = _moe_ffn(h2, x1, mod, wr, moe_w1, moe_w3, moe_w2, j, gn, modn, last,
                              F32 if last else BF16)
            if last:
                out = h1
    return out
```

```python
import functools

import numpy as np
import jax
import jax.numpy as jnp
from jax import lax
from jax.experimental import pallas as pl
from jax.experimental.pallas import tpu as pltpu

F32 = jnp.float32
BF16 = jnp.bfloat16

D = 1024
BATCH = 8
SEQ = 2048
CTX = 256
TT = CTX + SEQ
NTOK = BATCH * TT
DEPTH = 4
GRID_W = 64
GRID_H = SEQ // GRID_W

A_HEADS = 4
A_DQK = 128
A_DV = 256
A_QK = A_HEADS * A_DQK
A_V = A_HEADS * A_DV
ROPE_BASE = 10000.0
LCH = 256
NCH = TT // LCH
SCAN_MB = 4

B_HEADS = 16
B_DH = 64
NA_ROWS = 8
NA_COLS = 16
NA_QROWS = 1
NA_KROWS = NA_QROWS + NA_ROWS - 1
NA_NQ = NA_QROWS * GRID_W
NA_NK = NA_KROWS * GRID_W

D_FF = 2816
N_EXPERTS = 8
D_FF_EXPERT = 3584
EPS = 1e-6
LOG2E = 1.4426950408889634

RB = 256
NRB = TT // RB
ROW_MB = 2
MM_TM = 2048
MOE_TM = 1024
MOE_FC = 512
MOE_SUB = 256
LANES = 128

PB_QK, PB_V, PB_O, PB_NQ, PB_NK, PB_NV, PB_GA, PB_GB = range(8)
P_COLS = 8 * D

V7X_VMEM_BYTES = 64 * 1024 * 1024
VMEM_LIMIT = V7X_VMEM_BYTES * 7 // 8


def _cp(sem, vmem=VMEM_LIMIT):
    return pltpu.CompilerParams(dimension_semantics=sem, vmem_limit_bytes=vmem)


def _sigmoid(x):
    return 1.0 / (1.0 + jnp.exp(-x))


def _silu(x):
    return x * _sigmoid(x)


def _log_sigmoid(x):
    return jnp.minimum(x, 0.0) - jnp.log(1.0 + jnp.exp(-jnp.abs(x)))


def _norm_mod(x, g, sc, sh):
    ms = jnp.mean(x * x, axis=-1, keepdims=True)
    y = x * lax.rsqrt(ms + EPS)
    return (y * g) * (1.0 + sc) + sh


def _mod_row(b, i):
    return jnp.where(i == 0, BATCH, b)


def _mod_kernel(c_ref, w_ref, b_ref, o_ref):
    c = c_ref[...]
    s = _silu(c).astype(BF16)
    o_ref[0] = jnp.dot(s, w_ref[0].astype(BF16), preferred_element_type=F32) + b_ref[0]


def _modulation(cc, w_mod, b_mod):
    tn = 2048
    nl = w_mod.shape[0]
    return pl.pallas_call(
        _mod_kernel,
        grid=(nl, 6 * D // tn),
        in_specs=[pl.BlockSpec((16, D), lambda l, j: (0, 0)),
                  pl.BlockSpec((1, D, tn), lambda l, j: (l, 0, j)),
                  pl.BlockSpec((1, 1, tn), lambda l, j: (l, 0, j))],
        out_specs=pl.BlockSpec((1, 16, tn), lambda l, j: (l, 0, j)),
        out_shape=jax.ShapeDtypeStruct((nl, 16, 6 * D), F32),
        compiler_params=_cp(("parallel", "parallel")),
        name="modulation",
    )(cc, w_mod, b_mod.reshape(nl, 1, 6 * D))


def _norm_kernel(x_ref, g_ref, mod_ref, o_ref):
    mod = mod_ref[0]
    o_ref[0] = _norm_mod(x_ref[0], g_ref[...], mod[1:2], mod[0:1]).astype(o_ref.dtype)


def _first_norm(x, g, mod):
    return pl.pallas_call(
        _norm_kernel,
        grid=(BATCH, NRB),
        in_specs=[pl.BlockSpec((1, RB, D), lambda b, i: (b, i, 0)),
                  pl.BlockSpec((1, D), lambda b, i: (0, 0)),
                  pl.BlockSpec((1, 6, D), lambda b, i: (_mod_row(b, i), 0, 0))],
        out_specs=pl.BlockSpec((1, RB, D), lambda b, i: (b, i, 0)),
        out_shape=jax.ShapeDtypeStruct((BATCH, TT, D), BF16),
        compiler_params=_cp(("parallel", "parallel")),
        name="first_norm",
    )(x, g, mod)


def _mm_kernel(a_ref, w_ref, o_ref):
    o_ref[...] = jnp.dot(a_ref[...], w_ref[...], preferred_element_type=F32).astype(o_ref.dtype)


def _in_proj(h, w):
    tn = 1024
    return pl.pallas_call(
        _mm_kernel,
        grid=(P_COLS // tn, NTOK // MM_TM),
        in_specs=[pl.BlockSpec((MM_TM, D), lambda j, i: (i, 0)),
                  pl.BlockSpec((D, tn), lambda j, i: (0, j))],
        out_specs=pl.BlockSpec((MM_TM, tn), lambda j, i: (i, j)),
        out_shape=jax.ShapeDtypeStruct((NTOK, P_COLS), BF16),
        compiler_params=_cp(("parallel", "parallel")),
        name="in_proj",
    )(h, w)


NGATE = 4 * A_HEADS


def _gate_kernel(a_ref, w_ref, b_ref, g_ref, lf_ref):
    g = jnp.dot(a_ref[...], w_ref[...], preferred_element_type=F32) + b_ref[...]
    g_ref[...] = g
    lf = _log_sigmoid(g)
    p0 = lf.astype(BF16).astype(F32)
    r1 = lf - p0
    p1 = r1.astype(BF16).astype(F32)
    p2 = (r1 - p1).astype(BF16).astype(F32)
    lane = lax.broadcasted_iota(jnp.int32, (1, LANES), 1)
    parts = jnp.where(lane < NGATE, p0,
                      jnp.where(lane < 2 * NGATE, pltpu.roll(p1, NGATE, 1),
                                jnp.where(lane < 3 * NGATE, pltpu.roll(p2, 2 * NGATE, 1), 0.0)))
    lf_ref[...] = parts.astype(BF16)


def _gate_proj(h, wg, bg):
    return pl.pallas_call(
        _gate_kernel,
        grid=(NTOK // MM_TM,),
        in_specs=[pl.BlockSpec((MM_TM, D), lambda i: (i, 0)),
                  pl.BlockSpec((D, LANES), lambda i: (0, 0)),
                  pl.BlockSpec((1, LANES), lambda i: (0, 0))],
        out_specs=[pl.BlockSpec((MM_TM, LANES), lambda i: (i, 0)),
                   pl.BlockSpec((MM_TM, LANES), lambda i: (i, 0))],
        out_shape=[jax.ShapeDtypeStruct((NTOK, LANES), F32),
                   jax.ShapeDtypeStruct((NTOK, LANES), BF16)],
        compiler_params=_cp(("parallel",)),
        name="gate_proj",
    )(h, wg, bg)


def _prep_kernel(u_ref, up_ref, un_ref, w_ref, c_ref, s_ref, o_ref):
    i = pl.program_id(1)
    u = u_ref[0].astype(F32)
    prev_row = jnp.where(i >= 2, up_ref[0, 15:16, :].astype(F32), 0.0)
    next_row = jnp.where((i >= 1) & (i <= NRB - 2), un_ref[0, 0:1, :].astype(F32), 0.0)
    rid = lax.broadcasted_iota(jnp.int32, (RB, 1), 0)
    u_m1 = jnp.where(rid == 0, prev_row, pltpu.roll(u, 1, 0))
    u_p1 = jnp.where(rid == RB - 1, next_row, pltpu.roll(u, RB - 1, 0))
    w = w_ref[...]
    y = w[0:1] * u_m1 + w[1:2] * u + w[2:3] * u_p1
    y = _silu(y)
    c = c_ref[...]
    s = s_ref[...]
    cfull = jnp.concatenate([c[:, :A_DQK]] * A_HEADS + [c[:, A_DQK:]] * A_HEADS, axis=1)
    sfull = jnp.concatenate([s[:, :A_DQK]] * A_HEADS + [s[:, A_DQK:]] * A_HEADS, axis=1)
    lane = lax.broadcasted_iota(jnp.int32, (1, 2 * A_QK), 1)
    partner = jnp.where((lane & 32) == 0,
                        pltpu.roll(y, 2 * A_QK - 32, 1), pltpu.roll(y, 32, 1))
    o_ref[0] = (y * cfull + partner * sfull).astype(o_ref.dtype)


def _mlstm_prep(p3, conv_w, rope_c, rope_s):
    nb16 = TT // 16
    return pl.pallas_call(
        _prep_kernel,
        grid=(BATCH, NRB),
        in_specs=[pl.BlockSpec((1, RB, D), lambda b, i: (b, i, PB_QK)),
                  pl.BlockSpec((1, 16, D), lambda b, i: (b, jnp.maximum(i * (RB // 16) - 1, 0), PB_QK)),
                  pl.BlockSpec((1, 16, D), lambda b, i: (b, jnp.minimum((i + 1) * (RB // 16), nb16 - 1), PB_QK)),
                  pl.BlockSpec((3, D), lambda b, i: (0, 0)),
                  pl.BlockSpec((RB, 2 * A_DQK), lambda b, i: (i, 0)),
                  pl.BlockSpec((RB, 2 * A_DQK), lambda b, i: (i, 0))],
        out_specs=pl.BlockSpec((1, RB, D), lambda b, i: (b, i, 0)),
        out_shape=jax.ShapeDtypeStruct((BATCH, TT, D), BF16),
        compiler_params=_cp(("parallel", "parallel")),
        name="mlstm_prep",
    )(p3, p3, p3, conv_w, rope_c, rope_s)


def _mlstm_kernel(qkf_ref, vf_ref, gf_ref, lff_ref, qkb_ref, vb_ref, gb_ref, lfb_ref,
                  of_ref, ob_ref, ct_ref, n_ref, m_ref):
    nscan = 2 * SCAN_MB
    per_dir = [(qkf_ref, vf_ref, gf_ref, lff_ref, of_ref), (qkb_ref, vb_ref, gb_ref, lfb_ref, ob_ref)]
    ins = [per_dir[d % 2] + (d // 2,) for d in range(nscan)]

    @pl.when(pl.program_id(1) == 0)
    def _():
        ct_ref[...] = jnp.zeros_like(ct_ref)
        n_ref[...] = jnp.zeros_like(n_ref)
        m_ref[...] = jnp.zeros_like(m_ref)

    r = lax.broadcasted_iota(jnp.int32, (LCH, LCH), 0)
    c = lax.broadcasted_iota(jnp.int32, (LCH, LCH), 1)
    dir_masks = [c <= r, c >= r]
    masks = [dir_masks[d % 2] for d in range(nscan)]
    b_all, g_all, b_t, g_t, b_end = [], [], [], [], []
    for d in range(nscan):
        tri = jnp.where(masks[d], 1.0, 0.0).astype(BF16)
        bc = jnp.dot(tri, ins[d][3][ins[d][5]], preferred_element_type=F32)
        ba = bc + pltpu.roll(bc, LANES - NGATE, 1) + pltpu.roll(bc, LANES - 2 * NGATE, 1)
        ga = ins[d][2][ins[d][5]]
        if d % 2 == 1:
            ba = pltpu.roll(ba, LANES - 2 * A_HEADS, 1)
            ga = pltpu.roll(ga, LANES - 2 * A_HEADS, 1)
        b_all.append(ba)
        g_all.append(ga)
        b_t.append(ba.T)
        g_t.append(ga.T)
        b_end.append(ba[LCH - 1:LCH, :] if d % 2 == 0 else ba[0:1, :])
    ones = jnp.ones((LCH, LANES), BF16)
    tn_dims = (((0,), (0,)), ((), ()))
    combos = [(d, h) for d in range(nscan) for h in range(A_HEADS)]
    idx = range(len(combos))
    qs = [ins[d][0][ins[d][5], :, h * A_DQK:(h + 1) * A_DQK] for d, h in combos]
    ks = [ins[d][0][ins[d][5], :, A_QK + h * A_DQK:A_QK + (h + 1) * A_DQK] for d, h in combos]
    qk = [lax.dot_general(qs[i], ks[i], NT_DIMS, preferred_element_type=F32) for i in idx]
    b_col = [b_all[d][:, A_HEADS + h:A_HEADS + h + 1] for d, h in combos]
    b_last = [b_end[d][:, A_HEADS + h:A_HEADS + h + 1] for d, h in combos]
    m_old = [m_ref[i][:, 0:1] for i in idx]
    r_row = [b_t[d][A_HEADS + h:A_HEADS + h + 1, :] - g_t[d][h:h + 1, :] for d, h in combos]
    m_row = [jnp.maximum(b_col[i] + m_old[i],
                         jnp.max(jnp.where(masks[d], b_col[i] - r_row[i], -jnp.inf), axis=1, keepdims=True))
             for i, (d, h) in enumerate(combos)]
    s = [(qk[i] * jnp.exp(jnp.where(masks[d], (b_col[i] - m_row[i]) - r_row[i], -jnp.inf))).astype(BF16)
         for i, (d, h) in enumerate(combos)]
    wq = [(jnp.exp(b_col[i] + m_old[i] - m_row[i]) * qs[i].astype(F32)).astype(BF16) for i in idx]
    g_col = [b_last[i] - b_col[i] + g_all[d][:, h:h + 1] for i, (d, h) in enumerate(combos)]
    m_new = [jnp.maximum(b_last[i] + m_old[i], jnp.max(g_col[i], axis=0, keepdims=True)) for i in idx]
    kw = [(jnp.exp(g_col[i] - m_new[i]) * ks[i].astype(F32)).astype(BF16) for i in idx]
    decay = [jnp.exp(b_last[i] + m_old[i] - m_new[i]) for i in idx]
    for i, (d, h) in enumerate(combos):
        v = ins[d][1][ins[d][5], :, h * A_DV:(h + 1) * A_DV]
        lhs = jnp.concatenate([s[i], wq[i]], axis=1)
        den = jnp.dot(lhs, jnp.concatenate([ones, n_ref[i].astype(BF16)], axis=0),
                      preferred_element_type=F32)[:, 0:1]
        rinv = 1.0 / jnp.maximum(jnp.abs(den), jnp.exp(-m_row[i]))
        num = jnp.dot(lhs, jnp.concatenate([v, ct_ref[i].astype(BF16)], axis=0), preferred_element_type=F32)
        ins[d][4][ins[d][5], :, h * A_DV:(h + 1) * A_DV] = (num * rinv).astype(ins[d][4].dtype)
    for i, (d, h) in enumerate(combos):
        v = ins[d][1][ins[d][5], :, h * A_DV:(h + 1) * A_DV]
        ct_ref[i] = decay[i] * ct_ref[i] + lax.dot_general(kw[i], v, tn_dims, preferred_element_type=F32)
        n_ref[i] = decay[i] * n_ref[i] + lax.dot_general(kw[i], ones, tn_dims, preferred_element_type=F32)
        m_ref[i] = jnp.broadcast_to(m_new[i], (1, LANES))


def _rev_chunk(s):
    ncc = CTX // LCH
    return jnp.where(s < ncc, ncc - 1 - s, NCH + ncc - 1 - s)


def _mlstm_scan(qk, p3, gates, lfp):
    fwd = lambda b, s: (b, s, 0)
    bwd = lambda b, s: (b, _rev_chunk(s), 0)
    specs = lambda im, imv: [pl.BlockSpec((SCAN_MB, LCH, D), im), pl.BlockSpec((SCAN_MB, LCH, A_V), imv),
                             pl.BlockSpec((SCAN_MB, LCH, LANES), im), pl.BlockSpec((SCAN_MB, LCH, LANES), im)]
    nstate = 2 * SCAN_MB * A_HEADS
    return pl.pallas_call(
        _mlstm_kernel,
        grid=(BATCH // SCAN_MB, NCH),
        in_specs=(specs(fwd, lambda b, s: (b, s, PB_V))
                  + specs(bwd, lambda b, s: (b, _rev_chunk(s), PB_V))),
        out_specs=[pl.BlockSpec((SCAN_MB, LCH, A_V), fwd), pl.BlockSpec((SCAN_MB, LCH, A_V), bwd)],
        out_shape=[jax.ShapeDtypeStruct((BATCH, TT, A_V), BF16)] * 2,
        scratch_shapes=[pltpu.VMEM((nstate, A_DQK, A_DV), F32),
                        pltpu.VMEM((nstate, A_DQK, LANES), F32),
                        pltpu.VMEM((nstate, 1, LANES), F32)],
        compiler_params=_cp(("parallel", "arbitrary")),
        name="mlstm_scan",
    )(qk, p3, gates, lfp, qk, p3, gates, lfp)


NPAIR = B_HEADS // 2
NT_DIMS = (((1,), (1,)), ((), ()))


def _stack_pair(q, low):
    zero = jnp.zeros_like(q)
    return jnp.concatenate([jnp.where(low, q, zero), jnp.where(low, zero, q)], axis=0)


def _softmax_rows(s_ref, p_ref):
    s = s_ref[...]
    p = jnp.exp2(s - jnp.max(s, axis=2, keepdims=True))
    p_ref[...] = p.astype(p_ref.dtype)
    return 1.0 / jnp.sum(p, axis=2, keepdims=True)


def _na_kernel(q_ref, k_ref, v_ref, *rest):
    bias_refs, (o_ref, s_ref, p_ref) = rest[:NA_SUB], rest[NA_SUB:]
    low = lax.broadcasted_iota(jnp.int32, (1, 2 * B_DH), 1) < B_DH
    starts = []
    for sub in range(NA_SUB):
        row = pl.program_id(1) * NA_SUB + sub
        rs = jnp.clip(row - NA_ROWS // 2, 0, GRID_H - NA_ROWS)
        starts.append(pl.multiple_of(CTX + rs * GRID_W, GRID_W))
    for sub in range(NA_SUB):
        qrows = pl.ds(sub * NA_NQ, NA_NQ)
        for hp in range(NPAIR):
            lo, hi = hp * 2 * B_DH, (hp + 1) * 2 * B_DH
            q2 = _stack_pair(q_ref[0, qrows, lo:hi], low)
            idx = sub * NPAIR + hp
            s_ref[idx, :, 0:NA_NK] = lax.dot_general(q2, k_ref[0, pl.ds(starts[sub], NA_NK), lo:hi], NT_DIMS,
                                                     preferred_element_type=F32
                                                     ) + bias_refs[sub][0, 0, hp].astype(F32)
            s_ref[idx, :, NA_NK:] = lax.dot_general(q2, k_ref[0, 0:CTX, lo:hi], NT_DIMS,
                                                    preferred_element_type=F32)
    rinv = _softmax_rows(s_ref, p_ref)
    for sub in range(NA_SUB):
        qrows = pl.ds(sub * NA_NQ, NA_NQ)
        for hp in range(NPAIR):
            lo, hi = hp * 2 * B_DH, (hp + 1) * 2 * B_DH
            idx = sub * NPAIR + hp
            o2 = (jnp.dot(p_ref[idx, :, 0:NA_NK], v_ref[0, pl.ds(starts[sub], NA_NK), lo:hi],
                          preferred_element_type=F32)
                  + jnp.dot(p_ref[idx, :, NA_NK:], v_ref[0, 0:CTX, lo:hi], preferred_element_type=F32))
            o2 = o2 * rinv[idx]
            o_ref[0, qrows, lo:hi] = jnp.where(low, o2[0:NA_NQ], o2[NA_NQ:]).astype(o_ref.dtype)


def _na_pattern(row):
    edge = NA_ROWS // 2
    return jnp.where(row < edge, row, jnp.where(row <= GRID_H - edge, edge, row - (GRID_H - 2 * edge)))


NA_NPAT = NA_ROWS
NA_SUB = 4


def _na_attention(p3, bias, layer):
    nq = NA_SUB * NA_NQ
    qoff = CTX // nq
    bias_specs = [pl.BlockSpec((1, 1, NPAIR, 2 * NA_NQ, NA_NK),
                               lambda b, i, sub=sub: (layer, _na_pattern(i * NA_SUB + sub), 0, 0, 0))
                  for sub in range(NA_SUB)]
    return pl.pallas_call(
        _na_kernel,
        grid=(BATCH, GRID_H // NA_SUB),
        in_specs=[pl.BlockSpec((1, nq, D), lambda b, i: (b, i + qoff, PB_NQ)),
                  pl.BlockSpec((1, TT, D), lambda b, i: (b, 0, PB_NK)),
                  pl.BlockSpec((1, TT, D), lambda b, i: (b, 0, PB_NV))] + bias_specs,
        out_specs=pl.BlockSpec((1, nq, D), lambda b, i: (b, i, 0)),
        out_shape=jax.ShapeDtypeStruct((BATCH, SEQ, D), BF16),
        scratch_shapes=[pltpu.VMEM((NA_SUB * NPAIR, 2 * NA_NQ, NA_NK + CTX), F32),
                        pltpu.VMEM((NA_SUB * NPAIR, 2 * NA_NQ, NA_NK + CTX), BF16)],
        compiler_params=_cp(("parallel", "arbitrary")),
        name="na_attention",
    )(p3, p3, p3, *([bias] * NA_SUB))


def _ctx_attn_kernel(q_ref, k_ref, v_ref, o_ref, s_ref, p_ref):
    low = lax.broadcasted_iota(jnp.int32, (1, 2 * B_DH), 1) < B_DH
    for hp in range(NPAIR):
        lo, hi = hp * 2 * B_DH, (hp + 1) * 2 * B_DH
        s_ref[hp] = lax.dot_general(_stack_pair(q_ref[0, :, lo:hi], low), k_ref[0, :, lo:hi], NT_DIMS,
                                    preferred_element_type=F32)
    rinv = _softmax_rows(s_ref, p_ref)
    for hp in range(NPAIR):
        lo, hi = hp * 2 * B_DH, (hp + 1) * 2 * B_DH
        o2 = jnp.dot(p_ref[hp], v_ref[0, :, lo:hi], preferred_element_type=F32) * rinv[hp]
        o_ref[0, :, lo:hi] = jnp.where(low, o2[0:CTX], o2[CTX:]).astype(o_ref.dtype)


def _ctx_attention(p3):
    return pl.pallas_call(
        _ctx_attn_kernel,
        grid=(BATCH,),
        in_specs=[pl.BlockSpec((1, CTX, D), lambda b: (b, 0, PB_NQ)),
                  pl.BlockSpec((1, CTX, D), lambda b: (b, 0, PB_NK)),
                  pl.BlockSpec((1, CTX, D), lambda b: (b, 0, PB_NV))],
        out_specs=pl.BlockSpec((1, CTX, D), lambda b: (b, 0, 0)),
        out_shape=jax.ShapeDtypeStruct((BATCH, CTX, D), BF16),
        scratch_shapes=[pltpu.VMEM((NPAIR, 2 * CTX, CTX), F32),
                        pltpu.VMEM((NPAIR, 2 * CTX, CTX), BF16)],
        compiler_params=_cp(("parallel",)),
        name="ctx_attention",
    )(p3, p3, p3)


def _merge_kernel(hf_ref, hbw_ref, o_ref, ga_ref, gb_ref, hnl_ref, hnc_ref, x_ref, mod_ref, ghn_ref,
                  wa_ref, wb_ref, wo_ref, g2_ref, x1_ref, h2_ref):
    rows = ROW_MB * RB
    flat = lambda ref: ref[...].reshape(rows, ref.shape[-1])
    hn = jnp.where(pl.program_id(1) == 0, flat(hnc_ref), flat(hnl_ref))
    bm = jnp.dot(hn, wb_ref[...], preferred_element_type=F32)
    hs = flat(hf_ref).astype(F32) + flat(hbw_ref).astype(F32)
    parts = []
    for h in range(A_HEADS):
        seg = hs[:, h * A_DV:(h + 1) * A_DV]
        mu = jnp.mean(seg, axis=-1, keepdims=True)
        cen = seg - mu
        var = jnp.mean(cen * cen, axis=-1, keepdims=True)
        parts.append(cen * lax.rsqrt(var + EPS))
    ya = jnp.concatenate(parts, axis=1) * ghn_ref[...] * _sigmoid(flat(o_ref).astype(F32))
    a = jnp.dot(ya.astype(BF16), wa_ref[...], preferred_element_type=F32)
    mrg = _sigmoid(flat(ga_ref).astype(F32)) * a + _sigmoid(flat(gb_ref).astype(F32)) * bm
    y = jnp.dot(mrg.astype(BF16), wo_ref[...], preferred_element_type=F32)
    for m in range(ROW_MB):
        mod = mod_ref[m]
        x1 = x_ref[m] + mod[2:3] * y[m * RB:(m + 1) * RB]
        x1_ref[m] = x1
        h2_ref[m] = _norm_mod(x1, g2_ref[...], mod[4:5], mod[3:4]).astype(h2_ref.dtype)


def _merge(hf, hbw, p3, hb_lat, hb_ctx, x, mod, ghn, wa, wb, wo, g2, h2_dtype):
    row = lambda b, i: (b, i, 0)
    full = lambda b, i: (0, 0)
    once = pl.Buffered(1)
    return pl.pallas_call(
        _merge_kernel,
        grid=(BATCH // ROW_MB, NRB),
        in_specs=[pl.BlockSpec((ROW_MB, RB, A_V), row),
                  pl.BlockSpec((ROW_MB, RB, A_V), row),
                  pl.BlockSpec((ROW_MB, RB, D), lambda b, i: (b, i, PB_O)),
                  pl.BlockSpec((ROW_MB, RB, D), lambda b, i: (b, i, PB_GA)),
                  pl.BlockSpec((ROW_MB, RB, D), lambda b, i: (b, i, PB_GB)),
                  pl.BlockSpec((ROW_MB, RB, D), lambda b, i: (b, jnp.maximum(i - 1, 0), 0)),
                  pl.BlockSpec((ROW_MB, CTX, D), lambda b, i: (b, 0, 0)),
                  pl.BlockSpec((ROW_MB, RB, D), row),
                  pl.BlockSpec((ROW_MB, 6, D), lambda b, i: (_mod_block(b, i), 0, 0)),
                  pl.BlockSpec((1, A_V), full),
                  pl.BlockSpec((A_V, D), full, pipeline_mode=once),
                  pl.BlockSpec((D, D), full, pipeline_mode=once),
                  pl.BlockSpec((D, D), full, pipeline_mode=once),
                  pl.BlockSpec((1, D), full)],
        out_specs=[pl.BlockSpec((ROW_MB, RB, D), row), pl.BlockSpec((ROW_MB, RB, D), row)],
        out_shape=[jax.ShapeDtypeStruct((BATCH, TT, D), F32),
                   jax.ShapeDtypeStruct((BATCH, TT, D), h2_dtype)],
        compiler_params=_cp(("parallel", "parallel")),
        name="merge",
    )(hf, hbw, p3, p3, p3, hb_lat, hb_ctx, x, _pad_mod(mod), ghn, wa, wb, wo, g2)


def _ffn_kernel(h_ref, x_ref, mod_ref, w1_ref, w3_ref, w2_ref, gn_ref, modn_ref, x2_ref, hn_ref):
    h = h_ref[...].reshape(ROW_MB * RB, D)
    a = jnp.dot(h, w1_ref[...], preferred_element_type=F32)
    g = jnp.dot(h, w3_ref[...], preferred_element_type=F32)
    y = jnp.dot((_silu(a) * g).astype(BF16), w2_ref[...], preferred_element_type=F32)
    for m in range(ROW_MB):
        mod = mod_ref[m]
        x2 = x_ref[m] + mod[5:6] * y[m * RB:(m + 1) * RB]
        x2_ref[m] = x2
        modn = modn_ref[m]
        hn_ref[m] = _norm_mod(x2, gn_ref[...], modn[1:2], modn[0:1]).astype(hn_ref.dtype)


def _pad_mod(mod):
    return jnp.concatenate([mod] + [mod[BATCH:]] * (ROW_MB - 1), axis=0)


def _mod_block(b, i):
    return jnp.where(i == 0, BATCH // ROW_MB, b)


def _dense_ffn(h2, x1, mod, w1, w3, w2, gn, modn):
    row = lambda b, i: (b, i, 0)
    full = lambda b, i: (0, 0)
    modspec = pl.BlockSpec((ROW_MB, 6, D), lambda b, i: (_mod_block(b, i), 0, 0))
    once = pl.Buffered(1)
    return pl.pallas_call(
        _ffn_kernel,
        grid=(BATCH // ROW_MB, NRB),
        in_specs=[pl.BlockSpec((ROW_MB, RB, D), row), pl.BlockSpec((ROW_MB, RB, D), row), modspec,
                  pl.BlockSpec((D, D_FF), full, pipeline_mode=once),
                  pl.BlockSpec((D, D_FF), full, pipeline_mode=once),
                  pl.BlockSpec((D_FF, D), full, pipeline_mode=once),
                  pl.BlockSpec((1, D), full), modspec],
        out_specs=[pl.BlockSpec((ROW_MB, RB, D), row), pl.BlockSpec((ROW_MB, RB, D), row)],
        out_shape=[jax.ShapeDtypeStruct((BATCH, TT, D), F32),
                   jax.ShapeDtypeStruct((BATCH, TT, D), BF16)],
        compiler_params=_cp(("parallel", "parallel")),
        name="dense_ffn",
    )(h2, x1, _pad_mod(mod), w1, w3, w2, gn, _pad_mod(modn))


def _router_kernel(h_ref, wr_ref, route_ref, cnt_ref, run_ref):
    i = pl.program_id(0)

    @pl.when(i == 0)
    def _():
        run_ref[...] = jnp.zeros_like(run_ref)

    logits = jnp.dot(h_ref[...].astype(BF16), wr_ref[...], preferred_element_type=F32)
    lane = lax.broadcasted_iota(jnp.int32, (RB, LANES), 1).astype(F32)
    lg = jnp.where(lane < N_EXPERTS, logits, -jnp.inf)
    v1 = jnp.max(lg, axis=1, keepdims=True)
    i1 = jnp.min(jnp.where(lg == v1, lane, float(LANES)), axis=1, keepdims=True)
    lg2 = jnp.where(lane == i1, -jnp.inf, lg)
    v2 = jnp.max(lg2, axis=1, keepdims=True)
    i2 = jnp.min(jnp.where(lg2 == v2, lane, float(LANES)), axis=1, keepdims=True)
    e = jnp.exp(v2 - v1)
    w1 = 1.0 / (1.0 + e)
    w2 = e / (1.0 + e)
    oh1 = (lane == i1).astype(F32)
    oh2 = (lane == i2).astype(F32)
    r = lax.broadcasted_iota(jnp.int32, (RB, RB), 0)
    c = lax.broadcasted_iota(jnp.int32, (RB, RB), 1)
    tri = (r > c).astype(BF16)
    cs1 = jnp.dot(tri, oh1.astype(BF16), preferred_element_type=F32)
    cs2 = jnp.dot(tri, oh2.astype(BF16), preferred_element_type=F32)
    tot1 = jnp.sum(oh1, axis=0, keepdims=True)
    tot2 = jnp.sum(oh2, axis=0, keepdims=True)
    run = run_ref[...]
    rank1 = jnp.sum(oh1 * (run + cs1), axis=1, keepdims=True)
    rank2 = jnp.sum(oh2 * (run + tot1 + cs2), axis=1, keepdims=True)
    new_run = run + tot1 + tot2
    run_ref[...] = new_run
    cnt_ref[...] = new_run
    out = jnp.where(lane == 0, i1,
          jnp.where(lane == 1, i2,
          jnp.where(lane == 2, w1,
          jnp.where(lane == 3, w2,
          jnp.where(lane == 4, rank1,
          jnp.where(lane == 5, rank2, 0.0))))))
    route_ref[...] = out


def _router(h2_flat, wr, nblk, blk_of):
    return pl.pallas_call(
        _router_kernel,
        grid=(nblk,),
        in_specs=[pl.BlockSpec((RB, D), lambda i: (blk_of(i), 0)),
                  pl.BlockSpec((D, LANES), lambda i: (0, 0))],
        out_specs=[pl.BlockSpec((RB, LANES), lambda i: (i, 0)),
                   pl.BlockSpec((1, LANES), lambda i: (0, 0))],
        out_shape=[jax.ShapeDtypeStruct((nblk * RB, LANES), F32),
                   jax.ShapeDtypeStruct((1, LANES), F32)],
        scratch_shapes=[pltpu.VMEM((1, LANES), F32)],
        compiler_params=_cp(("arbitrary",)),
        name="moe_router",
    )(h2_flat, wr)


DMA_UNROLL = 8
N_ZERO = 2 * N_EXPERTS


def _dispatch_kernel(pos_ref, zstart_ref, h_ref, o_ref, zbuf, stage, sems, zsem):
    @pl.when(pl.program_id(0) == 0)
    def _():
        zbuf[...] = jnp.zeros_like(zbuf)
        for z in range(N_ZERO):
            @pl.when(zstart_ref[z] >= 0)
            def _():
                zs = pl.multiple_of(zstart_ref[z], MOE_TM)
                pltpu.make_async_copy(zbuf, o_ref.at[pl.ds(zs, MOE_TM)], zsem).start()
        for z in range(N_ZERO):
            @pl.when(zstart_ref[z] >= 0)
            def _():
                pltpu.make_async_copy(zbuf, o_ref.at[pl.ds(0, MOE_TM)], zsem).wait()

    step = pl.program_id(0)
    slot = step % 2
    stage[slot] = h_ref[...]

    def issue(t, carry):
        src = stage.at[slot, pl.ds(t, 1)]
        pltpu.make_async_copy(src, o_ref.at[pl.ds(pos_ref[0, 0, t], 1)], sems.at[slot]).start(priority=0)
        pltpu.make_async_copy(src, o_ref.at[pl.ds(pos_ref[0, 0, RB + t], 1)], sems.at[slot]).start(priority=1)
        return carry

    lax.fori_loop(0, RB, issue, 0, unroll=DMA_UNROLL)

    def drain(sl):
        for _ in range(2):
            pltpu.make_async_copy(stage.at[sl], o_ref.at[pl.ds(0, RB)], sems.at[sl]).wait()

    @pl.when(step > 0)
    def _():
        drain(1 - slot)

    @pl.when(step == pl.num_programs(0) - 1)
    def _():
        drain(slot)


def _dispatch(pos, zstart, h2_flat, nblk, blk_of, n_sorted):
    return pl.pallas_call(
        _dispatch_kernel,
        grid=(nblk,),
        in_specs=[pl.BlockSpec((1, 1, 2 * RB), lambda i: (i, 0, 0), memory_space=pltpu.SMEM),
                  pl.BlockSpec(memory_space=pltpu.SMEM),
                  pl.BlockSpec((RB, D), lambda i: (blk_of(i), 0))],
        out_specs=pl.BlockSpec(memory_space=pl.ANY),
        out_shape=jax.ShapeDtypeStruct((n_sorted, D), F32),
        scratch_shapes=[pltpu.VMEM((MOE_TM, D), F32), pltpu.VMEM((2, RB, D), F32),
                        pltpu.SemaphoreType.DMA((2,)), pltpu.SemaphoreType.DMA(())],
        compiler_params=_cp(("arbitrary",)),
        name="moe_dispatch",
    )(pos, zstart, h2_flat)


MOE_NF = D_FF_EXPERT // MOE_FC


def _gmm_kernel(be_ref, nb_ref, a_ref, w1_ref, w3_ref, w2_ref, o_ref, w1b, w3b, w2b, sems, *, layer):
    i = pl.program_id(0)
    nb = nb_ref[0]

    def copies(e, c, slot):
        cols = pl.ds(c * MOE_FC, MOE_FC)
        return (pltpu.make_async_copy(w1_ref.at[layer, e, :, cols], w1b.at[slot], sems.at[slot, 0]),
                pltpu.make_async_copy(w3_ref.at[layer, e, :, cols], w3b.at[slot], sems.at[slot, 1]),
                pltpu.make_async_copy(w2_ref.at[layer, e, cols, :], w2b.at[slot], sems.at[slot, 2]))

    @pl.when(i < nb)
    def _():
        e = be_ref[i]

        @pl.when(i == 0)
        def _():
            for cp in copies(e, 0, 0):
                cp.start()

        a = a_ref[...].astype(BF16)
        acc = None
        for c in range(MOE_NF):
            slot = (i + c) % 2
            for cp in copies(e, c, slot):
                cp.wait()
            if c + 1 < MOE_NF:
                for cp in copies(e, c + 1, 1 - slot):
                    cp.start()
            else:
                @pl.when(i + 1 < nb)
                def _():
                    for cp in copies(be_ref[i + 1], 0, 1 - slot):
                        cp.start()
            gs = []
            for s in range(MOE_FC // MOE_SUB):
                lo, hi = s * MOE_SUB, (s + 1) * MOE_SUB
                h1 = jnp.dot(a, w1b[slot, :, lo:hi].astype(BF16), preferred_element_type=F32)
                h3 = jnp.dot(a, w3b[slot, :, lo:hi].astype(BF16), preferred_element_type=F32)
                gs.append((_silu(h1) * h3).astype(BF16))
            part = jnp.dot(jnp.concatenate(gs, axis=1), w2b[slot].astype(BF16), preferred_element_type=F32)
            acc = part if acc is None else acc + part
        o_ref[...] = acc

    @pl.when(i >= nb)
    def _():
        o_ref[...] = jnp.zeros_like(o_ref)


def _expert_ffn(blk_e, nb, hs, w1, w3, w2, j):
    nbmax = hs.shape[0] // MOE_TM
    grid_spec = pltpu.PrefetchScalarGridSpec(
        num_scalar_prefetch=2,
        grid=(nbmax,),
        in_specs=[pl.BlockSpec((MOE_TM, D), lambda i, be, nbr: (jnp.minimum(i, nbr[0] - 1), 0)),
                  pl.BlockSpec(memory_space=pl.ANY),
                  pl.BlockSpec(memory_space=pl.ANY),
                  pl.BlockSpec(memory_space=pl.ANY)],
        out_specs=pl.BlockSpec((MOE_TM, D), lambda i, be, nbr: (i, 0)),
        scratch_shapes=[pltpu.VMEM((2, D, MOE_FC), F32), pltpu.VMEM((2, D, MOE_FC), F32),
                        pltpu.VMEM((2, MOE_FC, D), F32), pltpu.SemaphoreType.DMA((2, 3))],
    )
    return pl.pallas_call(
        functools.partial(_gmm_kernel, layer=j),
        grid_spec=grid_spec,
        out_shape=jax.ShapeDtypeStruct(hs.shape, F32),
        compiler_params=_cp(("arbitrary",)),
        name="moe_expert_ffn",
    )(blk_e, nb, hs, w1, w3, w2)


def _combine_kernel(pos_ref, posn_ref, y_ref, route_ref, x_ref, mod_ref, gn_ref, modn_ref,
                    x2_ref, hn_ref, ybuf, sems):
    t = pl.program_id(0) * pl.num_programs(1) + pl.program_id(1)
    nsteps = pl.num_programs(0) * pl.num_programs(1)
    slot = t % 2

    def start_block(p_ref, sl):
        def issue(r, carry):
            pltpu.make_async_copy(y_ref.at[pl.ds(p_ref[0, 0, r], 1)],
                                  ybuf.at[sl, 0, pl.ds(r, 1)], sems.at[sl]).start(priority=0)
            pltpu.make_async_copy(y_ref.at[pl.ds(p_ref[0, 0, RB + r], 1)],
                                  ybuf.at[sl, 1, pl.ds(r, 1)], sems.at[sl]).start(priority=1)
            return carry
        lax.fori_loop(0, RB, issue, 0, unroll=DMA_UNROLL)

    @pl.when(t == 0)
    def _():
        start_block(pos_ref, 0)

    @pl.when(t + 1 < nsteps)
    def _():
        start_block(posn_ref, 1 - slot)

    for e in range(2):
        pltpu.make_async_copy(y_ref.at[pl.ds(0, RB)], ybuf.at[slot, e], sems.at[slot]).wait()
    route = route_ref[...]
    y = route[:, 2:3] * ybuf[slot, 0] + route[:, 3:4] * ybuf[slot, 1]
    mod = mod_ref[0]
    x2 = x_ref[0] + mod[5:6] * y
    x2_ref[0] = x2
    modn = modn_ref[0]
    hn_ref[0] = _norm_mod(x2, gn_ref[...], modn[1:2], modn[0:1]).astype(hn_ref.dtype)


def _combine(pos, ys, route, x1, mod, gn, modn, latent_only, hn_dtype):
    nrb = NRB - 1 if latent_only else NRB
    off = 1 if latent_only else 0
    rows = nrb * RB
    full = lambda b, i: (0, 0)
    modspec = pl.BlockSpec((1, 6, D), lambda b, i: (_mod_row(b, i + off), 0, 0))
    last = BATCH * nrb - 1
    return pl.pallas_call(
        _combine_kernel,
        grid=(BATCH, nrb),
        in_specs=[pl.BlockSpec((1, 1, 2 * RB), lambda b, i: (b * nrb + i, 0, 0), memory_space=pltpu.SMEM),
                  pl.BlockSpec((1, 1, 2 * RB), lambda b, i: (jnp.minimum(b * nrb + i + 1, last), 0, 0),
                               memory_space=pltpu.SMEM),
                  pl.BlockSpec(memory_space=pl.ANY),
                  pl.BlockSpec((RB, LANES), lambda b, i: (b * nrb + i, 0)),
                  pl.BlockSpec((1, RB, D), lambda b, i: (b, i + off, 0)),
                  modspec,
                  pl.BlockSpec((1, D), full),
                  modspec],
        out_specs=[pl.BlockSpec((1, RB, D), lambda b, i: (b, i, 0)),
                   pl.BlockSpec((1, RB, D), lambda b, i: (b, i, 0))],
        out_shape=[jax.ShapeDtypeStruct((BATCH, rows, D), F32),
                   jax.ShapeDtypeStruct((BATCH, rows, D), hn_dtype)],
        scratch_shapes=[pltpu.VMEM((2, 2, RB, D), F32), pltpu.SemaphoreType.DMA((2,))],
        compiler_params=_cp(("arbitrary", "arbitrary")),
        name="moe_combine",
    )(pos, pos, ys, route, x1, mod, gn, modn)


def _moe_ffn(h2, x1, mod, wr, w1, w3, w2, j, gn, modn, latent_only, hn_dtype):
    h2_flat = h2.reshape(NTOK, D)
    if latent_only:
        nrb = NRB - 1
        blk_of = lambda i: (i // nrb) * NRB + (i % nrb) + 1
    else:
        nrb = NRB
        blk_of = lambda i: i
    nblk = BATCH * nrb
    n_pairs = 2 * nblk * RB
    nbmax = n_pairs // MOE_TM + N_EXPERTS
    n_sorted = nbmax * MOE_TM

    route, counts = _router(h2_flat, wr, nblk, blk_of)
    cnt = counts[0, :N_EXPERTS].astype(jnp.int32)
    gsz = ((cnt + MOE_TM - 1) // MOE_TM) * MOE_TM
    gend = jnp.cumsum(gsz)
    goff = gend - gsz
    e12 = route[:, 0:2].astype(jnp.int32)
    rank = route[:, 4:6].astype(jnp.int32)
    pos = goff[e12] + rank
    pos = pos.reshape(nblk, RB, 2).transpose(0, 2, 1).reshape(nblk, 1, 2 * RB)
    nb = (gend[-1] // MOE_TM).reshape(1)
    blk_start = jnp.arange(nbmax, dtype=jnp.int32) * MOE_TM
    blk_e = jnp.minimum(jnp.sum(blk_start[:, None] >= gend[None, :], axis=1), N_EXPERTS - 1).astype(jnp.int32)

    trail = (nbmax - N_EXPERTS + jnp.arange(N_EXPERTS, dtype=jnp.int32)) * MOE_TM
    zstart = jnp.concatenate([jnp.where(cnt > 0, gend - MOE_TM, -1),
                              jnp.where(trail >= gend[-1], trail, -1)]).astype(jnp.int32)

    hs = _dispatch(pos, zstart, h2_flat, nblk, blk_of, n_sorted)
    ys = _expert_ffn(blk_e, nb, hs, w1, w3, w2, j)
    return _combine(pos, ys, route, x1, mod, gn, modn, latent_only, hn_dtype)


def _rope_tables():
    quarter = A_DQK // 4
    inv = 1.0 / (ROPE_BASE ** (jnp.arange(quarter, dtype=F32) / quarter))
    pos = jnp.arange(SEQ)
    rows = (pos // GRID_W).astype(F32)
    cols = (pos % GRID_W).astype(F32)
    ang_r = rows[:, None] * inv[None, :]
    ang_c = cols[:, None] * inv[None, :]
    cos = jnp.concatenate([jnp.cos(ang_r)] * 2 + [jnp.cos(ang_c)] * 2, axis=1)
    sin = jnp.concatenate([-jnp.sin(ang_r), jnp.sin(ang_r), -jnp.sin(ang_c), jnp.sin(ang_c)], axis=1)
    cos = jnp.concatenate([jnp.ones((CTX, A_DQK), F32), cos], axis=0)
    sin = jnp.concatenate([jnp.zeros((CTX, A_DQK), F32), sin], axis=0)
    kscale = A_DQK ** -0.5
    return (jnp.concatenate([cos, cos * kscale], axis=1),
            jnp.concatenate([sin, sin * kscale], axis=1))


def _na_bias_tables(rpb):
    rs = np.clip(np.arange(GRID_H) - NA_ROWS // 2, 0, GRID_H - NA_ROWS)
    cs = np.clip(np.arange(GRID_W) - NA_COLS // 2, 0, GRID_W - NA_COLS)
    rows = [0, 1, 2, 3, NA_ROWS // 2, GRID_H - 3, GRID_H - 2, GRID_H - 1]
    col = np.arange(GRID_W)
    dc = np.clip(col[None, :] - col[:, None] + NA_COLS - 1, 0, 2 * NA_COLS - 2)
    valid_c = (col[None, :] >= cs[:, None]) & (col[None, :] < cs[:, None] + NA_COLS)
    sel_c = np.eye(2 * NA_COLS - 1, dtype=np.float32)[dc]
    t = jnp.einsum('lhab,uvb->lhuav', rpb * LOG2E, jnp.asarray(sel_c),
                   precision=lax.Precision.HIGHEST)
    neg = np.where(valid_c, 0.0, -np.inf).astype(np.float32)[:, None, :]
    nl = rpb.shape[0]
    t = (t + jnp.asarray(neg)).reshape(nl, NPAIR, 2 * NA_NQ, (2 * NA_ROWS - 1) * GRID_W)
    first = [rs[r] - r + NA_ROWS - 1 for r in rows]
    return jnp.stack([t[..., d0 * GRID_W:d0 * GRID_W + NA_NK] for d0 in first], axis=1).astype(BF16)


def kernel(x, c, ctx, c_ctx, w_mod, b_mod, g_norm1, g_norm2, w_in, a_conv, a_gate_b, a_hnorm_g, na_rpb,
           w_br_a, w_br_b, w_out, ffn_w1, ffn_w3, ffn_w2, moe_router, moe_w1, moe_w3, moe_w2, g_final):
    cc = jnp.concatenate([c, c_ctx[None, :], jnp.zeros((16 - BATCH - 1, D), F32)], axis=0)
    mod_all = _modulation(cc, w_mod, b_mod).reshape(DEPTH, 16, 6, D)[:, :BATCH + 1]
    mod_zero = jnp.zeros((BATCH + 1, 6, D), F32)
    rope_c, rope_s = _rope_tables()
    na_bias = _na_bias_tables(na_rpb)

    xs = jnp.concatenate([ctx, x], axis=1)
    h1 = _first_norm(xs, g_norm1[0][None, :], mod_all[0])
    out = None
    for l in range(DEPTH):
        last = l == DEPTH - 1
        mod = mod_all[l]
        wl = w_in[l]
        g0 = 3 * D
        g1 = g0 + NGATE
        wp = jnp.concatenate([wl[:, :g0], wl[:, g1:g1 + D] * (B_DH ** -0.5 * LOG2E), wl[:, g1 + D:]],
                             axis=1).astype(BF16)
        wg = jnp.pad(wl[:, g0:g1], ((0, 0), (0, LANES - NGATE))).astype(BF16)
        bg = jnp.pad(a_gate_b[l], (0, LANES - NGATE))[None, :]

        h1_flat = h1.reshape(NTOK, D)
        p3 = _in_proj(h1_flat, wp).reshape(BATCH, TT, P_COLS)
        gates, lfp = _gate_proj(h1_flat, wg, bg)

        qk = _mlstm_prep(p3, a_conv[l], rope_c, rope_s)
        hf, hbw = _mlstm_scan(qk, p3, gates.reshape(BATCH, TT, LANES), lfp.reshape(BATCH, TT, LANES))
        hb_lat = _na_attention(p3, na_bias, l)
        hb_ctx = _ctx_attention(p3)

        moe = l % 2 == 1
        x1, h2 = _merge(hf, hbw, p3, hb_lat, hb_ctx, xs, mod, a_hnorm_g[l][None, :],
                        w_br_a[l].astype(BF16), w_br_b[l].astype(BF16), w_out[l].astype(BF16),
                        g_norm2[l][None, :], F32 if moe else BF16)
        if last:
            gn, modn = g_final[None, :], mod_zero
        else:
            gn, modn = g_norm1[l + 1][None, :], mod_all[l + 1]
        j = l // 2
        if not moe:
            xs, h1 = _dense_ffn(h2, x1, mod, ffn_w1[j].astype(BF16), ffn_w3[j].astype(BF16),
                                ffn_w2[j].astype(BF16), gn, modn)
        else:
            wr = jnp.pad(moe_router[j], ((0, 0), (0, LANES - N_EXPERTS))).astype(BF16)
            xs, h1 = _moe_ffn(h2, x1, mod, wr, moe_w1, moe_w3, moe_w2, j, gn, modn, last,
                              F32 if last else BF16)
            if last:
                out = h1
    return out
```

```python
import numpy as np
import jax
import jax.numpy as jnp
from jax import lax
from jax.experimental import pallas as pl
from jax.experimental.pallas import tpu as pltpu

F32 = jnp.float32
BF16 = jnp.bfloat16

D = 1024
BATCH = 8
SEQ = 2048
CTX = 256
TT = CTX + SEQ
NTOK = BATCH * TT
DEPTH = 4
GRID_W = 64
GRID_H = SEQ // GRID_W

A_HEADS = 4
A_DQK = 128
A_DV = 256
A_QK = A_HEADS * A_DQK
A_V = A_HEADS * A_DV
ROPE_BASE = 10000.0
LCH = 256
NCH = TT // LCH
SCAN_MB = 4

B_HEADS = 16
B_DH = 64
NA_ROWS = 8
NA_COLS = 16
NA_QROWS = 1
NA_KROWS = NA_QROWS + NA_ROWS - 1
NA_NQ = NA_QROWS * GRID_W
NA_NK = NA_KROWS * GRID_W

D_FF = 2816
N_EXPERTS = 8
D_FF_EXPERT = 3584
EPS = 1e-6
LOG2E = 1.4426950408889634

RB = 256
NRB = TT // RB
ROW_MB = 2
MM_TM = 2048
MOE_TM = 512
MOE_FC = 512
MOE_SUB = 256
LANES = 128

PB_QK, PB_V, PB_O, PB_NQ, PB_NK, PB_NV, PB_GA, PB_GB = range(8)
P_COLS = 8 * D

V7X_VMEM_BYTES = 64 * 1024 * 1024
VMEM_LIMIT = V7X_VMEM_BYTES * 7 // 8


def _cp(sem, vmem=VMEM_LIMIT):
    return pltpu.CompilerParams(dimension_semantics=sem, vmem_limit_bytes=vmem)


def _sigmoid(x):
    return 1.0 / (1.0 + jnp.exp(-x))


def _silu(x):
    return x * _sigmoid(x)


def _log_sigmoid(x):
    return jnp.minimum(x, 0.0) - jnp.log(1.0 + jnp.exp(-jnp.abs(x)))


def _norm_mod(x, g, sc, sh):
    ms = jnp.mean(x * x, axis=-1, keepdims=True)
    y = x * lax.rsqrt(ms + EPS)
    return (y * g) * (1.0 + sc) + sh


def _mod_row(b, i):
    return jnp.where(i == 0, BATCH, b)


def _mod_kernel(c_ref, w_ref, b_ref, o_ref):
    c = c_ref[...]
    s = _silu(c).astype(BF16)
    o_ref[0] = jnp.dot(s, w_ref[0].astype(BF16), preferred_element_type=F32) + b_ref[0]


def _modulation(cc, w_mod, b_mod):
    tn = 2048
    nl = w_mod.shape[0]
    return pl.pallas_call(
        _mod_kernel,
        grid=(nl, 6 * D // tn),
        in_specs=[pl.BlockSpec((16, D), lambda l, j: (0, 0)),
                  pl.BlockSpec((1, D, tn), lambda l, j: (l, 0, j)),
                  pl.BlockSpec((1, 1, tn), lambda l, j: (l, 0, j))],
        out_specs=pl.BlockSpec((1, 16, tn), lambda l, j: (l, 0, j)),
        out_shape=jax.ShapeDtypeStruct((nl, 16, 6 * D), F32),
        compiler_params=_cp(("parallel", "parallel")),
        name="modulation",
    )(cc, w_mod, b_mod.reshape(nl, 1, 6 * D))


def _norm_kernel(x_ref, g_ref, mod_ref, o_ref):
    mod = mod_ref[0]
    o_ref[0] = _norm_mod(x_ref[0], g_ref[...], mod[1:2], mod[0:1]).astype(o_ref.dtype)


def _first_norm(x, g, mod):
    return pl.pallas_call(
        _norm_kernel,
        grid=(BATCH, NRB),
        in_specs=[pl.BlockSpec((1, RB, D), lambda b, i: (b, i, 0)),
                  pl.BlockSpec((1, D), lambda b, i: (0, 0)),
                  pl.BlockSpec((1, 6, D), lambda b, i: (_mod_row(b, i), 0, 0))],
        out_specs=pl.BlockSpec((1, RB, D), lambda b, i: (b, i, 0)),
        out_shape=jax.ShapeDtypeStruct((BATCH, TT, D), BF16),
        compiler_params=_cp(("parallel", "parallel")),
        name="first_norm",
    )(x, g, mod)


def _mm_kernel(a_ref, w_ref, o_ref):
    o_ref[...] = jnp.dot(a_ref[...], w_ref[...], preferred_element_type=F32).astype(o_ref.dtype)


def _in_proj(h, w):
    tn = 1024
    return pl.pallas_call(
        _mm_kernel,
        grid=(P_COLS // tn, NTOK // MM_TM),
        in_specs=[pl.BlockSpec((MM_TM, D), lambda j, i: (i, 0)),
                  pl.BlockSpec((D, tn), lambda j, i: (0, j))],
        out_specs=pl.BlockSpec((MM_TM, tn), lambda j, i: (i, j)),
        out_shape=jax.ShapeDtypeStruct((NTOK, P_COLS), BF16),
        compiler_params=_cp(("parallel", "parallel")),
        name="in_proj",
    )(h, w)


NGATE = 4 * A_HEADS


def _gate_kernel(a_ref, w_ref, b_ref, g_ref, lf_ref):
    g = jnp.dot(a_ref[...], w_ref[...], preferred_element_type=F32) + b_ref[...]
    g_ref[...] = g
    lf = _log_sigmoid(g)
    p0 = lf.astype(BF16).astype(F32)
    r1 = lf - p0
    p1 = r1.astype(BF16).astype(F32)
    p2 = (r1 - p1).astype(BF16).astype(F32)
    lane = lax.broadcasted_iota(jnp.int32, (1, LANES), 1)
    parts = jnp.where(lane < NGATE, p0,
                      jnp.where(lane < 2 * NGATE, pltpu.roll(p1, NGATE, 1),
                                jnp.where(lane < 3 * NGATE, pltpu.roll(p2, 2 * NGATE, 1), 0.0)))
    lf_ref[...] = parts.astype(BF16)


def _gate_proj(h, wg, bg):
    return pl.pallas_call(
        _gate_kernel,
        grid=(NTOK // MM_TM,),
        in_specs=[pl.BlockSpec((MM_TM, D), lambda i: (i, 0)),
                  pl.BlockSpec((D, LANES), lambda i: (0, 0)),
                  pl.BlockSpec((1, LANES), lambda i: (0, 0))],
        out_specs=[pl.BlockSpec((MM_TM, LANES), lambda i: (i, 0)),
                   pl.BlockSpec((MM_TM, LANES), lambda i: (i, 0))],
        out_shape=[jax.ShapeDtypeStruct((NTOK, LANES), F32),
                   jax.ShapeDtypeStruct((NTOK, LANES), BF16)],
        compiler_params=_cp(("parallel",)),
        name="gate_proj",
    )(h, wg, bg)


def _prep_kernel(u_ref, up_ref, un_ref, w_ref, c_ref, s_ref, o_ref):
    i = pl.program_id(1)
    u = u_ref[0].astype(F32)
    prev_row = jnp.where(i >= 2, up_ref[0, 15:16, :].astype(F32), 0.0)
    next_row = jnp.where((i >= 1) & (i <= NRB - 2), un_ref[0, 0:1, :].astype(F32), 0.0)
    rid = lax.broadcasted_iota(jnp.int32, (RB, 1), 0)
    u_m1 = jnp.where(rid == 0, prev_row, pltpu.roll(u, 1, 0))
    u_p1 = jnp.where(rid == RB - 1, next_row, pltpu.roll(u, RB - 1, 0))
    w = w_ref[...]
    y = w[0:1] * u_m1 + w[1:2] * u + w[2:3] * u_p1
    y = _silu(y)
    c = c_ref[...]
    s = s_ref[...]
    cfull = jnp.concatenate([c[:, :A_DQK]] * A_HEADS + [c[:, A_DQK:]] * A_HEADS, axis=1)
    sfull = jnp.concatenate([s[:, :A_DQK]] * A_HEADS + [s[:, A_DQK:]] * A_HEADS, axis=1)
    lane = lax.broadcasted_iota(jnp.int32, (1, 2 * A_QK), 1)
    partner = jnp.where((lane & 32) == 0,
                        pltpu.roll(y, 2 * A_QK - 32, 1), pltpu.roll(y, 32, 1))
    o_ref[0] = (y * cfull + partner * sfull).astype(o_ref.dtype)


def _mlstm_prep(p3, conv_w, rope_c, rope_s):
    nb16 = TT // 16
    return pl.pallas_call(
        _prep_kernel,
        grid=(BATCH, NRB),
        in_specs=[pl.BlockSpec((1, RB, D), lambda b, i: (b, i, PB_QK)),
                  pl.BlockSpec((1, 16, D), lambda b, i: (b, jnp.maximum(i * (RB // 16) - 1, 0), PB_QK)),
                  pl.BlockSpec((1, 16, D), lambda b, i: (b, jnp.minimum((i + 1) * (RB // 16), nb16 - 1), PB_QK)),
                  pl.BlockSpec((3, D), lambda b, i: (0, 0)),
                  pl.BlockSpec((RB, 2 * A_DQK), lambda b, i: (i, 0)),
                  pl.BlockSpec((RB, 2 * A_DQK), lambda b, i: (i, 0))],
        out_specs=pl.BlockSpec((1, RB, D), lambda b, i: (b, i, 0)),
        out_shape=jax.ShapeDtypeStruct((BATCH, TT, D), BF16),
        compiler_params=_cp(("parallel", "parallel")),
        name="mlstm_prep",
    )(p3, p3, p3, conv_w, rope_c, rope_s)


def _mlstm_kernel(qkf_ref, vf_ref, gf_ref, lff_ref, qkb_ref, vb_ref, gb_ref, lfb_ref,
                  of_ref, ob_ref, ct_ref, n_ref, m_ref):
    nscan = 2 * SCAN_MB
    per_dir = [(qkf_ref, vf_ref, gf_ref, lff_ref, of_ref), (qkb_ref, vb_ref, gb_ref, lfb_ref, ob_ref)]
    ins = [per_dir[d % 2] + (d // 2,) for d in range(nscan)]

    @pl.when(pl.program_id(1) == 0)
    def _():
        ct_ref[...] = jnp.zeros_like(ct_ref)
        n_ref[...] = jnp.zeros_like(n_ref)
        m_ref[...] = jnp.zeros_like(m_ref)

    r = lax.broadcasted_iota(jnp.int32, (LCH, LCH), 0)
    c = lax.broadcasted_iota(jnp.int32, (LCH, LCH), 1)
    dir_masks = [c <= r, c >= r]
    masks = [dir_masks[d % 2] for d in range(nscan)]
    b_all, g_all, b_t, g_t, b_end = [], [], [], [], []
    for d in range(nscan):
        tri = jnp.where(masks[d], 1.0, 0.0).astype(BF16)
        bc = jnp.dot(tri, ins[d][3][ins[d][5]], preferred_element_type=F32)
        ba = bc + pltpu.roll(bc, LANES - NGATE, 1) + pltpu.roll(bc, LANES - 2 * NGATE, 1)
        ga = ins[d][2][ins[d][5]]
        if d % 2 == 1:
            ba = pltpu.roll(ba, LANES - 2 * A_HEADS, 1)
            ga = pltpu.roll(ga, LANES - 2 * A_HEADS, 1)
        b_all.append(ba)
        g_all.append(ga)
        b_t.append(ba.T)
        g_t.append(ga.T)
        b_end.append(ba[LCH - 1:LCH, :] if d % 2 == 0 else ba[0:1, :])
    ones = jnp.ones((LCH, LANES), BF16)
    tn_dims = (((0,), (0,)), ((), ()))
    combos = [(d, h) for d in range(nscan) for h in range(A_HEADS)]
    idx = range(len(combos))
    qs = [ins[d][0][ins[d][5], :, h * A_DQK:(h + 1) * A_DQK] for d, h in combos]
    ks = [ins[d][0][ins[d][5], :, A_QK + h * A_DQK:A_QK + (h + 1) * A_DQK] for d, h in combos]
    qk = [lax.dot_general(qs[i], ks[i], NT_DIMS, preferred_element_type=F32) for i in idx]
    b_col = [b_all[d][:, A_HEADS + h:A_HEADS + h + 1] for d, h in combos]
    b_last = [b_end[d][:, A_HEADS + h:A_HEADS + h + 1] for d, h in combos]
    m_old = [m_ref[i][:, 0:1] for i in idx]
    r_row = [b_t[d][A_HEADS + h:A_HEADS + h + 1, :] - g_t[d][h:h + 1, :] for d, h in combos]
    m_row = [jnp.maximum(b_col[i] + m_old[i],
                         jnp.max(jnp.where(masks[d], b_col[i] - r_row[i], -jnp.inf), axis=1, keepdims=True))
             for i, (d, h) in enumerate(combos)]
    s = [(qk[i] * jnp.exp(jnp.where(masks[d], (b_col[i] - m_row[i]) - r_row[i], -jnp.inf))).astype(BF16)
         for i, (d, h) in enumerate(combos)]
    wq = [(jnp.exp(b_col[i] + m_old[i] - m_row[i]) * qs[i].astype(F32)).astype(BF16) for i in idx]
    g_col = [b_last[i] - b_col[i] + g_all[d][:, h:h + 1] for i, (d, h) in enumerate(combos)]
    m_new = [jnp.maximum(b_last[i] + m_old[i], jnp.max(g_col[i], axis=0, keepdims=True)) for i in idx]
    kw = [(jnp.exp(g_col[i] - m_new[i]) * ks[i].astype(F32)).astype(BF16) for i in idx]
    decay = [jnp.exp(b_last[i] + m_old[i] - m_new[i]) for i in idx]
    for i, (d, h) in enumerate(combos):
        v = ins[d][1][ins[d][5], :, h * A_DV:(h + 1) * A_DV]
        lhs = jnp.concatenate([s[i], wq[i]], axis=1)
        den = jnp.dot(lhs, jnp.concatenate([ones, n_ref[i].astype(BF16)], axis=0),
                      preferred_element_type=F32)[:, 0:1]
        rinv = 1.0 / jnp.maximum(jnp.abs(den), jnp.exp(-m_row[i]))
        num = jnp.dot(lhs, jnp.concatenate([v, ct_ref[i].astype(BF16)], axis=0), preferred_element_type=F32)
        ins[d][4][ins[d][5], :, h * A_DV:(h + 1) * A_DV] = (num * rinv).astype(ins[d][4].dtype)
    for i, (d, h) in enumerate(combos):
        v = ins[d][1][ins[d][5], :, h * A_DV:(h + 1) * A_DV]
        ct_ref[i] = decay[i] * ct_ref[i] + lax.dot_general(kw[i], v, tn_dims, preferred_element_type=F32)
        n_ref[i] = decay[i] * n_ref[i] + lax.dot_general(kw[i], ones, tn_dims, preferred_element_type=F32)
        m_ref[i] = jnp.broadcast_to(m_new[i], (1, LANES))


def _rev_chunk(s):
    ncc = CTX // LCH
    return jnp.where(s < ncc, ncc - 1 - s, NCH + ncc - 1 - s)


def _mlstm_scan(qk, p3, gates, lfp):
    fwd = lambda b, s: (b, s, 0)
    bwd = lambda b, s: (b, _rev_chunk(s), 0)
    specs = lambda im, imv: [pl.BlockSpec((SCAN_MB, LCH, D), im), pl.BlockSpec((SCAN_MB, LCH, A_V), imv),
                             pl.BlockSpec((SCAN_MB, LCH, LANES), im), pl.BlockSpec((SCAN_MB, LCH, LANES), im)]
    nstate = 2 * SCAN_MB * A_HEADS
    return pl.pallas_call(
        _mlstm_kernel,
        grid=(BATCH // SCAN_MB, NCH),
        in_specs=(specs(fwd, lambda b, s: (b, s, PB_V))
                  + specs(bwd, lambda b, s: (b, _rev_chunk(s), PB_V))),
        out_specs=[pl.BlockSpec((SCAN_MB, LCH, A_V), fwd), pl.BlockSpec((SCAN_MB, LCH, A_V), bwd)],
        out_shape=[jax.ShapeDtypeStruct((BATCH, TT, A_V), BF16)] * 2,
        scratch_shapes=[pltpu.VMEM((nstate, A_DQK, A_DV), F32),
                        pltpu.VMEM((nstate, A_DQK, LANES), F32),
                        pltpu.VMEM((nstate, 1, LANES), F32)],
        compiler_params=_cp(("parallel", "arbitrary")),
        name="mlstm_scan",
    )(qk, p3, gates, lfp, qk, p3, gates, lfp)


NPAIR = B_HEADS // 2
NT_DIMS = (((1,), (1,)), ((), ()))


def _stack_pair(q, low):
    zero = jnp.zeros_like(q)
    return jnp.concatenate([jnp.where(low, q, zero), jnp.where(low, zero, q)], axis=0)


def _softmax_rows(s_ref, p_ref):
    s = s_ref[...]
    p = jnp.exp2(s - jnp.max(s, axis=2, keepdims=True))
    p_ref[...] = p.astype(p_ref.dtype)
    return 1.0 / jnp.sum(p, axis=2, keepdims=True)


def _na_kernel(q_ref, k_ref, v_ref, *rest):
    bias_refs, (o_ref, s_ref, p_ref) = rest[:NA_SUB], rest[NA_SUB:]
    low = lax.broadcasted_iota(jnp.int32, (1, 2 * B_DH), 1) < B_DH
    starts = []
    for sub in range(NA_SUB):
        row = pl.program_id(1) * NA_SUB + sub
        rs = jnp.clip(row - NA_ROWS // 2, 0, GRID_H - NA_ROWS)
        starts.append(pl.multiple_of(CTX + rs * GRID_W, GRID_W))
    for sub in range(NA_SUB):
        qrows = pl.ds(sub * NA_NQ, NA_NQ)
        for hp in range(NPAIR):
            lo, hi = hp * 2 * B_DH, (hp + 1) * 2 * B_DH
            q2 = _stack_pair(q_ref[0, qrows, lo:hi], low)
            idx = sub * NPAIR + hp
            s_ref[idx, :, 0:NA_NK] = lax.dot_general(q2, k_ref[0, pl.ds(starts[sub], NA_NK), lo:hi], NT_DIMS,
                                                     preferred_element_type=F32
                                                     ) + bias_refs[sub][0, 0, hp].astype(F32)
            s_ref[idx, :, NA_NK:] = lax.dot_general(q2, k_ref[0, 0:CTX, lo:hi], NT_DIMS,
                                                    preferred_element_type=F32)
    rinv = _softmax_rows(s_ref, p_ref)
    for sub in range(NA_SUB):
        qrows = pl.ds(sub * NA_NQ, NA_NQ)
        for hp in range(NPAIR):
            lo, hi = hp * 2 * B_DH, (hp + 1) * 2 * B_DH
            idx = sub * NPAIR + hp
            o2 = (jnp.dot(p_ref[idx, :, 0:NA_NK], v_ref[0, pl.ds(starts[sub], NA_NK), lo:hi],
                          preferred_element_type=F32)
                  + jnp.dot(p_ref[idx, :, NA_NK:], v_ref[0, 0:CTX, lo:hi], preferred_element_type=F32))
            o2 = o2 * rinv[idx]
            o_ref[0, qrows, lo:hi] = jnp.where(low, o2[0:NA_NQ], o2[NA_NQ:]).astype(o_ref.dtype)


def _na_pattern(row):
    edge = NA_ROWS // 2
    return jnp.where(row < edge, row, jnp.where(row <= GRID_H - edge, edge, row - (GRID_H - 2 * edge)))


NA_NPAT = NA_ROWS
NA_SUB = 4


def _na_attention(p3, bias, layer):
    nq = NA_SUB * NA_NQ
    qoff = CTX // nq
    bias_specs = [pl.BlockSpec((1, 1, NPAIR, 2 * NA_NQ, NA_NK),
                               lambda b, i, sub=sub: (layer, _na_pattern(i * NA_SUB + sub), 0, 0, 0))
                  for sub in range(NA_SUB)]
    return pl.pallas_call(
        _na_kernel,
        grid=(BATCH, GRID_H // NA_SUB),
        in_specs=[pl.BlockSpec((1, nq, D), lambda b, i: (b, i + qoff, PB_NQ)),
                  pl.BlockSpec((1, TT, D), lambda b, i: (b, 0, PB_NK)),
                  pl.BlockSpec((1, TT, D), lambda b, i: (b, 0, PB_NV))] + bias_specs,
        out_specs=pl.BlockSpec((1, nq, D), lambda b, i: (b, i, 0)),
        out_shape=jax.ShapeDtypeStruct((BATCH, SEQ, D), BF16),
        scratch_shapes=[pltpu.VMEM((NA_SUB * NPAIR, 2 * NA_NQ, NA_NK + CTX), F32),
                        pltpu.VMEM((NA_SUB * NPAIR, 2 * NA_NQ, NA_NK + CTX), BF16)],
        compiler_params=_cp(("parallel", "arbitrary")),
        name="na_attention",
    )(p3, p3, p3, *([bias] * NA_SUB))


def _ctx_attn_kernel(q_ref, k_ref, v_ref, o_ref, s_ref, p_ref):
    low = lax.broadcasted_iota(jnp.int32, (1, 2 * B_DH), 1) < B_DH
    for hp in range(NPAIR):
        lo, hi = hp * 2 * B_DH, (hp + 1) * 2 * B_DH
        s_ref[hp] = lax.dot_general(_stack_pair(q_ref[0, :, lo:hi], low), k_ref[0, :, lo:hi], NT_DIMS,
                                    preferred_element_type=F32)
    rinv = _softmax_rows(s_ref, p_ref)
    for hp in range(NPAIR):
        lo, hi = hp * 2 * B_DH, (hp + 1) * 2 * B_DH
        o2 = jnp.dot(p_ref[hp], v_ref[0, :, lo:hi], preferred_element_type=F32) * rinv[hp]
        o_ref[0, :, lo:hi] = jnp.where(low, o2[0:CTX], o2[CTX:]).astype(o_ref.dtype)


def _ctx_attention(p3):
    return pl.pallas_call(
        _ctx_attn_kernel,
        grid=(BATCH,),
        in_specs=[pl.BlockSpec((1, CTX, D), lambda b: (b, 0, PB_NQ)),
                  pl.BlockSpec((1, CTX, D), lambda b: (b, 0, PB_NK)),
                  pl.BlockSpec((1, CTX, D), lambda b: (b, 0, PB_NV))],
        out_specs=pl.BlockSpec((1, CTX, D), lambda b: (b, 0, 0)),
        out_shape=jax.ShapeDtypeStruct((BATCH, CTX, D), BF16),
        scratch_shapes=[pltpu.VMEM((NPAIR, 2 * CTX, CTX), F32),
                        pltpu.VMEM((NPAIR, 2 * CTX, CTX), BF16)],
        compiler_params=_cp(("parallel",)),
        name="ctx_attention",
    )(p3, p3, p3)


def _merge_kernel(hf_ref, hbw_ref, o_ref, ga_ref, gb_ref, hnl_ref, hnc_ref, x_ref, mod_ref, ghn_ref,
                  wa_ref, wb_ref, wo_ref, g2_ref, x1_ref, h2_ref):
    rows = ROW_MB * RB
    flat = lambda ref: ref[...].reshape(rows, ref.shape[-1])
    hn = jnp.where(pl.program_id(1) == 0, flat(hnc_ref), flat(hnl_ref))
    bm = jnp.dot(hn, wb_ref[...], preferred_element_type=F32)
    hs = flat(hf_ref).astype(F32) + flat(hbw_ref).astype(F32)
    parts = []
    for h in range(A_HEADS):
        seg = hs[:, h * A_DV:(h + 1) * A_DV]
        mu = jnp.mean(seg, axis=-1, keepdims=True)
        cen = seg - mu
        var = jnp.mean(cen * cen, axis=-1, keepdims=True)
        parts.append(cen * lax.rsqrt(var + EPS))
    ya = jnp.concatenate(parts, axis=1) * ghn_ref[...] * _sigmoid(flat(o_ref).astype(F32))
    a = jnp.dot(ya.astype(BF16), wa_ref[...], preferred_element_type=F32)
    mrg = _sigmoid(flat(ga_ref).astype(F32)) * a + _sigmoid(flat(gb_ref).astype(F32)) * bm
    y = jnp.dot(mrg.astype(BF16), wo_ref[...], preferred_element_type=F32)
    for m in range(ROW_MB):
        mod = mod_ref[m]
        x1 = x_ref[m] + mod[2:3] * y[m * RB:(m + 1) * RB]
        x1_ref[m] = x1
        h2_ref[m] = _norm_mod(x1, g2_ref[...], mod[4:5], mod[3:4]).astype(h2_ref.dtype)


def _merge(hf, hbw, p3, hb_lat, hb_ctx, x, mod, ghn, wa, wb, wo, g2, h2_dtype):
    row = lambda b, i: (b, i, 0)
    full = lambda b, i: (0, 0)
    once = pl.Buffered(1)
    return pl.pallas_call(
        _merge_kernel,
        grid=(BATCH // ROW_MB, NRB),
        in_specs=[pl.BlockSpec((ROW_MB, RB, A_V), row),
                  pl.BlockSpec((ROW_MB, RB, A_V), row),
                  pl.BlockSpec((ROW_MB, RB, D), lambda b, i: (b, i, PB_O)),
                  pl.BlockSpec((ROW_MB, RB, D), lambda b, i: (b, i, PB_GA)),
                  pl.BlockSpec((ROW_MB, RB, D), lambda b, i: (b, i, PB_GB)),
                  pl.BlockSpec((ROW_MB, RB, D), lambda b, i: (b, jnp.maximum(i - 1, 0), 0)),
                  pl.BlockSpec((ROW_MB, CTX, D), lambda b, i: (b, 0, 0)),
                  pl.BlockSpec((ROW_MB, RB, D), row),
                  pl.BlockSpec((ROW_MB, 6, D), lambda b, i: (_mod_block(b, i), 0, 0)),
                  pl.BlockSpec((1, A_V), full),
                  pl.BlockSpec((A_V, D), full, pipeline_mode=once),
                  pl.BlockSpec((D, D), full, pipeline_mode=once),
                  pl.BlockSpec((D, D), full, pipeline_mode=once),
                  pl.BlockSpec((1, D), full)],
        out_specs=[pl.BlockSpec((ROW_MB, RB, D), row), pl.BlockSpec((ROW_MB, RB, D), row)],
        out_shape=[jax.ShapeDtypeStruct((BATCH, TT, D), F32),
                   jax.ShapeDtypeStruct((BATCH, TT, D), h2_dtype)],
        compiler_params=_cp(("parallel", "parallel")),
        name="merge",
    )(hf, hbw, p3, p3, p3, hb_lat, hb_ctx, x, _pad_mod(mod), ghn, wa, wb, wo, g2)


def _ffn_kernel(h_ref, x_ref, mod_ref, w1_ref, w3_ref, w2_ref, gn_ref, modn_ref, x2_ref, hn_ref):
    h = h_ref[...].reshape(ROW_MB * RB, D)
    a = jnp.dot(h, w1_ref[...], preferred_element_type=F32)
    g = jnp.dot(h, w3_ref[...], preferred_element_type=F32)
    y = jnp.dot((_silu(a) * g).astype(BF16), w2_ref[...], preferred_element_type=F32)
    for m in range(ROW_MB):
        mod = mod_ref[m]
        x2 = x_ref[m] + mod[5:6] * y[m * RB:(m + 1) * RB]
        x2_ref[m] = x2
        modn = modn_ref[m]
        hn_ref[m] = _norm_mod(x2, gn_ref[...], modn[1:2], modn[0:1]).astype(hn_ref.dtype)


def _pad_mod(mod):
    return jnp.concatenate([mod] + [mod[BATCH:]] * (ROW_MB - 1), axis=0)


def _mod_block(b, i):
    return jnp.where(i == 0, BATCH // ROW_MB, b)


def _dense_ffn(h2, x1, mod, w1, w3, w2, gn, modn):
    row = lambda b, i: (b, i, 0)
    full = lambda b, i: (0, 0)
    modspec = pl.BlockSpec((ROW_MB, 6, D), lambda b, i: (_mod_block(b, i), 0, 0))
    once = pl.Buffered(1)
    return pl.pallas_call(
        _ffn_kernel,
        grid=(BATCH // ROW_MB, NRB),
        in_specs=[pl.BlockSpec((ROW_MB, RB, D), row), pl.BlockSpec((ROW_MB, RB, D), row), modspec,
                  pl.BlockSpec((D, D_FF), full, pipeline_mode=once),
                  pl.BlockSpec((D, D_FF), full, pipeline_mode=once),
                  pl.BlockSpec((D_FF, D), full, pipeline_mode=once),
                  pl.BlockSpec((1, D), full), modspec],
        out_specs=[pl.BlockSpec((ROW_MB, RB, D), row), pl.BlockSpec((ROW_MB, RB, D), row)],
        out_shape=[jax.ShapeDtypeStruct((BATCH, TT, D), F32),
                   jax.ShapeDtypeStruct((BATCH, TT, D), BF16)],
        compiler_params=_cp(("parallel", "parallel")),
        name="dense_ffn",
    )(h2, x1, _pad_mod(mod), w1, w3, w2, gn, _pad_mod(modn))


def _router_kernel(h_ref, wr_ref, route_ref, cnt_ref, run_ref):
    i = pl.program_id(0)

    @pl.when(i == 0)
    def _():
        run_ref[...] = jnp.zeros_like(run_ref)

    logits = jnp.dot(h_ref[...].astype(BF16), wr_ref[...], preferred_element_type=F32)
    lane = lax.broadcasted_iota(jnp.int32, (RB, LANES), 1).astype(F32)
    lg = jnp.where(lane < N_EXPERTS, logits, -jnp.inf)
    v1 = jnp.max(lg, axis=1, keepdims=True)
    i1 = jnp.min(jnp.where(lg == v1, lane, float(LANES)), axis=1, keepdims=True)
    lg2 = jnp.where(lane == i1, -jnp.inf, lg)
    v2 = jnp.max(lg2, axis=1, keepdims=True)
    i2 = jnp.min(jnp.where(lg2 == v2, lane, float(LANES)), axis=1, keepdims=True)
    e = jnp.exp(v2 - v1)
    w1 = 1.0 / (1.0 + e)
    w2 = e / (1.0 + e)
    oh1 = (lane == i1).astype(F32)
    oh2 = (lane == i2).astype(F32)
    r = lax.broadcasted_iota(jnp.int32, (RB, RB), 0)
    c = lax.broadcasted_iota(jnp.int32, (RB, RB), 1)
    tri = (r > c).astype(BF16)
    cs1 = jnp.dot(tri, oh1.astype(BF16), preferred_element_type=F32)
    cs2 = jnp.dot(tri, oh2.astype(BF16), preferred_element_type=F32)
    tot1 = jnp.sum(oh1, axis=0, keepdims=True)
    tot2 = jnp.sum(oh2, axis=0, keepdims=True)
    run = run_ref[...]
    rank1 = jnp.sum(oh1 * (run + cs1), axis=1, keepdims=True)
    rank2 = jnp.sum(oh2 * (run + tot1 + cs2), axis=1, keepdims=True)
    new_run = run + tot1 + tot2
    run_ref[...] = new_run
    cnt_ref[...] = new_run
    out = jnp.where(lane == 0, i1,
          jnp.where(lane == 1, i2,
          jnp.where(lane == 2, w1,
          jnp.where(lane == 3, w2,
          jnp.where(lane == 4, rank1,
          jnp.where(lane == 5, rank2, 0.0))))))
    route_ref[...] = out


def _router(h2_flat, wr, nblk, blk_of):
    return pl.pallas_call(
        _router_kernel,
        grid=(nblk,),
        in_specs=[pl.BlockSpec((RB, D), lambda i: (blk_of(i), 0)),
                  pl.BlockSpec((D, LANES), lambda i: (0, 0))],
        out_specs=[pl.BlockSpec((RB, LANES), lambda i: (i, 0)),
                   pl.BlockSpec((1, LANES), lambda i: (0, 0))],
        out_shape=[jax.ShapeDtypeStruct((nblk * RB, LANES), F32),
                   jax.ShapeDtypeStruct((1, LANES), F32)],
        scratch_shapes=[pltpu.VMEM((1, LANES), F32)],
        compiler_params=_cp(("arbitrary",)),
        name="moe_router",
    )(h2_flat, wr)


DMA_UNROLL = 8
N_ZERO = 2 * N_EXPERTS


def _dispatch_kernel(pos_ref, zstart_ref, h_ref, o_ref, zbuf, stage, sems, zsem):
    @pl.when(pl.program_id(0) == 0)
    def _():
        zbuf[...] = jnp.zeros_like(zbuf)
        for z in range(N_ZERO):
            @pl.when(zstart_ref[z] >= 0)
            def _():
                zs = pl.multiple_of(zstart_ref[z], MOE_TM)
                pltpu.make_async_copy(zbuf, o_ref.at[pl.ds(zs, MOE_TM)], zsem).start()
        for z in range(N_ZERO):
            @pl.when(zstart_ref[z] >= 0)
            def _():
                pltpu.make_async_copy(zbuf, o_ref.at[pl.ds(0, MOE_TM)], zsem).wait()

    step = pl.program_id(0)
    slot = step % 2
    stage[slot] = h_ref[...]

    def issue(t, carry):
        src = stage.at[slot, pl.ds(t, 1)]
        pltpu.make_async_copy(src, o_ref.at[pl.ds(pos_ref[0, 0, t], 1)], sems.at[slot]).start(priority=0)
        pltpu.make_async_copy(src, o_ref.at[pl.ds(pos_ref[0, 0, RB + t], 1)], sems.at[slot]).start(priority=1)
        return carry

    lax.fori_loop(0, RB, issue, 0, unroll=DMA_UNROLL)

    def drain(sl):
        for _ in range(2):
            pltpu.make_async_copy(stage.at[sl], o_ref.at[pl.ds(0, RB)], sems.at[sl]).wait()

    @pl.when(step > 0)
    def _():
        drain(1 - slot)

    @pl.when(step == pl.num_programs(0) - 1)
    def _():
        drain(slot)


def _dispatch(pos, zstart, h2_flat, nblk, blk_of, n_sorted):
    return pl.pallas_call(
        _dispatch_kernel,
        grid=(nblk,),
        in_specs=[pl.BlockSpec((1, 1, 2 * RB), lambda i: (i, 0, 0), memory_space=pltpu.SMEM),
                  pl.BlockSpec(memory_space=pltpu.SMEM),
                  pl.BlockSpec((RB, D), lambda i: (blk_of(i), 0))],
        out_specs=pl.BlockSpec(memory_space=pl.ANY),
        out_shape=jax.ShapeDtypeStruct((n_sorted, D), F32),
        scratch_shapes=[pltpu.VMEM((MOE_TM, D), F32), pltpu.VMEM((2, RB, D), F32),
                        pltpu.SemaphoreType.DMA((2,)), pltpu.SemaphoreType.DMA(())],
        compiler_params=_cp(("arbitrary",)),
        name="moe_dispatch",
    )(pos, zstart, h2_flat)


def _gmm_kernel(be_ref, nb_ref, a_ref, w1_ref, w3_ref, w2_ref, o_ref, abf_ref):
    del be_ref
    i = pl.program_id(0)
    f = pl.program_id(1)

    @pl.when((i < nb_ref[0]) & (f == 0))
    def _():
        abf_ref[...] = a_ref[...].astype(BF16)

    @pl.when(i < nb_ref[0])
    def _():
        a = abf_ref[...]
        gs = []
        for c in range(MOE_FC // MOE_SUB):
            lo, hi = c * MOE_SUB, (c + 1) * MOE_SUB
            h1 = jnp.dot(a, w1_ref[0, :, lo:hi].astype(BF16), preferred_element_type=F32)
            h3 = jnp.dot(a, w3_ref[0, :, lo:hi].astype(BF16), preferred_element_type=F32)
            gs.append((_silu(h1) * h3).astype(BF16))
        part = jnp.dot(jnp.concatenate(gs, axis=1), w2_ref[0].astype(BF16), preferred_element_type=F32)

        @pl.when(f == 0)
        def _():
            o_ref[...] = part

        @pl.when(f > 0)
        def _():
            o_ref[...] += part

    @pl.when((i >= nb_ref[0]) & (f == 0))
    def _():
        o_ref[...] = jnp.zeros_like(o_ref)


def _expert_ffn(blk_e, nb, hs, w1, w3, w2, j):
    nbmax = hs.shape[0] // MOE_TM
    nf = D_FF_EXPERT // MOE_FC

    def ieff(i, nb_ref):
        return jnp.minimum(i, nb_ref[0] - 1)

    def feff(i, f, nb_ref):
        return jnp.where(i < nb_ref[0], f, nf - 1)

    grid_spec = pltpu.PrefetchScalarGridSpec(
        num_scalar_prefetch=2,
        grid=(nbmax, nf),
        in_specs=[pl.BlockSpec((MOE_TM, D), lambda i, f, be, nbr: (ieff(i, nbr), 0)),
                  pl.BlockSpec((None, 1, D, MOE_FC),
                               lambda i, f, be, nbr: (j, be[ieff(i, nbr)], 0, feff(i, f, nbr))),
                  pl.BlockSpec((None, 1, D, MOE_FC),
                               lambda i, f, be, nbr: (j, be[ieff(i, nbr)], 0, feff(i, f, nbr))),
                  pl.BlockSpec((None, 1, MOE_FC, D),
                               lambda i, f, be, nbr: (j, be[ieff(i, nbr)], feff(i, f, nbr), 0))],
        out_specs=pl.BlockSpec((MOE_TM, D), lambda i, f, be, nbr: (i, 0)),
        scratch_shapes=[pltpu.VMEM((MOE_TM, D), BF16)],
    )
    return pl.pallas_call(
        _gmm_kernel,
        grid_spec=grid_spec,
        out_shape=jax.ShapeDtypeStruct(hs.shape, F32),
        compiler_params=_cp(("arbitrary", "arbitrary")),
        name="moe_expert_ffn",
    )(blk_e, nb, hs, w1, w3, w2)


def _combine_kernel(pos_ref, posn_ref, y_ref, route_ref, x_ref, mod_ref, gn_ref, modn_ref,
                    x2_ref, hn_ref, ybuf, sems):
    t = pl.program_id(0) * pl.num_programs(1) + pl.program_id(1)
    nsteps = pl.num_programs(0) * pl.num_programs(1)
    slot = t % 2

    def start_block(p_ref, sl):
        def issue(r, carry):
            pltpu.make_async_copy(y_ref.at[pl.ds(p_ref[0, 0, r], 1)],
                                  ybuf.at[sl, 0, pl.ds(r, 1)], sems.at[sl]).start(priority=0)
            pltpu.make_async_copy(y_ref.at[pl.ds(p_ref[0, 0, RB + r], 1)],
                                  ybuf.at[sl, 1, pl.ds(r, 1)], sems.at[sl]).start(priority=1)
            return carry
        lax.fori_loop(0, RB, issue, 0, unroll=DMA_UNROLL)

    @pl.when(t == 0)
    def _():
        start_block(pos_ref, 0)

    @pl.when(t + 1 < nsteps)
    def _():
        start_block(posn_ref, 1 - slot)

    for e in range(2):
        pltpu.make_async_copy(y_ref.at[pl.ds(0, RB)], ybuf.at[slot, e], sems.at[slot]).wait()
    route = route_ref[...]
    y = route[:, 2:3] * ybuf[slot, 0] + route[:, 3:4] * ybuf[slot, 1]
    mod = mod_ref[0]
    x2 = x_ref[0] + mod[5:6] * y
    x2_ref[0] = x2
    modn = modn_ref[0]
    hn_ref[0] = _norm_mod(x2, gn_ref[...], modn[1:2], modn[0:1]).astype(hn_ref.dtype)


def _combine(pos, ys, route, x1, mod, gn, modn, latent_only, hn_dtype):
    nrb = NRB - 1 if latent_only else NRB
    off = 1 if latent_only else 0
    rows = nrb * RB
    full = lambda b, i: (0, 0)
    modspec = pl.BlockSpec((1, 6, D), lambda b, i: (_mod_row(b, i + off), 0, 0))
    last = BATCH * nrb - 1
    return pl.pallas_call(
        _combine_kernel,
        grid=(BATCH, nrb),
        in_specs=[pl.BlockSpec((1, 1, 2 * RB), lambda b, i: (b * nrb + i, 0, 0), memory_space=pltpu.SMEM),
                  pl.BlockSpec((1, 1, 2 * RB), lambda b, i: (jnp.minimum(b * nrb + i + 1, last), 0, 0),
                               memory_space=pltpu.SMEM),
                  pl.BlockSpec(memory_space=pl.ANY),
                  pl.BlockSpec((RB, LANES), lambda b, i: (b * nrb + i, 0)),
                  pl.BlockSpec((1, RB, D), lambda b, i: (b, i + off, 0)),
                  modspec,
                  pl.BlockSpec((1, D), full),
                  modspec],
        out_specs=[pl.BlockSpec((1, RB, D), lambda b, i: (b, i, 0)),
                   pl.BlockSpec((1, RB, D), lambda b, i: (b, i, 0))],
        out_shape=[jax.ShapeDtypeStruct((BATCH, rows, D), F32),
                   jax.ShapeDtypeStruct((BATCH, rows, D), hn_dtype)],
        scratch_shapes=[pltpu.VMEM((2, 2, RB, D), F32), pltpu.SemaphoreType.DMA((2,))],
        compiler_params=_cp(("arbitrary", "arbitrary")),
        name="moe_combine",
    )(pos, pos, ys, route, x1, mod, gn, modn)


def _moe_ffn(h2, x1, mod, wr, w1, w3, w2, j, gn, modn, latent_only, hn_dtype):
    h2_flat = h2.reshape(NTOK, D)
    if latent_only:
        nrb = NRB - 1
        blk_of = lambda i: (i // nrb) * NRB + (i % nrb) + 1
    else:
        nrb = NRB
        blk_of = lambda i: i
    nblk = BATCH * nrb
    n_pairs = 2 * nblk * RB
    nbmax = n_pairs // MOE_TM + N_EXPERTS
    n_sorted = nbmax * MOE_TM

    route, counts = _router(h2_flat, wr, nblk, blk_of)
    cnt = counts[0, :N_EXPERTS].astype(jnp.int32)
    gsz = ((cnt + MOE_TM - 1) // MOE_TM) * MOE_TM
    gend = jnp.cumsum(gsz)
    goff = gend - gsz
    e12 = route[:, 0:2].astype(jnp.int32)
    rank = route[:, 4:6].astype(jnp.int32)
    pos = goff[e12] + rank
    pos = pos.reshape(nblk, RB, 2).transpose(0, 2, 1).reshape(nblk, 1, 2 * RB)
    nb = (gend[-1] // MOE_TM).reshape(1)
    blk_start = jnp.arange(nbmax, dtype=jnp.int32) * MOE_TM
    blk_e = jnp.minimum(jnp.sum(blk_start[:, None] >= gend[None, :], axis=1), N_EXPERTS - 1).astype(jnp.int32)

    trail = (nbmax - N_EXPERTS + jnp.arange(N_EXPERTS, dtype=jnp.int32)) * MOE_TM
    zstart = jnp.concatenate([jnp.where(cnt > 0, gend - MOE_TM, -1),
                              jnp.where(trail >= gend[-1], trail, -1)]).astype(jnp.int32)

    hs = _dispatch(pos, zstart, h2_flat, nblk, blk_of, n_sorted)
    ys = _expert_ffn(blk_e, nb, hs, w1, w3, w2, j)
    return _combine(pos, ys, route, x1, mod, gn, modn, latent_only, hn_dtype)


def _rope_tables():
    quarter = A_DQK // 4
    inv = 1.0 / (ROPE_BASE ** (jnp.arange(quarter, dtype=F32) / quarter))
    pos = jnp.arange(SEQ)
    rows = (pos // GRID_W).astype(F32)
    cols = (pos % GRID_W).astype(F32)
    ang_r = rows[:, None] * inv[None, :]
    ang_c = cols[:, None] * inv[None, :]
    cos = jnp.concatenate([jnp.cos(ang_r)] * 2 + [jnp.cos(ang_c)] * 2, axis=1)
    sin = jnp.concatenate([-jnp.sin(ang_r), jnp.sin(ang_r), -jnp.sin(ang_c), jnp.sin(ang_c)], axis=1)
    cos = jnp.concatenate([jnp.ones((CTX, A_DQK), F32), cos], axis=0)
    sin = jnp.concatenate([jnp.zeros((CTX, A_DQK), F32), sin], axis=0)
    kscale = A_DQK ** -0.5
    return (jnp.concatenate([cos, cos * kscale], axis=1),
            jnp.concatenate([sin, sin * kscale], axis=1))


def _na_bias_tables(rpb):
    rs = np.clip(np.arange(GRID_H) - NA_ROWS // 2, 0, GRID_H - NA_ROWS)
    cs = np.clip(np.arange(GRID_W) - NA_COLS // 2, 0, GRID_W - NA_COLS)
    rows = [0, 1, 2, 3, NA_ROWS // 2, GRID_H - 3, GRID_H - 2, GRID_H - 1]
    col = np.arange(GRID_W)
    dc = np.clip(col[None, :] - col[:, None] + NA_COLS - 1, 0, 2 * NA_COLS - 2)
    valid_c = (col[None, :] >= cs[:, None]) & (col[None, :] < cs[:, None] + NA_COLS)
    sel_c = np.eye(2 * NA_COLS - 1, dtype=np.float32)[dc]
    t = jnp.einsum('lhab,uvb->lhuav', rpb * LOG2E, jnp.asarray(sel_c),
                   precision=lax.Precision.HIGHEST)
    neg = np.where(valid_c, 0.0, -np.inf).astype(np.float32)[:, None, :]
    nl = rpb.shape[0]
    t = (t + jnp.asarray(neg)).reshape(nl, NPAIR, 2 * NA_NQ, (2 * NA_ROWS - 1) * GRID_W)
    first = [rs[r] - r + NA_ROWS - 1 for r in rows]
    return jnp.stack([t[..., d0 * GRID_W:d0 * GRID_W + NA_NK] for d0 in first], axis=1).astype(BF16)


def kernel(x, c, ctx, c_ctx, w_mod, b_mod, g_norm1, g_norm2, w_in, a_conv, a_gate_b, a_hnorm_g, na_rpb,
           w_br_a, w_br_b, w_out, ffn_w1, ffn_w3, ffn_w2, moe_router, moe_w1, moe_w3, moe_w2, g_final):
    cc = jnp.concatenate([c, c_ctx[None, :], jnp.zeros((16 - BATCH - 1, D), F32)], axis=0)
    mod_all = _modulation(cc, w_mod, b_mod).reshape(DEPTH, 16, 6, D)[:, :BATCH + 1]
    mod_zero = jnp.zeros((BATCH + 1, 6, D), F32)
    rope_c, rope_s = _rope_tables()
    na_bias = _na_bias_tables(na_rpb)

    xs = jnp.concatenate([ctx, x], axis=1)
    h1 = _first_norm(xs, g_norm1[0][None, :], mod_all[0])
    out = None
    for l in range(DEPTH):
        last = l == DEPTH - 1
        mod = mod_all[l]
        wl = w_in[l]
        g0 = 3 * D
        g1 = g0 + NGATE
        wp = jnp.concatenate([wl[:, :g0], wl[:, g1:g1 + D] * (B_DH ** -0.5 * LOG2E), wl[:, g1 + D:]],
                             axis=1).astype(BF16)
        wg = jnp.pad(wl[:, g0:g1], ((0, 0), (0, LANES - NGATE))).astype(BF16)
        bg = jnp.pad(a_gate_b[l], (0, LANES - NGATE))[None, :]

        h1_flat = h1.reshape(NTOK, D)
        p3 = _in_proj(h1_flat, wp).reshape(BATCH, TT, P_COLS)
        gates, lfp = _gate_proj(h1_flat, wg, bg)

        qk = _mlstm_prep(p3, a_conv[l], rope_c, rope_s)
        hf, hbw = _mlstm_scan(qk, p3, gates.reshape(BATCH, TT, LANES), lfp.reshape(BATCH, TT, LANES))
        hb_lat = _na_attention(p3, na_bias, l)
        hb_ctx = _ctx_attention(p3)

        moe = l % 2 == 1
        x1, h2 = _merge(hf, hbw, p3, hb_lat, hb_ctx, xs, mod, a_hnorm_g[l][None, :],
                        w_br_a[l].astype(BF16), w_br_b[l].astype(BF16), w_out[l].astype(BF16),
                        g_norm2[l][None, :], F32 if moe else BF16)
        if last:
            gn, modn = g_final[None, :], mod_zero
        else:
            gn, modn = g_norm1[l + 1][None, :], mod_all[l + 1]
        j = l // 2
        if not moe:
            xs, h1 = _dense_ffn(h2, x1, mod, ffn_w1[j].astype(BF16), ffn_w3[j].astype(BF16),
                                ffn_w2[j].astype(BF16), gn, modn)
        else:
            wr = jnp.pad(moe_router[j], ((0, 0), (0, LANES - N_EXPERTS))).astype(BF16)
            xs, h1 = _moe_ffn(h2, x1, mod, wr, moe_w1, moe_w3, moe_w2, j, gn, modn, last,
                              F32 if last else BF16)
            if last:
                out = h1
    return out
```
